```python
import math
import jax
import jax.numpy as jnp
from jax import lax
import numpy as np

D_MODEL = 1024
BATCH = 4
SEQ = 4096
DEPTH = 1

GRID_W = 64
CTX_LEN = 256
D_MIX = D_MODEL

HY_WIDTH = D_MIX // 2
HY_ORDER = 2
HY_SHORT = 3
HY_BANDS = 8
HY_POS_DIM = 1 + 2 * HY_BANDS
HY_FILTER_HIDDEN = 64
HY_FILTER_SCALE = 0.05
HY_DECAY_TARGET = 1e-2
HY_FAST_DECAY_PCT = 0.3
HY_SLOW_DECAY_PCT = 1.5
HY_WINDOW_SHIFT = 0.05

ATT_HEADS = 8
ATT_KV_HEADS = 2
ATT_REP = ATT_HEADS // ATT_KV_HEADS
HEAD_DIM = 64
ATT_WIDTH = ATT_HEADS * HEAD_DIM
WINDOW = 128
BLOCK = 128
ROPE_BASE = 10000.0
ROPE_FREQS = HEAD_DIM // 4

PROJ_HY = (HY_ORDER + 1) * HY_WIDTH
PROJ_Q = ATT_WIDTH
PROJ_KV = ATT_KV_HEADS * HEAD_DIM
KV_START = PROJ_HY + PROJ_Q
PROJ_TOTAL = KV_START + 2 * PROJ_KV

PEER_KEYS = 128
PEER_EXPERTS = PEER_KEYS * PEER_KEYS
PEER_HEADS = 8
PEER_QDIM = 256
PEER_TOPK = 16
PEER_BLOCK = 128

LN_EPS = 1e-5
NEG_INF = -1e30
DEEPNORM_ALPHA = (2.0 * DEPTH) ** 0.25
DEEPNORM_BETA = (8.0 * DEPTH) ** -0.25

kernel_name = 'hybrid_hyena_swa_peer_diffusion_block'


def layer_norm(x, g, b):
    xf = x.astype(jnp.float32)
    mu = jnp.mean(xf, axis=-1, keepdims=True)
    var = jnp.mean(jnp.square(xf - mu), axis=-1, keepdims=True)
    return ((xf - mu) * lax.rsqrt(var + LN_EPS)).astype(x.dtype) * g + b


def short_conv(u, w, b):
    n = u.shape[1]
    up = jnp.pad(u, ((0, 0), (1, 1), (0, 0)))
    return up[:, :n] * w[0] + up[:, 1:n + 1] * w[1] + up[:, 2:] * w[2] + b


def hyena_filters(n, w1, b1, f1, w2, b2, f2, w3, b3):
    f32 = jnp.float32
    t = jnp.linspace(0.0, 1.0, n, dtype=f32)[:, None]
    w = 2.0 * math.pi * jnp.arange(n, dtype=f32)[:, None] / n
    bands = jnp.linspace(1e-4, HY_BANDS - 1, HY_BANDS, dtype=f32)[None, :]
    z = jnp.concatenate([t, jnp.cos(bands * w), -jnp.sin(bands * w)], axis=-1)
    h = jnp.sin(f1.astype(f32) * (z @ w1.astype(f32) + b1.astype(f32)))
    h = jnp.sin(f2.astype(f32) * (h @ w2.astype(f32) + b2.astype(f32)))
    h = (h @ w3.astype(f32) + b3.astype(f32)).reshape(n, 2, HY_ORDER, HY_WIDTH)
    min_decay = math.log(HY_DECAY_TARGET) / HY_SLOW_DECAY_PCT
    max_decay = math.log(HY_DECAY_TARGET) / HY_FAST_DECAY_PCT
    deltas = jnp.abs(jnp.linspace(min_decay, max_decay, HY_WIDTH, dtype=f32))
    window = jnp.exp(-t * deltas[None, :])[:, None, None, :]
    return h * (window + HY_WINDOW_SHIFT)


def two_sided_spectrum(h):
    n = h.shape[0]
    taps = jnp.concatenate([h[:, 0], jnp.zeros_like(h[:1, 0]), h[1:, 1][::-1]], axis=0)
    return jnp.fft.rfft(taps, axis=0)


def fft_long_conv(u, spec, skip):
    n = u.shape[1]
    y = jnp.fft.irfft(jnp.fft.rfft(u, n=2 * n, axis=1) * spec[None], n=2 * n, axis=1)[:, :n]
    return y + u * skip


def hyena_mixer(p, conv_w, conv_b, w1, b1, f1, w2, b2, f2, w3, b3, skip):
    n = p.shape[1]
    u = short_conv(p, conv_w, conv_b).astype(jnp.float32)
    v, x1, x2 = jnp.split(u, HY_ORDER + 1, axis=-1)
    spec = two_sided_spectrum(hyena_filters(n, w1, b1, f1, w2, b2, f2, w3, b3))
    skip = skip.astype(jnp.float32)
    y = x1 * fft_long_conv(v, spec[:, 0], skip[0])
    y = x2 * fft_long_conv(y, spec[:, 1], skip[1])
    return y.astype(p.dtype)


def axial_rope_tables(n):
    rows = n // GRID_W
    row = jnp.repeat(jnp.arange(rows, dtype=jnp.float32), GRID_W)
    col = jnp.tile(jnp.arange(GRID_W, dtype=jnp.float32), rows)
    inv = ROPE_BASE ** (-jnp.arange(ROPE_FREQS, dtype=jnp.float32) / ROPE_FREQS)
    ang = jnp.stack([row[:, None] * inv, col[:, None] * inv], axis=1)
    return jnp.cos(ang), jnp.sin(ang)


def apply_axial_rope(t, cos, sin):
    xs = t.astype(jnp.float32).reshape(t.shape[:-1] + (2, 2, ROPE_FREQS))
    a, b = xs[..., 0, :], xs[..., 1, :]
    cos, sin = cos[None, :, None], sin[None, :, None]
    out = jnp.stack([a * cos - b * sin, b * cos + a * sin], axis=-2)
    return out.reshape(t.shape).astype(t.dtype)


def band_blocks(t):
    b, n = t.shape[:2]
    nb = n // BLOCK
    tp = jnp.pad(t, ((0, 0), (BLOCK, BLOCK), (0, 0), (0, 0))).reshape((b, nb + 2, BLOCK) + t.shape[2:])
    return jnp.concatenate([tp[:, :-2], tp[:, 1:-1], tp[:, 2:]], axis=2)


def latent_attention(q, k, v, k_ctx, v_ctx, sink):
    b, n = q.shape[:2]
    nb = n // BLOCK
    n_loc = 3 * BLOCK
    n_ctx = k_ctx.shape[1]
    scale = HEAD_DIM ** -0.5
    qb = q.reshape(b, nb, BLOCK, ATT_KV_HEADS, ATT_REP, HEAD_DIM)
    kb, vb = band_blocks(k), band_blocks(v)
    s_loc = jnp.einsum('bnqgrd,bnkgd->bngrqk', qb, kb).astype(jnp.float32) * scale
    s_ctx = jnp.einsum('bnqgrd,bcgd->bngrqc', qb, k_ctx).astype(jnp.float32) * scale
    qi = jnp.arange(BLOCK)[:, None]
    kj = jnp.arange(n_loc)[None, :]
    kpos = (jnp.arange(nb)[:, None, None] - 1) * BLOCK + kj[None]
    mask = (jnp.abs(kj - BLOCK - qi)[None] <= WINDOW) & (kpos >= 0) & (kpos < n)
    s_loc = jnp.where(mask[None, :, None, None], s_loc, NEG_INF)
    s_sink = jnp.broadcast_to(sink.astype(jnp.float32).reshape(1, 1, ATT_KV_HEADS, ATT_REP, 1, 1), s_loc.shape[:-1] + (1,))
    p = jax.nn.softmax(jnp.concatenate([s_loc, s_ctx, s_sink], axis=-1), axis=-1).astype(v.dtype)
    o = (jnp.einsum('bngrqk,bnkgd->bnqgrd', p[..., :n_loc], vb)
         + jnp.einsum('bngrqc,bcgd->bnqgrd', p[..., n_loc:n_loc + n_ctx], v_ctx))
    return o.reshape(b, n, ATT_WIDTH)


def context_attention(q, k, v, sink):
    b, n = q.shape[:2]
    qg = q.reshape(b, n, ATT_KV_HEADS, ATT_REP, HEAD_DIM)
    s = jnp.einsum('bqgrd,bkgd->bgrqk', qg, k).astype(jnp.float32) * HEAD_DIM ** -0.5
    s_sink = jnp.broadcast_to(sink.astype(jnp.float32).reshape(1, ATT_KV_HEADS, ATT_REP, 1, 1), s.shape[:-1] + (1,))
    p = jax.nn.softmax(jnp.concatenate([s, s_sink], axis=-1), axis=-1).astype(v.dtype)
    o = jnp.einsum('bgrqk,bkgd->bqgrd', p[..., :n], v)
    return o.reshape(b, n, ATT_WIDTH)


def peer_mixer(h, wq, keys1, keys2, u_tab, v_tab):
    b, n, d = h.shape
    nb = n // PEER_BLOCK
    half = PEER_QDIM // 2
    hb = jnp.swapaxes(h.reshape(b, nb, PEER_BLOCK, d), 0, 1)

    def retrieve(xb):
        q = (xb @ wq).reshape(b, PEER_BLOCK, PEER_HEADS, 2, half)
        s1 = jnp.einsum('bthd,kd->bthk', q[..., 0, :], keys1).astype(jnp.float32)
        s2 = jnp.einsum('bthd,kd->bthk', q[..., 1, :], keys2).astype(jnp.float32)
        v1, i1 = lax.top_k(s1, PEER_TOPK)
        v2, i2 = lax.top_k(s2, PEER_TOPK)
        cand_shape = (b, PEER_BLOCK, PEER_HEADS, PEER_TOPK * PEER_TOPK)
        cand_s = (v1[..., :, None] + v2[..., None, :]).reshape(cand_shape)
        cand_i = (i1[..., :, None] * PEER_KEYS + i2[..., None, :]).reshape(cand_shape)
        top_s, pos = lax.top_k(cand_s, PEER_TOPK)
        experts = jnp.take_along_axis(cand_i, pos, axis=-1)
        g = jax.nn.softmax(top_s, axis=-1).astype(xb.dtype)
        act = jax.nn.gelu(jnp.einsum('bthkd,btd->bthk', u_tab[experts], xb), approximate=False)
        return jnp.einsum('bthk,bthkd->btd', act * g, v_tab[experts])

    y = lax.map(retrieve, hb)
    return jnp.swapaxes(y, 0, 1).reshape(b, n, d)


def setup_inputs(seed: int = 0) -> dict:
    key = jax.random.key(seed)
    ks = jax.random.split(key, 32)
    f32 = jnp.float32

    def nrm(k, shape, std):
        return jax.random.normal(k, shape, f32) * std

    L = DEPTH
    return {
        'x': nrm(ks[0], (BATCH, SEQ, D_MODEL), 1.0),
        'c': nrm(ks[1], (BATCH, D_MODEL), 1.0),
        'ctx': nrm(ks[2], (BATCH, CTX_LEN, D_MODEL), 1.0),
        'c_ctx': nrm(ks[3], (D_MODEL,), 1.0),
        'w_mod': nrm(ks[4], (L, D_MODEL, 6 * D_MODEL), 0.5 * D_MODEL ** -0.5),
        'b_mod': nrm(ks[5], (L, 6 * D_MODEL), 0.02),
        'w_in': nrm(ks[6], (L, D_MODEL, PROJ_TOTAL), D_MODEL ** -0.5),
        'hy_conv_w': nrm(ks[7], (L, HY_SHORT, PROJ_HY), HY_SHORT ** -0.5),
        'hy_conv_b': nrm(ks[8], (L, PROJ_HY), 0.02),
        'hy_f_w1': nrm(ks[9], (L, HY_POS_DIM, HY_FILTER_HIDDEN), HY_POS_DIM ** -0.5),
        'hy_f_b1': nrm(ks[10], (L, HY_FILTER_HIDDEN), 0.02),
        'hy_f_freq1': 1.0 + nrm(ks[11], (L, HY_FILTER_HIDDEN), 0.02),
        'hy_f_w2': nrm(ks[12], (L, HY_FILTER_HIDDEN, HY_FILTER_HIDDEN), HY_FILTER_HIDDEN ** -0.5),
        'hy_f_b2': nrm(ks[13], (L, HY_FILTER_HIDDEN), 0.02),
        'hy_f_freq2': 1.0 + nrm(ks[14], (L, HY_FILTER_HIDDEN), 0.02),
        'hy_f_w3': nrm(ks[15], (L, HY_FILTER_HIDDEN, 2 * HY_ORDER * HY_WIDTH), HY_FILTER_SCALE * HY_FILTER_HIDDEN ** -0.5),
        'hy_f_b3': nrm(ks[16], (L, 2 * HY_ORDER * HY_WIDTH), 0.002),
        'hy_skip': nrm(ks[17], (L, HY_ORDER, HY_WIDTH), 0.1),
        'attn_sink': nrm(ks[18], (L, ATT_HEADS), 0.5),
        'w_out': nrm(ks[19], (L, D_MIX, D_MODEL), DEEPNORM_BETA * D_MIX ** -0.5),
        'ln1_g': 1.0 + nrm(ks[20], (L, D_MODEL), 0.02),
        'ln1_b': nrm(ks[21], (L, D_MODEL), 0.02),
        'peer_wq': nrm(ks[22], (L, D_MODEL, PEER_HEADS * PEER_QDIM), D_MODEL ** -0.5),
        'peer_keys1': nrm(ks[23], (L, PEER_KEYS, PEER_QDIM // 2), (PEER_QDIM // 2) ** -0.5),
        'peer_keys2': nrm(ks[24], (L, PEER_KEYS, PEER_QDIM // 2), (PEER_QDIM // 2) ** -0.5),
        'peer_u': nrm(ks[25], (L, PEER_EXPERTS, D_MODEL), D_MODEL ** -0.5),
        'peer_v': nrm(ks[26], (L, PEER_EXPERTS, D_MODEL), DEEPNORM_BETA),
        'ln2_g': 1.0 + nrm(ks[27], (L, D_MODEL), 0.02),
        'ln2_b': nrm(ks[28], (L, D_MODEL), 0.02),
    }


def reference(x, c, ctx, c_ctx, w_mod, b_mod, w_in, hy_conv_w, hy_conv_b, hy_f_w1, hy_f_b1,
              hy_f_freq1, hy_f_w2, hy_f_b2, hy_f_freq2, hy_f_w3, hy_f_b3, hy_skip, attn_sink,
              w_out, ln1_g, ln1_b, peer_wq, peer_keys1, peer_keys2, peer_u, peer_v, ln2_g, ln2_b):
    b, n, d = x.shape
    n_ctx = ctx.shape[1]
    cos, sin = axial_rope_tables(n)
    for l in range(DEPTH):
        last = l == DEPTH - 1
        mod = (jax.nn.silu(c) @ w_mod[l] + b_mod[l]).reshape(b, 6, 1, d)
        sh1, sc1, g1, sh2, sc2, g2 = (mod[:, i] for i in range(6))
        mod_c = (jax.nn.silu(c_ctx) @ w_mod[l] + b_mod[l]).reshape(6, d)
        csh1, csc1, cg1, csh2, csc2, cg2 = (mod_c[i] for i in range(6))
        hy_params = (hy_conv_w[l], hy_conv_b[l], hy_f_w1[l], hy_f_b1[l], hy_f_freq1[l], hy_f_w2[l],
                     hy_f_b2[l], hy_f_freq2[l], hy_f_w3[l], hy_f_b3[l], hy_skip[l])
        peer_params = (peer_wq[l], peer_keys1[l], peer_keys2[l], peer_u[l], peer_v[l])

        h_lat = x * (1.0 + sc1) + sh1
        h_ctx = ctx * (1.0 + csc1) + csh1
        p_lat = h_lat @ w_in[l]
        kv_ctx = h_ctx @ w_in[l][:, KV_START:]
        k_c = kv_ctx[..., :PROJ_KV].reshape(b, n_ctx, ATT_KV_HEADS, HEAD_DIM)
        v_c = kv_ctx[..., PROJ_KV:].reshape(b, n_ctx, ATT_KV_HEADS, HEAD_DIM)
        q_lat = apply_axial_rope(p_lat[..., PROJ_HY:KV_START].reshape(b, n, ATT_HEADS, HEAD_DIM), cos, sin)
        k_lat = apply_axial_rope(p_lat[..., KV_START:KV_START + PROJ_KV].reshape(b, n, ATT_KV_HEADS, HEAD_DIM), cos, sin)
        v_lat = p_lat[..., KV_START + PROJ_KV:].reshape(b, n, ATT_KV_HEADS, HEAD_DIM)
        att_lat = latent_attention(q_lat, k_lat, v_lat, k_c, v_c, attn_sink[l])
        hy_lat = hyena_mixer(p_lat[..., :PROJ_HY], *hy_params)
        y_lat = jnp.concatenate([hy_lat, att_lat], axis=-1) @ w_out[l]
        if not last:
            p_c = h_ctx @ w_in[l][:, :KV_START]
            q_c = p_c[..., PROJ_HY:].reshape(b, n_ctx, ATT_HEADS, HEAD_DIM)
            att_c = context_attention(q_c, k_c, v_c, attn_sink[l])
            hy_c = hyena_mixer(p_c[..., :PROJ_HY], *hy_params)
            y_c = jnp.concatenate([hy_c, att_c], axis=-1) @ w_out[l]
            ctx = layer_norm(DEEPNORM_ALPHA * ctx + cg1 * y_c, ln1_g[l], ln1_b[l])
            ctx = layer_norm(DEEPNORM_ALPHA * ctx + cg2 * peer_mixer(ctx * (1.0 + csc2) + csh2, *peer_params),
                             ln2_g[l], ln2_b[l])
        x = layer_norm(DEEPNORM_ALPHA * x + g1 * y_lat, ln1_g[l], ln1_b[l])
        x = layer_norm(DEEPNORM_ALPHA * x + g2 * peer_mixer(x * (1.0 + sc2) + sh2, *peer_params),
                       ln2_g[l], ln2_b[l])
    return x
```

```python
import functools
import math

import jax
import jax.numpy as jnp
import numpy as np
from jax import lax
from jax.experimental import pallas as pl
from jax.experimental.pallas import tpu as pltpu

F32 = jnp.float32
BF16 = jnp.bfloat16
HIGHEST = lax.Precision.HIGHEST

LANES = 128
VMEM_LIMIT = 56 * 1024 * 1024

D_MODEL = 1024
GRID_W = 64
HY_WIDTH = 512
HY_ORDER = 2
HY_BANDS = 8
HY_DECAY_TARGET = 1e-2
HY_FAST_DECAY_PCT = 0.3
HY_SLOW_DECAY_PCT = 1.5
HY_WINDOW_SHIFT = 0.05
ATT_HEADS = 8
ATT_KV_HEADS = 2
ATT_REP = ATT_HEADS // ATT_KV_HEADS
HEAD_DIM = 64
WINDOW = 128
BLOCK = 128
ROPE_BASE = 10000.0
ROPE_FREQS = HEAD_DIM // 4
PROJ_HY = (HY_ORDER + 1) * HY_WIDTH
PROJ_Q = ATT_HEADS * HEAD_DIM
PROJ_KV = ATT_KV_HEADS * HEAD_DIM
KV_START = PROJ_HY + PROJ_Q
PEER_KEYS = 128
PEER_HEADS = 8
PEER_QDIM = 256
PEER_TOPK = 16
LN_EPS = 1e-5
NEG_INF = -1e30
DEPTH = 1
DEEPNORM_ALPHA = (2.0 * DEPTH) ** 0.25

QPAD = ATT_HEADS * LANES
KVPAD = ATT_KV_HEADS * LANES
COL_Q = 0
COL_HY = QPAD
COL_K = QPAD + PROJ_HY
COL_V = COL_K + KVPAD
PROJ_PAD = COL_V + KVPAD


def _cparams(sem):
    return pltpu.CompilerParams(dimension_semantics=sem, vmem_limit_bytes=VMEM_LIMIT)


def _mod_body(c_ref, w_ref, b_ref, o_ref):
    c = c_ref[...]
    a = c * jax.nn.sigmoid(c)
    o_ref[...] = jnp.dot(a, w_ref[...], preferred_element_type=F32, precision=HIGHEST) + b_ref[...]


def _modulation(cc, w_mod, b_mod):
    rows, d = cc.shape
    n_out = w_mod.shape[1]
    tn = 1536
    return pl.pallas_call(
        _mod_body,
        grid=(n_out // tn,),
        in_specs=[
            pl.BlockSpec((rows, d), lambda j: (0, 0)),
            pl.BlockSpec((d, tn), lambda j: (0, j)),
            pl.BlockSpec((1, tn), lambda j: (0, j)),
        ],
        out_specs=pl.BlockSpec((rows, tn), lambda j: (0, j)),
        out_shape=jax.ShapeDtypeStruct((rows, n_out), F32),
        compiler_params=_cparams(("arbitrary",)),
        name="modulation",
    )(cc, w_mod, b_mod)


def _mod_matmul_body(x_ref, sc_ref, sh_ref, w_ref, o_ref):
    h = x_ref[0] * (1.0 + sc_ref[0]) + sh_ref[0]
    o_ref[0] = jnp.dot(h.astype(BF16), w_ref[...], preferred_element_type=F32)


def _mod_matmul(x, sc, sh, w, tm, name):
    b, n, d = x.shape
    n_out = w.shape[1]
    per_batch = sc.shape[0] == b
    mod_map = (lambda bi, i: (bi, 0, 0)) if per_batch else (lambda bi, i: (0, 0, 0))
    return pl.pallas_call(
        _mod_matmul_body,
        grid=(b, n // tm),
        in_specs=[
            pl.BlockSpec((1, tm, d), lambda bi, i: (bi, i, 0)),
            pl.BlockSpec((1, 1, d), mod_map),
            pl.BlockSpec((1, 1, d), mod_map),
            pl.BlockSpec((d, n_out), lambda bi, i: (0, 0)),
        ],
        out_specs=pl.BlockSpec((1, tm, n_out), lambda bi, i: (bi, i, 0)),
        out_shape=jax.ShapeDtypeStruct((b, n, n_out), F32),
        compiler_params=_cparams(("parallel", "parallel")),
        name=name,
    )(x, sc, sh, w)


def _rope_head(x, cos, sin_signed):
    lane = lax.broadcasted_iota(jnp.int32, x.shape, 1)
    first_half = (lane % 32) < 16
    partner = jnp.where(first_half, pltpu.roll(x, LANES - 16, 1), pltpu.roll(x, 16, 1))
    return x * cos + partner * sin_signed


ATT_TQ = 512
ROPE_CHUNK = 512


def _attn_body(sink_ref, q_ref, k_ref, v_ref, kvc_ref, cosq_ref, sinq_ref, cosk_ref, sink_tab_ref,
               o_ref, ks_ref, vs_ref, kcs_ref, vcs_ref):
    n = k_ref.shape[1]
    iq = pl.program_id(1)
    scale = HEAD_DIM ** -0.5

    @pl.when(iq == 0)
    def _prepare_keys():
        def chunk(ci, carry):
            r0 = pl.multiple_of(ci * ROPE_CHUNK, ROPE_CHUNK)
            cos = cosk_ref[pl.ds(r0, ROPE_CHUNK), :]
            sin = sink_tab_ref[pl.ds(r0, ROPE_CHUNK), :]
            for g in range(ATT_KV_HEADS):
                kg = k_ref[0, pl.ds(r0, ROPE_CHUNK), g * LANES:(g + 1) * LANES]
                ks_ref[pl.ds(r0, ROPE_CHUNK), g * LANES:(g + 1) * LANES] = _rope_head(kg, cos, sin).astype(BF16)
            vs_ref[pl.ds(r0, ROPE_CHUNK), :] = v_ref[0, pl.ds(r0, ROPE_CHUNK), :].astype(BF16)
            return carry
        lax.fori_loop(0, n // ROPE_CHUNK, chunk, 0)
        kcs_ref[...] = kvc_ref[0, :, 0:KVPAD].astype(BF16)
        vcs_ref[...] = kvc_ref[0, :, KVPAD:2 * KVPAD].astype(BF16)

    n_loc = 3 * BLOCK
    rows = ATT_REP * BLOCK
    row_i = lax.broadcasted_iota(jnp.int32, (rows, n_loc), 0)
    col_i = lax.broadcasted_iota(jnp.int32, (rows, n_loc), 1)
    rel = col_i - (row_i % BLOCK)
    head_of_row = lax.broadcasted_iota(jnp.int32, (rows, 1), 0) // BLOCK

    for j in range(ATT_TQ // BLOCK):
        blk = iq * (ATT_TQ // BLOCK) + j
        start = pl.multiple_of(jnp.clip((blk - 1) * BLOCK, 0, n - n_loc), BLOCK)
        qrows = slice(j * BLOCK, (j + 1) * BLOCK)
        cosq = cosq_ref[qrows, :]
        sinq = sinq_ref[qrows, :]
        delta = rel + (start - blk * BLOCK)
        in_window = jnp.abs(delta) <= WINDOW
        for g in range(ATT_KV_HEADS):
            heads = [ATT_REP * g + r for r in range(ATT_REP)]
            qg = jnp.concatenate(
                [(_rope_head(q_ref[0, qrows, h * LANES:(h + 1) * LANES], cosq, sinq) * scale).astype(BF16)
                 for h in heads], axis=0)
            kg = ks_ref[pl.ds(start, n_loc), g * LANES:(g + 1) * LANES]
            vg = vs_ref[pl.ds(start, n_loc), g * LANES:(g + 1) * LANES]
            kcg = kcs_ref[:, g * LANES:(g + 1) * LANES]
            vcg = vcs_ref[:, g * LANES:(g + 1) * LANES]
            nt = (((1,), (1,)), ((), ()))
            s_loc = lax.dot_general(qg, kg, nt, preferred_element_type=F32)
            s_ctx = lax.dot_general(qg, kcg, nt, preferred_element_type=F32)
            s_loc = jnp.where(in_window, s_loc, NEG_INF)
            sink_col = jnp.zeros((rows, 1), F32)
            for r, h in enumerate(heads):
                sink_col = jnp.where(head_of_row == r, sink_ref[h], sink_col)
            m = jnp.maximum(jnp.maximum(jnp.max(s_loc, axis=1, keepdims=True),
                                        jnp.max(s_ctx, axis=1, keepdims=True)), sink_col)
            p_loc = jnp.exp(s_loc - m)
            p_ctx = jnp.exp(s_ctx - m)
            den = (jnp.sum(p_loc, axis=1, keepdims=True) + jnp.sum(p_ctx, axis=1, keepdims=True)
                   + jnp.exp(sink_col - m))
            o = (jnp.dot(p_loc.astype(BF16), vg, preferred_element_type=F32)
                 + jnp.dot(p_ctx.astype(BF16), vcg, preferred_element_type=F32)) / den
            for r, h in enumerate(heads):
                o_ref[0, qrows, h * LANES:(h + 1) * LANES] = o[r * BLOCK:(r + 1) * BLOCK].astype(BF16)


def _attention(p_lat, kv_ctx, sink, cos_tab, sin_tab):
    b, n, _ = p_lat.shape
    n_ctx = kv_ctx.shape[1]
    grid_spec = pltpu.PrefetchScalarGridSpec(
        num_scalar_prefetch=1,
        grid=(b, n // ATT_TQ),
        in_specs=[
            pl.BlockSpec((1, ATT_TQ, QPAD), lambda bi, i, s: (bi, i, COL_Q // QPAD)),
            pl.BlockSpec((1, n, KVPAD), lambda bi, i, s: (bi, 0, COL_K // KVPAD)),
            pl.BlockSpec((1, n, KVPAD), lambda bi, i, s: (bi, 0, COL_V // KVPAD)),
            pl.BlockSpec((1, n_ctx, 2 * KVPAD), lambda bi, i, s: (bi, 0, 0)),
            pl.BlockSpec((ATT_TQ, LANES), lambda bi, i, s: (i, 0)),
            pl.BlockSpec((ATT_TQ, LANES), lambda bi, i, s: (i, 0)),
            pl.BlockSpec((n, LANES), lambda bi, i, s: (0, 0)),
            pl.BlockSpec((n, LANES), lambda bi, i, s: (0, 0)),
        ],
        out_specs=pl.BlockSpec((1, ATT_TQ, QPAD), lambda bi, i, s: (bi, i, 0)),
        scratch_shapes=[
            pltpu.VMEM((n, KVPAD), BF16),
            pltpu.VMEM((n, KVPAD), BF16),
            pltpu.VMEM((n_ctx, KVPAD), BF16),
            pltpu.VMEM((n_ctx, KVPAD), BF16),
        ],
    )
    return pl.pallas_call(
        _attn_body,
        grid_spec=grid_spec,
        out_shape=jax.ShapeDtypeStruct((b, n, QPAD), BF16),
        compiler_params=_cparams(("parallel", "arbitrary")),
        name="window_attention",
    )(sink, p_lat, p_lat, p_lat, kv_ctx, cos_tab, sin_tab, cos_tab, sin_tab)


def _rope_tables(n):
    rows = n // GRID_W
    row = jnp.repeat(jnp.arange(rows, dtype=F32), GRID_W)
    col = jnp.tile(jnp.arange(GRID_W, dtype=F32), rows)
    inv = ROPE_BASE ** (-jnp.arange(ROPE_FREQS, dtype=F32) / ROPE_FREQS)
    ang_r = row[:, None] * inv
    ang_c = col[:, None] * inv
    pad1 = jnp.ones((n, LANES - HEAD_DIM), F32)
    pad0 = jnp.zeros((n, LANES - HEAD_DIM), F32)
    cos = jnp.concatenate([jnp.cos(ang_r), jnp.cos(ang_r), jnp.cos(ang_c), jnp.cos(ang_c), pad1], axis=1)
    sin = jnp.concatenate([-jnp.sin(ang_r), jnp.sin(ang_r), -jnp.sin(ang_c), jnp.sin(ang_c), pad0], axis=1)
    return cos, sin


def _layer_norm(r, g, b):
    mu = jnp.mean(r, axis=-1, keepdims=True)
    var = jnp.mean(jnp.square(r - mu), axis=-1, keepdims=True)
    return (r - mu) * lax.rsqrt(var + LN_EPS) * g + b


def _outproj_body(hy_ref, att_ref, x_ref, wh_ref, wa_ref, g1_ref, sc2_ref, sh2_ref, lg_ref, lb_ref,
                  x1_ref, hq_ref):
    y = (jnp.dot(hy_ref[0].astype(BF16), wh_ref[...], preferred_element_type=F32)
         + jnp.dot(att_ref[0], wa_ref[...], preferred_element_type=F32))
    x1 = _layer_norm(DEEPNORM_ALPHA * x_ref[0] + g1_ref[0] * y, lg_ref[...], lb_ref[...])
    x1_ref[0] = x1
    hq_ref[0] = (x1 * (1.0 + sc2_ref[0]) + sh2_ref[0]).astype(BF16)


def _outproj_ln(hy, att, x, w_hy, w_att, g1, sc2, sh2, ln_g, ln_b, tm=512):
    b, n, d = x.shape
    modspec = pl.BlockSpec((1, 1, d), lambda bi, i: (bi, 0, 0))
    vecspec = pl.BlockSpec((1, d), lambda bi, i: (0, 0))
    return pl.pallas_call(
        _outproj_body,
        grid=(b, n // tm),
        in_specs=[
            pl.BlockSpec((1, tm, HY_WIDTH), lambda bi, i: (bi, i, 0)),
            pl.BlockSpec((1, tm, QPAD), lambda bi, i: (bi, i, 0)),
            pl.BlockSpec((1, tm, d), lambda bi, i: (bi, i, 0)),
            pl.BlockSpec(w_hy.shape, lambda bi, i: (0, 0)),
            pl.BlockSpec(w_att.shape, lambda bi, i: (0, 0)),
            modspec, modspec, modspec, vecspec, vecspec,
        ],
        out_specs=[
            pl.BlockSpec((1, tm, d), lambda bi, i: (bi, i, 0)),
            pl.BlockSpec((1, tm, d), lambda bi, i: (bi, i, 0)),
        ],
        out_shape=[jax.ShapeDtypeStruct((b, n, d), F32), jax.ShapeDtypeStruct((b, n, d), BF16)],
        compiler_params=_cparams(("parallel", "parallel")),
        name="outproj_ln1",
    )(hy, att, x, w_hy, w_att, g1, sc2, sh2, ln_g, ln_b)


PEER_TM = 256
_STAIR = sorted(((i, j) for i in range(PEER_TOPK) for j in range(PEER_TOPK) if (i + 1) * (j + 1) <= PEER_TOPK),
                key=lambda p: p[0] * PEER_TOPK + p[1])
_STAIR_ROWS = -(-len(_STAIR) // 8) * 8


def _select_topk(s_ref, n_rows, write_row):
    tm = s_ref.shape[1]
    row_id = lax.broadcasted_iota(jnp.int32, (n_rows, tm), 0)

    def step(k, prev_idx):
        s = jnp.where(row_id == prev_idx, -jnp.inf, s_ref[...])
        s_ref[...] = s
        m = jnp.max(s, axis=0, keepdims=True)
        idx = jnp.min(jnp.where(s == m, row_id, n_rows), axis=0, keepdims=True)
        write_row(k, m, idx)
        return idx

    lax.fori_loop(0, PEER_TOPK, step, jnp.full((1, tm), -1, jnp.int32))


def _peer_topk_body(hq_ref, wq_ref, k1_ref, k2_ref, a_ref, b_ref, g_ref,
                    q_s, s_s, v1_s, i1_s, v2_s, i2_s, c_s, ca_s, cb_s, t_s, ao_s, bo_s, go_s):
    tm = hq_ref.shape[0]
    q = jnp.dot(hq_ref[...], wq_ref[...], preferred_element_type=F32)
    for c in range(2 * PEER_HEADS):
        q_s[c] = q[:, c * LANES:(c + 1) * LANES]
    nt = (((1,), (1,)), ((), ()))

    def head(h, carry):
        for half, (kref, vs, is_) in enumerate(((k1_ref, v1_s, i1_s), (k2_ref, v2_s, i2_s))):
            s_s[...] = lax.dot_general(kref[...], q_s[2 * h + half], nt, preferred_element_type=F32,
                                       precision=HIGHEST)

            def write(k, val, idx, vs=vs, is_=is_):
                vs[pl.ds(k, 1), :] = val
                is_[pl.ds(k, 1), :] = idx.astype(F32)
            _select_topk(s_s, PEER_KEYS, write)

        c_s[...] = jnp.full(c_s.shape, -jnp.inf, F32)
        ca_s[...] = jnp.zeros(ca_s.shape, F32)
        cb_s[...] = jnp.zeros(cb_s.shape, F32)
        for r, (i, j) in enumerate(_STAIR):
            c_s[r:r + 1, :] = v1_s[i:i + 1, :] + v2_s[j:j + 1, :]
            ca_s[r:r + 1, :] = i1_s[i:i + 1, :]
            cb_s[r:r + 1, :] = i2_s[j:j + 1, :]

        row_id = lax.broadcasted_iota(jnp.int32, (_STAIR_ROWS, tm), 0)

        def write2(k, val, idx):
            hit = row_id == idx
            t_s[pl.ds(k, 1), :] = val
            ao_s[pl.ds(h * PEER_TOPK + k, 1), :] = jnp.max(jnp.where(hit, ca_s[...], -1.0), axis=0, keepdims=True)
            bo_s[pl.ds(h * PEER_TOPK + k, 1), :] = jnp.max(jnp.where(hit, cb_s[...], -1.0), axis=0, keepdims=True)
        _select_topk(c_s, _STAIR_ROWS, write2)

        t = t_s[...]
        e = jnp.exp(t - jnp.max(t, axis=0, keepdims=True))
        go_s[pl.ds(pl.multiple_of(h * PEER_TOPK, PEER_TOPK), PEER_TOPK), :] = e / jnp.sum(e, axis=0, keepdims=True)
        return carry

    lax.fori_loop(0, PEER_HEADS, head, 0)
    a_ref[...] = ao_s[...].T
    b_ref[...] = bo_s[...].T
    g_ref[...] = go_s[...].T


def _peer_topk(hq, wq, keys1, keys2):
    t, d = hq.shape
    tm = PEER_TM
    hk = PEER_HEADS * PEER_TOPK
    out = jax.ShapeDtypeStruct((t, hk), F32)
    ospec = pl.BlockSpec((tm, hk), lambda i: (i, 0))
    return pl.pallas_call(
        _peer_topk_body,
        grid=(t // tm,),
        in_specs=[
            pl.BlockSpec((tm, d), lambda i: (i, 0)),
            pl.BlockSpec(wq.shape, lambda i: (0, 0)),
            pl.BlockSpec(keys1.shape, lambda i: (0, 0)),
            pl.BlockSpec(keys2.shape, lambda i: (0, 0)),
        ],
        out_specs=[ospec, ospec, ospec],
        out_shape=[out, out, out],
        scratch_shapes=[
            pltpu.VMEM((2 * PEER_HEADS, tm, LANES), F32),
            pltpu.VMEM((PEER_KEYS, tm), F32),
            pltpu.VMEM((PEER_TOPK, tm), F32), pltpu.VMEM((PEER_TOPK, tm), F32),
            pltpu.VMEM((PEER_TOPK, tm), F32), pltpu.VMEM((PEER_TOPK, tm), F32),
            pltpu.VMEM((_STAIR_ROWS, tm), F32), pltpu.VMEM((_STAIR_ROWS, tm), F32), pltpu.VMEM((_STAIR_ROWS, tm), F32),
            pltpu.VMEM((PEER_TOPK, tm), F32),
            pltpu.VMEM((hk, tm), F32), pltpu.VMEM((hk, tm), F32), pltpu.VMEM((hk, tm), F32),
        ],
        compiler_params=_cparams(("parallel",)),
        name="peer_topk",
    )(hq, wq, keys1, keys2)


PEER_TE = 2048
PEER_GROUPS = PEER_TE // PEER_KEYS
GATE_PITCH = PEER_KEYS + 8


def _peer_dense_body(hq_ref, a_ref, b_ref, g_ref, ut_ref, v_ref, x1_ref, g2_ref, lg_ref, lb_ref, o_ref,
                     ssel_s, w_s, hd_s, acc_s):
    tm = hq_ref.shape[0]
    n_e = (PEER_KEYS * PEER_KEYS) // PEER_TE
    e = pl.program_id(1)
    nt = (((1,), (1,)), ((), ()))

    @pl.when(e == 0)
    def _init():
        ssel_s[...] = jnp.zeros(ssel_s.shape, F32)

    @pl.when(e < n_e)
    def _scores():
        s = jnp.dot(hq_ref[...], ut_ref[...], preferred_element_type=F32)
        b_idx = b_ref[...].astype(jnp.int32)
        a_val = a_ref[...]
        cur = ssel_s[...]
        for jj in range(PEER_GROUPS):
            cand = jnp.take_along_axis(s[:, jj * LANES:(jj + 1) * LANES], b_idx, axis=1)
            cur = jnp.where(a_val == (e * PEER_GROUPS + jj).astype(F32), cand, cur)
        ssel_s[...] = cur

    @pl.when(e == n_e - 1)
    def _gates():
        s_sel = ssel_s[...]
        act = 0.5 * s_sel * (1.0 + lax.erf(s_sel * (2.0 ** -0.5)))
        w_s[...] = g_ref[...] * act
        sub = lax.broadcasted_iota(jnp.int32, (PEER_KEYS, LANES), 0).astype(F32)

        def token(t, carry):
            a_row = a_ref[pl.ds(t, 1), :]
            b_row = b_ref[pl.ds(t, 1), :]
            w_row = w_s[pl.ds(t, 1), :]
            lhs = jnp.where(sub == a_row, w_row, 0.0).astype(BF16)
            rhs = jnp.where(sub == b_row, 1.0, 0.0).astype(BF16)
            tile = lax.dot_general(lhs, rhs, nt, preferred_element_type=F32)
            hd_s[pl.ds(pl.multiple_of(t * GATE_PITCH, 8), PEER_KEYS), :] = tile
            return carry
        lax.fori_loop(0, tm, token, 0)
        acc_s[...] = jnp.zeros(acc_s.shape, F32)

    @pl.when(e >= n_e)
    def _values():
        j0 = (e - n_e) * PEER_GROUPS
        lhs = jnp.concatenate(
            [hd_s[pl.ds(j0 + jj, tm, stride=GATE_PITCH), :].astype(BF16) for jj in range(PEER_GROUPS)], axis=1)
        acc_s[...] += jnp.dot(lhs, v_ref[...], preferred_element_type=F32)

    @pl.when(e == 2 * n_e - 1)
    def _finish():
        o_ref[...] = _layer_norm(DEEPNORM_ALPHA * x1_ref[...] + g2_ref[0] * acc_s[...], lg_ref[...], lb_ref[...])


def _peer_dense(hq, a, b, g, u_t, v_tab, x1, g2, ln_g, ln_b, tokens_per_batch):
    t, d = hq.shape
    tm = PEER_TM
    hk = a.shape[1]
    n_e = u_t.shape[1] // PEER_TE
    tiles_per_batch = tokens_per_batch // tm
    tok = lambda i, e: (i, 0)
    return pl.pallas_call(
        _peer_dense_body,
        grid=(t // tm, 2 * n_e),
        in_specs=[
            pl.BlockSpec((tm, d), tok),
            pl.BlockSpec((tm, hk), tok), pl.BlockSpec((tm, hk), tok), pl.BlockSpec((tm, hk), tok),
            pl.BlockSpec((d, PEER_TE), lambda i, e: (0, jnp.minimum(e, n_e - 1))),
            pl.BlockSpec((PEER_TE, d), lambda i, e: (jnp.maximum(e - n_e, 0), 0)),
            pl.BlockSpec((tm, d), tok),
            pl.BlockSpec((1, 1, d), lambda i, e: (i // tiles_per_batch, 0, 0)),
            pl.BlockSpec((1, d), lambda i, e: (0, 0)),
            pl.BlockSpec((1, d), lambda i, e: (0, 0)),
        ],
        out_specs=pl.BlockSpec((tm, d), tok),
        out_shape=jax.ShapeDtypeStruct((t, d), F32),
        scratch_shapes=[
            pltpu.VMEM((tm, hk), F32),
            pltpu.VMEM((tm, hk), F32),
            pltpu.VMEM((tm * GATE_PITCH, LANES), F32),
            pltpu.VMEM((tm, d), F32),
        ],
        compiler_params=_cparams(("parallel", "arbitrary")),
        name="peer_dense",
    )(hq, a, b, g, u_t, v_tab, x1, g2, ln_g, ln_b)


def _short_conv(u, w, b):
    n = u.shape[1]
    up = jnp.pad(u, ((0, 0), (1, 1), (0, 0)))
    return up[:, :n] * w[0] + up[:, 1:n + 1] * w[1] + up[:, 2:] * w[2] + b


def _hyena_filters(n, w1, b1, f1, w2, b2, f2, w3, b3):
    t = jnp.linspace(0.0, 1.0, n, dtype=F32)[:, None]
    w = 2.0 * math.pi * jnp.arange(n, dtype=F32)[:, None] / n
    bands = jnp.linspace(1e-4, HY_BANDS - 1, HY_BANDS, dtype=F32)[None, :]
    z = jnp.concatenate([t, jnp.cos(bands * w), -jnp.sin(bands * w)], axis=-1)
    h = jnp.sin(f1 * (z @ w1 + b1))
    h = jnp.sin(f2 * (h @ w2 + b2))
    h = (h @ w3 + b3).reshape(n, 2, HY_ORDER, HY_WIDTH)
    min_decay = math.log(HY_DECAY_TARGET) / HY_SLOW_DECAY_PCT
    max_decay = math.log(HY_DECAY_TARGET) / HY_FAST_DECAY_PCT
    deltas = jnp.abs(jnp.linspace(min_decay, max_decay, HY_WIDTH, dtype=F32))
    window = jnp.exp(-t * deltas[None, :])[:, None, None, :]
    return h * (window + HY_WINDOW_SHIFT)


def _hyena(p, conv_w, conv_b, w1, b1, f1, w2, b2, f2, w3, b3, skip):
    n = p.shape[1]
    u = _short_conv(p, conv_w, conv_b)
    v, x1, x2 = jnp.split(u, HY_ORDER + 1, axis=-1)
    h = _hyena_filters(n, w1, b1, f1, w2, b2, f2, w3, b3)
    taps = jnp.concatenate([h[:, 0], jnp.zeros_like(h[:1, 0]), h[1:, 1][::-1]], axis=0)
    spec = jnp.fft.rfft(taps, axis=0)

    def conv(u_, sp, sk):
        y = jnp.fft.irfft(jnp.fft.rfft(u_, n=2 * n, axis=1) * sp[None], n=2 * n, axis=1)[:, :n]
        return y + u_ * sk
    y = x1 * conv(v, spec[:, 0], skip[0])
    return x2 * conv(y, spec[:, 1], skip[1])


def _pad_heads(w, heads):
    d = w.shape[0]
    w = w.reshape(d, heads, HEAD_DIM)
    w = jnp.pad(w, ((0, 0), (0, 0), (0, LANES - HEAD_DIM)))
    return w.reshape(d, heads * LANES)


def kernel(x, c, ctx, c_ctx, w_mod, b_mod, w_in, hy_conv_w, hy_conv_b, hy_f_w1, hy_f_b1, hy_f_freq1, hy_f_w2,
           hy_f_b2, hy_f_freq2, hy_f_w3, hy_f_b3, hy_skip, attn_sink, w_out, ln1_g, ln1_b, peer_wq, peer_keys1,
           peer_keys2, peer_u, peer_v, ln2_g, ln2_b):
    b, n, d = x.shape
    l = 0
    cc = jnp.concatenate([c, c_ctx[None], jnp.zeros((8 - b - 1, d), F32)], axis=0)
    mod = _modulation(cc, w_mod[l], b_mod[l][None])
    mod_lat = mod[:b].reshape(b, 6, 1, d)
    sh1, sc1, g1, sh2, sc2, g2 = (mod_lat[:, i] for i in range(6))
    mod_c = mod[b].reshape(6, 1, 1, d)
    csh1, csc1 = mod_c[0], mod_c[1]

    w = w_in[l]
    w_q = _pad_heads(w[:, PROJ_HY:KV_START], ATT_HEADS)
    w_k = _pad_heads(w[:, KV_START:KV_START + PROJ_KV], ATT_KV_HEADS)
    w_v = _pad_heads(w[:, KV_START + PROJ_KV:], ATT_KV_HEADS)
    w_pad = jnp.concatenate([w_q, w[:, :PROJ_HY], w_k, w_v], axis=1).astype(BF16)
    w_kv = jnp.concatenate([w_k, w_v], axis=1).astype(BF16)

    p_lat = _mod_matmul(x, sc1, sh1, w_pad, 512, "in_proj")
    kv_ctx = _mod_matmul(ctx, csc1, csh1, w_kv, ctx.shape[1], "ctx_kv_proj")

    cos_tab, sin_tab = _rope_tables(n)
    att = _attention(p_lat, kv_ctx, attn_sink[l], cos_tab, sin_tab)

    hy = _hyena(p_lat[..., COL_HY:COL_HY + PROJ_HY], hy_conv_w[l], hy_conv_b[l], hy_f_w1[l], hy_f_b1[l],
                hy_f_freq1[l], hy_f_w2[l], hy_f_b2[l], hy_f_freq2[l], hy_f_w3[l], hy_f_b3[l], hy_skip[l])

    wo = w_out[l]
    w_o_hy = wo[:HY_WIDTH].astype(BF16)
    w_o_att = jnp.pad(wo[HY_WIDTH:].reshape(ATT_HEADS, HEAD_DIM, d),
                      ((0, 0), (0, LANES - HEAD_DIM), (0, 0))).reshape(QPAD, d).astype(BF16)
    x1, hq = _outproj_ln(hy, att, x, w_o_hy, w_o_att, g1, sc2, sh2, ln1_g[l][None], ln1_b[l][None])

    hq2 = hq.reshape(b * n, d)
    a_idx, b_idx, gate = _peer_topk(hq2, peer_wq[l].astype(BF16), peer_keys1[l], peer_keys2[l])
    u_t = peer_u[l].astype(BF16).T
    out = _peer_dense(hq2, a_idx, b_idx, gate, u_t, peer_v[l].astype(BF16), x1.reshape(b * n, d), g2,
                      ln2_g[l][None], ln2_b[l][None], n)
    return out.reshape(b, n, d)
```

```python
import functools
import math

import jax
import jax.numpy as jnp
import numpy as np
from jax import lax
from jax.experimental import pallas as pl
from jax.experimental.pallas import tpu as pltpu

F32 = jnp.float32
BF16 = jnp.bfloat16
HIGHEST = lax.Precision.HIGHEST

LANES = 128
VMEM_LIMIT = 56 * 1024 * 1024

D_MODEL = 1024
GRID_W = 64
HY_WIDTH = 512
HY_ORDER = 2
HY_BANDS = 8
HY_DECAY_TARGET = 1e-2
HY_FAST_DECAY_PCT = 0.3
HY_SLOW_DECAY_PCT = 1.5
HY_WINDOW_SHIFT = 0.05
ATT_HEADS = 8
ATT_KV_HEADS = 2
ATT_REP = ATT_HEADS // ATT_KV_HEADS
HEAD_DIM = 64
WINDOW = 128
BLOCK = 128
ROPE_BASE = 10000.0
ROPE_FREQS = HEAD_DIM // 4
PROJ_HY = (HY_ORDER + 1) * HY_WIDTH
PROJ_Q = ATT_HEADS * HEAD_DIM
PROJ_KV = ATT_KV_HEADS * HEAD_DIM
KV_START = PROJ_HY + PROJ_Q
PEER_KEYS = 128
PEER_HEADS = 8
PEER_QDIM = 256
PEER_TOPK = 16
LN_EPS = 1e-5
NEG_INF = -1e30
DEPTH = 1
DEEPNORM_ALPHA = (2.0 * DEPTH) ** 0.25

QPAD = ATT_HEADS * LANES
KVPAD = ATT_KV_HEADS * LANES
COL_Q = 0
COL_HY = QPAD
COL_K = QPAD + PROJ_HY
COL_V = COL_K + KVPAD
PROJ_PAD = COL_V + KVPAD


def _cparams(sem):
    return pltpu.CompilerParams(dimension_semantics=sem, vmem_limit_bytes=VMEM_LIMIT)


def _mod_body(c_ref, w_ref, b_ref, o_ref):
    c = c_ref[...]
    a = c * jax.nn.sigmoid(c)
    o_ref[...] = jnp.dot(a, w_ref[...], preferred_element_type=F32, precision=HIGHEST) + b_ref[...]


def _modulation(cc, w_mod, b_mod):
    rows, d = cc.shape
    n_out = w_mod.shape[1]
    tn = 1536
    return pl.pallas_call(
        _mod_body,
        grid=(n_out // tn,),
        in_specs=[
            pl.BlockSpec((rows, d), lambda j: (0, 0)),
            pl.BlockSpec((d, tn), lambda j: (0, j)),
            pl.BlockSpec((1, tn), lambda j: (0, j)),
        ],
        out_specs=pl.BlockSpec((rows, tn), lambda j: (0, j)),
        out_shape=jax.ShapeDtypeStruct((rows, n_out), F32),
        compiler_params=_cparams(("arbitrary",)),
        name="modulation",
    )(cc, w_mod, b_mod)


def _mod_matmul_body(x_ref, sc_ref, sh_ref, w_ref, o_ref):
    h = x_ref[0] * (1.0 + sc_ref[0]) + sh_ref[0]
    o_ref[0] = jnp.dot(h.astype(BF16), w_ref[...], preferred_element_type=F32)


def _mod_matmul(x, sc, sh, w, tm, name):
    b, n, d = x.shape
    n_out = w.shape[1]
    per_batch = sc.shape[0] == b
    mod_map = (lambda bi, i: (bi, 0, 0)) if per_batch else (lambda bi, i: (0, 0, 0))
    return pl.pallas_call(
        _mod_matmul_body,
        grid=(b, n // tm),
        in_specs=[
            pl.BlockSpec((1, tm, d), lambda bi, i: (bi, i, 0)),
            pl.BlockSpec((1, 1, d), mod_map),
            pl.BlockSpec((1, 1, d), mod_map),
            pl.BlockSpec((d, n_out), lambda bi, i: (0, 0)),
        ],
        out_specs=pl.BlockSpec((1, tm, n_out), lambda bi, i: (bi, i, 0)),
        out_shape=jax.ShapeDtypeStruct((b, n, n_out), F32),
        compiler_params=_cparams(("parallel", "parallel")),
        name=name,
    )(x, sc, sh, w)


def _rope_head(x, cos, sin_signed):
    lane = lax.broadcasted_iota(jnp.int32, x.shape, 1)
    first_half = (lane % 32) < 16
    partner = jnp.where(first_half, pltpu.roll(x, LANES - 16, 1), pltpu.roll(x, 16, 1))
    return x * cos + partner * sin_signed


ATT_TQ = 512
ROPE_CHUNK = 512


def _attn_body(sink_ref, q_ref, k_ref, v_ref, kvc_ref, cosq_ref, sinq_ref, cosk_ref, sink_tab_ref,
               o_ref, ks_ref, vs_ref, kcs_ref, vcs_ref):
    n = k_ref.shape[1]
    iq = pl.program_id(1)
    scale = HEAD_DIM ** -0.5

    @pl.when(iq == 0)
    def _prepare_keys():
        def chunk(ci, carry):
            r0 = pl.multiple_of(ci * ROPE_CHUNK, ROPE_CHUNK)
            cos = cosk_ref[pl.ds(r0, ROPE_CHUNK), :]
            sin = sink_tab_ref[pl.ds(r0, ROPE_CHUNK), :]
            for g in range(ATT_KV_HEADS):
                kg = k_ref[0, pl.ds(r0, ROPE_CHUNK), g * LANES:(g + 1) * LANES]
                ks_ref[pl.ds(r0, ROPE_CHUNK), g * LANES:(g + 1) * LANES] = _rope_head(kg, cos, sin).astype(BF16)
            vs_ref[pl.ds(r0, ROPE_CHUNK), :] = v_ref[0, pl.ds(r0, ROPE_CHUNK), :].astype(BF16)
            return carry
        lax.fori_loop(0, n // ROPE_CHUNK, chunk, 0)
        kcs_ref[...] = kvc_ref[0, :, 0:KVPAD].astype(BF16)
        vcs_ref[...] = kvc_ref[0, :, KVPAD:2 * KVPAD].astype(BF16)

    n_loc = 3 * BLOCK
    rows = ATT_REP * BLOCK
    row_i = lax.broadcasted_iota(jnp.int32, (rows, n_loc), 0)
    col_i = lax.broadcasted_iota(jnp.int32, (rows, n_loc), 1)
    rel = col_i - (row_i % BLOCK)
    head_of_row = lax.broadcasted_iota(jnp.int32, (rows, 1), 0) // BLOCK

    for j in range(ATT_TQ // BLOCK):
        blk = iq * (ATT_TQ // BLOCK) + j
        start = pl.multiple_of(jnp.clip((blk - 1) * BLOCK, 0, n - n_loc), BLOCK)
        qrows = slice(j * BLOCK, (j + 1) * BLOCK)
        cosq = cosq_ref[qrows, :]
        sinq = sinq_ref[qrows, :]
        delta = rel + (start - blk * BLOCK)
        in_window = jnp.abs(delta) <= WINDOW
        for g in range(ATT_KV_HEADS):
            heads = [ATT_REP * g + r for r in range(ATT_REP)]
            qg = jnp.concatenate(
                [(_rope_head(q_ref[0, qrows, h * LANES:(h + 1) * LANES], cosq, sinq) * scale).astype(BF16)
                 for h in heads], axis=0)
            kg = ks_ref[pl.ds(start, n_loc), g * LANES:(g + 1) * LANES]
            vg = vs_ref[pl.ds(start, n_loc), g * LANES:(g + 1) * LANES]
            kcg = kcs_ref[:, g * LANES:(g + 1) * LANES]
            vcg = vcs_ref[:, g * LANES:(g + 1) * LANES]
            nt = (((1,), (1,)), ((), ()))
            s_loc = lax.dot_general(qg, kg, nt, preferred_element_type=F32)
            s_ctx = lax.dot_general(qg, kcg, nt, preferred_element_type=F32)
            s_loc = jnp.where(in_window, s_loc, NEG_INF)
            sink_col = jnp.zeros((rows, 1), F32)
            for r, h in enumerate(heads):
                sink_col = jnp.where(head_of_row == r, sink_ref[h], sink_col)
            m = jnp.maximum(jnp.maximum(jnp.max(s_loc, axis=1, keepdims=True),
                                        jnp.max(s_ctx, axis=1, keepdims=True)), sink_col)
            p_loc = jnp.exp(s_loc - m)
            p_ctx = jnp.exp(s_ctx - m)
            den = (jnp.sum(p_loc, axis=1, keepdims=True) + jnp.sum(p_ctx, axis=1, keepdims=True)
                   + jnp.exp(sink_col - m))
            o = (jnp.dot(p_loc.astype(BF16), vg, preferred_element_type=F32)
                 + jnp.dot(p_ctx.astype(BF16), vcg, preferred_element_type=F32)) / den
            for r, h in enumerate(heads):
                o_ref[0, qrows, h * LANES:(h + 1) * LANES] = o[r * BLOCK:(r + 1) * BLOCK].astype(BF16)


def _attention(p_lat, kv_ctx, sink, cos_tab, sin_tab):
    b, n, _ = p_lat.shape
    n_ctx = kv_ctx.shape[1]
    grid_spec = pltpu.PrefetchScalarGridSpec(
        num_scalar_prefetch=1,
        grid=(b, n // ATT_TQ),
        in_specs=[
            pl.BlockSpec((1, ATT_TQ, QPAD), lambda bi, i, s: (bi, i, COL_Q // QPAD)),
            pl.BlockSpec((1, n, KVPAD), lambda bi, i, s: (bi, 0, COL_K // KVPAD)),
            pl.BlockSpec((1, n, KVPAD), lambda bi, i, s: (bi, 0, COL_V // KVPAD)),
            pl.BlockSpec((1, n_ctx, 2 * KVPAD), lambda bi, i, s: (bi, 0, 0)),
            pl.BlockSpec((ATT_TQ, LANES), lambda bi, i, s: (i, 0)),
            pl.BlockSpec((ATT_TQ, LANES), lambda bi, i, s: (i, 0)),
            pl.BlockSpec((n, LANES), lambda bi, i, s: (0, 0)),
            pl.BlockSpec((n, LANES), lambda bi, i, s: (0, 0)),
        ],
        out_specs=pl.BlockSpec((1, ATT_TQ, QPAD), lambda bi, i, s: (bi, i, 0)),
        scratch_shapes=[
            pltpu.VMEM((n, KVPAD), BF16),
            pltpu.VMEM((n, KVPAD), BF16),
            pltpu.VMEM((n_ctx, KVPAD), BF16),
            pltpu.VMEM((n_ctx, KVPAD), BF16),
        ],
    )
    return pl.pallas_call(
        _attn_body,
        grid_spec=grid_spec,
        out_shape=jax.ShapeDtypeStruct((b, n, QPAD), BF16),
        compiler_params=_cparams(("parallel", "arbitrary")),
        name="window_attention",
    )(sink, p_lat, p_lat, p_lat, kv_ctx, cos_tab, sin_tab, cos_tab, sin_tab)


def _rope_tables(n):
    rows = n // GRID_W
    row = jnp.repeat(jnp.arange(rows, dtype=F32), GRID_W)
    col = jnp.tile(jnp.arange(GRID_W, dtype=F32), rows)
    inv = ROPE_BASE ** (-jnp.arange(ROPE_FREQS, dtype=F32) / ROPE_FREQS)
    ang_r = row[:, None] * inv
    ang_c = col[:, None] * inv
    pad1 = jnp.ones((n, LANES - HEAD_DIM), F32)
    pad0 = jnp.zeros((n, LANES - HEAD_DIM), F32)
    cos = jnp.concatenate([jnp.cos(ang_r), jnp.cos(ang_r), jnp.cos(ang_c), jnp.cos(ang_c), pad1], axis=1)
    sin = jnp.concatenate([-jnp.sin(ang_r), jnp.sin(ang_r), -jnp.sin(ang_c), jnp.sin(ang_c), pad0], axis=1)
    return cos, sin


def _layer_norm(r, g, b):
    mu = jnp.mean(r, axis=-1, keepdims=True)
    var = jnp.mean(jnp.square(r - mu), axis=-1, keepdims=True)
    return (r - mu) * lax.rsqrt(var + LN_EPS) * g + b


def _outproj_body(hy_ref, att_ref, x_ref, wh_ref, wa_ref, g1_ref, sc2_ref, sh2_ref, lg_ref, lb_ref,
                  x1_ref, hq_ref):
    y = (jnp.dot(hy_ref[0].astype(BF16), wh_ref[...], preferred_element_type=F32)
         + jnp.dot(att_ref[0], wa_ref[...], preferred_element_type=F32))
    x1 = _layer_norm(DEEPNORM_ALPHA * x_ref[0] + g1_ref[0] * y, lg_ref[...], lb_ref[...])
    x1_ref[0] = x1
    hq_ref[0] = (x1 * (1.0 + sc2_ref[0]) + sh2_ref[0]).astype(BF16)


def _outproj_ln(hy, att, x, w_hy, w_att, g1, sc2, sh2, ln_g, ln_b, tm=512):
    b, n, d = x.shape
    modspec = pl.BlockSpec((1, 1, d), lambda bi, i: (bi, 0, 0))
    vecspec = pl.BlockSpec((1, d), lambda bi, i: (0, 0))
    return pl.pallas_call(
        _outproj_body,
        grid=(b, n // tm),
        in_specs=[
            pl.BlockSpec((1, tm, HY_WIDTH), lambda bi, i: (bi, i, 0)),
            pl.BlockSpec((1, tm, QPAD), lambda bi, i: (bi, i, 0)),
            pl.BlockSpec((1, tm, d), lambda bi, i: (bi, i, 0)),
            pl.BlockSpec(w_hy.shape, lambda bi, i: (0, 0)),
            pl.BlockSpec(w_att.shape, lambda bi, i: (0, 0)),
            modspec, modspec, modspec, vecspec, vecspec,
        ],
        out_specs=[
            pl.BlockSpec((1, tm, d), lambda bi, i: (bi, i, 0)),
            pl.BlockSpec((1, tm, d), lambda bi, i: (bi, i, 0)),
        ],
        out_shape=[jax.ShapeDtypeStruct((b, n, d), F32), jax.ShapeDtypeStruct((b, n, d), BF16)],
        compiler_params=_cparams(("parallel", "parallel")),
        name="outproj_ln1",
    )(hy, att, x, w_hy, w_att, g1, sc2, sh2, ln_g, ln_b)


PEER_TM = 256
_STAIR = sorted(((i, j) for i in range(PEER_TOPK) for j in range(PEER_TOPK) if (i + 1) * (j + 1) <= PEER_TOPK),
                key=lambda p: p[0] * PEER_TOPK + p[1])
_STAIR_ROWS = -(-len(_STAIR) // 8) * 8


def _select_topk(s_ref, n_rows, write_row):
    tm = s_ref.shape[1]
    row_id = lax.broadcasted_iota(jnp.int32, (n_rows, tm), 0)

    def step(k, prev_idx):
        s = jnp.where(row_id == prev_idx, -jnp.inf, s_ref[...])
        s_ref[...] = s
        m = jnp.max(s, axis=0, keepdims=True)
        idx = jnp.min(jnp.where(s == m, row_id, n_rows), axis=0, keepdims=True)
        write_row(k, m, idx)
        return idx

    lax.fori_loop(0, PEER_TOPK, step, jnp.full((1, tm), -1, jnp.int32))


def _peer_topk_body(hq_ref, wq_ref, k1_ref, k2_ref, a_ref, b_ref, g_ref,
                    q_s, s_s, v1_s, i1_s, v2_s, i2_s, c_s, ca_s, cb_s, t_s, ao_s, bo_s, go_s):
    tm = hq_ref.shape[0]
    q = jnp.dot(hq_ref[...], wq_ref[...], preferred_element_type=F32)
    for c in range(2 * PEER_HEADS):
        q_s[c] = q[:, c * LANES:(c + 1) * LANES]
    nt = (((1,), (1,)), ((), ()))

    def head(h, carry):
        for half, (kref, vs, is_) in enumerate(((k1_ref, v1_s, i1_s), (k2_ref, v2_s, i2_s))):
            s_s[...] = lax.dot_general(kref[...], q_s[2 * h + half], nt, preferred_element_type=F32,
                                       precision=HIGHEST)

            def write(k, val, idx, vs=vs, is_=is_):
                vs[pl.ds(k, 1), :] = val
                is_[pl.ds(k, 1), :] = idx.astype(F32)
            _select_topk(s_s, PEER_KEYS, write)

        c_s[...] = jnp.full(c_s.shape, -jnp.inf, F32)
        ca_s[...] = jnp.zeros(ca_s.shape, F32)
        cb_s[...] = jnp.zeros(cb_s.shape, F32)
        for r, (i, j) in enumerate(_STAIR):
            c_s[r:r + 1, :] = v1_s[i:i + 1, :] + v2_s[j:j + 1, :]
            ca_s[r:r + 1, :] = i1_s[i:i + 1, :]
            cb_s[r:r + 1, :] = i2_s[j:j + 1, :]

        row_id = lax.broadcasted_iota(jnp.int32, (_STAIR_ROWS, tm), 0)

        def write2(k, val, idx):
            hit = row_id == idx
            t_s[pl.ds(k, 1), :] = val
            ao_s[pl.ds(h * PEER_TOPK + k, 1), :] = jnp.max(jnp.where(hit, ca_s[...], -1.0), axis=0, keepdims=True)
            bo_s[pl.ds(h * PEER_TOPK + k, 1), :] = jnp.max(jnp.where(hit, cb_s[...], -1.0), axis=0, keepdims=True)
        _select_topk(c_s, _STAIR_ROWS, write2)

        t = t_s[...]
        e = jnp.exp(t - jnp.max(t, axis=0, keepdims=True))
        go_s[pl.ds(pl.multiple_of(h * PEER_TOPK, PEER_TOPK), PEER_TOPK), :] = e / jnp.sum(e, axis=0, keepdims=True)
        return carry

    lax.fori_loop(0, PEER_HEADS, head, 0)
    a_ref[...] = ao_s[...].T
    b_ref[...] = bo_s[...].T
    g_ref[...] = go_s[...].T


def _peer_topk(hq, wq, keys1, keys2):
    t, d = hq.shape
    tm = PEER_TM
    hk = PEER_HEADS * PEER_TOPK
    out = jax.ShapeDtypeStruct((t, hk), F32)
    ospec = pl.BlockSpec((tm, hk), lambda i: (i, 0))
    return pl.pallas_call(
        _peer_topk_body,
        grid=(t // tm,),
        in_specs=[
            pl.BlockSpec((tm, d), lambda i: (i, 0)),
            pl.BlockSpec(wq.shape, lambda i: (0, 0)),
            pl.BlockSpec(keys1.shape, lambda i: (0, 0)),
            pl.BlockSpec(keys2.shape, lambda i: (0, 0)),
        ],
        out_specs=[ospec, ospec, ospec],
        out_shape=[out, out, out],
        scratch_shapes=[
            pltpu.VMEM((2 * PEER_HEADS, tm, LANES), F32),
            pltpu.VMEM((PEER_KEYS, tm), F32),
            pltpu.VMEM((PEER_TOPK, tm), F32), pltpu.VMEM((PEER_TOPK, tm), F32),
            pltpu.VMEM((PEER_TOPK, tm), F32), pltpu.VMEM((PEER_TOPK, tm), F32),
            pltpu.VMEM((_STAIR_ROWS, tm), F32), pltpu.VMEM((_STAIR_ROWS, tm), F32), pltpu.VMEM((_STAIR_ROWS, tm), F32),
            pltpu.VMEM((PEER_TOPK, tm), F32),
            pltpu.VMEM((hk, tm), F32), pltpu.VMEM((hk, tm), F32), pltpu.VMEM((hk, tm), F32),
        ],
        compiler_params=_cparams(("parallel",)),
        name="peer_topk",
    )(hq, wq, keys1, keys2)


PEER_TE = 2048
PEER_GROUPS = PEER_TE // PEER_KEYS
GATE_PITCH = PEER_KEYS + 8


def _peer_dense_body(hq_ref, a_ref, b_ref, g_ref, ut_ref, v_ref, x1_ref, g2_ref, lg_ref, lb_ref, o_ref,
                     ssel_s, w_s, hd_s, acc_s):
    tm = hq_ref.shape[0]
    n_e = (PEER_KEYS * PEER_KEYS) // PEER_TE
    e = pl.program_id(1)
    nt = (((1,), (1,)), ((), ()))

    @pl.when(e == 0)
    def _init():
        ssel_s[...] = jnp.zeros(ssel_s.shape, F32)

    @pl.when(e < n_e)
    def _scores():
        s = jnp.dot(hq_ref[...], ut_ref[...], preferred_element_type=F32)
        b_idx = b_ref[...].astype(jnp.int32)
        a_val = a_ref[...]
        cur = ssel_s[...]
        for jj in range(PEER_GROUPS):
            cand = jnp.take_along_axis(s[:, jj * LANES:(jj + 1) * LANES], b_idx, axis=1)
            cur = jnp.where(a_val == (e * PEER_GROUPS + jj).astype(F32), cand, cur)
        ssel_s[...] = cur

    @pl.when(e == n_e - 1)
    def _gates():
        s_sel = ssel_s[...]
        act = 0.5 * s_sel * (1.0 + lax.erf(s_sel * (2.0 ** -0.5)))
        w_s[...] = g_ref[...] * act
        sub = lax.broadcasted_iota(jnp.int32, (PEER_KEYS, LANES), 0).astype(F32)

        def token(t, carry):
            a_row = a_ref[pl.ds(t, 1), :]
            b_row = b_ref[pl.ds(t, 1), :]
            w_row = w_s[pl.ds(t, 1), :]
            lhs = jnp.where(sub == a_row, w_row, 0.0).astype(BF16)
            rhs = jnp.where(sub == b_row, 1.0, 0.0).astype(BF16)
            tile = lax.dot_general(lhs, rhs, nt, preferred_element_type=F32)
            hd_s[pl.ds(pl.multiple_of(t * GATE_PITCH, 8), PEER_KEYS), :] = tile
            return carry
        lax.fori_loop(0, tm, token, 0)
        acc_s[...] = jnp.zeros(acc_s.shape, F32)

    @pl.when(e >= n_e)
    def _values():
        j0 = (e - n_e) * PEER_GROUPS
        lhs = jnp.concatenate(
            [hd_s[pl.ds(j0 + jj, tm, stride=GATE_PITCH), :].astype(BF16) for jj in range(PEER_GROUPS)], axis=1)
        acc_s[...] += jnp.dot(lhs, v_ref[...], preferred_element_type=F32)

    @pl.when(e == 2 * n_e - 1)
    def _finish():
        o_ref[...] = _layer_norm(DEEPNORM_ALPHA * x1_ref[...] + g2_ref[0] * acc_s[...], lg_ref[...], lb_ref[...])


def _peer_dense(hq, a, b, g, u_t, v_tab, x1, g2, ln_g, ln_b, tokens_per_batch):
    t, d = hq.shape
    tm = PEER_TM
    hk = a.shape[1]
    n_e = u_t.shape[1] // PEER_TE
    tiles_per_batch = tokens_per_batch // tm
    tok = lambda i, e: (i, 0)
    return pl.pallas_call(
        _peer_dense_body,
        grid=(t // tm, 2 * n_e),
        in_specs=[
            pl.BlockSpec((tm, d), tok),
            pl.BlockSpec((tm, hk), tok), pl.BlockSpec((tm, hk), tok), pl.BlockSpec((tm, hk), tok),
            pl.BlockSpec((d, PEER_TE), lambda i, e: (0, jnp.minimum(e, n_e - 1))),
            pl.BlockSpec((PEER_TE, d), lambda i, e: (jnp.maximum(e - n_e, 0), 0)),
            pl.BlockSpec((tm, d), tok),
            pl.BlockSpec((1, 1, d), lambda i, e: (i // tiles_per_batch, 0, 0)),
            pl.BlockSpec((1, d), lambda i, e: (0, 0)),
            pl.BlockSpec((1, d), lambda i, e: (0, 0)),
        ],
        out_specs=pl.BlockSpec((tm, d), tok),
        out_shape=jax.ShapeDtypeStruct((t, d), F32),
        scratch_shapes=[
            pltpu.VMEM((tm, hk), F32),
            pltpu.VMEM((tm, hk), F32),
            pltpu.VMEM((tm * GATE_PITCH, LANES), F32),
            pltpu.VMEM((tm, d), F32),
        ],
        compiler_params=_cparams(("parallel", "arbitrary")),
        name="peer_dense",
    )(hq, a, b, g, u_t, v_tab, x1, g2, ln_g, ln_b)


FFT_N2 = 128
FFT_PITCH = FFT_N2 + 8
HY_POS_PAD = 32


def _dft_constants(n):
    big = 2 * n
    n1c = big // FFT_N2
    half = n1c // 2
    k1 = np.arange(n1c)[:, None]
    n1 = np.arange(half)[None, :]
    ang = 2 * np.pi * k1 * n1 / n1c
    c, s = np.cos(ang), np.sin(ang)
    f1 = np.block([[c, s], [-s, c]])
    k2 = np.arange(FFT_N2)[:, None]
    n2 = np.arange(FFT_N2)[None, :]
    ang = 2 * np.pi * k2 * n2 / FFT_N2
    c, s = np.cos(ang), np.sin(ang)
    d3 = np.block([[c, s], [-s, c]])
    d3i = np.block([[c, -s], [s, c]])
    ang = 2 * np.pi * n1.T * k1.T / n1c
    c, s = np.cos(ang), np.sin(ang)
    f3 = np.block([[c, -s], [s, c]]) / big
    ang = 2 * np.pi * (np.arange(n1c)[:, None] * np.arange(FFT_N2)[None, :]) / big
    twr = np.repeat(np.cos(ang).reshape(-1, 1), LANES, axis=1)
    twi = np.repeat(-np.sin(ang).reshape(-1, 1), LANES, axis=1)
    as32 = lambda a: np.asarray(a, np.float32)
    return as32(f1), as32(d3), as32(d3i), as32(f3), as32(twr), as32(twi)


def _hdot(a, b):
    return jnp.dot(a, b, preferred_element_type=F32, precision=HIGHEST)


def _fft_stage1(u_re, u_im, f1_ref, a_re, a_im):
    half = f1_ref.shape[1] // 2
    n1c = f1_ref.shape[0] // 2

    def body(n2, carry):
        x = jnp.concatenate([u_re[pl.ds(n2, half, stride=FFT_PITCH), :],
                             u_im[pl.ds(n2, half, stride=FFT_PITCH), :]], axis=0)
        a = _hdot(f1_ref[...], x)
        a_re[pl.ds(n2, n1c, stride=FFT_PITCH), :] = a[:n1c]
        a_im[pl.ds(n2, n1c, stride=FFT_PITCH), :] = a[n1c:]
        return carry
    lax.fori_loop(0, FFT_N2, body, 0)


def _twiddled_block(a_re, a_im, twr_ref, twi_ref, k1):
    r0 = pl.multiple_of(k1 * FFT_PITCH, 8)
    t0 = pl.multiple_of(k1 * FFT_N2, FFT_N2)
    ar = a_re[pl.ds(r0, FFT_N2), :]
    ai = a_im[pl.ds(r0, FFT_N2), :]
    twr = twr_ref[pl.ds(t0, FFT_N2), :]
    twi = twi_ref[pl.ds(t0, FFT_N2), :]
    return r0, ar, ai, twr, twi


def _fft_conv_middle(a_re, a_im, d3_ref, d3i_ref, twr_ref, twi_ref, hre_ref, him_ref):
    n1c = hre_ref.shape[1]

    def body(k1, carry):
        r0, ar, ai, twr, twi = _twiddled_block(a_re, a_im, twr_ref, twi_ref, k1)
        x = _hdot(d3_ref[...], jnp.concatenate([ar * twr - ai * twi, ar * twi + ai * twr], axis=0))
        xr, xi = x[:FFT_N2], x[FFT_N2:]
        hr = hre_ref[0, k1]
        hi = him_ref[0, k1]
        y = _hdot(d3i_ref[...], jnp.concatenate([xr * hr - xi * hi, xr * hi + xi * hr], axis=0))
        br, bi = y[:FFT_N2], y[FFT_N2:]
        a_re[pl.ds(r0, FFT_N2), :] = br * twr + bi * twi
        a_im[pl.ds(r0, FFT_N2), :] = bi * twr - br * twi
        return carry
    lax.fori_loop(0, n1c, body, 0)


def _fft_stage_inv(a_re, a_im, f3_ref, y_re, y_im):
    half = f3_ref.shape[0] // 2
    n1c = f3_ref.shape[1] // 2

    def body(n2, carry):
        b = jnp.concatenate([a_re[pl.ds(n2, n1c, stride=FFT_PITCH), :],
                             a_im[pl.ds(n2, n1c, stride=FFT_PITCH), :]], axis=0)
        y = _hdot(f3_ref[...], b)
        y_re[pl.ds(n2, half, stride=FFT_PITCH), :] = y[:half]
        y_im[pl.ds(n2, half, stride=FFT_PITCH), :] = y[half:]
        return carry
    lax.fori_loop(0, FFT_N2, body, 0)


def _short_conv_chunk(x_ref, bi, j, nblk, w_ref, b_ref):
    r0 = j * FFT_N2
    cur = x_ref[bi, r0:r0 + FFT_N2, :]
    row = lax.broadcasted_iota(jnp.int32, cur.shape, 0)
    if j == 0:
        prev = jnp.where(row == 0, 0.0, pltpu.roll(cur, 1, 0))
    else:
        prev = x_ref[bi, r0 - 1:r0 + FFT_N2 - 1, :]
    if j == nblk - 1:
        nxt = jnp.where(row == FFT_N2 - 1, 0.0, pltpu.roll(cur, FFT_N2 - 1, 0))
    else:
        nxt = x_ref[bi, r0 + 1:r0 + FFT_N2 + 1, :]
    return prev * w_ref[0:1, :] + cur * w_ref[1:2, :] + nxt * w_ref[2:3, :] + b_ref[...]


def _hy_conv_body(conv_a, a_ref, g_ref, wa_ref, ba_ref, wg_ref, bg_ref, skip_ref, hre_ref, him_ref,
                  f1_ref, d3_ref, d3i_ref, f3_ref, twr_ref, twi_ref, o_ref,
                  u_re, u_im, a_re, a_im, y_re, y_im):
    nblk = a_ref.shape[1] // FFT_N2
    for bi, dst in ((0, u_re), (1, u_im)):
        for j in range(nblk):
            if conv_a:
                blk = _short_conv_chunk(a_ref, bi, j, nblk, wa_ref, ba_ref)
            else:
                blk = a_ref[bi, j * FFT_N2:(j + 1) * FFT_N2, :]
            dst[j * FFT_PITCH:j * FFT_PITCH + FFT_N2, :] = blk
    _fft_stage1(u_re, u_im, f1_ref, a_re, a_im)
    _fft_conv_middle(a_re, a_im, d3_ref, d3i_ref, twr_ref, twi_ref, hre_ref, him_ref)
    _fft_stage_inv(a_re, a_im, f3_ref, y_re, y_im)
    skip = skip_ref[...]
    for bi, (ysrc, usrc) in enumerate(((y_re, u_re), (y_im, u_im))):
        for j in range(nblk):
            rows = slice(j * FFT_PITCH, j * FFT_PITCH + FFT_N2)
            gate = _short_conv_chunk(g_ref, bi, j, nblk, wg_ref, bg_ref)
            o_ref[bi, j * FFT_N2:(j + 1) * FFT_N2, :] = gate * (ysrc[rows, :] + usrc[rows, :] * skip)


def _hy_conv(a, a_col, g, g_col, conv_w, conv_b, skip, hre, him, order, consts, conv_a):
    b, n, _ = a.shape
    f1, d3, d3i, f3, twr, twi = consts
    n1c = f1.shape[0] // 2
    half = n1c // 2
    w = LANES
    tiles = HY_WIDTH // w
    one = pl.Buffered(1)
    data = lambda col: pl.BlockSpec((2, n, w), lambda ct, p: (p, 0, col + ct), pipeline_mode=one)
    wspec = lambda col: pl.BlockSpec((3, w), lambda ct, p: (0, col + ct))
    bspec = lambda col: pl.BlockSpec((1, w), lambda ct, p: (0, col + ct))
    hspec = pl.BlockSpec((1, n1c, FFT_N2, w), lambda ct, p: (order, 0, 0, ct), pipeline_mode=one)
    cs = lambda arr: pl.BlockSpec(arr.shape, lambda ct, p: (0, 0), pipeline_mode=one)
    a_wcol = a_col if conv_a else g_col
    return pl.pallas_call(
        functools.partial(_hy_conv_body, conv_a),
        grid=(tiles, b // 2),
        in_specs=[data(a_col), data(g_col), wspec(a_wcol), bspec(a_wcol), wspec(g_col), bspec(g_col),
                  pl.BlockSpec((1, w), lambda ct, p: (0, ct)), hspec, hspec,
                  cs(f1), cs(d3), cs(d3i), cs(f3), cs(twr), cs(twi)],
        out_specs=pl.BlockSpec((2, n, w), lambda ct, p: (p, 0, ct), pipeline_mode=one),
        out_shape=jax.ShapeDtypeStruct((b, n, HY_WIDTH), F32),
        scratch_shapes=[pltpu.VMEM((half * FFT_PITCH, w), F32), pltpu.VMEM((half * FFT_PITCH, w), F32),
                        pltpu.VMEM((n1c * FFT_PITCH, w), F32), pltpu.VMEM((n1c * FFT_PITCH, w), F32),
                        pltpu.VMEM((half * FFT_PITCH, w), F32), pltpu.VMEM((half * FFT_PITCH, w), F32)],
        compiler_params=_cparams(("parallel", "arbitrary")),
        name=f"hyena_conv{order + 1}",
    )(a, g, conv_w, conv_b, conv_w, conv_b, skip, hre, him, f1, d3, d3i, f3, twr, twi)


def _hy_filter_body(z_ref, w1_ref, b1_ref, f1_ref, w2_ref, b2_ref, f2_ref, w3_ref, b3_ref, dl_ref, o_ref):
    z = z_ref[...]
    h = jnp.sin(f1_ref[...] * (_hdot(z, w1_ref[...]) + b1_ref[...]))
    h = jnp.sin(f2_ref[...] * (_hdot(h, w2_ref[...]) + b2_ref[...]))
    h = _hdot(h, w3_ref[...]) + b3_ref[...]
    t = z[:, 0:1]
    o_ref[...] = h * (jnp.exp(-t * dl_ref[...]) + HY_WINDOW_SHIFT)


def _hy_filters(n, w1, b1, fr1, w2, b2, fr2, w3, b3):
    t = jnp.linspace(0.0, 1.0, n, dtype=F32)[:, None]
    wv = 2.0 * math.pi * jnp.arange(n, dtype=F32)[:, None] / n
    bands = jnp.linspace(1e-4, HY_BANDS - 1, HY_BANDS, dtype=F32)[None, :]
    z = jnp.concatenate([t, jnp.cos(bands * wv), -jnp.sin(bands * wv)], axis=-1)
    pos = z.shape[1]
    z = jnp.pad(z, ((0, 0), (0, HY_POS_PAD - pos)))
    w1p = jnp.pad(w1, ((0, HY_POS_PAD - pos), (0, 0)))
    min_decay = math.log(HY_DECAY_TARGET) / HY_SLOW_DECAY_PCT
    max_decay = math.log(HY_DECAY_TARGET) / HY_FAST_DECAY_PCT
    deltas = jnp.abs(jnp.linspace(min_decay, max_decay, HY_WIDTH, dtype=F32))
    n_out = w3.shape[1]
    dl = jnp.tile(deltas, n_out // HY_WIDTH)[None, :]
    tn = 512
    hid = w2.shape[0]
    full = lambda shape: pl.BlockSpec(shape, lambda i: (0, 0))
    return pl.pallas_call(
        _hy_filter_body,
        grid=(n // tn,),
        in_specs=[pl.BlockSpec((tn, HY_POS_PAD), lambda i: (i, 0)), full((HY_POS_PAD, hid)), full((1, hid)),
                  full((1, hid)), full((hid, hid)), full((1, hid)), full((1, hid)), full((hid, n_out)),
                  full((1, n_out)), full((1, n_out))],
        out_specs=pl.BlockSpec((tn, n_out), lambda i: (i, 0)),
        out_shape=jax.ShapeDtypeStruct((n, n_out), F32),
        compiler_params=_cparams(("parallel",)),
        name="hyena_filters",
    )(z, w1p, b1[None], fr1[None], w2, b2[None], fr2[None], w3, b3[None], dl)


def _hy_spectrum_body(f_ref, b_ref, f1_ref, d3_ref, twr_ref, twi_ref, hre_ref, him_ref, u_re, u_im, a_re, a_im):
    nblk = f_ref.shape[0] // FFT_N2
    n1c = hre_ref.shape[1]
    for part, out_ref in enumerate((hre_ref, him_ref)):
        for j in range(nblk):
            f = f_ref[j * FFT_N2:(j + 1) * FFT_N2, :]
            bw = b_ref[j * FFT_N2:(j + 1) * FFT_N2, :]
            if j == 0:
                row = lax.broadcasted_iota(jnp.int32, bw.shape, 0)
                bw = jnp.where(row == 0, 0.0, bw)
            rows = slice(j * FFT_PITCH, j * FFT_PITCH + FFT_N2)
            u_re[rows, :] = f + bw if part == 0 else f - bw
            u_im[rows, :] = jnp.zeros((FFT_N2, LANES), F32)
        _fft_stage1(u_re, u_im, f1_ref, a_re, a_im)

        def body(k1, carry, part=part, out_ref=out_ref):
            _, ar, ai, twr, twi = _twiddled_block(a_re, a_im, twr_ref, twi_ref, k1)
            x = _hdot(d3_ref[...], jnp.concatenate([ar * twr - ai * twi, ar * twi + ai * twr], axis=0))
            out_ref[0, k1] = x[:FFT_N2] if part == 0 else x[FFT_N2:]
            return carry
        lax.fori_loop(0, n1c, body, 0)


def _hy_spectrum(h, consts):
    n = h.shape[0]
    f1, d3, _, _, twr, twi = consts
    n1c = f1.shape[0] // 2
    half = n1c // 2
    tiles = HY_WIDTH // LANES
    cs = lambda arr: pl.BlockSpec(arr.shape, lambda o, ct: (0, 0))
    out = jax.ShapeDtypeStruct((HY_ORDER, n1c, FFT_N2, HY_WIDTH), F32)
    ospec = pl.BlockSpec((1, n1c, FFT_N2, LANES), lambda o, ct: (o, 0, 0, ct))
    return pl.pallas_call(
        _hy_spectrum_body,
        grid=(HY_ORDER, tiles),
        in_specs=[pl.BlockSpec((n, LANES), lambda o, ct: (0, o * tiles + ct)),
                  pl.BlockSpec((n, LANES), lambda o, ct: (0, (HY_ORDER + o) * tiles + ct)),
                  cs(f1), cs(d3), cs(twr), cs(twi)],
        out_specs=[ospec, ospec],
        out_shape=[out, out],
        scratch_shapes=[pltpu.VMEM((half * FFT_PITCH, LANES), F32), pltpu.VMEM((half * FFT_PITCH, LANES), F32),
                        pltpu.VMEM((n1c * FFT_PITCH, LANES), F32), pltpu.VMEM((n1c * FFT_PITCH, LANES), F32)],
        compiler_params=_cparams(("parallel", "parallel")),
        name="hyena_spectrum",
    )(h, h, f1, d3, twr, twi)


def _hyena(p_lat, conv_w, conv_b, w1, b1, fr1, w2, b2, fr2, w3, b3, skip):
    n = p_lat.shape[1]
    consts = _dft_constants(n)
    h = _hy_filters(n, w1, b1, fr1, w2, b2, fr2, w3, b3)
    hre, him = _hy_spectrum(h, consts)
    tiles = HY_WIDTH // LANES
    col = COL_HY // LANES
    cw = jnp.pad(conv_w, ((0, 0), (COL_HY, 0)))
    cb = jnp.pad(conv_b[None], ((0, 0), (COL_HY, 0)))
    y = _hy_conv(p_lat, col, p_lat, col + tiles, cw, cb, skip[0][None], hre, him, 0, consts, True)
    return _hy_conv(y, 0, p_lat, col + 2 * tiles, cw, cb, skip[1][None], hre, him, 1, consts, False)


def _pad_heads(w, heads):
    d = w.shape[0]
    w = w.reshape(d, heads, HEAD_DIM)
    w = jnp.pad(w, ((0, 0), (0, 0), (0, LANES - HEAD_DIM)))
    return w.reshape(d, heads * LANES)


def kernel(x, c, ctx, c_ctx, w_mod, b_mod, w_in, hy_conv_w, hy_conv_b, hy_f_w1, hy_f_b1, hy_f_freq1, hy_f_w2,
           hy_f_b2, hy_f_freq2, hy_f_w3, hy_f_b3, hy_skip, attn_sink, w_out, ln1_g, ln1_b, peer_wq, peer_keys1,
           peer_keys2, peer_u, peer_v, ln2_g, ln2_b):
    b, n, d = x.shape
    l = 0
    cc = jnp.concatenate([c, c_ctx[None], jnp.zeros((8 - b - 1, d), F32)], axis=0)
    mod = _modulation(cc, w_mod[l], b_mod[l][None])
    mod_lat = mod[:b].reshape(b, 6, 1, d)
    sh1, sc1, g1, sh2, sc2, g2 = (mod_lat[:, i] for i in range(6))
    mod_c = mod[b].reshape(6, 1, 1, d)
    csh1, csc1 = mod_c[0], mod_c[1]

    w = w_in[l]
    w_q = _pad_heads(w[:, PROJ_HY:KV_START], ATT_HEADS)
    w_k = _pad_heads(w[:, KV_START:KV_START + PROJ_KV], ATT_KV_HEADS)
    w_v = _pad_heads(w[:, KV_START + PROJ_KV:], ATT_KV_HEADS)
    w_pad = jnp.concatenate([w_q, w[:, :PROJ_HY], w_k, w_v], axis=1).astype(BF16)
    w_kv = jnp.concatenate([w_k, w_v], axis=1).astype(BF16)

    p_lat = _mod_matmul(x, sc1, sh1, w_pad, 512, "in_proj")
    kv_ctx = _mod_matmul(ctx, csc1, csh1, w_kv, ctx.shape[1], "ctx_kv_proj")

    cos_tab, sin_tab = _rope_tables(n)
    att = _attention(p_lat, kv_ctx, attn_sink[l], cos_tab, sin_tab)

    hy = _hyena(p_lat, hy_conv_w[l], hy_conv_b[l], hy_f_w1[l], hy_f_b1[l], hy_f_freq1[l], hy_f_w2[l], hy_f_b2[l],
                hy_f_freq2[l], hy_f_w3[l], hy_f_b3[l], hy_skip[l])

    wo = w_out[l]
    w_o_hy = wo[:HY_WIDTH].astype(BF16)
    w_o_att = jnp.pad(wo[HY_WIDTH:].reshape(ATT_HEADS, HEAD_DIM, d),
                      ((0, 0), (0, LANES - HEAD_DIM), (0, 0))).reshape(QPAD, d).astype(BF16)
    x1, hq = _outproj_ln(hy, att, x, w_o_hy, w_o_att, g1, sc2, sh2, ln1_g[l][None], ln1_b[l][None])

    hq2 = hq.reshape(b * n, d)
    a_idx, b_idx, gate = _peer_topk(hq2, peer_wq[l].astype(BF16), peer_keys1[l], peer_keys2[l])
    u_t = peer_u[l].astype(BF16).T
    out = _peer_dense(hq2, a_idx, b_idx, gate, u_t, peer_v[l].astype(BF16), x1.reshape(b * n, d), g2,
                      ln2_g[l][None], ln2_b[l][None], n)
    return out.reshape(b, n, d)
```

```python
import functools
import math

import jax
import jax.numpy as jnp
import numpy as np
from jax import lax
from jax.experimental import pallas as pl
from jax.experimental.pallas import tpu as pltpu

F32 = jnp.float32
BF16 = jnp.bfloat16
HIGHEST = lax.Precision.HIGHEST

LANES = 128
VMEM_LIMIT = 56 * 1024 * 1024

D_MODEL = 1024
GRID_W = 64
HY_WIDTH = 512
HY_ORDER = 2
HY_BANDS = 8
HY_DECAY_TARGET = 1e-2
HY_FAST_DECAY_PCT = 0.3
HY_SLOW_DECAY_PCT = 1.5
HY_WINDOW_SHIFT = 0.05
ATT_HEADS = 8
ATT_KV_HEADS = 2
ATT_REP = ATT_HEADS // ATT_KV_HEADS
HEAD_DIM = 64
WINDOW = 128
BLOCK = 128
ROPE_BASE = 10000.0
ROPE_FREQS = HEAD_DIM // 4
PROJ_HY = (HY_ORDER + 1) * HY_WIDTH
PROJ_Q = ATT_HEADS * HEAD_DIM
PROJ_KV = ATT_KV_HEADS * HEAD_DIM
KV_START = PROJ_HY + PROJ_Q
PEER_KEYS = 128
PEER_HEADS = 8
PEER_QDIM = 256
PEER_TOPK = 16
LN_EPS = 1e-5
NEG_INF = -1e30
DEPTH = 1
DEEPNORM_ALPHA = (2.0 * DEPTH) ** 0.25

QPAD = ATT_HEADS * LANES
KVPAD = ATT_KV_HEADS * LANES
COL_Q = 0
COL_HY = QPAD
COL_K = QPAD + PROJ_HY
COL_V = COL_K + KVPAD
PROJ_PAD = COL_V + KVPAD


def _cparams(sem):
    return pltpu.CompilerParams(dimension_semantics=sem, vmem_limit_bytes=VMEM_LIMIT)


def _mod_body(c_ref, w_ref, b_ref, o_ref):
    c = c_ref[...]
    a = c * jax.nn.sigmoid(c)
    o_ref[...] = jnp.dot(a, w_ref[...], preferred_element_type=F32, precision=HIGHEST) + b_ref[...]


def _modulation(cc, w_mod, b_mod):
    rows, d = cc.shape
    n_out = w_mod.shape[1]
    tn = 1536
    return pl.pallas_call(
        _mod_body,
        grid=(n_out // tn,),
        in_specs=[
            pl.BlockSpec((rows, d), lambda j: (0, 0)),
            pl.BlockSpec((d, tn), lambda j: (0, j)),
            pl.BlockSpec((1, tn), lambda j: (0, j)),
        ],
        out_specs=pl.BlockSpec((rows, tn), lambda j: (0, j)),
        out_shape=jax.ShapeDtypeStruct((rows, n_out), F32),
        compiler_params=_cparams(("arbitrary",)),
        name="modulation",
    )(cc, w_mod, b_mod)


def _mod_matmul_body(x_ref, sc_ref, sh_ref, w_ref, o_ref):
    h = x_ref[0] * (1.0 + sc_ref[0]) + sh_ref[0]
    o_ref[0] = jnp.dot(h.astype(BF16), w_ref[...], preferred_element_type=F32)


def _mod_matmul(x, sc, sh, w, tm, name):
    b, n, d = x.shape
    n_out = w.shape[1]
    per_batch = sc.shape[0] == b
    mod_map = (lambda bi, i: (bi, 0, 0)) if per_batch else (lambda bi, i: (0, 0, 0))
    return pl.pallas_call(
        _mod_matmul_body,
        grid=(b, n // tm),
        in_specs=[
            pl.BlockSpec((1, tm, d), lambda bi, i: (bi, i, 0)),
            pl.BlockSpec((1, 1, d), mod_map),
            pl.BlockSpec((1, 1, d), mod_map),
            pl.BlockSpec((d, n_out), lambda bi, i: (0, 0)),
        ],
        out_specs=pl.BlockSpec((1, tm, n_out), lambda bi, i: (bi, i, 0)),
        out_shape=jax.ShapeDtypeStruct((b, n, n_out), F32),
        compiler_params=_cparams(("parallel", "parallel")),
        name=name,
    )(x, sc, sh, w)


def _rope_head(x, cos, sin_signed):
    lane = lax.broadcasted_iota(jnp.int32, x.shape, 1)
    first_half = (lane % 32) < 16
    partner = jnp.where(first_half, pltpu.roll(x, LANES - 16, 1), pltpu.roll(x, 16, 1))
    return x * cos + partner * sin_signed


ATT_TQ = 512
ROPE_CHUNK = 512


def _attn_body(sink_ref, q_ref, k_ref, v_ref, kvc_ref, cosq_ref, sinq_ref, cosk_ref, sink_tab_ref,
               o_ref, ks_ref, vs_ref, kcs_ref, vcs_ref):
    n = k_ref.shape[1]
    iq = pl.program_id(1)
    scale = HEAD_DIM ** -0.5

    @pl.when(iq == 0)
    def _prepare_keys():
        def chunk(ci, carry):
            r0 = pl.multiple_of(ci * ROPE_CHUNK, ROPE_CHUNK)
            cos = cosk_ref[pl.ds(r0, ROPE_CHUNK), :]
            sin = sink_tab_ref[pl.ds(r0, ROPE_CHUNK), :]
            for g in range(ATT_KV_HEADS):
                kg = k_ref[0, pl.ds(r0, ROPE_CHUNK), g * LANES:(g + 1) * LANES]
                ks_ref[pl.ds(r0, ROPE_CHUNK), g * LANES:(g + 1) * LANES] = _rope_head(kg, cos, sin).astype(BF16)
            vs_ref[pl.ds(r0, ROPE_CHUNK), :] = v_ref[0, pl.ds(r0, ROPE_CHUNK), :].astype(BF16)
            return carry
        lax.fori_loop(0, n // ROPE_CHUNK, chunk, 0)
        kcs_ref[...] = kvc_ref[0, :, 0:KVPAD].astype(BF16)
        vcs_ref[...] = kvc_ref[0, :, KVPAD:2 * KVPAD].astype(BF16)

    n_loc = 3 * BLOCK
    rows = ATT_REP * BLOCK
    row_i = lax.broadcasted_iota(jnp.int32, (rows, n_loc), 0)
    col_i = lax.broadcasted_iota(jnp.int32, (rows, n_loc), 1)
    rel = col_i - (row_i % BLOCK)
    head_of_row = lax.broadcasted_iota(jnp.int32, (rows, 1), 0) // BLOCK

    for j in range(ATT_TQ // BLOCK):
        blk = iq * (ATT_TQ // BLOCK) + j
        start = pl.multiple_of(jnp.clip((blk - 1) * BLOCK, 0, n - n_loc), BLOCK)
        qrows = slice(j * BLOCK, (j + 1) * BLOCK)
        cosq = cosq_ref[qrows, :]
        sinq = sinq_ref[qrows, :]
        delta = rel + (start - blk * BLOCK)
        in_window = jnp.abs(delta) <= WINDOW
        for g in range(ATT_KV_HEADS):
            heads = [ATT_REP * g + r for r in range(ATT_REP)]
            qg = jnp.concatenate(
                [(_rope_head(q_ref[0, qrows, h * LANES:(h + 1) * LANES], cosq, sinq) * scale).astype(BF16)
                 for h in heads], axis=0)
            kg = ks_ref[pl.ds(start, n_loc), g * LANES:(g + 1) * LANES]
            vg = vs_ref[pl.ds(start, n_loc), g * LANES:(g + 1) * LANES]
            kcg = kcs_ref[:, g * LANES:(g + 1) * LANES]
            vcg = vcs_ref[:, g * LANES:(g + 1) * LANES]
            nt = (((1,), (1,)), ((), ()))
            s_loc = lax.dot_general(qg, kg, nt, preferred_element_type=F32)
            s_ctx = lax.dot_general(qg, kcg, nt, preferred_element_type=F32)
            s_loc = jnp.where(in_window, s_loc, NEG_INF)
            sink_col = jnp.zeros((rows, 1), F32)
            for r, h in enumerate(heads):
                sink_col = jnp.where(head_of_row == r, sink_ref[h], sink_col)
            m = jnp.maximum(jnp.maximum(jnp.max(s_loc, axis=1, keepdims=True),
                                        jnp.max(s_ctx, axis=1, keepdims=True)), sink_col)
            p_loc = jnp.exp(s_loc - m)
            p_ctx = jnp.exp(s_ctx - m)
            den = (jnp.sum(p_loc, axis=1, keepdims=True) + jnp.sum(p_ctx, axis=1, keepdims=True)
                   + jnp.exp(sink_col - m))
            o = (jnp.dot(p_loc.astype(BF16), vg, preferred_element_type=F32)
                 + jnp.dot(p_ctx.astype(BF16), vcg, preferred_element_type=F32)) / den
            for r, h in enumerate(heads):
                o_ref[0, qrows, h * LANES:(h + 1) * LANES] = o[r * BLOCK:(r + 1) * BLOCK].astype(BF16)


def _attention(p_lat, kv_ctx, sink, cos_tab, sin_tab):
    b, n, _ = p_lat.shape
    n_ctx = kv_ctx.shape[1]
    grid_spec = pltpu.PrefetchScalarGridSpec(
        num_scalar_prefetch=1,
        grid=(b, n // ATT_TQ),
        in_specs=[
            pl.BlockSpec((1, ATT_TQ, QPAD), lambda bi, i, s: (bi, i, COL_Q // QPAD)),
            pl.BlockSpec((1, n, KVPAD), lambda bi, i, s: (bi, 0, COL_K // KVPAD)),
            pl.BlockSpec((1, n, KVPAD), lambda bi, i, s: (bi, 0, COL_V // KVPAD)),
            pl.BlockSpec((1, n_ctx, 2 * KVPAD), lambda bi, i, s: (bi, 0, 0)),
            pl.BlockSpec((ATT_TQ, LANES), lambda bi, i, s: (i, 0)),
            pl.BlockSpec((ATT_TQ, LANES), lambda bi, i, s: (i, 0)),
            pl.BlockSpec((n, LANES), lambda bi, i, s: (0, 0)),
            pl.BlockSpec((n, LANES), lambda bi, i, s: (0, 0)),
        ],
        out_specs=pl.BlockSpec((1, ATT_TQ, QPAD), lambda bi, i, s: (bi, i, 0)),
        scratch_shapes=[
            pltpu.VMEM((n, KVPAD), BF16),
            pltpu.VMEM((n, KVPAD), BF16),
            pltpu.VMEM((n_ctx, KVPAD), BF16),
            pltpu.VMEM((n_ctx, KVPAD), BF16),
        ],
    )
    return pl.pallas_call(
        _attn_body,
        grid_spec=grid_spec,
        out_shape=jax.ShapeDtypeStruct((b, n, QPAD), BF16),
        compiler_params=_cparams(("parallel", "arbitrary")),
        name="window_attention",
    )(sink, p_lat, p_lat, p_lat, kv_ctx, cos_tab, sin_tab, cos_tab, sin_tab)


def _rope_tables(n):
    rows = n // GRID_W
    row = jnp.repeat(jnp.arange(rows, dtype=F32), GRID_W)
    col = jnp.tile(jnp.arange(GRID_W, dtype=F32), rows)
    inv = ROPE_BASE ** (-jnp.arange(ROPE_FREQS, dtype=F32) / ROPE_FREQS)
    ang_r = row[:, None] * inv
    ang_c = col[:, None] * inv
    pad1 = jnp.ones((n, LANES - HEAD_DIM), F32)
    pad0 = jnp.zeros((n, LANES - HEAD_DIM), F32)
    cos = jnp.concatenate([jnp.cos(ang_r), jnp.cos(ang_r), jnp.cos(ang_c), jnp.cos(ang_c), pad1], axis=1)
    sin = jnp.concatenate([-jnp.sin(ang_r), jnp.sin(ang_r), -jnp.sin(ang_c), jnp.sin(ang_c), pad0], axis=1)
    return cos, sin


def _layer_norm(r, g, b):
    mu = jnp.mean(r, axis=-1, keepdims=True)
    var = jnp.mean(jnp.square(r - mu), axis=-1, keepdims=True)
    return (r - mu) * lax.rsqrt(var + LN_EPS) * g + b


def _outproj_body(hy_ref, att_ref, x_ref, wh_ref, wa_ref, g1_ref, sc2_ref, sh2_ref, lg_ref, lb_ref,
                  x1_ref, hq_ref):
    y = (jnp.dot(hy_ref[0].astype(BF16), wh_ref[...], preferred_element_type=F32)
         + jnp.dot(att_ref[0], wa_ref[...], preferred_element_type=F32))
    x1 = _layer_norm(DEEPNORM_ALPHA * x_ref[0] + g1_ref[0] * y, lg_ref[...], lb_ref[...])
    x1_ref[0] = x1
    hq_ref[0] = (x1 * (1.0 + sc2_ref[0]) + sh2_ref[0]).astype(BF16)


def _outproj_ln(hy, att, x, w_hy, w_att, g1, sc2, sh2, ln_g, ln_b, tm=512):
    b, n, d = x.shape
    modspec = pl.BlockSpec((1, 1, d), lambda bi, i: (bi, 0, 0))
    vecspec = pl.BlockSpec((1, d), lambda bi, i: (0, 0))
    return pl.pallas_call(
        _outproj_body,
        grid=(b, n // tm),
        in_specs=[
            pl.BlockSpec((1, tm, HY_WIDTH), lambda bi, i: (bi, i, 0)),
            pl.BlockSpec((1, tm, QPAD), lambda bi, i: (bi, i, 0)),
            pl.BlockSpec((1, tm, d), lambda bi, i: (bi, i, 0)),
            pl.BlockSpec(w_hy.shape, lambda bi, i: (0, 0)),
            pl.BlockSpec(w_att.shape, lambda bi, i: (0, 0)),
            modspec, modspec, modspec, vecspec, vecspec,
        ],
        out_specs=[
            pl.BlockSpec((1, tm, d), lambda bi, i: (bi, i, 0)),
            pl.BlockSpec((1, tm, d), lambda bi, i: (bi, i, 0)),
        ],
        out_shape=[jax.ShapeDtypeStruct((b, n, d), F32), jax.ShapeDtypeStruct((b, n, d), BF16)],
        compiler_params=_cparams(("parallel", "parallel")),
        name="outproj_ln1",
    )(hy, att, x, w_hy, w_att, g1, sc2, sh2, ln_g, ln_b)


PEER_TM = 256
_STAIR = sorted(((i, j) for i in range(PEER_TOPK) for j in range(PEER_TOPK) if (i + 1) * (j + 1) <= PEER_TOPK),
                key=lambda p: p[0] * PEER_TOPK + p[1])
_STAIR_ROWS = -(-len(_STAIR) // 8) * 8


def _select_topk(s_ref, n_rows, write_row):
    tm = s_ref.shape[1]
    row_id = lax.broadcasted_iota(jnp.int32, (n_rows, tm), 0).astype(F32)

    def step(k, prev_idx):
        s = jnp.where(row_id == prev_idx, -jnp.inf, s_ref[...])
        s_ref[...] = s
        m = jnp.max(s, axis=0, keepdims=True)
        idx = jnp.min(jnp.where(s == m, row_id, float(n_rows)), axis=0, keepdims=True)
        write_row(k, m, idx)
        return idx

    lax.fori_loop(0, PEER_TOPK, step, jnp.full((1, tm), -1.0, F32))


def _peer_topk_body(hq_ref, wq_ref, k1_ref, k2_ref, a_ref, b_ref, g_ref,
                    q_s, s_s, v1_s, i1_s, v2_s, i2_s, c_s, ca_s, cb_s, t_s, ao_s, bo_s, go_s):
    tm = hq_ref.shape[0]
    q = jnp.dot(hq_ref[...], wq_ref[...], preferred_element_type=F32)
    for c in range(2 * PEER_HEADS):
        q_s[c] = q[:, c * LANES:(c + 1) * LANES]
    nt = (((1,), (1,)), ((), ()))

    def head(h, carry):
        for half, (kref, vs, is_) in enumerate(((k1_ref, v1_s, i1_s), (k2_ref, v2_s, i2_s))):
            s_s[...] = lax.dot_general(kref[...], q_s[2 * h + half], nt, preferred_element_type=F32,
                                       precision=HIGHEST)

            def write(k, val, idx, vs=vs, is_=is_):
                vs[pl.ds(k, 1), :] = val
                is_[pl.ds(k, 1), :] = idx
            _select_topk(s_s, PEER_KEYS, write)

        c_s[...] = jnp.full(c_s.shape, -jnp.inf, F32)
        ca_s[...] = jnp.zeros(ca_s.shape, F32)
        cb_s[...] = jnp.zeros(cb_s.shape, F32)
        for r, (i, j) in enumerate(_STAIR):
            c_s[r:r + 1, :] = v1_s[i:i + 1, :] + v2_s[j:j + 1, :]
            ca_s[r:r + 1, :] = i1_s[i:i + 1, :]
            cb_s[r:r + 1, :] = i2_s[j:j + 1, :]

        row_id = lax.broadcasted_iota(jnp.int32, (_STAIR_ROWS, tm), 0).astype(F32)

        def write2(k, val, idx):
            hit = row_id == idx
            t_s[pl.ds(k, 1), :] = val
            ao_s[pl.ds(h * PEER_TOPK + k, 1), :] = jnp.max(jnp.where(hit, ca_s[...], -1.0), axis=0, keepdims=True)
            bo_s[pl.ds(h * PEER_TOPK + k, 1), :] = jnp.max(jnp.where(hit, cb_s[...], -1.0), axis=0, keepdims=True)
        _select_topk(c_s, _STAIR_ROWS, write2)

        t = t_s[...]
        e = jnp.exp(t - jnp.max(t, axis=0, keepdims=True))
        go_s[pl.ds(pl.multiple_of(h * PEER_TOPK, PEER_TOPK), PEER_TOPK), :] = e / jnp.sum(e, axis=0, keepdims=True)
        return carry

    lax.fori_loop(0, PEER_HEADS, head, 0)
    a_ref[...] = ao_s[...].T
    b_ref[...] = bo_s[...].T
    g_ref[...] = go_s[...].T


def _peer_topk(hq, wq, keys1, keys2):
    t, d = hq.shape
    tm = PEER_TM
    hk = PEER_HEADS * PEER_TOPK
    out = jax.ShapeDtypeStruct((t, hk), F32)
    ospec = pl.BlockSpec((tm, hk), lambda i: (i, 0))
    return pl.pallas_call(
        _peer_topk_body,
        grid=(t // tm,),
        in_specs=[
            pl.BlockSpec((tm, d), lambda i: (i, 0)),
            pl.BlockSpec(wq.shape, lambda i: (0, 0)),
            pl.BlockSpec(keys1.shape, lambda i: (0, 0)),
            pl.BlockSpec(keys2.shape, lambda i: (0, 0)),
        ],
        out_specs=[ospec, ospec, ospec],
        out_shape=[out, out, out],
        scratch_shapes=[
            pltpu.VMEM((2 * PEER_HEADS, tm, LANES), F32),
            pltpu.VMEM((PEER_KEYS, tm), F32),
            pltpu.VMEM((PEER_TOPK, tm), F32), pltpu.VMEM((PEER_TOPK, tm), F32),
            pltpu.VMEM((PEER_TOPK, tm), F32), pltpu.VMEM((PEER_TOPK, tm), F32),
            pltpu.VMEM((_STAIR_ROWS, tm), F32), pltpu.VMEM((_STAIR_ROWS, tm), F32), pltpu.VMEM((_STAIR_ROWS, tm), F32),
            pltpu.VMEM((PEER_TOPK, tm), F32),
            pltpu.VMEM((hk, tm), F32), pltpu.VMEM((hk, tm), F32), pltpu.VMEM((hk, tm), F32),
        ],
        compiler_params=_cparams(("parallel",)),
        name="peer_topk",
    )(hq, wq, keys1, keys2)


PEER_TE = 2048
PEER_GROUPS = PEER_TE // PEER_KEYS
GATE_PITCH = PEER_KEYS + 8
GATE_UNROLL = 8


def _peer_dense_body(hq_ref, a_ref, b_ref, g_ref, ut_ref, v_ref, x1_ref, g2_ref, lg_ref, lb_ref, o_ref,
                     ssel_s, w_s, hd_s, acc_s):
    tm = hq_ref.shape[0]
    n_e = (PEER_KEYS * PEER_KEYS) // PEER_TE
    e = pl.program_id(1)
    nt = (((1,), (1,)), ((), ()))

    @pl.when(e == 0)
    def _init():
        ssel_s[...] = jnp.zeros(ssel_s.shape, F32)

    @pl.when(e < n_e)
    def _scores():
        s = jnp.dot(hq_ref[...], ut_ref[...], preferred_element_type=F32)
        b_idx = b_ref[...].astype(jnp.int32)
        a_val = a_ref[...]
        cur = ssel_s[...]
        for jj in range(PEER_GROUPS):
            cand = jnp.take_along_axis(s[:, jj * LANES:(jj + 1) * LANES], b_idx, axis=1)
            cur = jnp.where(a_val == (e * PEER_GROUPS + jj).astype(F32), cand, cur)
        ssel_s[...] = cur

    @pl.when(e == n_e - 1)
    def _gates():
        s_sel = ssel_s[...]
        act = 0.5 * s_sel * (1.0 + lax.erf(s_sel * (2.0 ** -0.5)))
        w_s[...] = g_ref[...] * act
        sub = lax.broadcasted_iota(jnp.int32, (PEER_KEYS, LANES), 0).astype(F32)

        def token(t, carry):
            a_row = a_ref[pl.ds(t, 1), :]
            b_row = b_ref[pl.ds(t, 1), :]
            w_row = w_s[pl.ds(t, 1), :]
            lhs = jnp.where(sub == a_row, w_row, 0.0).astype(BF16)
            rhs = jnp.where(sub == b_row, 1.0, 0.0).astype(BF16)
            tile = lax.dot_general(lhs, rhs, nt, preferred_element_type=F32)
            hd_s[pl.ds(pl.multiple_of(t * GATE_PITCH, 8), PEER_KEYS), :] = tile
            return carry
        lax.fori_loop(0, tm, token, 0, unroll=GATE_UNROLL)
        acc_s[...] = jnp.zeros(acc_s.shape, F32)

    @pl.when(e >= n_e)
    def _values():
        j0 = (e - n_e) * PEER_GROUPS
        lhs = jnp.concatenate(
            [hd_s[pl.ds(j0 + jj, tm, stride=GATE_PITCH), :].astype(BF16) for jj in range(PEER_GROUPS)], axis=1)
        acc_s[...] += jnp.dot(lhs, v_ref[...], preferred_element_type=F32)

    @pl.when(e == 2 * n_e - 1)
    def _finish():
        o_ref[...] = _layer_norm(DEEPNORM_ALPHA * x1_ref[...] + g2_ref[0] * acc_s[...], lg_ref[...], lb_ref[...])


def _peer_dense(hq, a, b, g, u_t, v_tab, x1, g2, ln_g, ln_b, tokens_per_batch):
    t, d = hq.shape
    tm = PEER_TM
    hk = a.shape[1]
    n_e = u_t.shape[1] // PEER_TE
    tiles_per_batch = tokens_per_batch // tm
    tok = lambda i, e: (i, 0)
    return pl.pallas_call(
        _peer_dense_body,
        grid=(t // tm, 2 * n_e),
        in_specs=[
            pl.BlockSpec((tm, d), tok),
            pl.BlockSpec((tm, hk), tok), pl.BlockSpec((tm, hk), tok), pl.BlockSpec((tm, hk), tok),
            pl.BlockSpec((d, PEER_TE), lambda i, e: (0, jnp.minimum(e, n_e - 1))),
            pl.BlockSpec((PEER_TE, d), lambda i, e: (jnp.maximum(e - n_e, 0), 0)),
            pl.BlockSpec((tm, d), tok),
            pl.BlockSpec((1, 1, d), lambda i, e: (i // tiles_per_batch, 0, 0)),
            pl.BlockSpec((1, d), lambda i, e: (0, 0)),
            pl.BlockSpec((1, d), lambda i, e: (0, 0)),
        ],
        out_specs=pl.BlockSpec((tm, d), tok),
        out_shape=jax.ShapeDtypeStruct((t, d), F32),
        scratch_shapes=[
            pltpu.VMEM((tm, hk), F32),
            pltpu.VMEM((tm, hk), F32),
            pltpu.VMEM((tm * GATE_PITCH, LANES), F32),
            pltpu.VMEM((tm, d), F32),
        ],
        compiler_params=_cparams(("parallel", "arbitrary")),
        name="peer_dense",
    )(hq, a, b, g, u_t, v_tab, x1, g2, ln_g, ln_b)


FFT_N2 = 128
FFT_PITCH = FFT_N2 + 8
HY_POS_PAD = 32
FFT_UNROLL = 4


def _dft_constants(n):
    big = 2 * n
    n1c = big // FFT_N2
    half = n1c // 2
    k1 = np.arange(n1c)[:, None]
    n1 = np.arange(half)[None, :]
    ang = 2 * np.pi * k1 * n1 / n1c
    c, s = np.cos(ang), np.sin(ang)
    f1 = np.block([[c, s], [-s, c]])
    k2 = np.arange(FFT_N2)[:, None]
    n2 = np.arange(FFT_N2)[None, :]
    ang = 2 * np.pi * k2 * n2 / FFT_N2
    c, s = np.cos(ang), np.sin(ang)
    d3 = np.block([[c, s], [-s, c]])
    d3i = np.block([[c, -s], [s, c]])
    ang = 2 * np.pi * n1.T * k1.T / n1c
    c, s = np.cos(ang), np.sin(ang)
    f3 = np.block([[c, -s], [s, c]]) / big
    ang = 2 * np.pi * (np.arange(n1c)[:, None] * np.arange(FFT_N2)[None, :]) / big
    twr = np.repeat(np.cos(ang).reshape(-1, 1), LANES, axis=1)
    twi = np.repeat(-np.sin(ang).reshape(-1, 1), LANES, axis=1)
    as32 = lambda a: np.asarray(a, np.float32)
    return as32(f1), as32(d3), as32(d3i), as32(f3), as32(twr), as32(twi)


def _hdot(a, b):
    return jnp.dot(a, b, preferred_element_type=F32, precision=HIGHEST)


def _fft_stage1(u_re, u_im, f1_ref, a_re, a_im):
    half = f1_ref.shape[1] // 2
    n1c = f1_ref.shape[0] // 2

    def column(n2):
        return jnp.concatenate([u_re[pl.ds(n2, half, stride=FFT_PITCH), :],
                                u_im[pl.ds(n2, half, stride=FFT_PITCH), :]], axis=0)

    def body(i, carry):
        a = _hdot(f1_ref[...], jnp.concatenate([column(2 * i), column(2 * i + 1)], axis=1))
        for s in range(2):
            a_re[pl.ds(2 * i + s, n1c, stride=FFT_PITCH), :] = a[:n1c, s * LANES:(s + 1) * LANES]
            a_im[pl.ds(2 * i + s, n1c, stride=FFT_PITCH), :] = a[n1c:, s * LANES:(s + 1) * LANES]
        return carry
    lax.fori_loop(0, FFT_N2 // 2, body, 0, unroll=FFT_UNROLL)


def _twiddled_pair(a_re, a_im, twr_ref, twi_ref, i):
    t0 = pl.multiple_of(2 * i * FFT_N2, 2 * FFT_N2)
    rows = [pl.multiple_of((2 * i + s) * FFT_PITCH, 8) for s in range(2)]
    side = lambda ref, r: jnp.concatenate([ref[pl.ds(r[0], FFT_N2), :], ref[pl.ds(r[1], FFT_N2), :]], axis=1)
    tws = [pl.multiple_of(t0 + s * FFT_N2, FFT_N2) for s in range(2)]
    return rows, side(a_re, rows), side(a_im, rows), side(twr_ref, tws), side(twi_ref, tws)


def _fft_conv_middle(a_re, a_im, d3_ref, d3i_ref, twr_ref, twi_ref, hre_ref, him_ref):
    n1c = hre_ref.shape[1]

    def body(i, carry):
        rows, ar, ai, twr, twi = _twiddled_pair(a_re, a_im, twr_ref, twi_ref, i)
        x = _hdot(d3_ref[...], jnp.concatenate([ar * twr - ai * twi, ar * twi + ai * twr], axis=0))
        xr, xi = x[:FFT_N2], x[FFT_N2:]
        hr = jnp.concatenate([hre_ref[0, 2 * i], hre_ref[0, 2 * i + 1]], axis=1)
        hi = jnp.concatenate([him_ref[0, 2 * i], him_ref[0, 2 * i + 1]], axis=1)
        y = _hdot(d3i_ref[...], jnp.concatenate([xr * hr - xi * hi, xr * hi + xi * hr], axis=0))
        br, bi = y[:FFT_N2], y[FFT_N2:]
        out_re = br * twr + bi * twi
        out_im = bi * twr - br * twi
        for s in range(2):
            a_re[pl.ds(rows[s], FFT_N2), :] = out_re[:, s * LANES:(s + 1) * LANES]
            a_im[pl.ds(rows[s], FFT_N2), :] = out_im[:, s * LANES:(s + 1) * LANES]
        return carry
    lax.fori_loop(0, n1c // 2, body, 0, unroll=FFT_UNROLL)


def _fft_stage_inv(a_re, a_im, f3_ref, y_re, y_im):
    half = f3_ref.shape[0] // 2
    n1c = f3_ref.shape[1] // 2

    def column(n2):
        return jnp.concatenate([a_re[pl.ds(n2, n1c, stride=FFT_PITCH), :],
                                a_im[pl.ds(n2, n1c, stride=FFT_PITCH), :]], axis=0)

    def body(i, carry):
        y = _hdot(f3_ref[...], jnp.concatenate([column(2 * i), column(2 * i + 1)], axis=1))
        for s in range(2):
            y_re[pl.ds(2 * i + s, half, stride=FFT_PITCH), :] = y[:half, s * LANES:(s + 1) * LANES]
            y_im[pl.ds(2 * i + s, half, stride=FFT_PITCH), :] = y[half:, s * LANES:(s + 1) * LANES]
        return carry
    lax.fori_loop(0, FFT_N2 // 2, body, 0, unroll=FFT_UNROLL)


def _short_conv_chunk(x_ref, bi, j, nblk, w_ref, b_ref):
    r0 = j * FFT_N2
    cur = x_ref[bi, r0:r0 + FFT_N2, :]
    row = lax.broadcasted_iota(jnp.int32, cur.shape, 0)
    if j == 0:
        prev = jnp.where(row == 0, 0.0, pltpu.roll(cur, 1, 0))
    else:
        prev = x_ref[bi, r0 - 1:r0 + FFT_N2 - 1, :]
    if j == nblk - 1:
        nxt = jnp.where(row == FFT_N2 - 1, 0.0, pltpu.roll(cur, FFT_N2 - 1, 0))
    else:
        nxt = x_ref[bi, r0 + 1:r0 + FFT_N2 + 1, :]
    return prev * w_ref[0:1, :] + cur * w_ref[1:2, :] + nxt * w_ref[2:3, :] + b_ref[...]


def _hy_conv_body(conv_a, a_ref, g_ref, wa_ref, ba_ref, wg_ref, bg_ref, skip_ref, hre_ref, him_ref,
                  f1_ref, d3_ref, d3i_ref, f3_ref, twr_ref, twi_ref, o_ref,
                  u_re, u_im, a_re, a_im, y_re, y_im):
    nblk = a_ref.shape[1] // FFT_N2
    for bi, dst in ((0, u_re), (1, u_im)):
        for j in range(nblk):
            if conv_a:
                blk = _short_conv_chunk(a_ref, bi, j, nblk, wa_ref, ba_ref)
            else:
                blk = a_ref[bi, j * FFT_N2:(j + 1) * FFT_N2, :]
            dst[j * FFT_PITCH:j * FFT_PITCH + FFT_N2, :] = blk
    _fft_stage1(u_re, u_im, f1_ref, a_re, a_im)
    _fft_conv_middle(a_re, a_im, d3_ref, d3i_ref, twr_ref, twi_ref, hre_ref, him_ref)
    _fft_stage_inv(a_re, a_im, f3_ref, y_re, y_im)
    skip = skip_ref[...]
    for bi, (ysrc, usrc) in enumerate(((y_re, u_re), (y_im, u_im))):
        for j in range(nblk):
            rows = slice(j * FFT_PITCH, j * FFT_PITCH + FFT_N2)
            gate = _short_conv_chunk(g_ref, bi, j, nblk, wg_ref, bg_ref)
            o_ref[bi, j * FFT_N2:(j + 1) * FFT_N2, :] = gate * (ysrc[rows, :] + usrc[rows, :] * skip)


def _hy_conv(a, a_col, g, g_col, conv_w, conv_b, skip, hre, him, order, consts, conv_a):
    b, n, _ = a.shape
    f1, d3, d3i, f3, twr, twi = consts
    n1c = f1.shape[0] // 2
    half = n1c // 2
    w = LANES
    tiles = HY_WIDTH // w
    one = pl.Buffered(1)
    data = lambda col: pl.BlockSpec((2, n, w), lambda ct, p: (p, 0, col + ct), pipeline_mode=one)
    wspec = lambda col: pl.BlockSpec((3, w), lambda ct, p: (0, col + ct))
    bspec = lambda col: pl.BlockSpec((1, w), lambda ct, p: (0, col + ct))
    hspec = pl.BlockSpec((1, n1c, FFT_N2, w), lambda ct, p: (order, 0, 0, ct), pipeline_mode=one)
    cs = lambda arr: pl.BlockSpec(arr.shape, lambda ct, p: (0, 0), pipeline_mode=one)
    a_wcol = a_col if conv_a else g_col
    return pl.pallas_call(
        functools.partial(_hy_conv_body, conv_a),
        grid=(tiles, b // 2),
        in_specs=[data(a_col), data(g_col), wspec(a_wcol), bspec(a_wcol), wspec(g_col), bspec(g_col),
                  pl.BlockSpec((1, w), lambda ct, p: (0, ct)), hspec, hspec,
                  cs(f1), cs(d3), cs(d3i), cs(f3), cs(twr), cs(twi)],
        out_specs=pl.BlockSpec((2, n, w), lambda ct, p: (p, 0, ct), pipeline_mode=one),
        out_shape=jax.ShapeDtypeStruct((b, n, HY_WIDTH), F32),
        scratch_shapes=[pltpu.VMEM((half * FFT_PITCH, w), F32), pltpu.VMEM((half * FFT_PITCH, w), F32),
                        pltpu.VMEM((n1c * FFT_PITCH, w), F32), pltpu.VMEM((n1c * FFT_PITCH, w), F32),
                        pltpu.VMEM((half * FFT_PITCH, w), F32), pltpu.VMEM((half * FFT_PITCH, w), F32)],
        compiler_params=_cparams(("parallel", "arbitrary")),
        name=f"hyena_conv{order + 1}",
    )(a, g, conv_w, conv_b, conv_w, conv_b, skip, hre, him, f1, d3, d3i, f3, twr, twi)


def _hy_filter_body(z_ref, w1_ref, b1_ref, f1_ref, w2_ref, b2_ref, f2_ref, w3_ref, b3_ref, dl_ref, o_ref):
    z = z_ref[...]
    h = jnp.sin(f1_ref[...] * (_hdot(z, w1_ref[...]) + b1_ref[...]))
    h = jnp.sin(f2_ref[...] * (_hdot(h, w2_ref[...]) + b2_ref[...]))
    h = _hdot(h, w3_ref[...]) + b3_ref[...]
    t = z[:, 0:1]
    o_ref[...] = h * (jnp.exp(-t * dl_ref[...]) + HY_WINDOW_SHIFT)


def _hy_filters(n, w1, b1, fr1, w2, b2, fr2, w3, b3):
    t = jnp.linspace(0.0, 1.0, n, dtype=F32)[:, None]
    wv = 2.0 * math.pi * jnp.arange(n, dtype=F32)[:, None] / n
    bands = jnp.linspace(1e-4, HY_BANDS - 1, HY_BANDS, dtype=F32)[None, :]
    z = jnp.concatenate([t, jnp.cos(bands * wv), -jnp.sin(bands * wv)], axis=-1)
    pos = z.shape[1]
    z = jnp.pad(z, ((0, 0), (0, HY_POS_PAD - pos)))
    w1p = jnp.pad(w1, ((0, HY_POS_PAD - pos), (0, 0)))
    min_decay = math.log(HY_DECAY_TARGET) / HY_SLOW_DECAY_PCT
    max_decay = math.log(HY_DECAY_TARGET) / HY_FAST_DECAY_PCT
    deltas = jnp.abs(jnp.linspace(min_decay, max_decay, HY_WIDTH, dtype=F32))
    n_out = w3.shape[1]
    dl = jnp.tile(deltas, n_out // HY_WIDTH)[None, :]
    tn = 512
    hid = w2.shape[0]
    full = lambda shape: pl.BlockSpec(shape, lambda i: (0, 0))
    return pl.pallas_call(
        _hy_filter_body,
        grid=(n // tn,),
        in_specs=[pl.BlockSpec((tn, HY_POS_PAD), lambda i: (i, 0)), full((HY_POS_PAD, hid)), full((1, hid)),
                  full((1, hid)), full((hid, hid)), full((1, hid)), full((1, hid)), full((hid, n_out)),
                  full((1, n_out)), full((1, n_out))],
        out_specs=pl.BlockSpec((tn, n_out), lambda i: (i, 0)),
        out_shape=jax.ShapeDtypeStruct((n, n_out), F32),
        compiler_params=_cparams(("parallel",)),
        name="hyena_filters",
    )(z, w1p, b1[None], fr1[None], w2, b2[None], fr2[None], w3, b3[None], dl)


def _hy_spectrum_body(f_ref, b_ref, f1_ref, d3_ref, twr_ref, twi_ref, hre_ref, him_ref, u_re, u_im, a_re, a_im):
    nblk = f_ref.shape[0] // FFT_N2
    n1c = hre_ref.shape[1]
    for part, out_ref in enumerate((hre_ref, him_ref)):
        for j in range(nblk):
            f = f_ref[j * FFT_N2:(j + 1) * FFT_N2, :]
            bw = b_ref[j * FFT_N2:(j + 1) * FFT_N2, :]
            if j == 0:
                row = lax.broadcasted_iota(jnp.int32, bw.shape, 0)
                bw = jnp.where(row == 0, 0.0, bw)
            rows = slice(j * FFT_PITCH, j * FFT_PITCH + FFT_N2)
            u_re[rows, :] = f + bw if part == 0 else f - bw
            u_im[rows, :] = jnp.zeros((FFT_N2, LANES), F32)
        _fft_stage1(u_re, u_im, f1_ref, a_re, a_im)

        def body(i, carry, part=part, out_ref=out_ref):
            _, ar, ai, twr, twi = _twiddled_pair(a_re, a_im, twr_ref, twi_ref, i)
            x = _hdot(d3_ref[...], jnp.concatenate([ar * twr - ai * twi, ar * twi + ai * twr], axis=0))
            x = x[:FFT_N2] if part == 0 else x[FFT_N2:]
            for s in range(2):
                out_ref[0, 2 * i + s] = x[:, s * LANES:(s + 1) * LANES]
            return carry
        lax.fori_loop(0, n1c // 2, body, 0, unroll=FFT_UNROLL)


def _hy_spectrum(h, consts):
    n = h.shape[0]
    f1, d3, _, _, twr, twi = consts
    n1c = f1.shape[0] // 2
    half = n1c // 2
    tiles = HY_WIDTH // LANES
    cs = lambda arr: pl.BlockSpec(arr.shape, lambda o, ct: (0, 0))
    out = jax.ShapeDtypeStruct((HY_ORDER, n1c, FFT_N2, HY_WIDTH), F32)
    ospec = pl.BlockSpec((1, n1c, FFT_N2, LANES), lambda o, ct: (o, 0, 0, ct))
    return pl.pallas_call(
        _hy_spectrum_body,
        grid=(HY_ORDER, tiles),
        in_specs=[pl.BlockSpec((n, LANES), lambda o, ct: (0, o * tiles + ct)),
                  pl.BlockSpec((n, LANES), lambda o, ct: (0, (HY_ORDER + o) * tiles + ct)),
                  cs(f1), cs(d3), cs(twr), cs(twi)],
        out_specs=[ospec, ospec],
        out_shape=[out, out],
        scratch_shapes=[pltpu.VMEM((half * FFT_PITCH, LANES), F32), pltpu.VMEM((half * FFT_PITCH, LANES), F32),
                        pltpu.VMEM((n1c * FFT_PITCH, LANES), F32), pltpu.VMEM((n1c * FFT_PITCH, LANES), F32)],
        compiler_params=_cparams(("parallel", "parallel")),
        name="hyena_spectrum",
    )(h, h, f1, d3, twr, twi)


def _hyena(p_lat, conv_w, conv_b, w1, b1, fr1, w2, b2, fr2, w3, b3, skip):
    n = p_lat.shape[1]
    consts = _dft_constants(n)
    h = _hy_filters(n, w1, b1, fr1, w2, b2, fr2, w3, b3)
    hre, him = _hy_spectrum(h, consts)
    tiles = HY_WIDTH // LANES
    col = COL_HY // LANES
    cw = jnp.pad(conv_w, ((0, 0), (COL_HY, 0)))
    cb = jnp.pad(conv_b[None], ((0, 0), (COL_HY, 0)))
    y = _hy_conv(p_lat, col, p_lat, col + tiles, cw, cb, skip[0][None], hre, him, 0, consts, True)
    return _hy_conv(y, 0, p_lat, col + 2 * tiles, cw, cb, skip[1][None], hre, him, 1, consts, False)


def _pad_heads(w, heads):
    d = w.shape[0]
    w = w.reshape(d, heads, HEAD_DIM)
    w = jnp.pad(w, ((0, 0), (0, 0), (0, LANES - HEAD_DIM)))
    return w.reshape(d, heads * LANES)


def kernel(x, c, ctx, c_ctx, w_mod, b_mod, w_in, hy_conv_w, hy_conv_b, hy_f_w1, hy_f_b1, hy_f_freq1, hy_f_w2,
           hy_f_b2, hy_f_freq2, hy_f_w3, hy_f_b3, hy_skip, attn_sink, w_out, ln1_g, ln1_b, peer_wq, peer_keys1,
           peer_keys2, peer_u, peer_v, ln2_g, ln2_b):
    b, n, d = x.shape
    l = 0
    cc = jnp.concatenate([c, c_ctx[None], jnp.zeros((8 - b - 1, d), F32)], axis=0)
    mod = _modulation(cc, w_mod[l], b_mod[l][None])
    mod_lat = mod[:b].reshape(b, 6, 1, d)
    sh1, sc1, g1, sh2, sc2, g2 = (mod_lat[:, i] for i in range(6))
    mod_c = mod[b].reshape(6, 1, 1, d)
    csh1, csc1 = mod_c[0], mod_c[1]

    w = w_in[l]
    w_q = _pad_heads(w[:, PROJ_HY:KV_START], ATT_HEADS)
    w_k = _pad_heads(w[:, KV_START:KV_START + PROJ_KV], ATT_KV_HEADS)
    w_v = _pad_heads(w[:, KV_START + PROJ_KV:], ATT_KV_HEADS)
    w_pad = jnp.concatenate([w_q, w[:, :PROJ_HY], w_k, w_v], axis=1).astype(BF16)
    w_kv = jnp.concatenate([w_k, w_v], axis=1).astype(BF16)

    p_lat = _mod_matmul(x, sc1, sh1, w_pad, 512, "in_proj")
    kv_ctx = _mod_matmul(ctx, csc1, csh1, w_kv, ctx.shape[1], "ctx_kv_proj")

    cos_tab, sin_tab = _rope_tables(n)
    att = _attention(p_lat, kv_ctx, attn_sink[l], cos_tab, sin_tab)

    hy = _hyena(p_lat, hy_conv_w[l], hy_conv_b[l], hy_f_w1[l], hy_f_b1[l], hy_f_freq1[l], hy_f_w2[l], hy_f_b2[l],
                hy_f_freq2[l], hy_f_w3[l], hy_f_b3[l], hy_skip[l])

    wo = w_out[l]
    w_o_hy = wo[:HY_WIDTH].astype(BF16)
    w_o_att = jnp.pad(wo[HY_WIDTH:].reshape(ATT_HEADS, HEAD_DIM, d),
                      ((0, 0), (0, LANES - HEAD_DIM), (0, 0))).reshape(QPAD, d).astype(BF16)
    x1, hq = _outproj_ln(hy, att, x, w_o_hy, w_o_att, g1, sc2, sh2, ln1_g[l][None], ln1_b[l][None])

    hq2 = hq.reshape(b * n, d)
    a_idx, b_idx, gate = _peer_topk(hq2, peer_wq[l].astype(BF16), peer_keys1[l], peer_keys2[l])
    u_t = peer_u[l].astype(BF16).T
    out = _peer_dense(hq2, a_idx, b_idx, gate, u_t, peer_v[l].astype(BF16), x1.reshape(b * n, d), g2,
                      ln2_g[l][None], ln2_b[l][None], n)
    return out.reshape(b, n, d)
```

```python
import functools
import math

import jax
import jax.numpy as jnp
import numpy as np
from jax import lax
from jax.experimental import pallas as pl
from jax.experimental.pallas import tpu as pltpu

F32 = jnp.float32
BF16 = jnp.bfloat16
HIGHEST = lax.Precision.HIGHEST

LANES = 128
VMEM_LIMIT = 56 * 1024 * 1024

D_MODEL = 1024
GRID_W = 64
HY_WIDTH = 512
HY_ORDER = 2
HY_BANDS = 8
HY_DECAY_TARGET = 1e-2
HY_FAST_DECAY_PCT = 0.3
HY_SLOW_DECAY_PCT = 1.5
HY_WINDOW_SHIFT = 0.05
ATT_HEADS = 8
ATT_KV_HEADS = 2
ATT_REP = ATT_HEADS // ATT_KV_HEADS
HEAD_DIM = 64
WINDOW = 128
BLOCK = 128
ROPE_BASE = 10000.0
ROPE_FREQS = HEAD_DIM // 4
PROJ_HY = (HY_ORDER + 1) * HY_WIDTH
PROJ_Q = ATT_HEADS * HEAD_DIM
PROJ_KV = ATT_KV_HEADS * HEAD_DIM
KV_START = PROJ_HY + PROJ_Q
PEER_KEYS = 128
PEER_HEADS = 8
PEER_QDIM = 256
PEER_TOPK = 16
LN_EPS = 1e-5
NEG_INF = -1e30
DEPTH = 1
DEEPNORM_ALPHA = (2.0 * DEPTH) ** 0.25

QPAD = ATT_HEADS * LANES
KVPAD = ATT_KV_HEADS * LANES
COL_Q = 0
COL_HY = QPAD
COL_K = QPAD + PROJ_HY
COL_V = COL_K + KVPAD
PROJ_PAD = COL_V + KVPAD


def _cparams(sem):
    return pltpu.CompilerParams(dimension_semantics=sem, vmem_limit_bytes=VMEM_LIMIT)


def _mod_body(c_ref, w_ref, b_ref, o_ref):
    c = c_ref[...]
    a = c * jax.nn.sigmoid(c)
    o_ref[...] = jnp.dot(a, w_ref[...], preferred_element_type=F32, precision=HIGHEST) + b_ref[...]


def _modulation(cc, w_mod, b_mod):
    rows, d = cc.shape
    n_out = w_mod.shape[1]
    tn = 1536
    return pl.pallas_call(
        _mod_body,
        grid=(n_out // tn,),
        in_specs=[
            pl.BlockSpec((rows, d), lambda j: (0, 0)),
            pl.BlockSpec((d, tn), lambda j: (0, j)),
            pl.BlockSpec((1, tn), lambda j: (0, j)),
        ],
        out_specs=pl.BlockSpec((rows, tn), lambda j: (0, j)),
        out_shape=jax.ShapeDtypeStruct((rows, n_out), F32),
        compiler_params=_cparams(("arbitrary",)),
        name="modulation",
    )(cc, w_mod, b_mod)


def _mod_matmul_body(x_ref, sc_ref, sh_ref, w_ref, o_ref):
    h = x_ref[0] * (1.0 + sc_ref[0]) + sh_ref[0]
    o_ref[0] = jnp.dot(h.astype(BF16), w_ref[...], preferred_element_type=F32)


def _mod_matmul(x, sc, sh, w, tm, name):
    b, n, d = x.shape
    n_out = w.shape[1]
    per_batch = sc.shape[0] == b
    mod_map = (lambda bi, i: (bi, 0, 0)) if per_batch else (lambda bi, i: (0, 0, 0))
    return pl.pallas_call(
        _mod_matmul_body,
        grid=(b, n // tm),
        in_specs=[
            pl.BlockSpec((1, tm, d), lambda bi, i: (bi, i, 0)),
            pl.BlockSpec((1, 1, d), mod_map),
            pl.BlockSpec((1, 1, d), mod_map),
            pl.BlockSpec((d, n_out), lambda bi, i: (0, 0)),
        ],
        out_specs=pl.BlockSpec((1, tm, n_out), lambda bi, i: (bi, i, 0)),
        out_shape=jax.ShapeDtypeStruct((b, n, n_out), F32),
        compiler_params=_cparams(("parallel", "parallel")),
        name=name,
    )(x, sc, sh, w)


def _rope_head(x, cos, sin_signed):
    lane = lax.broadcasted_iota(jnp.int32, x.shape, 1)
    first_half = (lane % 32) < 16
    partner = jnp.where(first_half, pltpu.roll(x, LANES - 16, 1), pltpu.roll(x, 16, 1))
    return x * cos + partner * sin_signed


ATT_TQ = 512
ROPE_CHUNK = 512


def _attn_body(sink_ref, q_ref, k_ref, v_ref, kvc_ref, cosq_ref, sinq_ref, cosk_ref, sink_tab_ref,
               o_ref, ks_ref, vs_ref, kcs_ref, vcs_ref):
    n = k_ref.shape[1]
    iq = pl.program_id(1)
    scale = HEAD_DIM ** -0.5

    @pl.when(iq == 0)
    def _prepare_keys():
        def chunk(ci, carry):
            r0 = pl.multiple_of(ci * ROPE_CHUNK, ROPE_CHUNK)
            cos = cosk_ref[pl.ds(r0, ROPE_CHUNK), :]
            sin = sink_tab_ref[pl.ds(r0, ROPE_CHUNK), :]
            for g in range(ATT_KV_HEADS):
                kg = k_ref[0, pl.ds(r0, ROPE_CHUNK), g * LANES:(g + 1) * LANES]
                ks_ref[pl.ds(r0, ROPE_CHUNK), g * LANES:(g + 1) * LANES] = _rope_head(kg, cos, sin).astype(BF16)
            vs_ref[pl.ds(r0, ROPE_CHUNK), :] = v_ref[0, pl.ds(r0, ROPE_CHUNK), :].astype(BF16)
            return carry
        lax.fori_loop(0, n // ROPE_CHUNK, chunk, 0)
        kcs_ref[...] = kvc_ref[0, :, 0:KVPAD].astype(BF16)
        vcs_ref[...] = kvc_ref[0, :, KVPAD:2 * KVPAD].astype(BF16)

    n_loc = 3 * BLOCK
    rows = ATT_REP * BLOCK
    row_i = lax.broadcasted_iota(jnp.int32, (rows, n_loc), 0)
    col_i = lax.broadcasted_iota(jnp.int32, (rows, n_loc), 1)
    rel = col_i - (row_i % BLOCK)
    head_of_row = lax.broadcasted_iota(jnp.int32, (rows, 1), 0) // BLOCK

    for j in range(ATT_TQ // BLOCK):
        blk = iq * (ATT_TQ // BLOCK) + j
        start = pl.multiple_of(jnp.clip((blk - 1) * BLOCK, 0, n - n_loc), BLOCK)
        qrows = slice(j * BLOCK, (j + 1) * BLOCK)
        cosq = cosq_ref[qrows, :]
        sinq = sinq_ref[qrows, :]
        delta = rel + (start - blk * BLOCK)
        in_window = jnp.abs(delta) <= WINDOW
        for g in range(ATT_KV_HEADS):
            heads = [ATT_REP * g + r for r in range(ATT_REP)]
            qg = jnp.concatenate(
                [(_rope_head(q_ref[0, qrows, h * LANES:(h + 1) * LANES], cosq, sinq) * scale).astype(BF16)
                 for h in heads], axis=0)
            kg = ks_ref[pl.ds(start, n_loc), g * LANES:(g + 1) * LANES]
            vg = vs_ref[pl.ds(start, n_loc), g * LANES:(g + 1) * LANES]
            kcg = kcs_ref[:, g * LANES:(g + 1) * LANES]
            vcg = vcs_ref[:, g * LANES:(g + 1) * LANES]
            nt = (((1,), (1,)), ((), ()))
            s_loc = lax.dot_general(qg, kg, nt, preferred_element_type=F32)
            s_ctx = lax.dot_general(qg, kcg, nt, preferred_element_type=F32)
            s_loc = jnp.where(in_window, s_loc, NEG_INF)
            sink_col = jnp.zeros((rows, 1), F32)
            for r, h in enumerate(heads):
                sink_col = jnp.where(head_of_row == r, sink_ref[h], sink_col)
            m = jnp.maximum(jnp.maximum(jnp.max(s_loc, axis=1, keepdims=True),
                                        jnp.max(s_ctx, axis=1, keepdims=True)), sink_col)
            p_loc = jnp.exp(s_loc - m)
            p_ctx = jnp.exp(s_ctx - m)
            den = (jnp.sum(p_loc, axis=1, keepdims=True) + jnp.sum(p_ctx, axis=1, keepdims=True)
                   + jnp.exp(sink_col - m))
            o = (jnp.dot(p_loc.astype(BF16), vg, preferred_element_type=F32)
                 + jnp.dot(p_ctx.astype(BF16), vcg, preferred_element_type=F32)) / den
            for r, h in enumerate(heads):
                o_ref[0, qrows, h * LANES:(h + 1) * LANES] = o[r * BLOCK:(r + 1) * BLOCK].astype(BF16)


def _attention(p_lat, kv_ctx, sink, cos_tab, sin_tab):
    b, n, _ = p_lat.shape
    n_ctx = kv_ctx.shape[1]
    grid_spec = pltpu.PrefetchScalarGridSpec(
        num_scalar_prefetch=1,
        grid=(b, n // ATT_TQ),
        in_specs=[
            pl.BlockSpec((1, ATT_TQ, QPAD), lambda bi, i, s: (bi, i, COL_Q // QPAD)),
            pl.BlockSpec((1, n, KVPAD), lambda bi, i, s: (bi, 0, COL_K // KVPAD)),
            pl.BlockSpec((1, n, KVPAD), lambda bi, i, s: (bi, 0, COL_V // KVPAD)),
            pl.BlockSpec((1, n_ctx, 2 * KVPAD), lambda bi, i, s: (bi, 0, 0)),
            pl.BlockSpec((ATT_TQ, LANES), lambda bi, i, s: (i, 0)),
            pl.BlockSpec((ATT_TQ, LANES), lambda bi, i, s: (i, 0)),
            pl.BlockSpec((n, LANES), lambda bi, i, s: (0, 0)),
            pl.BlockSpec((n, LANES), lambda bi, i, s: (0, 0)),
        ],
        out_specs=pl.BlockSpec((1, ATT_TQ, QPAD), lambda bi, i, s: (bi, i, 0)),
        scratch_shapes=[
            pltpu.VMEM((n, KVPAD), BF16),
            pltpu.VMEM((n, KVPAD), BF16),
            pltpu.VMEM((n_ctx, KVPAD), BF16),
            pltpu.VMEM((n_ctx, KVPAD), BF16),
        ],
    )
    return pl.pallas_call(
        _attn_body,
        grid_spec=grid_spec,
        out_shape=jax.ShapeDtypeStruct((b, n, QPAD), BF16),
        compiler_params=_cparams(("parallel", "arbitrary")),
        name="window_attention",
    )(sink, p_lat, p_lat, p_lat, kv_ctx, cos_tab, sin_tab, cos_tab, sin_tab)


def _rope_tables(n):
    rows = n // GRID_W
    row = jnp.repeat(jnp.arange(rows, dtype=F32), GRID_W)
    col = jnp.tile(jnp.arange(GRID_W, dtype=F32), rows)
    inv = ROPE_BASE ** (-jnp.arange(ROPE_FREQS, dtype=F32) / ROPE_FREQS)
    ang_r = row[:, None] * inv
    ang_c = col[:, None] * inv
    pad1 = jnp.ones((n, LANES - HEAD_DIM), F32)
    pad0 = jnp.zeros((n, LANES - HEAD_DIM), F32)
    cos = jnp.concatenate([jnp.cos(ang_r), jnp.cos(ang_r), jnp.cos(ang_c), jnp.cos(ang_c), pad1], axis=1)
    sin = jnp.concatenate([-jnp.sin(ang_r), jnp.sin(ang_r), -jnp.sin(ang_c), jnp.sin(ang_c), pad0], axis=1)
    return cos, sin


def _layer_norm(r, g, b):
    mu = jnp.mean(r, axis=-1, keepdims=True)
    var = jnp.mean(jnp.square(r - mu), axis=-1, keepdims=True)
    return (r - mu) * lax.rsqrt(var + LN_EPS) * g + b


def _outproj_body(hy_ref, att_ref, x_ref, wh_ref, wa_ref, g1_ref, sc2_ref, sh2_ref, lg_ref, lb_ref,
                  x1_ref, hq_ref):
    y = (jnp.dot(hy_ref[0].astype(BF16), wh_ref[...], preferred_element_type=F32)
         + jnp.dot(att_ref[0], wa_ref[...], preferred_element_type=F32))
    x1 = _layer_norm(DEEPNORM_ALPHA * x_ref[0] + g1_ref[0] * y, lg_ref[...], lb_ref[...])
    x1_ref[0] = x1
    hq_ref[0] = (x1 * (1.0 + sc2_ref[0]) + sh2_ref[0]).astype(BF16)


def _outproj_ln(hy, att, x, w_hy, w_att, g1, sc2, sh2, ln_g, ln_b, tm=512):
    b, n, d = x.shape
    modspec = pl.BlockSpec((1, 1, d), lambda bi, i: (bi, 0, 0))
    vecspec = pl.BlockSpec((1, d), lambda bi, i: (0, 0))
    return pl.pallas_call(
        _outproj_body,
        grid=(b, n // tm),
        in_specs=[
            pl.BlockSpec((1, tm, HY_WIDTH), lambda bi, i: (bi, i, 0)),
            pl.BlockSpec((1, tm, QPAD), lambda bi, i: (bi, i, 0)),
            pl.BlockSpec((1, tm, d), lambda bi, i: (bi, i, 0)),
            pl.BlockSpec(w_hy.shape, lambda bi, i: (0, 0)),
            pl.BlockSpec(w_att.shape, lambda bi, i: (0, 0)),
            modspec, modspec, modspec, vecspec, vecspec,
        ],
        out_specs=[
            pl.BlockSpec((1, tm, d), lambda bi, i: (bi, i, 0)),
            pl.BlockSpec((1, tm, d), lambda bi, i: (bi, i, 0)),
        ],
        out_shape=[jax.ShapeDtypeStruct((b, n, d), F32), jax.ShapeDtypeStruct((b, n, d), BF16)],
        compiler_params=_cparams(("parallel", "parallel")),
        name="outproj_ln1",
    )(hy, att, x, w_hy, w_att, g1, sc2, sh2, ln_g, ln_b)


PEER_TM = 256
_STAIR = sorted(((i, j) for i in range(PEER_TOPK) for j in range(PEER_TOPK) if (i + 1) * (j + 1) <= PEER_TOPK),
                key=lambda p: p[0] * PEER_TOPK + p[1])
_STAIR_ROWS = -(-len(_STAIR) // 8) * 8
_STAIR_COUNT = [PEER_TOPK // (i + 1) for i in range(PEER_TOPK)]
_STAIR_START = [sum(_STAIR_COUNT[:i]) for i in range(PEER_TOPK)]


def _stair_ids(tm):
    ids = np.full((_STAIR_ROWS,), float(PEER_TOPK * PEER_TOPK), np.float32)
    ids[:len(_STAIR)] = [i * PEER_TOPK + j for i, j in _STAIR]
    return np.repeat(ids[:, None], tm, axis=1)


def _select_topk(problems, write_row):
    tm = problems[0][0].shape[1]

    def step(k, prev):
        new = []
        for p, ((s_ref, ids, pad_id), prev_id) in enumerate(zip(problems, prev)):
            s = jnp.where(ids == prev_id, -jnp.inf, s_ref[...])
            s_ref[...] = s
            m = jnp.max(s, axis=0, keepdims=True)
            win = jnp.min(jnp.where(s == m, ids, pad_id), axis=0, keepdims=True)
            write_row(p, k, m, win)
            new.append(win)
        return tuple(new)

    lax.fori_loop(0, PEER_TOPK, step, tuple(jnp.full((1, tm), -1.0, F32) for _ in problems))


def _rows_by_rank(rank, table_ref):
    out = jnp.zeros(rank.shape, F32)
    for p in range(PEER_TOPK):
        out = jnp.where(rank == float(p), table_ref[p:p + 1, :], out)
    return out


def _peer_topk_body(hq_ref, wq_ref, k1_ref, k2_ref, sid_ref, a_ref, b_ref, g_ref,
                    q_s, s1_s, s2_s, v1_s, i1_s, v2_s, i2_s, c_s, t_s, f_s, ao_s, bo_s, go_s):
    tm = hq_ref.shape[0]
    q = jnp.dot(hq_ref[...], wq_ref[...], preferred_element_type=F32)
    for c in range(2 * PEER_HEADS):
        q_s[c] = q[:, c * LANES:(c + 1) * LANES]
    nt = (((1,), (1,)), ((), ()))
    key_id = lax.broadcasted_iota(jnp.int32, (PEER_KEYS, tm), 0).astype(F32)

    def head(h, carry):
        for half, (kref, s_s) in enumerate(((k1_ref, s1_s), (k2_ref, s2_s))):
            s_s[...] = lax.dot_general(kref[...], q_s[2 * h + half], nt, preferred_element_type=F32,
                                       precision=HIGHEST)

        def write1(p, k, val, idx):
            vs, is_ = ((v1_s, i1_s), (v2_s, i2_s))[p]
            vs[pl.ds(k, 1), :] = val
            is_[pl.ds(k, 1), :] = idx
        _select_topk([(s1_s, key_id, float(PEER_KEYS)), (s2_s, key_id, float(PEER_KEYS))], write1)

        c_s[...] = jnp.full(c_s.shape, -jnp.inf, F32)
        for i in range(PEER_TOPK):
            r0, cnt = _STAIR_START[i], _STAIR_COUNT[i]
            c_s[r0:r0 + cnt, :] = v1_s[i:i + 1, :] + v2_s[0:cnt, :]

        def write2(p, k, val, idx):
            t_s[pl.ds(k, 1), :] = val
            f_s[pl.ds(k, 1), :] = idx
        _select_topk([(c_s, sid_ref[...], float(PEER_TOPK * PEER_TOPK))], write2)

        flat = f_s[...]
        rank1 = jnp.floor(flat * (1.0 / PEER_TOPK))
        rank2 = flat - rank1 * PEER_TOPK
        rows = pl.ds(pl.multiple_of(h * PEER_TOPK, PEER_TOPK), PEER_TOPK)
        ao_s[rows, :] = _rows_by_rank(rank1, i1_s)
        bo_s[rows, :] = _rows_by_rank(rank2, i2_s)
        t = t_s[...]
        e = jnp.exp(t - jnp.max(t, axis=0, keepdims=True))
        go_s[rows, :] = e / jnp.sum(e, axis=0, keepdims=True)
        return carry

    lax.fori_loop(0, PEER_HEADS, head, 0)
    a_ref[...] = ao_s[...].T
    b_ref[...] = bo_s[...].T
    g_ref[...] = go_s[...].T


def _peer_topk(hq, wq, keys1, keys2):
    t, d = hq.shape
    tm = PEER_TM
    hk = PEER_HEADS * PEER_TOPK
    out = jax.ShapeDtypeStruct((t, hk), F32)
    ospec = pl.BlockSpec((tm, hk), lambda i: (i, 0))
    sid = _stair_ids(tm)
    return pl.pallas_call(
        _peer_topk_body,
        grid=(t // tm,),
        in_specs=[
            pl.BlockSpec((tm, d), lambda i: (i, 0)),
            pl.BlockSpec(wq.shape, lambda i: (0, 0)),
            pl.BlockSpec(keys1.shape, lambda i: (0, 0)),
            pl.BlockSpec(keys2.shape, lambda i: (0, 0)),
            pl.BlockSpec(sid.shape, lambda i: (0, 0)),
        ],
        out_specs=[ospec, ospec, ospec],
        out_shape=[out, out, out],
        scratch_shapes=[
            pltpu.VMEM((2 * PEER_HEADS, tm, LANES), F32),
            pltpu.VMEM((PEER_KEYS, tm), F32), pltpu.VMEM((PEER_KEYS, tm), F32),
            pltpu.VMEM((PEER_TOPK, tm), F32), pltpu.VMEM((PEER_TOPK, tm), F32),
            pltpu.VMEM((PEER_TOPK, tm), F32), pltpu.VMEM((PEER_TOPK, tm), F32),
            pltpu.VMEM((_STAIR_ROWS, tm), F32),
            pltpu.VMEM((PEER_TOPK, tm), F32), pltpu.VMEM((PEER_TOPK, tm), F32),
            pltpu.VMEM((hk, tm), F32), pltpu.VMEM((hk, tm), F32), pltpu.VMEM((hk, tm), F32),
        ],
        compiler_params=_cparams(("parallel",)),
        name="peer_topk",
    )(hq, wq, keys1, keys2, sid)


PEER_TE = 2048
PEER_GROUPS = PEER_TE // PEER_KEYS
GATE_PITCH = PEER_KEYS + 8
GATE_UNROLL = 32


def _peer_dense_body(hq_ref, a_ref, b_ref, g_ref, ut_ref, v_ref, x1_ref, g2_ref, lg_ref, lb_ref, o_ref,
                     ssel_s, w_s, hd_s, acc_s):
    tm = hq_ref.shape[0]
    n_e = (PEER_KEYS * PEER_KEYS) // PEER_TE
    e = pl.program_id(1)
    nt = (((1,), (1,)), ((), ()))

    @pl.when(e == 0)
    def _init():
        ssel_s[...] = jnp.zeros(ssel_s.shape, F32)

    @pl.when(e < n_e)
    def _scores():
        s = jnp.dot(hq_ref[...], ut_ref[...], preferred_element_type=F32)
        b_idx = b_ref[...].astype(jnp.int32)
        a_val = a_ref[...]
        cur = ssel_s[...]
        for jj in range(PEER_GROUPS):
            cand = jnp.take_along_axis(s[:, jj * LANES:(jj + 1) * LANES], b_idx, axis=1)
            cur = jnp.where(a_val == (e * PEER_GROUPS + jj).astype(F32), cand, cur)
        ssel_s[...] = cur

    @pl.when(e == n_e - 1)
    def _gates():
        s_sel = ssel_s[...]
        act = 0.5 * s_sel * (1.0 + lax.erf(s_sel * (2.0 ** -0.5)))
        w_s[...] = g_ref[...] * act
        sub = lax.broadcasted_iota(jnp.int32, (PEER_KEYS, LANES), 0).astype(F32)

        def token(t, carry):
            a_row = a_ref[pl.ds(t, 1), :]
            b_row = b_ref[pl.ds(t, 1), :]
            w_row = w_s[pl.ds(t, 1), :]
            lhs = jnp.where(sub == a_row, w_row, 0.0).astype(BF16)
            rhs = jnp.where(sub == b_row, 1.0, 0.0).astype(BF16)
            tile = lax.dot_general(lhs, rhs, nt, preferred_element_type=F32)
            hd_s[pl.ds(pl.multiple_of(t * GATE_PITCH, 8), PEER_KEYS), :] = tile
            return carry
        lax.fori_loop(0, tm, token, 0, unroll=GATE_UNROLL)
        acc_s[...] = jnp.zeros(acc_s.shape, F32)

    @pl.when(e >= n_e)
    def _values():
        j0 = (e - n_e) * PEER_GROUPS
        lhs = jnp.concatenate(
            [hd_s[pl.ds(j0 + jj, tm, stride=GATE_PITCH), :].astype(BF16) for jj in range(PEER_GROUPS)], axis=1)
        acc_s[...] += jnp.dot(lhs, v_ref[...], preferred_element_type=F32)

    @pl.when(e == 2 * n_e - 1)
    def _finish():
        o_ref[...] = _layer_norm(DEEPNORM_ALPHA * x1_ref[...] + g2_ref[0] * acc_s[...], lg_ref[...], lb_ref[...])


def _peer_dense(hq, a, b, g, u_t, v_tab, x1, g2, ln_g, ln_b, tokens_per_batch):
    t, d = hq.shape
    tm = PEER_TM
    hk = a.shape[1]
    n_e = u_t.shape[1] // PEER_TE
    tiles_per_batch = tokens_per_batch // tm
    tok = lambda i, e: (i, 0)
    return pl.pallas_call(
        _peer_dense_body,
        grid=(t // tm, 2 * n_e),
        in_specs=[
            pl.BlockSpec((tm, d), tok),
            pl.BlockSpec((tm, hk), tok), pl.BlockSpec((tm, hk), tok), pl.BlockSpec((tm, hk), tok),
            pl.BlockSpec((d, PEER_TE), lambda i, e: (0, jnp.minimum(e, n_e - 1))),
            pl.BlockSpec((PEER_TE, d), lambda i, e: (jnp.maximum(e - n_e, 0), 0)),
            pl.BlockSpec((tm, d), tok),
            pl.BlockSpec((1, 1, d), lambda i, e: (i // tiles_per_batch, 0, 0)),
            pl.BlockSpec((1, d), lambda i, e: (0, 0)),
            pl.BlockSpec((1, d), lambda i, e: (0, 0)),
        ],
        out_specs=pl.BlockSpec((tm, d), tok),
        out_shape=jax.ShapeDtypeStruct((t, d), F32),
        scratch_shapes=[
            pltpu.VMEM((tm, hk), F32),
            pltpu.VMEM((tm, hk), F32),
            pltpu.VMEM((tm * GATE_PITCH, LANES), F32),
            pltpu.VMEM((tm, d), F32),
        ],
        compiler_params=_cparams(("parallel", "arbitrary")),
        name="peer_dense",
    )(hq, a, b, g, u_t, v_tab, x1, g2, ln_g, ln_b)


FFT_N2 = 128
FFT_PITCH = FFT_N2 + 8
HY_POS_PAD = 32
FFT_UNROLL = 4


def _dft_constants(n):
    big = 2 * n
    n1c = big // FFT_N2
    half = n1c // 2
    k1 = np.arange(n1c)[:, None]
    n1 = np.arange(half)[None, :]
    ang = 2 * np.pi * k1 * n1 / n1c
    c, s = np.cos(ang), np.sin(ang)
    f1 = np.block([[c, s], [-s, c]])
    k2 = np.arange(FFT_N2)[:, None]
    n2 = np.arange(FFT_N2)[None, :]
    ang = 2 * np.pi * k2 * n2 / FFT_N2
    c, s = np.cos(ang), np.sin(ang)
    d3 = np.block([[c, s], [-s, c]])
    d3i = np.block([[c, -s], [s, c]])
    ang = 2 * np.pi * n1.T * k1.T / n1c
    c, s = np.cos(ang), np.sin(ang)
    f3 = np.block([[c, -s], [s, c]]) / big
    ang = 2 * np.pi * (np.arange(n1c)[:, None] * np.arange(FFT_N2)[None, :]) / big
    twr = np.repeat(np.cos(ang).reshape(-1, 1), LANES, axis=1)
    twi = np.repeat(-np.sin(ang).reshape(-1, 1), LANES, axis=1)
    as32 = lambda a: np.asarray(a, np.float32)
    return as32(f1), as32(d3), as32(d3i), as32(f3), as32(twr), as32(twi)


def _hdot(a, b):
    return jnp.dot(a, b, preferred_element_type=F32, precision=HIGHEST)


def _fft_stage1(u_re, u_im, f1_ref, a_re, a_im):
    half = f1_ref.shape[1] // 2
    n1c = f1_ref.shape[0] // 2

    def column(n2):
        return jnp.concatenate([u_re[pl.ds(n2, half, stride=FFT_PITCH), :],
                                u_im[pl.ds(n2, half, stride=FFT_PITCH), :]], axis=0)

    def body(i, carry):
        a = _hdot(f1_ref[...], jnp.concatenate([column(2 * i), column(2 * i + 1)], axis=1))
        for s in range(2):
            a_re[pl.ds(2 * i + s, n1c, stride=FFT_PITCH), :] = a[:n1c, s * LANES:(s + 1) * LANES]
            a_im[pl.ds(2 * i + s, n1c, stride=FFT_PITCH), :] = a[n1c:, s * LANES:(s + 1) * LANES]
        return carry
    lax.fori_loop(0, FFT_N2 // 2, body, 0, unroll=FFT_UNROLL)


def _twiddled_pair(a_re, a_im, twr_ref, twi_ref, i):
    t0 = pl.multiple_of(2 * i * FFT_N2, 2 * FFT_N2)
    rows = [pl.multiple_of((2 * i + s) * FFT_PITCH, 8) for s in range(2)]
    side = lambda ref, r: jnp.concatenate([ref[pl.ds(r[0], FFT_N2), :], ref[pl.ds(r[1], FFT_N2), :]], axis=1)
    tws = [pl.multiple_of(t0 + s * FFT_N2, FFT_N2) for s in range(2)]
    return rows, side(a_re, rows), side(a_im, rows), side(twr_ref, tws), side(twi_ref, tws)


def _fft_conv_middle(a_re, a_im, d3_ref, d3i_ref, twr_ref, twi_ref, hre_ref, him_ref):
    n1c = hre_ref.shape[1]

    def body(i, carry):
        rows, ar, ai, twr, twi = _twiddled_pair(a_re, a_im, twr_ref, twi_ref, i)
        x = _hdot(d3_ref[...], jnp.concatenate([ar * twr - ai * twi, ar * twi + ai * twr], axis=0))
        xr, xi = x[:FFT_N2], x[FFT_N2:]
        hr = jnp.concatenate([hre_ref[0, 2 * i], hre_ref[0, 2 * i + 1]], axis=1)
        hi = jnp.concatenate([him_ref[0, 2 * i], him_ref[0, 2 * i + 1]], axis=1)
        y = _hdot(d3i_ref[...], jnp.concatenate([xr * hr - xi * hi, xr * hi + xi * hr], axis=0))
        br, bi = y[:FFT_N2], y[FFT_N2:]
        out_re = br * twr + bi * twi
        out_im = bi * twr - br * twi
        for s in range(2):
            a_re[pl.ds(rows[s], FFT_N2), :] = out_re[:, s * LANES:(s + 1) * LANES]
            a_im[pl.ds(rows[s], FFT_N2), :] = out_im[:, s * LANES:(s + 1) * LANES]
        return carry
    lax.fori_loop(0, n1c // 2, body, 0, unroll=FFT_UNROLL)


def _fft_stage_inv(a_re, a_im, f3_ref, y_re, y_im):
    half = f3_ref.shape[0] // 2
    n1c = f3_ref.shape[1] // 2

    def column(n2):
        return jnp.concatenate([a_re[pl.ds(n2, n1c, stride=FFT_PITCH), :],
                                a_im[pl.ds(n2, n1c, stride=FFT_PITCH), :]], axis=0)

    def body(i, carry):
        y = _hdot(f3_ref[...], jnp.concatenate([column(2 * i), column(2 * i + 1)], axis=1))
        for s in range(2):
            y_re[pl.ds(2 * i + s, half, stride=FFT_PITCH), :] = y[:half, s * LANES:(s + 1) * LANES]
            y_im[pl.ds(2 * i + s, half, stride=FFT_PITCH), :] = y[half:, s * LANES:(s + 1) * LANES]
        return carry
    lax.fori_loop(0, FFT_N2 // 2, body, 0, unroll=FFT_UNROLL)


def _short_conv_chunk(x_ref, bi, j, nblk, w_ref, b_ref):
    r0 = j * FFT_N2
    cur = x_ref[bi, r0:r0 + FFT_N2, :]
    row = lax.broadcasted_iota(jnp.int32, cur.shape, 0)
    if j == 0:
        prev = jnp.where(row == 0, 0.0, pltpu.roll(cur, 1, 0))
    else:
        prev = x_ref[bi, r0 - 1:r0 + FFT_N2 - 1, :]
    if j == nblk - 1:
        nxt = jnp.where(row == FFT_N2 - 1, 0.0, pltpu.roll(cur, FFT_N2 - 1, 0))
    else:
        nxt = x_ref[bi, r0 + 1:r0 + FFT_N2 + 1, :]
    return prev * w_ref[0:1, :] + cur * w_ref[1:2, :] + nxt * w_ref[2:3, :] + b_ref[...]


def _hy_conv_body(conv_a, a_ref, g_ref, wa_ref, ba_ref, wg_ref, bg_ref, skip_ref, hre_ref, him_ref,
                  f1_ref, d3_ref, d3i_ref, f3_ref, twr_ref, twi_ref, o_ref,
                  u_re, u_im, a_re, a_im, y_re, y_im):
    nblk = a_ref.shape[1] // FFT_N2
    for bi, dst in ((0, u_re), (1, u_im)):
        for j in range(nblk):
            if conv_a:
                blk = _short_conv_chunk(a_ref, bi, j, nblk, wa_ref, ba_ref)
            else:
                blk = a_ref[bi, j * FFT_N2:(j + 1) * FFT_N2, :]
            dst[j * FFT_PITCH:j * FFT_PITCH + FFT_N2, :] = blk
    _fft_stage1(u_re, u_im, f1_ref, a_re, a_im)
    _fft_conv_middle(a_re, a_im, d3_ref, d3i_ref, twr_ref, twi_ref, hre_ref, him_ref)
    _fft_stage_inv(a_re, a_im, f3_ref, y_re, y_im)
    skip = skip_ref[...]
    for bi, (ysrc, usrc) in enumerate(((y_re, u_re), (y_im, u_im))):
        for j in range(nblk):
            rows = slice(j * FFT_PITCH, j * FFT_PITCH + FFT_N2)
            gate = _short_conv_chunk(g_ref, bi, j, nblk, wg_ref, bg_ref)
            o_ref[bi, j * FFT_N2:(j + 1) * FFT_N2, :] = gate * (ysrc[rows, :] + usrc[rows, :] * skip)


def _hy_conv(a, a_col, g, g_col, conv_w, conv_b, skip, hre, him, order, consts, conv_a):
    b, n, _ = a.shape
    f1, d3, d3i, f3, twr, twi = consts
    n1c = f1.shape[0] // 2
    half = n1c // 2
    w = LANES
    tiles = HY_WIDTH // w
    one = pl.Buffered(1)
    data = lambda col: pl.BlockSpec((2, n, w), lambda ct, p: (p, 0, col + ct), pipeline_mode=one)
    wspec = lambda col: pl.BlockSpec((3, w), lambda ct, p: (0, col + ct))
    bspec = lambda col: pl.BlockSpec((1, w), lambda ct, p: (0, col + ct))
    hspec = pl.BlockSpec((1, n1c, FFT_N2, w), lambda ct, p: (order, 0, 0, ct), pipeline_mode=one)
    cs = lambda arr: pl.BlockSpec(arr.shape, lambda ct, p: (0, 0), pipeline_mode=one)
    a_wcol = a_col if conv_a else g_col
    return pl.pallas_call(
        functools.partial(_hy_conv_body, conv_a),
        grid=(tiles, b // 2),
        in_specs=[data(a_col), data(g_col), wspec(a_wcol), bspec(a_wcol), wspec(g_col), bspec(g_col),
                  pl.BlockSpec((1, w), lambda ct, p: (0, ct)), hspec, hspec,
                  cs(f1), cs(d3), cs(d3i), cs(f3), cs(twr), cs(twi)],
        out_specs=pl.BlockSpec((2, n, w), lambda ct, p: (p, 0, ct), pipeline_mode=one),
        out_shape=jax.ShapeDtypeStruct((b, n, HY_WIDTH), F32),
        scratch_shapes=[pltpu.VMEM((half * FFT_PITCH, w), F32), pltpu.VMEM((half * FFT_PITCH, w), F32),
                        pltpu.VMEM((n1c * FFT_PITCH, w), F32), pltpu.VMEM((n1c * FFT_PITCH, w), F32),
                        pltpu.VMEM((half * FFT_PITCH, w), F32), pltpu.VMEM((half * FFT_PITCH, w), F32)],
        compiler_params=_cparams(("parallel", "arbitrary")),
        name=f"hyena_conv{order + 1}",
    )(a, g, conv_w, conv_b, conv_w, conv_b, skip, hre, him, f1, d3, d3i, f3, twr, twi)


def _hy_filter_body(z_ref, w1_ref, b1_ref, f1_ref, w2_ref, b2_ref, f2_ref, w3_ref, b3_ref, dl_ref, o_ref):
    z = z_ref[...]
    h = jnp.sin(f1_ref[...] * (_hdot(z, w1_ref[...]) + b1_ref[...]))
    h = jnp.sin(f2_ref[...] * (_hdot(h, w2_ref[...]) + b2_ref[...]))
    h = _hdot(h, w3_ref[...]) + b3_ref[...]
    t = z[:, 0:1]
    o_ref[...] = h * (jnp.exp(-t * dl_ref[...]) + HY_WINDOW_SHIFT)


def _hy_filters(n, w1, b1, fr1, w2, b2, fr2, w3, b3):
    t = jnp.linspace(0.0, 1.0, n, dtype=F32)[:, None]
    wv = 2.0 * math.pi * jnp.arange(n, dtype=F32)[:, None] / n
    bands = jnp.linspace(1e-4, HY_BANDS - 1, HY_BANDS, dtype=F32)[None, :]
    z = jnp.concatenate([t, jnp.cos(bands * wv), -jnp.sin(bands * wv)], axis=-1)
    pos = z.shape[1]
    z = jnp.pad(z, ((0, 0), (0, HY_POS_PAD - pos)))
    w1p = jnp.pad(w1, ((0, HY_POS_PAD - pos), (0, 0)))
    min_decay = math.log(HY_DECAY_TARGET) / HY_SLOW_DECAY_PCT
    max_decay = math.log(HY_DECAY_TARGET) / HY_FAST_DECAY_PCT
    deltas = jnp.abs(jnp.linspace(min_decay, max_decay, HY_WIDTH, dtype=F32))
    n_out = w3.shape[1]
    dl = jnp.tile(deltas, n_out // HY_WIDTH)[None, :]
    tn = 512
    hid = w2.shape[0]
    full = lambda shape: pl.BlockSpec(shape, lambda i: (0, 0))
    return pl.pallas_call(
        _hy_filter_body,
        grid=(n // tn,),
        in_specs=[pl.BlockSpec((tn, HY_POS_PAD), lambda i: (i, 0)), full((HY_POS_PAD, hid)), full((1, hid)),
                  full((1, hid)), full((hid, hid)), full((1, hid)), full((1, hid)), full((hid, n_out)),
                  full((1, n_out)), full((1, n_out))],
        out_specs=pl.BlockSpec((tn, n_out), lambda i: (i, 0)),
        out_shape=jax.ShapeDtypeStruct((n, n_out), F32),
        compiler_params=_cparams(("parallel",)),
        name="hyena_filters",
    )(z, w1p, b1[None], fr1[None], w2, b2[None], fr2[None], w3, b3[None], dl)


def _hy_spectrum_body(f_ref, b_ref, f1_ref, d3_ref, twr_ref, twi_ref, hre_ref, him_ref, u_re, u_im, a_re, a_im):
    nblk = f_ref.shape[0] // FFT_N2
    n1c = hre_ref.shape[1]
    for part, out_ref in enumerate((hre_ref, him_ref)):
        for j in range(nblk):
            f = f_ref[j * FFT_N2:(j + 1) * FFT_N2, :]
            bw = b_ref[j * FFT_N2:(j + 1) * FFT_N2, :]
            if j == 0:
                row = lax.broadcasted_iota(jnp.int32, bw.shape, 0)
                bw = jnp.where(row == 0, 0.0, bw)
            rows = slice(j * FFT_PITCH, j * FFT_PITCH + FFT_N2)
            u_re[rows, :] = f + bw if part == 0 else f - bw
            u_im[rows, :] = jnp.zeros((FFT_N2, LANES), F32)
        _fft_stage1(u_re, u_im, f1_ref, a_re, a_im)

        def body(i, carry, part=part, out_ref=out_ref):
            _, ar, ai, twr, twi = _twiddled_pair(a_re, a_im, twr_ref, twi_ref, i)
            x = _hdot(d3_ref[...], jnp.concatenate([ar * twr - ai * twi, ar * twi + ai * twr], axis=0))
            x = x[:FFT_N2] if part == 0 else x[FFT_N2:]
            for s in range(2):
                out_ref[0, 2 * i + s] = x[:, s * LANES:(s + 1) * LANES]
            return carry
        lax.fori_loop(0, n1c // 2, body, 0, unroll=FFT_UNROLL)


def _hy_spectrum(h, consts):
    n = h.shape[0]
    f1, d3, _, _, twr, twi = consts
    n1c = f1.shape[0] // 2
    half = n1c // 2
    tiles = HY_WIDTH // LANES
    cs = lambda arr: pl.BlockSpec(arr.shape, lambda o, ct: (0, 0))
    out = jax.ShapeDtypeStruct((HY_ORDER, n1c, FFT_N2, HY_WIDTH), F32)
    ospec = pl.BlockSpec((1, n1c, FFT_N2, LANES), lambda o, ct: (o, 0, 0, ct))
    return pl.pallas_call(
        _hy_spectrum_body,
        grid=(HY_ORDER, tiles),
        in_specs=[pl.BlockSpec((n, LANES), lambda o, ct: (0, o * tiles + ct)),
                  pl.BlockSpec((n, LANES), lambda o, ct: (0, (HY_ORDER + o) * tiles + ct)),
                  cs(f1), cs(d3), cs(twr), cs(twi)],
        out_specs=[ospec, ospec],
        out_shape=[out, out],
        scratch_shapes=[pltpu.VMEM((half * FFT_PITCH, LANES), F32), pltpu.VMEM((half * FFT_PITCH, LANES), F32),
                        pltpu.VMEM((n1c * FFT_PITCH, LANES), F32), pltpu.VMEM((n1c * FFT_PITCH, LANES), F32)],
        compiler_params=_cparams(("parallel", "parallel")),
        name="hyena_spectrum",
    )(h, h, f1, d3, twr, twi)


def _hyena(p_lat, conv_w, conv_b, w1, b1, fr1, w2, b2, fr2, w3, b3, skip):
    n = p_lat.shape[1]
    consts = _dft_constants(n)
    h = _hy_filters(n, w1, b1, fr1, w2, b2, fr2, w3, b3)
    hre, him = _hy_spectrum(h, consts)
    tiles = HY_WIDTH // LANES
    col = COL_HY // LANES
    cw = jnp.pad(conv_w, ((0, 0), (COL_HY, 0)))
    cb = jnp.pad(conv_b[None], ((0, 0), (COL_HY, 0)))
    y = _hy_conv(p_lat, col, p_lat, col + tiles, cw, cb, skip[0][None], hre, him, 0, consts, True)
    return _hy_conv(y, 0, p_lat, col + 2 * tiles, cw, cb, skip[1][None], hre, him, 1, consts, False)


def _pad_heads(w, heads):
    d = w.shape[0]
    w = w.reshape(d, heads, HEAD_DIM)
    w = jnp.pad(w, ((0, 0), (0, 0), (0, LANES - HEAD_DIM)))
    return w.reshape(d, heads * LANES)


def kernel(x, c, ctx, c_ctx, w_mod, b_mod, w_in, hy_conv_w, hy_conv_b, hy_f_w1, hy_f_b1, hy_f_freq1, hy_f_w2,
           hy_f_b2, hy_f_freq2, hy_f_w3, hy_f_b3, hy_skip, attn_sink, w_out, ln1_g, ln1_b, peer_wq, peer_keys1,
           peer_keys2, peer_u, peer_v, ln2_g, ln2_b):
    b, n, d = x.shape
    l = 0
    cc = jnp.concatenate([c, c_ctx[None], jnp.zeros((8 - b - 1, d), F32)], axis=0)
    mod = _modulation(cc, w_mod[l], b_mod[l][None])
    mod_lat = mod[:b].reshape(b, 6, 1, d)
    sh1, sc1, g1, sh2, sc2, g2 = (mod_lat[:, i] for i in range(6))
    mod_c = mod[b].reshape(6, 1, 1, d)
    csh1, csc1 = mod_c[0], mod_c[1]

    w = w_in[l]
    w_q = _pad_heads(w[:, PROJ_HY:KV_START], ATT_HEADS)
    w_k = _pad_heads(w[:, KV_START:KV_START + PROJ_KV], ATT_KV_HEADS)
    w_v = _pad_heads(w[:, KV_START + PROJ_KV:], ATT_KV_HEADS)
    w_pad = jnp.concatenate([w_q, w[:, :PROJ_HY], w_k, w_v], axis=1).astype(BF16)
    w_kv = jnp.concatenate([w_k, w_v], axis=1).astype(BF16)

    p_lat = _mod_matmul(x, sc1, sh1, w_pad, 512, "in_proj")
    kv_ctx = _mod_matmul(ctx, csc1, csh1, w_kv, ctx.shape[1], "ctx_kv_proj")

    cos_tab, sin_tab = _rope_tables(n)
    att = _attention(p_lat, kv_ctx, attn_sink[l], cos_tab, sin_tab)

    hy = _hyena(p_lat, hy_conv_w[l], hy_conv_b[l], hy_f_w1[l], hy_f_b1[l], hy_f_freq1[l], hy_f_w2[l], hy_f_b2[l],
                hy_f_freq2[l], hy_f_w3[l], hy_f_b3[l], hy_skip[l])

    wo = w_out[l]
    w_o_hy = wo[:HY_WIDTH].astype(BF16)
    w_o_att = jnp.pad(wo[HY_WIDTH:].reshape(ATT_HEADS, HEAD_DIM, d),
                      ((0, 0), (0, LANES - HEAD_DIM), (0, 0))).reshape(QPAD, d).astype(BF16)
    x1, hq = _outproj_ln(hy, att, x, w_o_hy, w_o_att, g1, sc2, sh2, ln1_g[l][None], ln1_b[l][None])

    hq2 = hq.reshape(b * n, d)
    a_idx, b_idx, gate = _peer_topk(hq2, peer_wq[l].astype(BF16), peer_keys1[l], peer_keys2[l])
    u_t = peer_u[l].astype(BF16).T
    out = _peer_dense(hq2, a_idx, b_idx, gate, u_t, peer_v[l].astype(BF16), x1.reshape(b * n, d), g2,
                      ln2_g[l][None], ln2_b[l][None], n)
    return out.reshape(b, n, d)
```

```python
import functools
import math

import jax
import jax.numpy as jnp
import numpy as np
from jax import lax
from jax.experimental import pallas as pl
from jax.experimental.pallas import tpu as pltpu

F32 = jnp.float32
BF16 = jnp.bfloat16
HIGHEST = lax.Precision.HIGHEST

LANES = 128
VMEM_LIMIT = 60000 * 1024

D_MODEL = 1024
GRID_W = 64
HY_WIDTH = 512
HY_ORDER = 2
HY_BANDS = 8
HY_DECAY_TARGET = 1e-2
HY_FAST_DECAY_PCT = 0.3
HY_SLOW_DECAY_PCT = 1.5
HY_WINDOW_SHIFT = 0.05
ATT_HEADS = 8
ATT_KV_HEADS = 2
ATT_REP = ATT_HEADS // ATT_KV_HEADS
HEAD_DIM = 64
WINDOW = 128
BLOCK = 128
ROPE_BASE = 10000.0
ROPE_FREQS = HEAD_DIM // 4
PROJ_HY = (HY_ORDER + 1) * HY_WIDTH
PROJ_Q = ATT_HEADS * HEAD_DIM
PROJ_KV = ATT_KV_HEADS * HEAD_DIM
KV_START = PROJ_HY + PROJ_Q
PEER_KEYS = 128
PEER_HEADS = 8
PEER_QDIM = 256
PEER_TOPK = 16
LN_EPS = 1e-5
NEG_INF = -1e30
DEPTH = 1
DEEPNORM_ALPHA = (2.0 * DEPTH) ** 0.25

QPAD = ATT_HEADS * LANES
KVPAD = ATT_KV_HEADS * LANES
COL_Q = 0
COL_HY = QPAD
COL_K = QPAD + PROJ_HY
COL_V = COL_K + KVPAD
PROJ_PAD = COL_V + KVPAD


def _cparams(sem):
    return pltpu.CompilerParams(dimension_semantics=sem, vmem_limit_bytes=VMEM_LIMIT)


def _mod_body(c_ref, w_ref, b_ref, o_ref):
    c = c_ref[...]
    a = c * jax.nn.sigmoid(c)
    o_ref[...] = jnp.dot(a, w_ref[...], preferred_element_type=F32, precision=HIGHEST) + b_ref[...]


def _modulation(cc, w_mod, b_mod):
    rows, d = cc.shape
    n_out = w_mod.shape[1]
    tn = 1536
    return pl.pallas_call(
        _mod_body,
        grid=(n_out // tn,),
        in_specs=[
            pl.BlockSpec((rows, d), lambda j: (0, 0)),
            pl.BlockSpec((d, tn), lambda j: (0, j)),
            pl.BlockSpec((1, tn), lambda j: (0, j)),
        ],
        out_specs=pl.BlockSpec((rows, tn), lambda j: (0, j)),
        out_shape=jax.ShapeDtypeStruct((rows, n_out), F32),
        compiler_params=_cparams(("arbitrary",)),
        name="modulation",
    )(cc, w_mod, b_mod)


def _mod_matmul_body(x_ref, sc_ref, sh_ref, w_ref, o_ref):
    h = x_ref[0] * (1.0 + sc_ref[0]) + sh_ref[0]
    o_ref[0] = jnp.dot(h.astype(BF16), w_ref[...], preferred_element_type=F32)


def _mod_matmul(x, sc, sh, w, tm, name):
    b, n, d = x.shape
    n_out = w.shape[1]
    per_batch = sc.shape[0] == b
    mod_map = (lambda bi, i: (bi, 0, 0)) if per_batch else (lambda bi, i: (0, 0, 0))
    return pl.pallas_call(
        _mod_matmul_body,
        grid=(b, n // tm),
        in_specs=[
            pl.BlockSpec((1, tm, d), lambda bi, i: (bi, i, 0)),
            pl.BlockSpec((1, 1, d), mod_map),
            pl.BlockSpec((1, 1, d), mod_map),
            pl.BlockSpec((d, n_out), lambda bi, i: (0, 0)),
        ],
        out_specs=pl.BlockSpec((1, tm, n_out), lambda bi, i: (bi, i, 0)),
        out_shape=jax.ShapeDtypeStruct((b, n, n_out), F32),
        compiler_params=_cparams(("parallel", "parallel")),
        name=name,
    )(x, sc, sh, w)


def _rope_head(x, cos, sin_signed):
    lane = lax.broadcasted_iota(jnp.int32, x.shape, 1)
    first_half = (lane % 32) < 16
    partner = jnp.where(first_half, pltpu.roll(x, LANES - 16, 1), pltpu.roll(x, 16, 1))
    return x * cos + partner * sin_signed


ATT_TQ = 512
ROPE_CHUNK = 512


def _attn_body(sink_ref, q_ref, k_ref, v_ref, kvc_ref, cosq_ref, sinq_ref, cosk_ref, sink_tab_ref,
               o_ref, ks_ref, vs_ref, kcs_ref, vcs_ref):
    n = k_ref.shape[1]
    iq = pl.program_id(1)
    scale = HEAD_DIM ** -0.5

    @pl.when(iq == 0)
    def _prepare_keys():
        def chunk(ci, carry):
            r0 = pl.multiple_of(ci * ROPE_CHUNK, ROPE_CHUNK)
            cos = cosk_ref[pl.ds(r0, ROPE_CHUNK), :]
            sin = sink_tab_ref[pl.ds(r0, ROPE_CHUNK), :]
            for g in range(ATT_KV_HEADS):
                kg = k_ref[0, pl.ds(r0, ROPE_CHUNK), g * LANES:(g + 1) * LANES]
                ks_ref[pl.ds(r0, ROPE_CHUNK), g * LANES:(g + 1) * LANES] = _rope_head(kg, cos, sin).astype(BF16)
            vs_ref[pl.ds(r0, ROPE_CHUNK), :] = v_ref[0, pl.ds(r0, ROPE_CHUNK), :].astype(BF16)
            return carry
        lax.fori_loop(0, n // ROPE_CHUNK, chunk, 0)
        kcs_ref[...] = kvc_ref[0, :, 0:KVPAD].astype(BF16)
        vcs_ref[...] = kvc_ref[0, :, KVPAD:2 * KVPAD].astype(BF16)

    n_loc = 3 * BLOCK
    rows = ATT_REP * BLOCK
    row_i = lax.broadcasted_iota(jnp.int32, (rows, n_loc), 0)
    col_i = lax.broadcasted_iota(jnp.int32, (rows, n_loc), 1)
    rel = col_i - (row_i % BLOCK)
    head_of_row = lax.broadcasted_iota(jnp.int32, (rows, 1), 0) // BLOCK

    for j in range(ATT_TQ // BLOCK):
        blk = iq * (ATT_TQ // BLOCK) + j
        start = pl.multiple_of(jnp.clip((blk - 1) * BLOCK, 0, n - n_loc), BLOCK)
        qrows = slice(j * BLOCK, (j + 1) * BLOCK)
        cosq = cosq_ref[qrows, :]
        sinq = sinq_ref[qrows, :]
        delta = rel + (start - blk * BLOCK)
        in_window = jnp.abs(delta) <= WINDOW
        for g in range(ATT_KV_HEADS):
            heads = [ATT_REP * g + r for r in range(ATT_REP)]
            qg = jnp.concatenate(
                [(_rope_head(q_ref[0, qrows, h * LANES:(h + 1) * LANES], cosq, sinq) * scale).astype(BF16)
                 for h in heads], axis=0)
            kg = ks_ref[pl.ds(start, n_loc), g * LANES:(g + 1) * LANES]
            vg = vs_ref[pl.ds(start, n_loc), g * LANES:(g + 1) * LANES]
            kcg = kcs_ref[:, g * LANES:(g + 1) * LANES]
            vcg = vcs_ref[:, g * LANES:(g + 1) * LANES]
            nt = (((1,), (1,)), ((), ()))
            s_loc = lax.dot_general(qg, kg, nt, preferred_element_type=F32)
            s_ctx = lax.dot_general(qg, kcg, nt, preferred_element_type=F32)
            s_loc = jnp.where(in_window, s_loc, NEG_INF)
            sink_col = jnp.zeros((rows, 1), F32)
            for r, h in enumerate(heads):
                sink_col = jnp.where(head_of_row == r, sink_ref[h], sink_col)
            m = jnp.maximum(jnp.maximum(jnp.max(s_loc, axis=1, keepdims=True),
                                        jnp.max(s_ctx, axis=1, keepdims=True)), sink_col)
            p_loc = jnp.exp(s_loc - m)
            p_ctx = jnp.exp(s_ctx - m)
            den = (jnp.sum(p_loc, axis=1, keepdims=True) + jnp.sum(p_ctx, axis=1, keepdims=True)
                   + jnp.exp(sink_col - m))
            o = (jnp.dot(p_loc.astype(BF16), vg, preferred_element_type=F32)
                 + jnp.dot(p_ctx.astype(BF16), vcg, preferred_element_type=F32)) / den
            for r, h in enumerate(heads):
                o_ref[0, qrows, h * LANES:(h + 1) * LANES] = o[r * BLOCK:(r + 1) * BLOCK].astype(BF16)


def _attention(p_lat, kv_ctx, sink, cos_tab, sin_tab):
    b, n, _ = p_lat.shape
    n_ctx = kv_ctx.shape[1]
    grid_spec = pltpu.PrefetchScalarGridSpec(
        num_scalar_prefetch=1,
        grid=(b, n // ATT_TQ),
        in_specs=[
            pl.BlockSpec((1, ATT_TQ, QPAD), lambda bi, i, s: (bi, i, COL_Q // QPAD)),
            pl.BlockSpec((1, n, KVPAD), lambda bi, i, s: (bi, 0, COL_K // KVPAD)),
            pl.BlockSpec((1, n, KVPAD), lambda bi, i, s: (bi, 0, COL_V // KVPAD)),
            pl.BlockSpec((1, n_ctx, 2 * KVPAD), lambda bi, i, s: (bi, 0, 0)),
            pl.BlockSpec((ATT_TQ, LANES), lambda bi, i, s: (i, 0)),
            pl.BlockSpec((ATT_TQ, LANES), lambda bi, i, s: (i, 0)),
            pl.BlockSpec((n, LANES), lambda bi, i, s: (0, 0)),
            pl.BlockSpec((n, LANES), lambda bi, i, s: (0, 0)),
        ],
        out_specs=pl.BlockSpec((1, ATT_TQ, QPAD), lambda bi, i, s: (bi, i, 0)),
        scratch_shapes=[
            pltpu.VMEM((n, KVPAD), BF16),
            pltpu.VMEM((n, KVPAD), BF16),
            pltpu.VMEM((n_ctx, KVPAD), BF16),
            pltpu.VMEM((n_ctx, KVPAD), BF16),
        ],
    )
    return pl.pallas_call(
        _attn_body,
        grid_spec=grid_spec,
        out_shape=jax.ShapeDtypeStruct((b, n, QPAD), BF16),
        compiler_params=_cparams(("parallel", "arbitrary")),
        name="window_attention",
    )(sink, p_lat, p_lat, p_lat, kv_ctx, cos_tab, sin_tab, cos_tab, sin_tab)


def _rope_tables(n):
    rows = n // GRID_W
    row = jnp.repeat(jnp.arange(rows, dtype=F32), GRID_W)
    col = jnp.tile(jnp.arange(GRID_W, dtype=F32), rows)
    inv = ROPE_BASE ** (-jnp.arange(ROPE_FREQS, dtype=F32) / ROPE_FREQS)
    ang_r = row[:, None] * inv
    ang_c = col[:, None] * inv
    pad1 = jnp.ones((n, LANES - HEAD_DIM), F32)
    pad0 = jnp.zeros((n, LANES - HEAD_DIM), F32)
    cos = jnp.concatenate([jnp.cos(ang_r), jnp.cos(ang_r), jnp.cos(ang_c), jnp.cos(ang_c), pad1], axis=1)
    sin = jnp.concatenate([-jnp.sin(ang_r), jnp.sin(ang_r), -jnp.sin(ang_c), jnp.sin(ang_c), pad0], axis=1)
    return cos, sin


def _layer_norm(r, g, b):
    mu = jnp.mean(r, axis=-1, keepdims=True)
    var = jnp.mean(jnp.square(r - mu), axis=-1, keepdims=True)
    return (r - mu) * lax.rsqrt(var + LN_EPS) * g + b


def _outproj_body(hy_ref, att_ref, x_ref, wh_ref, wa_ref, g1_ref, sc2_ref, sh2_ref, lg_ref, lb_ref,
                  x1_ref, hq_ref):
    y = (jnp.dot(hy_ref[0].astype(BF16), wh_ref[...], preferred_element_type=F32)
         + jnp.dot(att_ref[0], wa_ref[...], preferred_element_type=F32))
    x1 = _layer_norm(DEEPNORM_ALPHA * x_ref[0] + g1_ref[0] * y, lg_ref[...], lb_ref[...])
    x1_ref[0] = x1
    hq_ref[0] = (x1 * (1.0 + sc2_ref[0]) + sh2_ref[0]).astype(BF16)


def _outproj_ln(hy, att, x, w_hy, w_att, g1, sc2, sh2, ln_g, ln_b, tm=512):
    b, n, d = x.shape
    modspec = pl.BlockSpec((1, 1, d), lambda bi, i: (bi, 0, 0))
    vecspec = pl.BlockSpec((1, d), lambda bi, i: (0, 0))
    return pl.pallas_call(
        _outproj_body,
        grid=(b, n // tm),
        in_specs=[
            pl.BlockSpec((1, tm, HY_WIDTH), lambda bi, i: (bi, i, 0)),
            pl.BlockSpec((1, tm, QPAD), lambda bi, i: (bi, i, 0)),
            pl.BlockSpec((1, tm, d), lambda bi, i: (bi, i, 0)),
            pl.BlockSpec(w_hy.shape, lambda bi, i: (0, 0)),
            pl.BlockSpec(w_att.shape, lambda bi, i: (0, 0)),
            modspec, modspec, modspec, vecspec, vecspec,
        ],
        out_specs=[
            pl.BlockSpec((1, tm, d), lambda bi, i: (bi, i, 0)),
            pl.BlockSpec((1, tm, d), lambda bi, i: (bi, i, 0)),
        ],
        out_shape=[jax.ShapeDtypeStruct((b, n, d), F32), jax.ShapeDtypeStruct((b, n, d), BF16)],
        compiler_params=_cparams(("parallel", "parallel")),
        name="outproj_ln1",
    )(hy, att, x, w_hy, w_att, g1, sc2, sh2, ln_g, ln_b)


PEER_TM = 256
_STAIR = sorted(((i, j) for i in range(PEER_TOPK) for j in range(PEER_TOPK) if (i + 1) * (j + 1) <= PEER_TOPK),
                key=lambda p: p[0] * PEER_TOPK + p[1])
_STAIR_ROWS = -(-len(_STAIR) // 8) * 8
_STAIR_COUNT = [PEER_TOPK // (i + 1) for i in range(PEER_TOPK)]
_STAIR_START = [sum(_STAIR_COUNT[:i]) for i in range(PEER_TOPK)]


def _stair_ids(tm):
    ids = np.full((_STAIR_ROWS,), float(PEER_TOPK * PEER_TOPK), np.float32)
    ids[:len(_STAIR)] = [i * PEER_TOPK + j for i, j in _STAIR]
    return np.repeat(ids[:, None], tm, axis=1)


def _select_topk(problems, write_row):
    tm = problems[0][0].shape[1]

    def step(k, prev):
        new = []
        for p, ((s_ref, ids, pad_id), prev_id) in enumerate(zip(problems, prev)):
            s = jnp.where(ids == prev_id, -jnp.inf, s_ref[...])
            s_ref[...] = s
            m = jnp.max(s, axis=0, keepdims=True)
            win = jnp.min(jnp.where(s == m, ids, pad_id), axis=0, keepdims=True)
            write_row(p, k, m, win)
            new.append(win)
        return tuple(new)

    lax.fori_loop(0, PEER_TOPK, step, tuple(jnp.full((1, tm), -1.0, F32) for _ in problems))


def _rows_by_rank(rank, table_ref):
    out = jnp.zeros(rank.shape, F32)
    for p in range(PEER_TOPK):
        out = jnp.where(rank == float(p), table_ref[p:p + 1, :], out)
    return out


def _peer_topk_body(hq_ref, wq_ref, k1_ref, k2_ref, sid_ref, a_ref, b_ref, g_ref,
                    q_s, s1_s, s2_s, v1_s, i1_s, v2_s, i2_s, c_s, t_s, f_s, ao_s, bo_s, go_s):
    tm = hq_ref.shape[0]
    q = jnp.dot(hq_ref[...], wq_ref[...], preferred_element_type=F32)
    for c in range(2 * PEER_HEADS):
        q_s[c] = q[:, c * LANES:(c + 1) * LANES]
    nt = (((1,), (1,)), ((), ()))
    key_id = lax.broadcasted_iota(jnp.int32, (PEER_KEYS, tm), 0).astype(F32)

    def head(h, carry):
        for half, (kref, s_s) in enumerate(((k1_ref, s1_s), (k2_ref, s2_s))):
            s_s[...] = lax.dot_general(kref[...], q_s[2 * h + half], nt, preferred_element_type=F32,
                                       precision=HIGHEST)

        def write1(p, k, val, idx):
            vs, is_ = ((v1_s, i1_s), (v2_s, i2_s))[p]
            vs[pl.ds(k, 1), :] = val
            is_[pl.ds(k, 1), :] = idx
        _select_topk([(s1_s, key_id, float(PEER_KEYS)), (s2_s, key_id, float(PEER_KEYS))], write1)

        c_s[...] = jnp.full(c_s.shape, -jnp.inf, F32)
        for i in range(PEER_TOPK):
            r0, cnt = _STAIR_START[i], _STAIR_COUNT[i]
            c_s[r0:r0 + cnt, :] = v1_s[i:i + 1, :] + v2_s[0:cnt, :]

        def write2(p, k, val, idx):
            t_s[pl.ds(k, 1), :] = val
            f_s[pl.ds(k, 1), :] = idx
        _select_topk([(c_s, sid_ref[...], float(PEER_TOPK * PEER_TOPK))], write2)

        flat = f_s[...]
        rank1 = jnp.floor(flat * (1.0 / PEER_TOPK))
        rank2 = flat - rank1 * PEER_TOPK
        rows = pl.ds(pl.multiple_of(h * PEER_TOPK, PEER_TOPK), PEER_TOPK)
        ao_s[rows, :] = _rows_by_rank(rank1, i1_s)
        bo_s[rows, :] = _rows_by_rank(rank2, i2_s)
        t = t_s[...]
        e = jnp.exp(t - jnp.max(t, axis=0, keepdims=True))
        go_s[rows, :] = e / jnp.sum(e, axis=0, keepdims=True)
        return carry

    lax.fori_loop(0, PEER_HEADS, head, 0)
    a_ref[...] = ao_s[...].T
    b_ref[...] = bo_s[...].T
    g_ref[...] = go_s[...].T


def _peer_topk(hq, wq, keys1, keys2):
    t, d = hq.shape
    tm = PEER_TM
    hk = PEER_HEADS * PEER_TOPK
    out = jax.ShapeDtypeStruct((t, hk), F32)
    ospec = pl.BlockSpec((tm, hk), lambda i: (i, 0))
    sid = _stair_ids(tm)
    return pl.pallas_call(
        _peer_topk_body,
        grid=(t // tm,),
        in_specs=[
            pl.BlockSpec((tm, d), lambda i: (i, 0)),
            pl.BlockSpec(wq.shape, lambda i: (0, 0)),
            pl.BlockSpec(keys1.shape, lambda i: (0, 0)),
            pl.BlockSpec(keys2.shape, lambda i: (0, 0)),
            pl.BlockSpec(sid.shape, lambda i: (0, 0)),
        ],
        out_specs=[ospec, ospec, ospec],
        out_shape=[out, out, out],
        scratch_shapes=[
            pltpu.VMEM((2 * PEER_HEADS, tm, LANES), F32),
            pltpu.VMEM((PEER_KEYS, tm), F32), pltpu.VMEM((PEER_KEYS, tm), F32),
            pltpu.VMEM((PEER_TOPK, tm), F32), pltpu.VMEM((PEER_TOPK, tm), F32),
            pltpu.VMEM((PEER_TOPK, tm), F32), pltpu.VMEM((PEER_TOPK, tm), F32),
            pltpu.VMEM((_STAIR_ROWS, tm), F32),
            pltpu.VMEM((PEER_TOPK, tm), F32), pltpu.VMEM((PEER_TOPK, tm), F32),
            pltpu.VMEM((hk, tm), F32), pltpu.VMEM((hk, tm), F32), pltpu.VMEM((hk, tm), F32),
        ],
        compiler_params=_cparams(("parallel",)),
        name="peer_topk",
    )(hq, wq, keys1, keys2, sid)


PEER_TE = 2048
PEER_GROUPS = PEER_TE // PEER_KEYS
PEER_SCORE_TM = 512
PEER_VALUE_TM = 512
PEER_VALUE_TE = 1024
GATE_PITCH = PEER_KEYS + 8
GATE_UNROLL = 32


def _peer_scores_body(hq_ref, a_ref, b_ref, ut_ref, o_ref):
    e = pl.program_id(0)
    s = jnp.dot(hq_ref[...], ut_ref[...], preferred_element_type=F32)
    b_idx = b_ref[...].astype(jnp.int32)
    a_val = a_ref[...]
    cur = jnp.zeros(a_val.shape, F32)
    for jj in range(PEER_GROUPS):
        cand = jnp.take_along_axis(s[:, jj * LANES:(jj + 1) * LANES], b_idx, axis=1)
        cur = jnp.where(a_val == (e * PEER_GROUPS + jj).astype(F32), cand, cur)
    o_ref[0] = cur


def _peer_scores(hq, a, b, u_t):
    t, d = hq.shape
    tm = PEER_SCORE_TM
    hk = a.shape[1]
    n_e = u_t.shape[1] // PEER_TE
    tok = lambda e, i: (i, 0)
    return pl.pallas_call(
        _peer_scores_body,
        grid=(n_e, t // tm),
        in_specs=[
            pl.BlockSpec((tm, d), tok),
            pl.BlockSpec((tm, hk), tok), pl.BlockSpec((tm, hk), tok),
            pl.BlockSpec((d, PEER_TE), lambda e, i: (0, e)),
        ],
        out_specs=pl.BlockSpec((1, tm, hk), lambda e, i: (e, i, 0)),
        out_shape=jax.ShapeDtypeStruct((n_e, t, hk), F32),
        compiler_params=_cparams(("arbitrary", "arbitrary")),
        name="peer_scores",
    )(hq, a, b, u_t)


def _peer_values_body(ss_ref, a_ref, b_ref, g_ref, v_ref, x1_ref, g2_ref, lg_ref, lb_ref, o_ref,
                      w_s, hd_s, acc_s):
    tm = a_ref.shape[0]
    n_e = pl.num_programs(1)
    e = pl.program_id(1)
    nt = (((1,), (1,)), ((), ()))

    @pl.when(e == 0)
    def _gates():
        s_sel = jnp.sum(ss_ref[...], axis=0)
        act = 0.5 * s_sel * (1.0 + lax.erf(s_sel * (2.0 ** -0.5)))
        w_s[...] = g_ref[...] * act
        sub = lax.broadcasted_iota(jnp.int32, (PEER_KEYS, LANES), 0).astype(F32)

        def token(t, carry):
            a_row = a_ref[pl.ds(t, 1), :]
            b_row = b_ref[pl.ds(t, 1), :]
            w_row = w_s[pl.ds(t, 1), :]
            lhs = jnp.where(sub == a_row, w_row, 0.0).astype(BF16)
            rhs = jnp.where(sub == b_row, 1.0, 0.0).astype(BF16)
            tile = lax.dot_general(lhs, rhs, nt, preferred_element_type=F32)
            hd_s[pl.ds(pl.multiple_of(t * GATE_PITCH, 8), PEER_KEYS), :] = tile
            return carry
        lax.fori_loop(0, tm, token, 0, unroll=GATE_UNROLL)
        acc_s[...] = jnp.zeros(acc_s.shape, F32)

    groups = v_ref.shape[0] // PEER_KEYS
    j0 = e * groups
    lhs = jnp.concatenate(
        [hd_s[pl.ds(j0 + jj, tm, stride=GATE_PITCH), :].astype(BF16) for jj in range(groups)], axis=1)
    acc_s[...] += jnp.dot(lhs, v_ref[...], preferred_element_type=F32)

    @pl.when(e == n_e - 1)
    def _finish():
        o_ref[...] = _layer_norm(DEEPNORM_ALPHA * x1_ref[...] + g2_ref[0] * acc_s[...], lg_ref[...], lb_ref[...])


def _peer_values(ssel, a, b, g, v_tab, x1, g2, ln_g, ln_b, tokens_per_batch):
    t, d = x1.shape
    tm = PEER_VALUE_TM
    hk = a.shape[1]
    n_e = v_tab.shape[0] // PEER_VALUE_TE
    tiles_per_batch = tokens_per_batch // tm
    tok = lambda i, e: (i, 0)
    one = pl.Buffered(1)
    return pl.pallas_call(
        _peer_values_body,
        grid=(t // tm, n_e),
        in_specs=[
            pl.BlockSpec((ssel.shape[0], tm, hk), lambda i, e: (0, i, 0), pipeline_mode=one),
            pl.BlockSpec((tm, hk), tok), pl.BlockSpec((tm, hk), tok), pl.BlockSpec((tm, hk), tok),
            pl.BlockSpec((PEER_VALUE_TE, d), lambda i, e: (e, 0)),
            pl.BlockSpec((tm, d), tok, pipeline_mode=one),
            pl.BlockSpec((1, 1, d), lambda i, e: (i // tiles_per_batch, 0, 0)),
            pl.BlockSpec((1, d), lambda i, e: (0, 0)),
            pl.BlockSpec((1, d), lambda i, e: (0, 0)),
        ],
        out_specs=pl.BlockSpec((tm, d), tok, pipeline_mode=one),
        out_shape=jax.ShapeDtypeStruct((t, d), F32),
        scratch_shapes=[
            pltpu.VMEM((tm, hk), F32),
            pltpu.VMEM((tm * GATE_PITCH, LANES), F32),
            pltpu.VMEM((tm, d), F32),
        ],
        compiler_params=_cparams(("parallel", "arbitrary")),
        name="peer_values",
    )(ssel, a, b, g, v_tab, x1, g2, ln_g, ln_b)


FFT_N2 = 128
FFT_PITCH = FFT_N2 + 8
HY_POS_PAD = 32
FFT_UNROLL = 4


def _dft_constants(n):
    big = 2 * n
    n1c = big // FFT_N2
    half = n1c // 2
    k1 = np.arange(n1c)[:, None]
    n1 = np.arange(half)[None, :]
    ang = 2 * np.pi * k1 * n1 / n1c
    c, s = np.cos(ang), np.sin(ang)
    f1 = np.block([[c, s], [-s, c]])
    k2 = np.arange(FFT_N2)[:, None]
    n2 = np.arange(FFT_N2)[None, :]
    ang = 2 * np.pi * k2 * n2 / FFT_N2
    c, s = np.cos(ang), np.sin(ang)
    d3 = np.block([[c, s], [-s, c]])
    d3i = np.block([[c, -s], [s, c]])
    ang = 2 * np.pi * n1.T * k1.T / n1c
    c, s = np.cos(ang), np.sin(ang)
    f3 = np.block([[c, -s], [s, c]]) / big
    ang = 2 * np.pi * (np.arange(n1c)[:, None] * np.arange(FFT_N2)[None, :]) / big
    twr = np.repeat(np.cos(ang).reshape(-1, 1), LANES, axis=1)
    twi = np.repeat(-np.sin(ang).reshape(-1, 1), LANES, axis=1)
    as32 = lambda a: np.asarray(a, np.float32)

    def split(a):
        parts, rest = [], np.asarray(a, np.float64)
        for _ in range(3):
            p = np.asarray(rest, BF16)
            parts.append(p)
            rest = rest - p.astype(np.float64)
        return np.stack(parts)
    return split(f1), split(d3), split(d3i), split(f3), as32(twr), as32(twi)


def _hdot(a, b):
    return jnp.dot(a, b, preferred_element_type=F32, precision=HIGHEST)


def _dft_dot(m_ref, x):
    x_hi = x.astype(BF16)
    x_lo = (x - x_hi.astype(F32)).astype(BF16)
    m_hi = m_ref[0]
    dot = functools.partial(jnp.dot, preferred_element_type=F32)
    return (dot(m_hi, x_hi) + dot(m_hi, x_lo)) + (dot(m_ref[1], x_hi) + dot(m_ref[2], x_hi))


def _fft_stage1(u_re, u_im, f1_ref, a_re, a_im):
    half = f1_ref.shape[2] // 2
    n1c = f1_ref.shape[1] // 2

    def column(n2):
        return jnp.concatenate([u_re[pl.ds(n2, half, stride=FFT_PITCH), :],
                                u_im[pl.ds(n2, half, stride=FFT_PITCH), :]], axis=0)

    def body(i, carry):
        a = _dft_dot(f1_ref, jnp.concatenate([column(2 * i), column(2 * i + 1)], axis=1))
        for s in range(2):
            a_re[pl.ds(2 * i + s, n1c, stride=FFT_PITCH), :] = a[:n1c, s * LANES:(s + 1) * LANES]
            a_im[pl.ds(2 * i + s, n1c, stride=FFT_PITCH), :] = a[n1c:, s * LANES:(s + 1) * LANES]
        return carry
    lax.fori_loop(0, FFT_N2 // 2, body, 0, unroll=FFT_UNROLL)


def _twiddled_pair(a_re, a_im, twr_ref, twi_ref, i):
    t0 = pl.multiple_of(2 * i * FFT_N2, 2 * FFT_N2)
    rows = [pl.multiple_of((2 * i + s) * FFT_PITCH, 8) for s in range(2)]
    side = lambda ref, r: jnp.concatenate([ref[pl.ds(r[0], FFT_N2), :], ref[pl.ds(r[1], FFT_N2), :]], axis=1)
    tws = [pl.multiple_of(t0 + s * FFT_N2, FFT_N2) for s in range(2)]
    return rows, side(a_re, rows), side(a_im, rows), side(twr_ref, tws), side(twi_ref, tws)


def _fft_conv_middle(a_re, a_im, d3_ref, d3i_ref, twr_ref, twi_ref, hre_ref, him_ref):
    n1c = hre_ref.shape[1]

    def body(i, carry):
        rows, ar, ai, twr, twi = _twiddled_pair(a_re, a_im, twr_ref, twi_ref, i)
        x = _dft_dot(d3_ref, jnp.concatenate([ar * twr - ai * twi, ar * twi + ai * twr], axis=0))
        xr, xi = x[:FFT_N2], x[FFT_N2:]
        hr = jnp.concatenate([hre_ref[0, 2 * i], hre_ref[0, 2 * i + 1]], axis=1)
        hi = jnp.concatenate([him_ref[0, 2 * i], him_ref[0, 2 * i + 1]], axis=1)
        y = _dft_dot(d3i_ref, jnp.concatenate([xr * hr - xi * hi, xr * hi + xi * hr], axis=0))
        br, bi = y[:FFT_N2], y[FFT_N2:]
        out_re = br * twr + bi * twi
        out_im = bi * twr - br * twi
        for s in range(2):
            a_re[pl.ds(rows[s], FFT_N2), :] = out_re[:, s * LANES:(s + 1) * LANES]
            a_im[pl.ds(rows[s], FFT_N2), :] = out_im[:, s * LANES:(s + 1) * LANES]
        return carry
    lax.fori_loop(0, n1c // 2, body, 0, unroll=FFT_UNROLL)


def _fft_stage_inv(a_re, a_im, f3_ref, y_re, y_im):
    half = f3_ref.shape[1] // 2
    n1c = f3_ref.shape[2] // 2

    def column(n2):
        return jnp.concatenate([a_re[pl.ds(n2, n1c, stride=FFT_PITCH), :],
                                a_im[pl.ds(n2, n1c, stride=FFT_PITCH), :]], axis=0)

    def body(i, carry):
        y = _dft_dot(f3_ref, jnp.concatenate([column(2 * i), column(2 * i + 1)], axis=1))
        for s in range(2):
            y_re[pl.ds(2 * i + s, half, stride=FFT_PITCH), :] = y[:half, s * LANES:(s + 1) * LANES]
            y_im[pl.ds(2 * i + s, half, stride=FFT_PITCH), :] = y[half:, s * LANES:(s + 1) * LANES]
        return carry
    lax.fori_loop(0, FFT_N2 // 2, body, 0, unroll=FFT_UNROLL)


def _short_conv_chunk(x_ref, bi, j, nblk, w_ref, b_ref):
    r0 = j * FFT_N2
    cur = x_ref[bi, r0:r0 + FFT_N2, :]
    row = lax.broadcasted_iota(jnp.int32, cur.shape, 0)
    if j == 0:
        prev = jnp.where(row == 0, 0.0, pltpu.roll(cur, 1, 0))
    else:
        prev = x_ref[bi, r0 - 1:r0 + FFT_N2 - 1, :]
    if j == nblk - 1:
        nxt = jnp.where(row == FFT_N2 - 1, 0.0, pltpu.roll(cur, FFT_N2 - 1, 0))
    else:
        nxt = x_ref[bi, r0 + 1:r0 + FFT_N2 + 1, :]
    return prev * w_ref[0:1, :] + cur * w_ref[1:2, :] + nxt * w_ref[2:3, :] + b_ref[...]


def _hy_conv_body(conv_a, a_ref, g_ref, wa_ref, ba_ref, wg_ref, bg_ref, skip_ref, hre_ref, him_ref,
                  f1_ref, d3_ref, d3i_ref, f3_ref, twr_ref, twi_ref, o_ref,
                  u_re, u_im, a_re, a_im, y_re, y_im):
    nblk = a_ref.shape[1] // FFT_N2
    for bi, dst in ((0, u_re), (1, u_im)):
        for j in range(nblk):
            if conv_a:
                blk = _short_conv_chunk(a_ref, bi, j, nblk, wa_ref, ba_ref)
            else:
                blk = a_ref[bi, j * FFT_N2:(j + 1) * FFT_N2, :]
            dst[j * FFT_PITCH:j * FFT_PITCH + FFT_N2, :] = blk
    _fft_stage1(u_re, u_im, f1_ref, a_re, a_im)
    _fft_conv_middle(a_re, a_im, d3_ref, d3i_ref, twr_ref, twi_ref, hre_ref, him_ref)
    _fft_stage_inv(a_re, a_im, f3_ref, y_re, y_im)
    skip = skip_ref[...]
    for bi, (ysrc, usrc) in enumerate(((y_re, u_re), (y_im, u_im))):
        for j in range(nblk):
            rows = slice(j * FFT_PITCH, j * FFT_PITCH + FFT_N2)
            gate = _short_conv_chunk(g_ref, bi, j, nblk, wg_ref, bg_ref)
            o_ref[bi, j * FFT_N2:(j + 1) * FFT_N2, :] = gate * (ysrc[rows, :] + usrc[rows, :] * skip)


def _hy_conv(a, a_col, g, g_col, conv_w, conv_b, skip, hre, him, order, consts, conv_a):
    b, n, _ = a.shape
    f1, d3, d3i, f3, twr, twi = consts
    n1c = f1.shape[1] // 2
    half = n1c // 2
    w = LANES
    tiles = HY_WIDTH // w
    one = pl.Buffered(1)
    data = lambda col: pl.BlockSpec((2, n, w), lambda ct, p: (p, 0, col + ct), pipeline_mode=one)
    wspec = lambda col: pl.BlockSpec((3, w), lambda ct, p: (0, col + ct))
    bspec = lambda col: pl.BlockSpec((1, w), lambda ct, p: (0, col + ct))
    hspec = pl.BlockSpec((1, n1c, FFT_N2, w), lambda ct, p: (order, 0, 0, ct), pipeline_mode=one)
    cs = lambda arr: pl.BlockSpec(arr.shape, lambda ct, p: (0,) * arr.ndim, pipeline_mode=one)
    a_wcol = a_col if conv_a else g_col
    return pl.pallas_call(
        functools.partial(_hy_conv_body, conv_a),
        grid=(tiles, b // 2),
        in_specs=[data(a_col), data(g_col), wspec(a_wcol), bspec(a_wcol), wspec(g_col), bspec(g_col),
                  pl.BlockSpec((1, w), lambda ct, p: (0, ct)), hspec, hspec,
                  cs(f1), cs(d3), cs(d3i), cs(f3), cs(twr), cs(twi)],
        out_specs=pl.BlockSpec((2, n, w), lambda ct, p: (p, 0, ct), pipeline_mode=one),
        out_shape=jax.ShapeDtypeStruct((b, n, HY_WIDTH), F32),
        scratch_shapes=[pltpu.VMEM((half * FFT_PITCH, w), F32), pltpu.VMEM((half * FFT_PITCH, w), F32),
                        pltpu.VMEM((n1c * FFT_PITCH, w), F32), pltpu.VMEM((n1c * FFT_PITCH, w), F32),
                        pltpu.VMEM((half * FFT_PITCH, w), F32), pltpu.VMEM((half * FFT_PITCH, w), F32)],
        compiler_params=_cparams(("parallel", "arbitrary")),
        name=f"hyena_conv{order + 1}",
    )(a, g, conv_w, conv_b, conv_w, conv_b, skip, hre, him, f1, d3, d3i, f3, twr, twi)


def _hy_filter_body(z_ref, w1_ref, b1_ref, f1_ref, w2_ref, b2_ref, f2_ref, w3_ref, b3_ref, dl_ref, o_ref):
    z = z_ref[...]
    h = jnp.sin(f1_ref[...] * (_hdot(z, w1_ref[...]) + b1_ref[...]))
    h = jnp.sin(f2_ref[...] * (_hdot(h, w2_ref[...]) + b2_ref[...]))
    h = _hdot(h, w3_ref[...]) + b3_ref[...]
    t = z[:, 0:1]
    o_ref[...] = h * (jnp.exp(-t * dl_ref[...]) + HY_WINDOW_SHIFT)


def _hy_filters(n, w1, b1, fr1, w2, b2, fr2, w3, b3):
    t = jnp.linspace(0.0, 1.0, n, dtype=F32)[:, None]
    wv = 2.0 * math.pi * jnp.arange(n, dtype=F32)[:, None] / n
    bands = jnp.linspace(1e-4, HY_BANDS - 1, HY_BANDS, dtype=F32)[None, :]
    z = jnp.concatenate([t, jnp.cos(bands * wv), -jnp.sin(bands * wv)], axis=-1)
    pos = z.shape[1]
    z = jnp.pad(z, ((0, 0), (0, HY_POS_PAD - pos)))
    w1p = jnp.pad(w1, ((0, HY_POS_PAD - pos), (0, 0)))
    min_decay = math.log(HY_DECAY_TARGET) / HY_SLOW_DECAY_PCT
    max_decay = math.log(HY_DECAY_TARGET) / HY_FAST_DECAY_PCT
    deltas = jnp.abs(jnp.linspace(min_decay, max_decay, HY_WIDTH, dtype=F32))
    n_out = w3.shape[1]
    dl = jnp.tile(deltas, n_out // HY_WIDTH)[None, :]
    tn = 512
    hid = w2.shape[0]
    full = lambda shape: pl.BlockSpec(shape, lambda i: (0, 0))
    return pl.pallas_call(
        _hy_filter_body,
        grid=(n // tn,),
        in_specs=[pl.BlockSpec((tn, HY_POS_PAD), lambda i: (i, 0)), full((HY_POS_PAD, hid)), full((1, hid)),
                  full((1, hid)), full((hid, hid)), full((1, hid)), full((1, hid)), full((hid, n_out)),
                  full((1, n_out)), full((1, n_out))],
        out_specs=pl.BlockSpec((tn, n_out), lambda i: (i, 0)),
        out_shape=jax.ShapeDtypeStruct((n, n_out), F32),
        compiler_params=_cparams(("parallel",)),
        name="hyena_filters",
    )(z, w1p, b1[None], fr1[None], w2, b2[None], fr2[None], w3, b3[None], dl)


def _hy_spectrum_body(f_ref, b_ref, f1_ref, d3_ref, twr_ref, twi_ref, hre_ref, him_ref, u_re, u_im, a_re, a_im):
    nblk = f_ref.shape[0] // FFT_N2
    n1c = hre_ref.shape[1]
    for part, out_ref in enumerate((hre_ref, him_ref)):
        for j in range(nblk):
            f = f_ref[j * FFT_N2:(j + 1) * FFT_N2, :]
            bw = b_ref[j * FFT_N2:(j + 1) * FFT_N2, :]
            if j == 0:
                row = lax.broadcasted_iota(jnp.int32, bw.shape, 0)
                bw = jnp.where(row == 0, 0.0, bw)
            rows = slice(j * FFT_PITCH, j * FFT_PITCH + FFT_N2)
            u_re[rows, :] = f + bw if part == 0 else f - bw
            u_im[rows, :] = jnp.zeros((FFT_N2, LANES), F32)
        _fft_stage1(u_re, u_im, f1_ref, a_re, a_im)

        def body(i, carry, part=part, out_ref=out_ref):
            _, ar, ai, twr, twi = _twiddled_pair(a_re, a_im, twr_ref, twi_ref, i)
            x = _dft_dot(d3_ref, jnp.concatenate([ar * twr - ai * twi, ar * twi + ai * twr], axis=0))
            x = x[:FFT_N2] if part == 0 else x[FFT_N2:]
            for s in range(2):
                out_ref[0, 2 * i + s] = x[:, s * LANES:(s + 1) * LANES]
            return carry
        lax.fori_loop(0, n1c // 2, body, 0, unroll=FFT_UNROLL)


def _hy_spectrum(h, consts):
    n = h.shape[0]
    f1, d3, _, _, twr, twi = consts
    n1c = f1.shape[1] // 2
    half = n1c // 2
    tiles = HY_WIDTH // LANES
    cs = lambda arr: pl.BlockSpec(arr.shape, lambda o, ct: (0,) * arr.ndim)
    out = jax.ShapeDtypeStruct((HY_ORDER, n1c, FFT_N2, HY_WIDTH), F32)
    ospec = pl.BlockSpec((1, n1c, FFT_N2, LANES), lambda o, ct: (o, 0, 0, ct))
    return pl.pallas_call(
        _hy_spectrum_body,
        grid=(HY_ORDER, tiles),
        in_specs=[pl.BlockSpec((n, LANES), lambda o, ct: (0, o * tiles + ct)),
                  pl.BlockSpec((n, LANES), lambda o, ct: (0, (HY_ORDER + o) * tiles + ct)),
                  cs(f1), cs(d3), cs(twr), cs(twi)],
        out_specs=[ospec, ospec],
        out_shape=[out, out],
        scratch_shapes=[pltpu.VMEM((half * FFT_PITCH, LANES), F32), pltpu.VMEM((half * FFT_PITCH, LANES), F32),
                        pltpu.VMEM((n1c * FFT_PITCH, LANES), F32), pltpu.VMEM((n1c * FFT_PITCH, LANES), F32)],
        compiler_params=_cparams(("parallel", "parallel")),
        name="hyena_spectrum",
    )(h, h, f1, d3, twr, twi)


def _hyena(p_lat, conv_w, conv_b, w1, b1, fr1, w2, b2, fr2, w3, b3, skip):
    n = p_lat.shape[1]
    consts = _dft_constants(n)
    h = _hy_filters(n, w1, b1, fr1, w2, b2, fr2, w3, b3)
    hre, him = _hy_spectrum(h, consts)
    tiles = HY_WIDTH // LANES
    col = COL_HY // LANES
    cw = jnp.pad(conv_w, ((0, 0), (COL_HY, 0)))
    cb = jnp.pad(conv_b[None], ((0, 0), (COL_HY, 0)))
    y = _hy_conv(p_lat, col, p_lat, col + tiles, cw, cb, skip[0][None], hre, him, 0, consts, True)
    return _hy_conv(y, 0, p_lat, col + 2 * tiles, cw, cb, skip[1][None], hre, him, 1, consts, False)


def _pad_heads(w, heads):
    d = w.shape[0]
    w = w.reshape(d, heads, HEAD_DIM)
    w = jnp.pad(w, ((0, 0), (0, 0), (0, LANES - HEAD_DIM)))
    return w.reshape(d, heads * LANES)


def kernel(x, c, ctx, c_ctx, w_mod, b_mod, w_in, hy_conv_w, hy_conv_b, hy_f_w1, hy_f_b1, hy_f_freq1, hy_f_w2,
           hy_f_b2, hy_f_freq2, hy_f_w3, hy_f_b3, hy_skip, attn_sink, w_out, ln1_g, ln1_b, peer_wq, peer_keys1,
           peer_keys2, peer_u, peer_v, ln2_g, ln2_b):
    b, n, d = x.shape
    l = 0
    cc = jnp.concatenate([c, c_ctx[None], jnp.zeros((8 - b - 1, d), F32)], axis=0)
    mod = _modulation(cc, w_mod[l], b_mod[l][None])
    mod_lat = mod[:b].reshape(b, 6, 1, d)
    sh1, sc1, g1, sh2, sc2, g2 = (mod_lat[:, i] for i in range(6))
    mod_c = mod[b].reshape(6, 1, 1, d)
    csh1, csc1 = mod_c[0], mod_c[1]

    w = w_in[l]
    w_q = _pad_heads(w[:, PROJ_HY:KV_START], ATT_HEADS)
    w_k = _pad_heads(w[:, KV_START:KV_START + PROJ_KV], ATT_KV_HEADS)
    w_v = _pad_heads(w[:, KV_START + PROJ_KV:], ATT_KV_HEADS)
    w_pad = jnp.concatenate([w_q, w[:, :PROJ_HY], w_k, w_v], axis=1).astype(BF16)
    w_kv = jnp.concatenate([w_k, w_v], axis=1).astype(BF16)

    p_lat = _mod_matmul(x, sc1, sh1, w_pad, 512, "in_proj")
    kv_ctx = _mod_matmul(ctx, csc1, csh1, w_kv, ctx.shape[1], "ctx_kv_proj")

    cos_tab, sin_tab = _rope_tables(n)
    att = _attention(p_lat, kv_ctx, attn_sink[l], cos_tab, sin_tab)

    hy = _hyena(p_lat, hy_conv_w[l], hy_conv_b[l], hy_f_w1[l], hy_f_b1[l], hy_f_freq1[l], hy_f_w2[l], hy_f_b2[l],
                hy_f_freq2[l], hy_f_w3[l], hy_f_b3[l], hy_skip[l])

    wo = w_out[l]
    w_o_hy = wo[:HY_WIDTH].astype(BF16)
    w_o_att = jnp.pad(wo[HY_WIDTH:].reshape(ATT_HEADS, HEAD_DIM, d),
                      ((0, 0), (0, LANES - HEAD_DIM), (0, 0))).reshape(QPAD, d).astype(BF16)
    x1, hq = _outproj_ln(hy, att, x, w_o_hy, w_o_att, g1, sc2, sh2, ln1_g[l][None], ln1_b[l][None])

    hq2 = hq.reshape(b * n, d)
    a_idx, b_idx, gate = _peer_topk(hq2, peer_wq[l].astype(BF16), peer_keys1[l], peer_keys2[l])
    u_t = peer_u[l].astype(BF16).T
    s_sel = _peer_scores(hq2, a_idx, b_idx, u_t)
    out = _peer_values(s_sel, a_idx, b_idx, gate, peer_v[l].astype(BF16), x1.reshape(b * n, d), g2,
                       ln2_g[l][None], ln2_b[l][None], n)
    return out.reshape(b, n, d)
```

```python
import functools
import math

import jax
import jax.numpy as jnp
import numpy as np
from jax import lax
from jax.experimental import pallas as pl
from jax.experimental.pallas import tpu as pltpu

F32 = jnp.float32
BF16 = jnp.bfloat16
HIGHEST = lax.Precision.HIGHEST

LANES = 128
VMEM_LIMIT = 60000 * 1024

D_MODEL = 1024
GRID_W = 64
HY_WIDTH = 512
HY_ORDER = 2
HY_BANDS = 8
HY_DECAY_TARGET = 1e-2
HY_FAST_DECAY_PCT = 0.3
HY_SLOW_DECAY_PCT = 1.5
HY_WINDOW_SHIFT = 0.05
ATT_HEADS = 8
ATT_KV_HEADS = 2
ATT_REP = ATT_HEADS // ATT_KV_HEADS
HEAD_DIM = 64
WINDOW = 128
BLOCK = 128
ROPE_BASE = 10000.0
ROPE_FREQS = HEAD_DIM // 4
PROJ_HY = (HY_ORDER + 1) * HY_WIDTH
PROJ_Q = ATT_HEADS * HEAD_DIM
PROJ_KV = ATT_KV_HEADS * HEAD_DIM
KV_START = PROJ_HY + PROJ_Q
PEER_KEYS = 128
PEER_HEADS = 8
PEER_QDIM = 256
PEER_TOPK = 16
LN_EPS = 1e-5
NEG_INF = -1e30
DEPTH = 1
DEEPNORM_ALPHA = (2.0 * DEPTH) ** 0.25

QPAD = ATT_HEADS * LANES
KVPAD = ATT_KV_HEADS * LANES
COL_Q = 0
COL_HY = QPAD
COL_K = QPAD + PROJ_HY
COL_V = COL_K + KVPAD
PROJ_PAD = COL_V + KVPAD


def _cparams(sem):
    return pltpu.CompilerParams(dimension_semantics=sem, vmem_limit_bytes=VMEM_LIMIT)


def _mod_body(c_ref, w_ref, b_ref, o_ref):
    c = c_ref[...]
    a = c * jax.nn.sigmoid(c)
    o_ref[...] = jnp.dot(a, w_ref[...], preferred_element_type=F32, precision=HIGHEST) + b_ref[...]


def _modulation(cc, w_mod, b_mod):
    rows, d = cc.shape
    n_out = w_mod.shape[1]
    tn = 1536
    return pl.pallas_call(
        _mod_body,
        grid=(n_out // tn,),
        in_specs=[
            pl.BlockSpec((rows, d), lambda j: (0, 0)),
            pl.BlockSpec((d, tn), lambda j: (0, j)),
            pl.BlockSpec((1, tn), lambda j: (0, j)),
        ],
        out_specs=pl.BlockSpec((rows, tn), lambda j: (0, j)),
        out_shape=jax.ShapeDtypeStruct((rows, n_out), F32),
        compiler_params=_cparams(("arbitrary",)),
        name="modulation",
    )(cc, w_mod, b_mod)


def _mod_matmul_body(x_ref, sc_ref, sh_ref, w_ref, o_ref):
    h = x_ref[0] * (1.0 + sc_ref[0]) + sh_ref[0]
    o_ref[0] = jnp.dot(h.astype(BF16), w_ref[...], preferred_element_type=F32)


def _mod_matmul(x, sc, sh, w, tm, name):
    b, n, d = x.shape
    n_out = w.shape[1]
    per_batch = sc.shape[0] == b
    mod_map = (lambda bi, i: (bi, 0, 0)) if per_batch else (lambda bi, i: (0, 0, 0))
    return pl.pallas_call(
        _mod_matmul_body,
        grid=(b, n // tm),
        in_specs=[
            pl.BlockSpec((1, tm, d), lambda bi, i: (bi, i, 0)),
            pl.BlockSpec((1, 1, d), mod_map),
            pl.BlockSpec((1, 1, d), mod_map),
            pl.BlockSpec((d, n_out), lambda bi, i: (0, 0)),
        ],
        out_specs=pl.BlockSpec((1, tm, n_out), lambda bi, i: (bi, i, 0)),
        out_shape=jax.ShapeDtypeStruct((b, n, n_out), F32),
        compiler_params=_cparams(("parallel", "parallel")),
        name=name,
    )(x, sc, sh, w)


def _rope_head(x, cos, sin_signed):
    lane = lax.broadcasted_iota(jnp.int32, x.shape, 1)
    first_half = (lane % 32) < 16
    partner = jnp.where(first_half, pltpu.roll(x, LANES - 16, 1), pltpu.roll(x, 16, 1))
    return x * cos + partner * sin_signed


ATT_TQ = 512
ROPE_CHUNK = 512


def _attn_body(sink_ref, q_ref, k_ref, v_ref, kvc_ref, cosq_ref, sinq_ref, cosk_ref, sink_tab_ref,
               o_ref, ks_ref, vs_ref, kcs_ref, vcs_ref):
    n = k_ref.shape[1]
    iq = pl.program_id(1)
    scale = HEAD_DIM ** -0.5

    @pl.when(iq == 0)
    def _prepare_keys():
        def chunk(ci, carry):
            r0 = pl.multiple_of(ci * ROPE_CHUNK, ROPE_CHUNK)
            cos = cosk_ref[pl.ds(r0, ROPE_CHUNK), :]
            sin = sink_tab_ref[pl.ds(r0, ROPE_CHUNK), :]
            for g in range(ATT_KV_HEADS):
                kg = k_ref[0, pl.ds(r0, ROPE_CHUNK), g * LANES:(g + 1) * LANES]
                ks_ref[pl.ds(r0, ROPE_CHUNK), g * LANES:(g + 1) * LANES] = _rope_head(kg, cos, sin).astype(BF16)
            vs_ref[pl.ds(r0, ROPE_CHUNK), :] = v_ref[0, pl.ds(r0, ROPE_CHUNK), :].astype(BF16)
            return carry
        lax.fori_loop(0, n // ROPE_CHUNK, chunk, 0)
        kcs_ref[...] = kvc_ref[0, :, 0:KVPAD].astype(BF16)
        vcs_ref[...] = kvc_ref[0, :, KVPAD:2 * KVPAD].astype(BF16)

    n_loc = 3 * BLOCK
    rows = ATT_REP * BLOCK
    row_i = lax.broadcasted_iota(jnp.int32, (rows, n_loc), 0)
    col_i = lax.broadcasted_iota(jnp.int32, (rows, n_loc), 1)
    rel = col_i - (row_i % BLOCK)
    head_of_row = lax.broadcasted_iota(jnp.int32, (rows, 1), 0) // BLOCK

    for j in range(ATT_TQ // BLOCK):
        blk = iq * (ATT_TQ // BLOCK) + j
        start = pl.multiple_of(jnp.clip((blk - 1) * BLOCK, 0, n - n_loc), BLOCK)
        qrows = slice(j * BLOCK, (j + 1) * BLOCK)
        cosq = cosq_ref[qrows, :]
        sinq = sinq_ref[qrows, :]
        delta = rel + (start - blk * BLOCK)
        in_window = jnp.abs(delta) <= WINDOW
        for g in range(ATT_KV_HEADS):
            heads = [ATT_REP * g + r for r in range(ATT_REP)]
            qg = jnp.concatenate(
                [(_rope_head(q_ref[0, qrows, h * LANES:(h + 1) * LANES], cosq, sinq) * scale).astype(BF16)
                 for h in heads], axis=0)
            kg = ks_ref[pl.ds(start, n_loc), g * LANES:(g + 1) * LANES]
            vg = vs_ref[pl.ds(start, n_loc), g * LANES:(g + 1) * LANES]
            kcg = kcs_ref[:, g * LANES:(g + 1) * LANES]
            vcg = vcs_ref[:, g * LANES:(g + 1) * LANES]
            nt = (((1,), (1,)), ((), ()))
            s_loc = lax.dot_general(qg, kg, nt, preferred_element_type=F32)
            s_ctx = lax.dot_general(qg, kcg, nt, preferred_element_type=F32)
            s_loc = jnp.where(in_window, s_loc, NEG_INF)
            sink_col = jnp.zeros((rows, 1), F32)
            for r, h in enumerate(heads):
                sink_col = jnp.where(head_of_row == r, sink_ref[h], sink_col)
            m = jnp.maximum(jnp.maximum(jnp.max(s_loc, axis=1, keepdims=True),
                                        jnp.max(s_ctx, axis=1, keepdims=True)), sink_col)
            p_loc = jnp.exp(s_loc - m)
            p_ctx = jnp.exp(s_ctx - m)
            den = (jnp.sum(p_loc, axis=1, keepdims=True) + jnp.sum(p_ctx, axis=1, keepdims=True)
                   + jnp.exp(sink_col - m))
            o = (jnp.dot(p_loc.astype(BF16), vg, preferred_element_type=F32)
                 + jnp.dot(p_ctx.astype(BF16), vcg, preferred_element_type=F32)) / den
            for r, h in enumerate(heads):
                o_ref[0, qrows, h * LANES:(h + 1) * LANES] = o[r * BLOCK:(r + 1) * BLOCK].astype(BF16)


def _attention(p_lat, kv_ctx, sink, cos_tab, sin_tab):
    b, n, _ = p_lat.shape
    n_ctx = kv_ctx.shape[1]
    grid_spec = pltpu.PrefetchScalarGridSpec(
        num_scalar_prefetch=1,
        grid=(b, n // ATT_TQ),
        in_specs=[
            pl.BlockSpec((1, ATT_TQ, QPAD), lambda bi, i, s: (bi, i, COL_Q // QPAD)),
            pl.BlockSpec((1, n, KVPAD), lambda bi, i, s: (bi, 0, COL_K // KVPAD)),
            pl.BlockSpec((1, n, KVPAD), lambda bi, i, s: (bi, 0, COL_V // KVPAD)),
            pl.BlockSpec((1, n_ctx, 2 * KVPAD), lambda bi, i, s: (bi, 0, 0)),
            pl.BlockSpec((ATT_TQ, LANES), lambda bi, i, s: (i, 0)),
            pl.BlockSpec((ATT_TQ, LANES), lambda bi, i, s: (i, 0)),
            pl.BlockSpec((n, LANES), lambda bi, i, s: (0, 0)),
            pl.BlockSpec((n, LANES), lambda bi, i, s: (0, 0)),
        ],
        out_specs=pl.BlockSpec((1, ATT_TQ, QPAD), lambda bi, i, s: (bi, i, 0)),
        scratch_shapes=[
            pltpu.VMEM((n, KVPAD), BF16),
            pltpu.VMEM((n, KVPAD), BF16),
            pltpu.VMEM((n_ctx, KVPAD), BF16),
            pltpu.VMEM((n_ctx, KVPAD), BF16),
        ],
    )
    return pl.pallas_call(
        _attn_body,
        grid_spec=grid_spec,
        out_shape=jax.ShapeDtypeStruct((b, n, QPAD), BF16),
        compiler_params=_cparams(("parallel", "arbitrary")),
        name="window_attention",
    )(sink, p_lat, p_lat, p_lat, kv_ctx, cos_tab, sin_tab, cos_tab, sin_tab)


def _rope_tables(n):
    rows = n // GRID_W
    row = jnp.repeat(jnp.arange(rows, dtype=F32), GRID_W)
    col = jnp.tile(jnp.arange(GRID_W, dtype=F32), rows)
    inv = ROPE_BASE ** (-jnp.arange(ROPE_FREQS, dtype=F32) / ROPE_FREQS)
    ang_r = row[:, None] * inv
    ang_c = col[:, None] * inv
    pad1 = jnp.ones((n, LANES - HEAD_DIM), F32)
    pad0 = jnp.zeros((n, LANES - HEAD_DIM), F32)
    cos = jnp.concatenate([jnp.cos(ang_r), jnp.cos(ang_r), jnp.cos(ang_c), jnp.cos(ang_c), pad1], axis=1)
    sin = jnp.concatenate([-jnp.sin(ang_r), jnp.sin(ang_r), -jnp.sin(ang_c), jnp.sin(ang_c), pad0], axis=1)
    return cos, sin


def _layer_norm(r, g, b):
    mu = jnp.mean(r, axis=-1, keepdims=True)
    var = jnp.mean(jnp.square(r - mu), axis=-1, keepdims=True)
    return (r - mu) * lax.rsqrt(var + LN_EPS) * g + b


def _outproj_body(hy_ref, att_ref, x_ref, wh_ref, wa_ref, g1_ref, sc2_ref, sh2_ref, lg_ref, lb_ref,
                  x1_ref, hq_ref):
    y = (jnp.dot(hy_ref[0].astype(BF16), wh_ref[...], preferred_element_type=F32)
         + jnp.dot(att_ref[0], wa_ref[...], preferred_element_type=F32))
    x1 = _layer_norm(DEEPNORM_ALPHA * x_ref[0] + g1_ref[0] * y, lg_ref[...], lb_ref[...])
    x1_ref[0] = x1
    hq_ref[0] = (x1 * (1.0 + sc2_ref[0]) + sh2_ref[0]).astype(BF16)


def _outproj_ln(hy, att, x, w_hy, w_att, g1, sc2, sh2, ln_g, ln_b, tm=512):
    b, n, d = x.shape
    modspec = pl.BlockSpec((1, 1, d), lambda bi, i: (bi, 0, 0))
    vecspec = pl.BlockSpec((1, d), lambda bi, i: (0, 0))
    return pl.pallas_call(
        _outproj_body,
        grid=(b, n // tm),
        in_specs=[
            pl.BlockSpec((1, tm, HY_WIDTH), lambda bi, i: (bi, i, 0)),
            pl.BlockSpec((1, tm, QPAD), lambda bi, i: (bi, i, 0)),
            pl.BlockSpec((1, tm, d), lambda bi, i: (bi, i, 0)),
            pl.BlockSpec(w_hy.shape, lambda bi, i: (0, 0)),
            pl.BlockSpec(w_att.shape, lambda bi, i: (0, 0)),
            modspec, modspec, modspec, vecspec, vecspec,
        ],
        out_specs=[
            pl.BlockSpec((1, tm, d), lambda bi, i: (bi, i, 0)),
            pl.BlockSpec((1, tm, d), lambda bi, i: (bi, i, 0)),
        ],
        out_shape=[jax.ShapeDtypeStruct((b, n, d), F32), jax.ShapeDtypeStruct((b, n, d), BF16)],
        compiler_params=_cparams(("parallel", "parallel")),
        name="outproj_ln1",
    )(hy, att, x, w_hy, w_att, g1, sc2, sh2, ln_g, ln_b)


PEER_TM = 256
_STAIR = sorted(((i, j) for i in range(PEER_TOPK) for j in range(PEER_TOPK) if (i + 1) * (j + 1) <= PEER_TOPK),
                key=lambda p: p[0] * PEER_TOPK + p[1])
_STAIR_ROWS = -(-len(_STAIR) // 8) * 8
_STAIR_COUNT = [PEER_TOPK // (i + 1) for i in range(PEER_TOPK)]
_STAIR_START = [sum(_STAIR_COUNT[:i]) for i in range(PEER_TOPK)]


def _stair_ids(tm):
    ids = np.full((_STAIR_ROWS,), float(PEER_TOPK * PEER_TOPK), np.float32)
    ids[:len(_STAIR)] = [i * PEER_TOPK + j for i, j in _STAIR]
    return np.repeat(ids[:, None], tm, axis=1)


def _select_topk(problems, write_row):
    tm = problems[0][0].shape[1]

    def step(k, prev):
        new = []
        for p, ((s_ref, ids, pad_id), prev_id) in enumerate(zip(problems, prev)):
            s = jnp.where(ids == prev_id, -jnp.inf, s_ref[...])
            s_ref[...] = s
            m = jnp.max(s, axis=0, keepdims=True)
            win = jnp.min(jnp.where(s == m, ids, pad_id), axis=0, keepdims=True)
            write_row(p, k, m, win)
            new.append(win)
        return tuple(new)

    lax.fori_loop(0, PEER_TOPK, step, tuple(jnp.full((1, tm), -1.0, F32) for _ in problems), unroll=2)


def _rows_by_rank(rank, table_ref):
    out = jnp.zeros(rank.shape, F32)
    for p in range(PEER_TOPK):
        out = jnp.where(rank == float(p), table_ref[p:p + 1, :], out)
    return out


def _peer_topk_body(hq_ref, wq_ref, k1_ref, k2_ref, sid_ref, a_ref, b_ref, g_ref,
                    q_s, s1_s, s2_s, v1_s, i1_s, v2_s, i2_s, c_s, t_s, f_s, ao_s, bo_s, go_s):
    tm = hq_ref.shape[0]
    q = jnp.dot(hq_ref[...], wq_ref[...], preferred_element_type=F32)
    for c in range(2 * PEER_HEADS):
        q_s[c] = q[:, c * LANES:(c + 1) * LANES]
    nt = (((1,), (1,)), ((), ()))
    key_id = lax.broadcasted_iota(jnp.int32, (PEER_KEYS, tm), 0).astype(F32)

    def head(h, carry):
        for half, (kref, s_s) in enumerate(((k1_ref, s1_s), (k2_ref, s2_s))):
            s_s[...] = lax.dot_general(kref[...], q_s[2 * h + half], nt, preferred_element_type=F32,
                                       precision=HIGHEST)

        def write1(p, k, val, idx):
            vs, is_ = ((v1_s, i1_s), (v2_s, i2_s))[p]
            vs[pl.ds(k, 1), :] = val
            is_[pl.ds(k, 1), :] = idx
        _select_topk([(s1_s, key_id, float(PEER_KEYS)), (s2_s, key_id, float(PEER_KEYS))], write1)

        c_s[...] = jnp.full(c_s.shape, -jnp.inf, F32)
        for i in range(PEER_TOPK):
            r0, cnt = _STAIR_START[i], _STAIR_COUNT[i]
            c_s[r0:r0 + cnt, :] = v1_s[i:i + 1, :] + v2_s[0:cnt, :]

        def write2(p, k, val, idx):
            t_s[pl.ds(k, 1), :] = val
            f_s[pl.ds(k, 1), :] = idx
        _select_topk([(c_s, sid_ref[...], float(PEER_TOPK * PEER_TOPK))], write2)

        flat = f_s[...]
        rank1 = jnp.floor(flat * (1.0 / PEER_TOPK))
        rank2 = flat - rank1 * PEER_TOPK
        rows = pl.ds(pl.multiple_of(h * PEER_TOPK, PEER_TOPK), PEER_TOPK)
        ao_s[rows, :] = _rows_by_rank(rank1, i1_s)
        bo_s[rows, :] = _rows_by_rank(rank2, i2_s)
        t = t_s[...]
        e = jnp.exp(t - jnp.max(t, axis=0, keepdims=True))
        go_s[rows, :] = e / jnp.sum(e, axis=0, keepdims=True)
        return carry

    lax.fori_loop(0, PEER_HEADS, head, 0)
    a_ref[...] = ao_s[...].T
    b_ref[...] = bo_s[...].T
    g_ref[...] = go_s[...].T


def _peer_topk(hq, wq, keys1, keys2):
    t, d = hq.shape
    tm = PEER_TM
    hk = PEER_HEADS * PEER_TOPK
    out = jax.ShapeDtypeStruct((t, hk), F32)
    ospec = pl.BlockSpec((tm, hk), lambda i: (i, 0))
    sid = _stair_ids(tm)
    return pl.pallas_call(
        _peer_topk_body,
        grid=(t // tm,),
        in_specs=[
            pl.BlockSpec((tm, d), lambda i: (i, 0)),
            pl.BlockSpec(wq.shape, lambda i: (0, 0)),
            pl.BlockSpec(keys1.shape, lambda i: (0, 0)),
            pl.BlockSpec(keys2.shape, lambda i: (0, 0)),
            pl.BlockSpec(sid.shape, lambda i: (0, 0)),
        ],
        out_specs=[ospec, ospec, ospec],
        out_shape=[out, out, out],
        scratch_shapes=[
            pltpu.VMEM((2 * PEER_HEADS, tm, LANES), F32),
            pltpu.VMEM((PEER_KEYS, tm), F32), pltpu.VMEM((PEER_KEYS, tm), F32),
            pltpu.VMEM((PEER_TOPK, tm), F32), pltpu.VMEM((PEER_TOPK, tm), F32),
            pltpu.VMEM((PEER_TOPK, tm), F32), pltpu.VMEM((PEER_TOPK, tm), F32),
            pltpu.VMEM((_STAIR_ROWS, tm), F32),
            pltpu.VMEM((PEER_TOPK, tm), F32), pltpu.VMEM((PEER_TOPK, tm), F32),
            pltpu.VMEM((hk, tm), F32), pltpu.VMEM((hk, tm), F32), pltpu.VMEM((hk, tm), F32),
        ],
        compiler_params=_cparams(("parallel",)),
        name="peer_topk",
    )(hq, wq, keys1, keys2, sid)


PEER_TE = 2048
PEER_GROUPS = PEER_TE // PEER_KEYS
PEER_SCORE_TM = 512
PEER_VALUE_TM = 512
PEER_VALUE_TE = 1024
GATE_PITCH = PEER_KEYS + 8
GATE_UNROLL = 32


def _peer_scores_body(hq_ref, a_ref, b_ref, u_ref, o_ref):
    e = pl.program_id(0)
    s = lax.dot_general(hq_ref[...], u_ref[...], (((1,), (1,)), ((), ())),
                        preferred_element_type=F32)
    b_idx = b_ref[...].astype(jnp.int32)
    a_val = a_ref[...]
    cur = jnp.zeros(a_val.shape, F32)
    for jj in range(PEER_GROUPS):
        cand = jnp.take_along_axis(s[:, jj * LANES:(jj + 1) * LANES], b_idx, axis=1)
        cur = jnp.where(a_val == (e * PEER_GROUPS + jj).astype(F32), cand, cur)
    o_ref[0] = cur


def _peer_scores(hq, a, b, u_tab):
    t, d = hq.shape
    tm = PEER_SCORE_TM
    hk = a.shape[1]
    n_e = u_tab.shape[0] // PEER_TE
    tok = lambda e, i: (i, 0)
    return pl.pallas_call(
        _peer_scores_body,
        grid=(n_e, t // tm),
        in_specs=[
            pl.BlockSpec((tm, d), tok),
            pl.BlockSpec((tm, hk), tok), pl.BlockSpec((tm, hk), tok),
            pl.BlockSpec((PEER_TE, d), lambda e, i: (e, 0)),
        ],
        out_specs=pl.BlockSpec((1, tm, hk), lambda e, i: (e, i, 0)),
        out_shape=jax.ShapeDtypeStruct((n_e, t, hk), F32),
        compiler_params=_cparams(("arbitrary", "arbitrary")),
        name="peer_scores",
    )(hq, a, b, u_tab)


def _peer_values_body(ss_ref, a_ref, b_ref, g_ref, v_ref, x1_ref, g2_ref, lg_ref, lb_ref, o_ref,
                      w_s, hd_s, acc_s):
    tm = a_ref.shape[0]
    n_e = pl.num_programs(1)
    e = pl.program_id(1)
    nt = (((1,), (1,)), ((), ()))

    @pl.when(e == 0)
    def _gates():
        s_sel = jnp.sum(ss_ref[...], axis=0)
        act = 0.5 * s_sel * (1.0 + lax.erf(s_sel * (2.0 ** -0.5)))
        w_s[...] = g_ref[...] * act
        sub = lax.broadcasted_iota(jnp.int32, (PEER_KEYS, LANES), 0).astype(F32)

        def token(t, carry):
            a_row = a_ref[pl.ds(t, 1), :]
            b_row = b_ref[pl.ds(t, 1), :]
            w_row = w_s[pl.ds(t, 1), :]
            lhs = jnp.where(sub == a_row, w_row, 0.0).astype(BF16)
            rhs = jnp.where(sub == b_row, 1.0, 0.0).astype(BF16)
            tile = lax.dot_general(lhs, rhs, nt, preferred_element_type=F32)
            hd_s[pl.ds(pl.multiple_of(t * GATE_PITCH, 8), PEER_KEYS), :] = tile
            return carry
        lax.fori_loop(0, tm, token, 0, unroll=GATE_UNROLL)
        acc_s[...] = jnp.zeros(acc_s.shape, F32)

    groups = v_ref.shape[0] // PEER_KEYS
    j0 = e * groups
    lhs = jnp.concatenate(
        [hd_s[pl.ds(j0 + jj, tm, stride=GATE_PITCH), :].astype(BF16) for jj in range(groups)], axis=1)
    acc_s[...] += jnp.dot(lhs, v_ref[...], preferred_element_type=F32)

    @pl.when(e == n_e - 1)
    def _finish():
        o_ref[...] = _layer_norm(DEEPNORM_ALPHA * x1_ref[...] + g2_ref[0] * acc_s[...], lg_ref[...], lb_ref[...])


def _peer_values(ssel, a, b, g, v_tab, x1, g2, ln_g, ln_b, tokens_per_batch):
    t, d = x1.shape
    tm = PEER_VALUE_TM
    hk = a.shape[1]
    n_e = v_tab.shape[0] // PEER_VALUE_TE
    tiles_per_batch = tokens_per_batch // tm
    tok = lambda i, e: (i, 0)
    one = pl.Buffered(1)
    return pl.pallas_call(
        _peer_values_body,
        grid=(t // tm, n_e),
        in_specs=[
            pl.BlockSpec((ssel.shape[0], tm, hk), lambda i, e: (0, i, 0), pipeline_mode=one),
            pl.BlockSpec((tm, hk), tok), pl.BlockSpec((tm, hk), tok), pl.BlockSpec((tm, hk), tok),
            pl.BlockSpec((PEER_VALUE_TE, d), lambda i, e: (e, 0)),
            pl.BlockSpec((tm, d), tok, pipeline_mode=one),
            pl.BlockSpec((1, 1, d), lambda i, e: (i // tiles_per_batch, 0, 0)),
            pl.BlockSpec((1, d), lambda i, e: (0, 0)),
            pl.BlockSpec((1, d), lambda i, e: (0, 0)),
        ],
        out_specs=pl.BlockSpec((tm, d), tok, pipeline_mode=one),
        out_shape=jax.ShapeDtypeStruct((t, d), F32),
        scratch_shapes=[
            pltpu.VMEM((tm, hk), F32),
            pltpu.VMEM((tm * GATE_PITCH, LANES), F32),
            pltpu.VMEM((tm, d), F32),
        ],
        compiler_params=_cparams(("parallel", "arbitrary")),
        name="peer_values",
    )(ssel, a, b, g, v_tab, x1, g2, ln_g, ln_b)


FFT_N2 = 128
FFT_PITCH = FFT_N2 + 8
HY_POS_PAD = 32
FFT_UNROLL = 4


def _dft_constants(n):
    big = 2 * n
    n1c = big // FFT_N2
    half = n1c // 2
    k1 = np.arange(n1c)[:, None]
    n1 = np.arange(half)[None, :]
    ang = 2 * np.pi * k1 * n1 / n1c
    c, s = np.cos(ang), np.sin(ang)
    f1 = np.block([[c, s], [-s, c]])
    k2 = np.arange(FFT_N2)[:, None]
    n2 = np.arange(FFT_N2)[None, :]
    ang = 2 * np.pi * k2 * n2 / FFT_N2
    c, s = np.cos(ang), np.sin(ang)
    d3 = np.block([[c, s], [-s, c]])
    d3i = np.block([[c, -s], [s, c]])
    ang = 2 * np.pi * n1.T * k1.T / n1c
    c, s = np.cos(ang), np.sin(ang)
    f3 = np.block([[c, -s], [s, c]]) / big
    ang = 2 * np.pi * (np.arange(n1c)[:, None] * np.arange(FFT_N2)[None, :]) / big
    twr = np.repeat(np.cos(ang).reshape(-1, 1), LANES, axis=1)
    twi = np.repeat(-np.sin(ang).reshape(-1, 1), LANES, axis=1)
    as32 = lambda a: np.asarray(a, np.float32)

    def split(a):
        parts, rest = [], np.asarray(a, np.float64)
        for _ in range(3):
            p = np.asarray(rest, BF16)
            parts.append(p)
            rest = rest - p.astype(np.float64)
        return np.stack(parts)
    return split(f1), split(d3), split(d3i), split(f3), as32(twr), as32(twi)


def _hdot(a, b):
    return jnp.dot(a, b, preferred_element_type=F32, precision=HIGHEST)


def _dft_dot(m_ref, x):
    x_hi = x.astype(BF16)
    x_lo = (x - x_hi.astype(F32)).astype(BF16)
    m_hi = m_ref[0]
    dot = functools.partial(jnp.dot, preferred_element_type=F32)
    return (dot(m_hi, x_hi) + dot(m_hi, x_lo)) + (dot(m_ref[1], x_hi) + dot(m_ref[2], x_hi))


def _fft_stage1(u_re, u_im, f1_ref, a_re, a_im):
    half = f1_ref.shape[2] // 2
    n1c = f1_ref.shape[1] // 2

    def column(n2):
        return jnp.concatenate([u_re[pl.ds(n2, half, stride=FFT_PITCH), :],
                                u_im[pl.ds(n2, half, stride=FFT_PITCH), :]], axis=0)

    def body(i, carry):
        a = _dft_dot(f1_ref, jnp.concatenate([column(2 * i), column(2 * i + 1)], axis=1))
        for s in range(2):
            a_re[pl.ds(2 * i + s, n1c, stride=FFT_PITCH), :] = a[:n1c, s * LANES:(s + 1) * LANES]
            a_im[pl.ds(2 * i + s, n1c, stride=FFT_PITCH), :] = a[n1c:, s * LANES:(s + 1) * LANES]
        return carry
    lax.fori_loop(0, FFT_N2 // 2, body, 0, unroll=FFT_UNROLL)


def _twiddled_pair(a_re, a_im, twr_ref, twi_ref, i):
    t0 = pl.multiple_of(2 * i * FFT_N2, 2 * FFT_N2)
    rows = [pl.multiple_of((2 * i + s) * FFT_PITCH, 8) for s in range(2)]
    side = lambda ref, r: jnp.concatenate([ref[pl.ds(r[0], FFT_N2), :], ref[pl.ds(r[1], FFT_N2), :]], axis=1)
    tws = [pl.multiple_of(t0 + s * FFT_N2, FFT_N2) for s in range(2)]
    return rows, side(a_re, rows), side(a_im, rows), side(twr_ref, tws), side(twi_ref, tws)


def _fft_conv_middle(a_re, a_im, d3_ref, d3i_ref, twr_ref, twi_ref, hre_ref, him_ref):
    n1c = hre_ref.shape[1]

    def body(i, carry):
        rows, ar, ai, twr, twi = _twiddled_pair(a_re, a_im, twr_ref, twi_ref, i)
        x = _dft_dot(d3_ref, jnp.concatenate([ar * twr - ai * twi, ar * twi + ai * twr], axis=0))
        xr, xi = x[:FFT_N2], x[FFT_N2:]
        hr = jnp.concatenate([hre_ref[0, 2 * i], hre_ref[0, 2 * i + 1]], axis=1)
        hi = jnp.concatenate([him_ref[0, 2 * i], him_ref[0, 2 * i + 1]], axis=1)
        y = _dft_dot(d3i_ref, jnp.concatenate([xr * hr - xi * hi, xr * hi + xi * hr], axis=0))
        br, bi = y[:FFT_N2], y[FFT_N2:]
        out_re = br * twr + bi * twi
        out_im = bi * twr - br * twi
        for s in range(2):
            a_re[pl.ds(rows[s], FFT_N2), :] = out_re[:, s * LANES:(s + 1) * LANES]
            a_im[pl.ds(rows[s], FFT_N2), :] = out_im[:, s * LANES:(s + 1) * LANES]
        return carry
    lax.fori_loop(0, n1c // 2, body, 0, unroll=FFT_UNROLL)


def _fft_stage_inv(a_re, a_im, f3_ref, y_re, y_im):
    half = f3_ref.shape[1] // 2
    n1c = f3_ref.shape[2] // 2

    def column(n2):
        return jnp.concatenate([a_re[pl.ds(n2, n1c, stride=FFT_PITCH), :],
                                a_im[pl.ds(n2, n1c, stride=FFT_PITCH), :]], axis=0)

    def body(i, carry):
        y = _dft_dot(f3_ref, jnp.concatenate([column(2 * i), column(2 * i + 1)], axis=1))
        for s in range(2):
            y_re[pl.ds(2 * i + s, half, stride=FFT_PITCH), :] = y[:half, s * LANES:(s + 1) * LANES]
            y_im[pl.ds(2 * i + s, half, stride=FFT_PITCH), :] = y[half:, s * LANES:(s + 1) * LANES]
        return carry
    lax.fori_loop(0, FFT_N2 // 2, body, 0, unroll=FFT_UNROLL)


def _short_conv_chunk(x_ref, bi, j, nblk, w_ref, b_ref):
    r0 = j * FFT_N2
    cur = x_ref[bi, r0:r0 + FFT_N2, :]
    row = lax.broadcasted_iota(jnp.int32, cur.shape, 0)
    if j == 0:
        prev = jnp.where(row == 0, 0.0, pltpu.roll(cur, 1, 0))
    else:
        prev = x_ref[bi, r0 - 1:r0 + FFT_N2 - 1, :]
    if j == nblk - 1:
        nxt = jnp.where(row == FFT_N2 - 1, 0.0, pltpu.roll(cur, FFT_N2 - 1, 0))
    else:
        nxt = x_ref[bi, r0 + 1:r0 + FFT_N2 + 1, :]
    return prev * w_ref[0:1, :] + cur * w_ref[1:2, :] + nxt * w_ref[2:3, :] + b_ref[...]


def _hy_conv_body(conv_a, a_ref, g_ref, wa_ref, ba_ref, wg_ref, bg_ref, skip_ref, hre_ref, him_ref,
                  f1_ref, d3_ref, d3i_ref, f3_ref, twr_ref, twi_ref, o_ref,
                  u_re, u_im, a_re, a_im, y_re, y_im):
    nblk = a_ref.shape[1] // FFT_N2
    for bi, dst in ((0, u_re), (1, u_im)):
        for j in range(nblk):
            if conv_a:
                blk = _short_conv_chunk(a_ref, bi, j, nblk, wa_ref, ba_ref)
            else:
                blk = a_ref[bi, j * FFT_N2:(j + 1) * FFT_N2, :]
            dst[j * FFT_PITCH:j * FFT_PITCH + FFT_N2, :] = blk
    _fft_stage1(u_re, u_im, f1_ref, a_re, a_im)
    _fft_conv_middle(a_re, a_im, d3_ref, d3i_ref, twr_ref, twi_ref, hre_ref, him_ref)
    _fft_stage_inv(a_re, a_im, f3_ref, y_re, y_im)
    skip = skip_ref[...]
    for bi, (ysrc, usrc) in enumerate(((y_re, u_re), (y_im, u_im))):
        for j in range(nblk):
            rows = slice(j * FFT_PITCH, j * FFT_PITCH + FFT_N2)
            gate = _short_conv_chunk(g_ref, bi, j, nblk, wg_ref, bg_ref)
            o_ref[bi, j * FFT_N2:(j + 1) * FFT_N2, :] = gate * (ysrc[rows, :] + usrc[rows, :] * skip)


def _hy_conv(a, a_col, g, g_col, conv_w, conv_b, skip, hre, him, order, consts, conv_a):
    b, n, _ = a.shape
    f1, d3, d3i, f3, twr, twi = consts
    n1c = f1.shape[1] // 2
    half = n1c // 2
    w = LANES
    tiles = HY_WIDTH // w
    one = pl.Buffered(1)
    data = lambda col: pl.BlockSpec((2, n, w), lambda ct, p: (p, 0, col + ct), pipeline_mode=one)
    wspec = lambda col: pl.BlockSpec((3, w), lambda ct, p: (0, col + ct))
    bspec = lambda col: pl.BlockSpec((1, w), lambda ct, p: (0, col + ct))
    hspec = pl.BlockSpec((1, n1c, FFT_N2, w), lambda ct, p: (order, 0, 0, ct), pipeline_mode=one)
    cs = lambda arr: pl.BlockSpec(arr.shape, lambda ct, p: (0,) * arr.ndim, pipeline_mode=one)
    a_wcol = a_col if conv_a else g_col
    return pl.pallas_call(
        functools.partial(_hy_conv_body, conv_a),
        grid=(tiles, b // 2),
        in_specs=[data(a_col), data(g_col), wspec(a_wcol), bspec(a_wcol), wspec(g_col), bspec(g_col),
                  pl.BlockSpec((1, w), lambda ct, p: (0, ct)), hspec, hspec,
                  cs(f1), cs(d3), cs(d3i), cs(f3), cs(twr), cs(twi)],
        out_specs=pl.BlockSpec((2, n, w), lambda ct, p: (p, 0, ct), pipeline_mode=one),
        out_shape=jax.ShapeDtypeStruct((b, n, HY_WIDTH), F32),
        scratch_shapes=[pltpu.VMEM((half * FFT_PITCH, w), F32), pltpu.VMEM((half * FFT_PITCH, w), F32),
                        pltpu.VMEM((n1c * FFT_PITCH, w), F32), pltpu.VMEM((n1c * FFT_PITCH, w), F32),
                        pltpu.VMEM((half * FFT_PITCH, w), F32), pltpu.VMEM((half * FFT_PITCH, w), F32)],
        compiler_params=_cparams(("parallel", "arbitrary")),
        name=f"hyena_conv{order + 1}",
    )(a, g, conv_w, conv_b, conv_w, conv_b, skip, hre, him, f1, d3, d3i, f3, twr, twi)


def _hy_filter_body(z_ref, w1_ref, b1_ref, f1_ref, w2_ref, b2_ref, f2_ref, w3_ref, b3_ref, dl_ref, o_ref):
    z = z_ref[...]
    h = jnp.sin(f1_ref[...] * (_hdot(z, w1_ref[...]) + b1_ref[...]))
    h = jnp.sin(f2_ref[...] * (_hdot(h, w2_ref[...]) + b2_ref[...]))
    h = _hdot(h, w3_ref[...]) + b3_ref[...]
    t = z[:, 0:1]
    o_ref[...] = h * (jnp.exp(-t * dl_ref[...]) + HY_WINDOW_SHIFT)


def _hy_filters(n, w1, b1, fr1, w2, b2, fr2, w3, b3):
    t = jnp.linspace(0.0, 1.0, n, dtype=F32)[:, None]
    wv = 2.0 * math.pi * jnp.arange(n, dtype=F32)[:, None] / n
    bands = jnp.linspace(1e-4, HY_BANDS - 1, HY_BANDS, dtype=F32)[None, :]
    z = jnp.concatenate([t, jnp.cos(bands * wv), -jnp.sin(bands * wv)], axis=-1)
    pos = z.shape[1]
    z = jnp.pad(z, ((0, 0), (0, HY_POS_PAD - pos)))
    w1p = jnp.pad(w1, ((0, HY_POS_PAD - pos), (0, 0)))
    min_decay = math.log(HY_DECAY_TARGET) / HY_SLOW_DECAY_PCT
    max_decay = math.log(HY_DECAY_TARGET) / HY_FAST_DECAY_PCT
    deltas = jnp.abs(jnp.linspace(min_decay, max_decay, HY_WIDTH, dtype=F32))
    n_out = w3.shape[1]
    dl = jnp.tile(deltas, n_out // HY_WIDTH)[None, :]
    tn = 512
    hid = w2.shape[0]
    full = lambda shape: pl.BlockSpec(shape, lambda i: (0, 0))
    return pl.pallas_call(
        _hy_filter_body,
        grid=(n // tn,),
        in_specs=[pl.BlockSpec((tn, HY_POS_PAD), lambda i: (i, 0)), full((HY_POS_PAD, hid)), full((1, hid)),
                  full((1, hid)), full((hid, hid)), full((1, hid)), full((1, hid)), full((hid, n_out)),
                  full((1, n_out)), full((1, n_out))],
        out_specs=pl.BlockSpec((tn, n_out), lambda i: (i, 0)),
        out_shape=jax.ShapeDtypeStruct((n, n_out), F32),
        compiler_params=_cparams(("parallel",)),
        name="hyena_filters",
    )(z, w1p, b1[None], fr1[None], w2, b2[None], fr2[None], w3, b3[None], dl)


def _hy_spectrum_body(f_ref, b_ref, f1_ref, d3_ref, twr_ref, twi_ref, hre_ref, him_ref, u_re, u_im, a_re, a_im):
    nblk = f_ref.shape[0] // FFT_N2
    n1c = hre_ref.shape[1]
    for part, out_ref in enumerate((hre_ref, him_ref)):
        for j in range(nblk):
            f = f_ref[j * FFT_N2:(j + 1) * FFT_N2, :]
            bw = b_ref[j * FFT_N2:(j + 1) * FFT_N2, :]
            if j == 0:
                row = lax.broadcasted_iota(jnp.int32, bw.shape, 0)
                bw = jnp.where(row == 0, 0.0, bw)
            rows = slice(j * FFT_PITCH, j * FFT_PITCH + FFT_N2)
            u_re[rows, :] = f + bw if part == 0 else f - bw
            u_im[rows, :] = jnp.zeros((FFT_N2, LANES), F32)
        _fft_stage1(u_re, u_im, f1_ref, a_re, a_im)

        def body(i, carry, part=part, out_ref=out_ref):
            _, ar, ai, twr, twi = _twiddled_pair(a_re, a_im, twr_ref, twi_ref, i)
            x = _dft_dot(d3_ref, jnp.concatenate([ar * twr - ai * twi, ar * twi + ai * twr], axis=0))
            x = x[:FFT_N2] if part == 0 else x[FFT_N2:]
            for s in range(2):
                out_ref[0, 2 * i + s] = x[:, s * LANES:(s + 1) * LANES]
            return carry
        lax.fori_loop(0, n1c // 2, body, 0, unroll=FFT_UNROLL)


def _hy_spectrum(h, consts):
    n = h.shape[0]
    f1, d3, _, _, twr, twi = consts
    n1c = f1.shape[1] // 2
    half = n1c // 2
    tiles = HY_WIDTH // LANES
    cs = lambda arr: pl.BlockSpec(arr.shape, lambda o, ct: (0,) * arr.ndim)
    out = jax.ShapeDtypeStruct((HY_ORDER, n1c, FFT_N2, HY_WIDTH), F32)
    ospec = pl.BlockSpec((1, n1c, FFT_N2, LANES), lambda o, ct: (o, 0, 0, ct))
    return pl.pallas_call(
        _hy_spectrum_body,
        grid=(HY_ORDER, tiles),
        in_specs=[pl.BlockSpec((n, LANES), lambda o, ct: (0, o * tiles + ct)),
                  pl.BlockSpec((n, LANES), lambda o, ct: (0, (HY_ORDER + o) * tiles + ct)),
                  cs(f1), cs(d3), cs(twr), cs(twi)],
        out_specs=[ospec, ospec],
        out_shape=[out, out],
        scratch_shapes=[pltpu.VMEM((half * FFT_PITCH, LANES), F32), pltpu.VMEM((half * FFT_PITCH, LANES), F32),
                        pltpu.VMEM((n1c * FFT_PITCH, LANES), F32), pltpu.VMEM((n1c * FFT_PITCH, LANES), F32)],
        compiler_params=_cparams(("parallel", "parallel")),
        name="hyena_spectrum",
    )(h, h, f1, d3, twr, twi)


def _hyena(p_lat, conv_w, conv_b, w1, b1, fr1, w2, b2, fr2, w3, b3, skip):
    n = p_lat.shape[1]
    consts = _dft_constants(n)
    h = _hy_filters(n, w1, b1, fr1, w2, b2, fr2, w3, b3)
    hre, him = _hy_spectrum(h, consts)
    tiles = HY_WIDTH // LANES
    col = COL_HY // LANES
    cw = jnp.pad(conv_w, ((0, 0), (COL_HY, 0)))
    cb = jnp.pad(conv_b[None], ((0, 0), (COL_HY, 0)))
    y = _hy_conv(p_lat, col, p_lat, col + tiles, cw, cb, skip[0][None], hre, him, 0, consts, True)
    return _hy_conv(y, 0, p_lat, col + 2 * tiles, cw, cb, skip[1][None], hre, him, 1, consts, False)


def _pad_heads(w, heads):
    d = w.shape[0]
    w = w.reshape(d, heads, HEAD_DIM)
    w = jnp.pad(w, ((0, 0), (0, 0), (0, LANES - HEAD_DIM)))
    return w.reshape(d, heads * LANES)


def kernel(x, c, ctx, c_ctx, w_mod, b_mod, w_in, hy_conv_w, hy_conv_b, hy_f_w1, hy_f_b1, hy_f_freq1, hy_f_w2,
           hy_f_b2, hy_f_freq2, hy_f_w3, hy_f_b3, hy_skip, attn_sink, w_out, ln1_g, ln1_b, peer_wq, peer_keys1,
           peer_keys2, peer_u, peer_v, ln2_g, ln2_b):
    b, n, d = x.shape
    l = 0
    cc = jnp.concatenate([c, c_ctx[None], jnp.zeros((8 - b - 1, d), F32)], axis=0)
    mod = _modulation(cc, w_mod[l], b_mod[l][None])
    mod_lat = mod[:b].reshape(b, 6, 1, d)
    sh1, sc1, g1, sh2, sc2, g2 = (mod_lat[:, i] for i in range(6))
    mod_c = mod[b].reshape(6, 1, 1, d)
    csh1, csc1 = mod_c[0], mod_c[1]

    w = w_in[l]
    w_q = _pad_heads(w[:, PROJ_HY:KV_START], ATT_HEADS)
    w_k = _pad_heads(w[:, KV_START:KV_START + PROJ_KV], ATT_KV_HEADS)
    w_v = _pad_heads(w[:, KV_START + PROJ_KV:], ATT_KV_HEADS)
    w_pad = jnp.concatenate([w_q, w[:, :PROJ_HY], w_k, w_v], axis=1).astype(BF16)
    w_kv = jnp.concatenate([w_k, w_v], axis=1).astype(BF16)

    p_lat = _mod_matmul(x, sc1, sh1, w_pad, 512, "in_proj")
    kv_ctx = _mod_matmul(ctx, csc1, csh1, w_kv, ctx.shape[1], "ctx_kv_proj")

    cos_tab, sin_tab = _rope_tables(n)
    att = _attention(p_lat, kv_ctx, attn_sink[l], cos_tab, sin_tab)

    hy = _hyena(p_lat, hy_conv_w[l], hy_conv_b[l], hy_f_w1[l], hy_f_b1[l], hy_f_freq1[l], hy_f_w2[l], hy_f_b2[l],
                hy_f_freq2[l], hy_f_w3[l], hy_f_b3[l], hy_skip[l])

    wo = w_out[l]
    w_o_hy = wo[:HY_WIDTH].astype(BF16)
    w_o_att = jnp.pad(wo[HY_WIDTH:].reshape(ATT_HEADS, HEAD_DIM, d),
                      ((0, 0), (0, LANES - HEAD_DIM), (0, 0))).reshape(QPAD, d).astype(BF16)
    x1, hq = _outproj_ln(hy, att, x, w_o_hy, w_o_att, g1, sc2, sh2, ln1_g[l][None], ln1_b[l][None])

    hq2 = hq.reshape(b * n, d)
    a_idx, b_idx, gate = _peer_topk(hq2, peer_wq[l].astype(BF16), peer_keys1[l], peer_keys2[l])
    s_sel = _peer_scores(hq2, a_idx, b_idx, peer_u[l].astype(BF16))
    out = _peer_values(s_sel, a_idx, b_idx, gate, peer_v[l].astype(BF16), x1.reshape(b * n, d), g2,
                       ln2_g[l][None], ln2_b[l][None], n)
    return out.reshape(b, n, d)
```

```python
import functools
import math

import jax
import jax.numpy as jnp
import numpy as np
from jax import lax
from jax.experimental import pallas as pl
from jax.experimental.pallas import tpu as pltpu

F32 = jnp.float32
BF16 = jnp.bfloat16
HIGHEST = lax.Precision.HIGHEST

LANES = 128
VMEM_LIMIT = 60000 * 1024

D_MODEL = 1024
GRID_W = 64
HY_WIDTH = 512
HY_ORDER = 2
HY_BANDS = 8
HY_DECAY_TARGET = 1e-2
HY_FAST_DECAY_PCT = 0.3
HY_SLOW_DECAY_PCT = 1.5
HY_WINDOW_SHIFT = 0.05
ATT_HEADS = 8
ATT_KV_HEADS = 2
ATT_REP = ATT_HEADS // ATT_KV_HEADS
HEAD_DIM = 64
WINDOW = 128
BLOCK = 128
ROPE_BASE = 10000.0
ROPE_FREQS = HEAD_DIM // 4
PROJ_HY = (HY_ORDER + 1) * HY_WIDTH
PROJ_Q = ATT_HEADS * HEAD_DIM
PROJ_KV = ATT_KV_HEADS * HEAD_DIM
KV_START = PROJ_HY + PROJ_Q
PEER_KEYS = 128
PEER_HEADS = 8
PEER_QDIM = 256
PEER_TOPK = 16
LN_EPS = 1e-5
NEG_INF = -1e30
DEPTH = 1
DEEPNORM_ALPHA = (2.0 * DEPTH) ** 0.25

QPAD = ATT_HEADS * LANES
KVPAD = ATT_KV_HEADS * LANES
COL_Q = 0
COL_HY = QPAD
COL_K = QPAD + PROJ_HY
COL_V = COL_K + KVPAD
PROJ_PAD = COL_V + KVPAD


def _cparams(sem):
    return pltpu.CompilerParams(dimension_semantics=sem, vmem_limit_bytes=VMEM_LIMIT)


def _mod_body(c_ref, w_ref, b_ref, o_ref):
    c = c_ref[...]
    a = c * jax.nn.sigmoid(c)
    o_ref[...] = jnp.dot(a, w_ref[...], preferred_element_type=F32, precision=HIGHEST) + b_ref[...]


def _modulation(cc, w_mod, b_mod):
    rows, d = cc.shape
    n_out = w_mod.shape[1]
    tn = 1536
    return pl.pallas_call(
        _mod_body,
        grid=(n_out // tn,),
        in_specs=[
            pl.BlockSpec((rows, d), lambda j: (0, 0)),
            pl.BlockSpec((d, tn), lambda j: (0, j)),
            pl.BlockSpec((1, tn), lambda j: (0, j)),
        ],
        out_specs=pl.BlockSpec((rows, tn), lambda j: (0, j)),
        out_shape=jax.ShapeDtypeStruct((rows, n_out), F32),
        compiler_params=_cparams(("arbitrary",)),
        name="modulation",
    )(cc, w_mod, b_mod)


def _mod_matmul_body(x_ref, sc_ref, sh_ref, w_ref, o_ref):
    h = x_ref[0] * (1.0 + sc_ref[0]) + sh_ref[0]
    o_ref[0] = jnp.dot(h.astype(BF16), w_ref[...], preferred_element_type=F32)


def _mod_matmul(x, sc, sh, w, tm, name):
    b, n, d = x.shape
    n_out = w.shape[1]
    per_batch = sc.shape[0] == b
    mod_map = (lambda bi, i: (bi, 0, 0)) if per_batch else (lambda bi, i: (0, 0, 0))
    return pl.pallas_call(
        _mod_matmul_body,
        grid=(b, n // tm),
        in_specs=[
            pl.BlockSpec((1, tm, d), lambda bi, i: (bi, i, 0)),
            pl.BlockSpec((1, 1, d), mod_map),
            pl.BlockSpec((1, 1, d), mod_map),
            pl.BlockSpec((d, n_out), lambda bi, i: (0, 0)),
        ],
        out_specs=pl.BlockSpec((1, tm, n_out), lambda bi, i: (bi, i, 0)),
        out_shape=jax.ShapeDtypeStruct((b, n, n_out), F32),
        compiler_params=_cparams(("parallel", "parallel")),
        name=name,
    )(x, sc, sh, w)


def _rope_head(x, cos, sin_signed):
    lane = lax.broadcasted_iota(jnp.int32, x.shape, 1)
    first_half = (lane % 32) < 16
    partner = jnp.where(first_half, pltpu.roll(x, LANES - 16, 1), pltpu.roll(x, 16, 1))
    return x * cos + partner * sin_signed


ATT_TQ = 512
ROPE_CHUNK = 512


def _attn_body(sink_ref, q_ref, k_ref, v_ref, kvc_ref, cosq_ref, sinq_ref, cosk_ref, sink_tab_ref,
               o_ref, ks_ref, vs_ref, kcs_ref, vcs_ref):
    n = k_ref.shape[1]
    iq = pl.program_id(1)
    scale = HEAD_DIM ** -0.5

    @pl.when(iq == 0)
    def _prepare_keys():
        def chunk(ci, carry):
            r0 = pl.multiple_of(ci * ROPE_CHUNK, ROPE_CHUNK)
            cos = cosk_ref[pl.ds(r0, ROPE_CHUNK), :]
            sin = sink_tab_ref[pl.ds(r0, ROPE_CHUNK), :]
            for g in range(ATT_KV_HEADS):
                kg = k_ref[0, pl.ds(r0, ROPE_CHUNK), g * LANES:(g + 1) * LANES]
                ks_ref[pl.ds(r0, ROPE_CHUNK), g * LANES:(g + 1) * LANES] = _rope_head(kg, cos, sin).astype(BF16)
            vs_ref[pl.ds(r0, ROPE_CHUNK), :] = v_ref[0, pl.ds(r0, ROPE_CHUNK), :].astype(BF16)
            return carry
        lax.fori_loop(0, n // ROPE_CHUNK, chunk, 0)
        kcs_ref[...] = kvc_ref[0, :, 0:KVPAD].astype(BF16)
        vcs_ref[...] = kvc_ref[0, :, KVPAD:2 * KVPAD].astype(BF16)

    n_loc = 3 * BLOCK
    rows = ATT_REP * BLOCK
    row_i = lax.broadcasted_iota(jnp.int32, (rows, n_loc), 0)
    col_i = lax.broadcasted_iota(jnp.int32, (rows, n_loc), 1)
    rel = col_i - (row_i % BLOCK)
    head_of_row = lax.broadcasted_iota(jnp.int32, (rows, 1), 0) // BLOCK

    for j in range(ATT_TQ // BLOCK):
        blk = iq * (ATT_TQ // BLOCK) + j
        start = pl.multiple_of(jnp.clip((blk - 1) * BLOCK, 0, n - n_loc), BLOCK)
        qrows = slice(j * BLOCK, (j + 1) * BLOCK)
        cosq = cosq_ref[qrows, :]
        sinq = sinq_ref[qrows, :]
        delta = rel + (start - blk * BLOCK)
        in_window = jnp.abs(delta) <= WINDOW
        for g in range(ATT_KV_HEADS):
            heads = [ATT_REP * g + r for r in range(ATT_REP)]
            qg = jnp.concatenate(
                [(_rope_head(q_ref[0, qrows, h * LANES:(h + 1) * LANES], cosq, sinq) * scale).astype(BF16)
                 for h in heads], axis=0)
            kg = ks_ref[pl.ds(start, n_loc), g * LANES:(g + 1) * LANES]
            vg = vs_ref[pl.ds(start, n_loc), g * LANES:(g + 1) * LANES]
            kcg = kcs_ref[:, g * LANES:(g + 1) * LANES]
            vcg = vcs_ref[:, g * LANES:(g + 1) * LANES]
            nt = (((1,), (1,)), ((), ()))
            s_loc = lax.dot_general(qg, kg, nt, preferred_element_type=F32)
            s_ctx = lax.dot_general(qg, kcg, nt, preferred_element_type=F32)
            s_loc = jnp.where(in_window, s_loc, NEG_INF)
            sink_col = jnp.zeros((rows, 1), F32)
            for r, h in enumerate(heads):
                sink_col = jnp.where(head_of_row == r, sink_ref[h], sink_col)
            m = jnp.maximum(jnp.maximum(jnp.max(s_loc, axis=1, keepdims=True),
                                        jnp.max(s_ctx, axis=1, keepdims=True)), sink_col)
            p_loc = jnp.exp(s_loc - m)
            p_ctx = jnp.exp(s_ctx - m)
            den = (jnp.sum(p_loc, axis=1, keepdims=True) + jnp.sum(p_ctx, axis=1, keepdims=True)
                   + jnp.exp(sink_col - m))
            o = (jnp.dot(p_loc.astype(BF16), vg, preferred_element_type=F32)
                 + jnp.dot(p_ctx.astype(BF16), vcg, preferred_element_type=F32)) / den
            for r, h in enumerate(heads):
                o_ref[0, qrows, h * LANES:(h + 1) * LANES] = o[r * BLOCK:(r + 1) * BLOCK].astype(BF16)


def _attention(p_lat, kv_ctx, sink, cos_tab, sin_tab):
    b, n, _ = p_lat.shape
    n_ctx = kv_ctx.shape[1]
    grid_spec = pltpu.PrefetchScalarGridSpec(
        num_scalar_prefetch=1,
        grid=(b, n // ATT_TQ),
        in_specs=[
            pl.BlockSpec((1, ATT_TQ, QPAD), lambda bi, i, s: (bi, i, COL_Q // QPAD)),
            pl.BlockSpec((1, n, KVPAD), lambda bi, i, s: (bi, 0, COL_K // KVPAD)),
            pl.BlockSpec((1, n, KVPAD), lambda bi, i, s: (bi, 0, COL_V // KVPAD)),
            pl.BlockSpec((1, n_ctx, 2 * KVPAD), lambda bi, i, s: (bi, 0, 0)),
            pl.BlockSpec((ATT_TQ, LANES), lambda bi, i, s: (i, 0)),
            pl.BlockSpec((ATT_TQ, LANES), lambda bi, i, s: (i, 0)),
            pl.BlockSpec((n, LANES), lambda bi, i, s: (0, 0)),
            pl.BlockSpec((n, LANES), lambda bi, i, s: (0, 0)),
        ],
        out_specs=pl.BlockSpec((1, ATT_TQ, QPAD), lambda bi, i, s: (bi, i, 0)),
        scratch_shapes=[
            pltpu.VMEM((n, KVPAD), BF16),
            pltpu.VMEM((n, KVPAD), BF16),
            pltpu.VMEM((n_ctx, KVPAD), BF16),
            pltpu.VMEM((n_ctx, KVPAD), BF16),
        ],
    )
    return pl.pallas_call(
        _attn_body,
        grid_spec=grid_spec,
        out_shape=jax.ShapeDtypeStruct((b, n, QPAD), BF16),
        compiler_params=_cparams(("parallel", "arbitrary")),
        name="window_attention",
    )(sink, p_lat, p_lat, p_lat, kv_ctx, cos_tab, sin_tab, cos_tab, sin_tab)


def _rope_tables(n):
    rows = n // GRID_W
    row = jnp.repeat(jnp.arange(rows, dtype=F32), GRID_W)
    col = jnp.tile(jnp.arange(GRID_W, dtype=F32), rows)
    inv = ROPE_BASE ** (-jnp.arange(ROPE_FREQS, dtype=F32) / ROPE_FREQS)
    ang_r = row[:, None] * inv
    ang_c = col[:, None] * inv
    pad1 = jnp.ones((n, LANES - HEAD_DIM), F32)
    pad0 = jnp.zeros((n, LANES - HEAD_DIM), F32)
    cos = jnp.concatenate([jnp.cos(ang_r), jnp.cos(ang_r), jnp.cos(ang_c), jnp.cos(ang_c), pad1], axis=1)
    sin = jnp.concatenate([-jnp.sin(ang_r), jnp.sin(ang_r), -jnp.sin(ang_c), jnp.sin(ang_c), pad0], axis=1)
    return cos, sin


def _layer_norm(r, g, b):
    mu = jnp.mean(r, axis=-1, keepdims=True)
    var = jnp.mean(jnp.square(r - mu), axis=-1, keepdims=True)
    return (r - mu) * lax.rsqrt(var + LN_EPS) * g + b


def _outproj_body(hy_ref, att_ref, x_ref, wh_ref, wa_ref, g1_ref, sc2_ref, sh2_ref, lg_ref, lb_ref,
                  x1_ref, hq_ref):
    y = (jnp.dot(hy_ref[0].astype(BF16), wh_ref[...], preferred_element_type=F32)
         + jnp.dot(att_ref[0], wa_ref[...], preferred_element_type=F32))
    x1 = _layer_norm(DEEPNORM_ALPHA * x_ref[0] + g1_ref[0] * y, lg_ref[...], lb_ref[...])
    x1_ref[0] = x1
    hq_ref[0] = (x1 * (1.0 + sc2_ref[0]) + sh2_ref[0]).astype(BF16)


def _outproj_ln(hy, att, x, w_hy, w_att, g1, sc2, sh2, ln_g, ln_b, tm=512):
    b, n, d = x.shape
    modspec = pl.BlockSpec((1, 1, d), lambda bi, i: (bi, 0, 0))
    vecspec = pl.BlockSpec((1, d), lambda bi, i: (0, 0))
    return pl.pallas_call(
        _outproj_body,
        grid=(b, n // tm),
        in_specs=[
            pl.BlockSpec((1, tm, HY_WIDTH), lambda bi, i: (bi, i, 0)),
            pl.BlockSpec((1, tm, QPAD), lambda bi, i: (bi, i, 0)),
            pl.BlockSpec((1, tm, d), lambda bi, i: (bi, i, 0)),
            pl.BlockSpec(w_hy.shape, lambda bi, i: (0, 0)),
            pl.BlockSpec(w_att.shape, lambda bi, i: (0, 0)),
            modspec, modspec, modspec, vecspec, vecspec,
        ],
        out_specs=[
            pl.BlockSpec((1, tm, d), lambda bi, i: (bi, i, 0)),
            pl.BlockSpec((1, tm, d), lambda bi, i: (bi, i, 0)),
        ],
        out_shape=[jax.ShapeDtypeStruct((b, n, d), F32), jax.ShapeDtypeStruct((b, n, d), BF16)],
        compiler_params=_cparams(("parallel", "parallel")),
        name="outproj_ln1",
    )(hy, att, x, w_hy, w_att, g1, sc2, sh2, ln_g, ln_b)


PEER_TM = 256
_STAIR = sorted(((i, j) for i in range(PEER_TOPK) for j in range(PEER_TOPK) if (i + 1) * (j + 1) <= PEER_TOPK),
                key=lambda p: p[0] * PEER_TOPK + p[1])
_STAIR_ROWS = -(-len(_STAIR) // 8) * 8
_STAIR_COUNT = [PEER_TOPK // (i + 1) for i in range(PEER_TOPK)]
_STAIR_START = [sum(_STAIR_COUNT[:i]) for i in range(PEER_TOPK)]


def _stair_ids(tm):
    ids = np.full((_STAIR_ROWS,), float(PEER_TOPK * PEER_TOPK), np.float32)
    ids[:len(_STAIR)] = [i * PEER_TOPK + j for i, j in _STAIR]
    return np.repeat(ids[:, None], tm, axis=1)


def _select_topk(problems, write_row):
    tm = problems[0][0].shape[1]

    def step(k, prev):
        new = []
        for p, ((s_ref, ids, pad_id), prev_id) in enumerate(zip(problems, prev)):
            s = jnp.where(ids == prev_id, -jnp.inf, s_ref[...])
            s_ref[...] = s
            m = jnp.max(s, axis=0, keepdims=True)
            win = jnp.min(jnp.where(s == m, ids, pad_id), axis=0, keepdims=True)
            write_row(p, k, m, win)
            new.append(win)
        return tuple(new)

    lax.fori_loop(0, PEER_TOPK, step, tuple(jnp.full((1, tm), -1.0, F32) for _ in problems), unroll=2)


def _rows_by_rank(rank, table_ref):
    out = jnp.zeros(rank.shape, F32)
    for p in range(PEER_TOPK):
        out = jnp.where(rank == float(p), table_ref[p:p + 1, :], out)
    return out


def _peer_topk_body(hq_ref, wq_ref, k1_ref, k2_ref, sid_ref, a_ref, b_ref, g_ref,
                    q_s, s1_s, s2_s, v1_s, i1_s, v2_s, i2_s, c_s, t_s, f_s, ao_s, bo_s, go_s):
    tm = hq_ref.shape[0]
    q = jnp.dot(hq_ref[...], wq_ref[...], preferred_element_type=F32)
    for c in range(2 * PEER_HEADS):
        q_s[c] = q[:, c * LANES:(c + 1) * LANES]
    nt = (((1,), (1,)), ((), ()))
    key_id = lax.broadcasted_iota(jnp.int32, (PEER_KEYS, tm), 0).astype(F32)

    def head(h, carry):
        for half, (kref, s_s) in enumerate(((k1_ref, s1_s), (k2_ref, s2_s))):
            qh = q_s[2 * h + half]
            q_hi = qh.astype(BF16)
            q_lo = (qh - q_hi.astype(F32)).astype(BF16)
            ntdot = functools.partial(lax.dot_general, dimension_numbers=nt, preferred_element_type=F32)
            s_s[...] = ntdot(kref[0], q_hi) + (ntdot(kref[0], q_lo) + ntdot(kref[1], q_hi))

        def write1(p, k, val, idx):
            vs, is_ = ((v1_s, i1_s), (v2_s, i2_s))[p]
            vs[pl.ds(k, 1), :] = val
            is_[pl.ds(k, 1), :] = idx
        _select_topk([(s1_s, key_id, float(PEER_KEYS)), (s2_s, key_id, float(PEER_KEYS))], write1)

        c_s[...] = jnp.full(c_s.shape, -jnp.inf, F32)
        for i in range(PEER_TOPK):
            r0, cnt = _STAIR_START[i], _STAIR_COUNT[i]
            c_s[r0:r0 + cnt, :] = v1_s[i:i + 1, :] + v2_s[0:cnt, :]

        def write2(p, k, val, idx):
            t_s[pl.ds(k, 1), :] = val
            f_s[pl.ds(k, 1), :] = idx
        _select_topk([(c_s, sid_ref[...], float(PEER_TOPK * PEER_TOPK))], write2)

        flat = f_s[...]
        rank1 = jnp.floor(flat * (1.0 / PEER_TOPK))
        rank2 = flat - rank1 * PEER_TOPK
        rows = pl.ds(pl.multiple_of(h * PEER_TOPK, PEER_TOPK), PEER_TOPK)
        ao_s[rows, :] = _rows_by_rank(rank1, i1_s)
        bo_s[rows, :] = _rows_by_rank(rank2, i2_s)
        t = t_s[...]
        e = jnp.exp(t - jnp.max(t, axis=0, keepdims=True))
        go_s[rows, :] = e / jnp.sum(e, axis=0, keepdims=True)
        return carry

    lax.fori_loop(0, PEER_HEADS, head, 0)
    a_ref[...] = ao_s[...].T
    b_ref[...] = bo_s[...].T
    g_ref[...] = go_s[...].T


def _peer_topk(hq, wq, keys1, keys2):
    t, d = hq.shape
    tm = PEER_TM
    hk = PEER_HEADS * PEER_TOPK
    out = jax.ShapeDtypeStruct((t, hk), F32)
    ospec = pl.BlockSpec((tm, hk), lambda i: (i, 0))
    sid = _stair_ids(tm)
    return pl.pallas_call(
        _peer_topk_body,
        grid=(t // tm,),
        in_specs=[
            pl.BlockSpec((tm, d), lambda i: (i, 0)),
            pl.BlockSpec(wq.shape, lambda i: (0, 0)),
            pl.BlockSpec(keys1.shape, lambda i: (0, 0, 0)),
            pl.BlockSpec(keys2.shape, lambda i: (0, 0, 0)),
            pl.BlockSpec(sid.shape, lambda i: (0, 0)),
        ],
        out_specs=[ospec, ospec, ospec],
        out_shape=[out, out, out],
        scratch_shapes=[
            pltpu.VMEM((2 * PEER_HEADS, tm, LANES), F32),
            pltpu.VMEM((PEER_KEYS, tm), F32), pltpu.VMEM((PEER_KEYS, tm), F32),
            pltpu.VMEM((PEER_TOPK, tm), F32), pltpu.VMEM((PEER_TOPK, tm), F32),
            pltpu.VMEM((PEER_TOPK, tm), F32), pltpu.VMEM((PEER_TOPK, tm), F32),
            pltpu.VMEM((_STAIR_ROWS, tm), F32),
            pltpu.VMEM((PEER_TOPK, tm), F32), pltpu.VMEM((PEER_TOPK, tm), F32),
            pltpu.VMEM((hk, tm), F32), pltpu.VMEM((hk, tm), F32), pltpu.VMEM((hk, tm), F32),
        ],
        compiler_params=_cparams(("parallel",)),
        name="peer_topk",
    )(hq, wq, keys1, keys2, sid)


PEER_TE = 2048
PEER_GROUPS = PEER_TE // PEER_KEYS
PEER_SCORE_TM = 512
PEER_VALUE_TM = 512
PEER_VALUE_TE = 1024
GATE_PITCH = PEER_KEYS + 8
GATE_UNROLL = 32


def _peer_scores_body(hq_ref, a_ref, b_ref, u_ref, o_ref):
    e = pl.program_id(0)
    s = lax.dot_general(hq_ref[...], u_ref[...], (((1,), (1,)), ((), ())),
                        preferred_element_type=F32)
    b_idx = b_ref[...].astype(jnp.int32)
    a_val = a_ref[...]
    cur = jnp.zeros(a_val.shape, F32)
    for jj in range(PEER_GROUPS):
        cand = jnp.take_along_axis(s[:, jj * LANES:(jj + 1) * LANES], b_idx, axis=1)
        cur = jnp.where(a_val == (e * PEER_GROUPS + jj).astype(F32), cand, cur)
    o_ref[0] = cur


def _peer_scores(hq, a, b, u_tab):
    t, d = hq.shape
    tm = PEER_SCORE_TM
    hk = a.shape[1]
    n_e = u_tab.shape[0] // PEER_TE
    tok = lambda e, i: (i, 0)
    return pl.pallas_call(
        _peer_scores_body,
        grid=(n_e, t // tm),
        in_specs=[
            pl.BlockSpec((tm, d), tok),
            pl.BlockSpec((tm, hk), tok), pl.BlockSpec((tm, hk), tok),
            pl.BlockSpec((PEER_TE, d), lambda e, i: (e, 0)),
        ],
        out_specs=pl.BlockSpec((1, tm, hk), lambda e, i: (e, i, 0)),
        out_shape=jax.ShapeDtypeStruct((n_e, t, hk), F32),
        compiler_params=_cparams(("arbitrary", "arbitrary")),
        name="peer_scores",
    )(hq, a, b, u_tab)


def _peer_values_body(ss_ref, a_ref, b_ref, g_ref, v_ref, x1_ref, g2_ref, lg_ref, lb_ref, o_ref,
                      w_s, hd_s, acc_s):
    tm = a_ref.shape[0]
    n_e = pl.num_programs(1)
    e = pl.program_id(1)
    nt = (((1,), (1,)), ((), ()))

    @pl.when(e == 0)
    def _gates():
        s_sel = jnp.sum(ss_ref[...], axis=0)
        act = 0.5 * s_sel * (1.0 + lax.erf(s_sel * (2.0 ** -0.5)))
        w_s[...] = g_ref[...] * act
        sub = lax.broadcasted_iota(jnp.int32, (PEER_KEYS, LANES), 0).astype(F32)

        def token(t, carry):
            a_row = a_ref[pl.ds(t, 1), :]
            b_row = b_ref[pl.ds(t, 1), :]
            w_row = w_s[pl.ds(t, 1), :]
            lhs = jnp.where(sub == a_row, w_row, 0.0).astype(BF16)
            rhs = jnp.where(sub == b_row, 1.0, 0.0).astype(BF16)
            tile = lax.dot_general(lhs, rhs, nt, preferred_element_type=F32)
            hd_s[pl.ds(pl.multiple_of(t * GATE_PITCH, 8), PEER_KEYS), :] = tile
            return carry
        lax.fori_loop(0, tm, token, 0, unroll=GATE_UNROLL)
        acc_s[...] = jnp.zeros(acc_s.shape, F32)

    groups = v_ref.shape[0] // PEER_KEYS
    j0 = e * groups
    lhs = jnp.concatenate(
        [hd_s[pl.ds(j0 + jj, tm, stride=GATE_PITCH), :].astype(BF16) for jj in range(groups)], axis=1)
    acc_s[...] += jnp.dot(lhs, v_ref[...], preferred_element_type=F32)

    @pl.when(e == n_e - 1)
    def _finish():
        o_ref[...] = _layer_norm(DEEPNORM_ALPHA * x1_ref[...] + g2_ref[0] * acc_s[...], lg_ref[...], lb_ref[...])


def _peer_values(ssel, a, b, g, v_tab, x1, g2, ln_g, ln_b, tokens_per_batch):
    t, d = x1.shape
    tm = PEER_VALUE_TM
    hk = a.shape[1]
    n_e = v_tab.shape[0] // PEER_VALUE_TE
    tiles_per_batch = tokens_per_batch // tm
    tok = lambda i, e: (i, 0)
    one = pl.Buffered(1)
    return pl.pallas_call(
        _peer_values_body,
        grid=(t // tm, n_e),
        in_specs=[
            pl.BlockSpec((ssel.shape[0], tm, hk), lambda i, e: (0, i, 0)),
            pl.BlockSpec((tm, hk), tok), pl.BlockSpec((tm, hk), tok), pl.BlockSpec((tm, hk), tok),
            pl.BlockSpec((PEER_VALUE_TE, d), lambda i, e: (e, 0)),
            pl.BlockSpec((tm, d), tok),
            pl.BlockSpec((1, 1, d), lambda i, e: (i // tiles_per_batch, 0, 0)),
            pl.BlockSpec((1, d), lambda i, e: (0, 0)),
            pl.BlockSpec((1, d), lambda i, e: (0, 0)),
        ],
        out_specs=pl.BlockSpec((tm, d), tok, pipeline_mode=one),
        out_shape=jax.ShapeDtypeStruct((t, d), F32),
        scratch_shapes=[
            pltpu.VMEM((tm, hk), F32),
            pltpu.VMEM((tm * GATE_PITCH, LANES), F32),
            pltpu.VMEM((tm, d), F32),
        ],
        compiler_params=_cparams(("parallel", "arbitrary")),
        name="peer_values",
    )(ssel, a, b, g, v_tab, x1, g2, ln_g, ln_b)


FFT_N2 = 128
FFT_PITCH = FFT_N2 + 8
HY_POS_PAD = 32
FFT_UNROLL = 4


def _dft_constants(n):
    big = 2 * n
    n1c = big // FFT_N2
    half = n1c // 2
    k1 = np.arange(n1c)[:, None]
    n1 = np.arange(half)[None, :]
    ang = 2 * np.pi * k1 * n1 / n1c
    c, s = np.cos(ang), np.sin(ang)
    f1 = np.block([[c, s], [-s, c]])
    k2 = np.arange(FFT_N2)[:, None]
    n2 = np.arange(FFT_N2)[None, :]
    ang = 2 * np.pi * k2 * n2 / FFT_N2
    c, s = np.cos(ang), np.sin(ang)
    d3 = np.block([[c, s], [-s, c]])
    d3i = np.block([[c, -s], [s, c]])
    ang = 2 * np.pi * n1.T * k1.T / n1c
    c, s = np.cos(ang), np.sin(ang)
    f3 = np.block([[c, -s], [s, c]]) / big
    ang = 2 * np.pi * (np.arange(n1c)[:, None] * np.arange(FFT_N2)[None, :]) / big
    twr = np.repeat(np.cos(ang).reshape(-1, 1), LANES, axis=1)
    twi = np.repeat(-np.sin(ang).reshape(-1, 1), LANES, axis=1)
    as32 = lambda a: np.asarray(a, np.float32)

    def split(a):
        parts, rest = [], np.asarray(a, np.float64)
        for _ in range(3):
            p = np.asarray(rest, BF16)
            parts.append(p)
            rest = rest - p.astype(np.float64)
        return np.stack(parts)
    return split(f1), split(d3), split(d3i), split(f3), as32(twr), as32(twi)


def _hdot(a, b):
    return jnp.dot(a, b, preferred_element_type=F32, precision=HIGHEST)


def _dft_dot(m_ref, x):
    x_hi = x.astype(BF16)
    x_lo = (x - x_hi.astype(F32)).astype(BF16)
    m_hi = m_ref[0]
    dot = functools.partial(jnp.dot, preferred_element_type=F32)
    return (dot(m_hi, x_hi) + dot(m_hi, x_lo)) + (dot(m_ref[1], x_hi) + dot(m_ref[2], x_hi))


def _fft_stage1(u_re, u_im, f1_ref, a_re, a_im):
    half = f1_ref.shape[2] // 2
    n1c = f1_ref.shape[1] // 2

    def column(n2):
        return jnp.concatenate([u_re[pl.ds(n2, half, stride=FFT_PITCH), :],
                                u_im[pl.ds(n2, half, stride=FFT_PITCH), :]], axis=0)

    def body(i, carry):
        a = _dft_dot(f1_ref, jnp.concatenate([column(2 * i), column(2 * i + 1)], axis=1))
        for s in range(2):
            a_re[pl.ds(2 * i + s, n1c, stride=FFT_PITCH), :] = a[:n1c, s * LANES:(s + 1) * LANES]
            a_im[pl.ds(2 * i + s, n1c, stride=FFT_PITCH), :] = a[n1c:, s * LANES:(s + 1) * LANES]
        return carry
    lax.fori_loop(0, FFT_N2 // 2, body, 0, unroll=FFT_UNROLL)


def _twiddled_pair(a_re, a_im, twr_ref, twi_ref, i):
    t0 = pl.multiple_of(2 * i * FFT_N2, 2 * FFT_N2)
    rows = [pl.multiple_of((2 * i + s) * FFT_PITCH, 8) for s in range(2)]
    side = lambda ref, r: jnp.concatenate([ref[pl.ds(r[0], FFT_N2), :], ref[pl.ds(r[1], FFT_N2), :]], axis=1)
    tws = [pl.multiple_of(t0 + s * FFT_N2, FFT_N2) for s in range(2)]
    return rows, side(a_re, rows), side(a_im, rows), side(twr_ref, tws), side(twi_ref, tws)


def _fft_conv_middle(a_re, a_im, d3_ref, d3i_ref, twr_ref, twi_ref, hre_ref, him_ref):
    n1c = hre_ref.shape[1]

    def body(i, carry):
        rows, ar, ai, twr, twi = _twiddled_pair(a_re, a_im, twr_ref, twi_ref, i)
        x = _dft_dot(d3_ref, jnp.concatenate([ar * twr - ai * twi, ar * twi + ai * twr], axis=0))
        xr, xi = x[:FFT_N2], x[FFT_N2:]
        hr = jnp.concatenate([hre_ref[0, 2 * i], hre_ref[0, 2 * i + 1]], axis=1)
        hi = jnp.concatenate([him_ref[0, 2 * i], him_ref[0, 2 * i + 1]], axis=1)
        y = _dft_dot(d3i_ref, jnp.concatenate([xr * hr - xi * hi, xr * hi + xi * hr], axis=0))
        br, bi = y[:FFT_N2], y[FFT_N2:]
        out_re = br * twr + bi * twi
        out_im = bi * twr - br * twi
        for s in range(2):
            a_re[pl.ds(rows[s], FFT_N2), :] = out_re[:, s * LANES:(s + 1) * LANES]
            a_im[pl.ds(rows[s], FFT_N2), :] = out_im[:, s * LANES:(s + 1) * LANES]
        return carry
    lax.fori_loop(0, n1c // 2, body, 0, unroll=FFT_UNROLL)


def _fft_stage_inv(a_re, a_im, f3_ref, y_re, y_im):
    half = f3_ref.shape[1] // 2
    n1c = f3_ref.shape[2] // 2

    def column(n2):
        return jnp.concatenate([a_re[pl.ds(n2, n1c, stride=FFT_PITCH), :],
                                a_im[pl.ds(n2, n1c, stride=FFT_PITCH), :]], axis=0)

    def body(i, carry):
        y = _dft_dot(f3_ref, jnp.concatenate([column(2 * i), column(2 * i + 1)], axis=1))
        for s in range(2):
            y_re[pl.ds(2 * i + s, half, stride=FFT_PITCH), :] = y[:half, s * LANES:(s + 1) * LANES]
            y_im[pl.ds(2 * i + s, half, stride=FFT_PITCH), :] = y[half:, s * LANES:(s + 1) * LANES]
        return carry
    lax.fori_loop(0, FFT_N2 // 2, body, 0, unroll=FFT_UNROLL)


def _short_conv_chunk(x_ref, bi, j, nblk, w_ref, b_ref):
    r0 = j * FFT_N2
    cur = x_ref[bi, r0:r0 + FFT_N2, :]
    row = lax.broadcasted_iota(jnp.int32, cur.shape, 0)
    if j == 0:
        prev = jnp.where(row == 0, 0.0, pltpu.roll(cur, 1, 0))
    else:
        prev = x_ref[bi, r0 - 1:r0 + FFT_N2 - 1, :]
    if j == nblk - 1:
        nxt = jnp.where(row == FFT_N2 - 1, 0.0, pltpu.roll(cur, FFT_N2 - 1, 0))
    else:
        nxt = x_ref[bi, r0 + 1:r0 + FFT_N2 + 1, :]
    return prev * w_ref[0:1, :] + cur * w_ref[1:2, :] + nxt * w_ref[2:3, :] + b_ref[...]


def _hy_conv_body(conv_a, a_ref, g_ref, wa_ref, ba_ref, wg_ref, bg_ref, skip_ref, hre_ref, him_ref,
                  f1_ref, d3_ref, d3i_ref, f3_ref, twr_ref, twi_ref, o_ref,
                  u_re, u_im, a_re, a_im, y_re, y_im):
    nblk = a_ref.shape[1] // FFT_N2
    for bi, dst in ((0, u_re), (1, u_im)):
        for j in range(nblk):
            if conv_a:
                blk = _short_conv_chunk(a_ref, bi, j, nblk, wa_ref, ba_ref)
            else:
                blk = a_ref[bi, j * FFT_N2:(j + 1) * FFT_N2, :]
            dst[j * FFT_PITCH:j * FFT_PITCH + FFT_N2, :] = blk
    _fft_stage1(u_re, u_im, f1_ref, a_re, a_im)
    _fft_conv_middle(a_re, a_im, d3_ref, d3i_ref, twr_ref, twi_ref, hre_ref, him_ref)
    _fft_stage_inv(a_re, a_im, f3_ref, y_re, y_im)
    skip = skip_ref[...]
    for bi, (ysrc, usrc) in enumerate(((y_re, u_re), (y_im, u_im))):
        for j in range(nblk):
            rows = slice(j * FFT_PITCH, j * FFT_PITCH + FFT_N2)
            gate = _short_conv_chunk(g_ref, bi, j, nblk, wg_ref, bg_ref)
            o_ref[bi, j * FFT_N2:(j + 1) * FFT_N2, :] = gate * (ysrc[rows, :] + usrc[rows, :] * skip)


def _hy_conv(a, a_col, g, g_col, conv_w, conv_b, skip, hre, him, order, consts, conv_a):
    b, n, _ = a.shape
    f1, d3, d3i, f3, twr, twi = consts
    n1c = f1.shape[1] // 2
    half = n1c // 2
    w = LANES
    tiles = HY_WIDTH // w
    one = pl.Buffered(1)
    data = lambda col: pl.BlockSpec((2, n, w), lambda ct, p: (p, 0, col + ct))
    wspec = lambda col: pl.BlockSpec((3, w), lambda ct, p: (0, col + ct))
    bspec = lambda col: pl.BlockSpec((1, w), lambda ct, p: (0, col + ct))
    hspec = pl.BlockSpec((1, n1c, FFT_N2, w), lambda ct, p: (order, 0, 0, ct), pipeline_mode=one)
    cs = lambda arr: pl.BlockSpec(arr.shape, lambda ct, p: (0,) * arr.ndim, pipeline_mode=one)
    a_wcol = a_col if conv_a else g_col
    return pl.pallas_call(
        functools.partial(_hy_conv_body, conv_a),
        grid=(tiles, b // 2),
        in_specs=[data(a_col), data(g_col), wspec(a_wcol), bspec(a_wcol), wspec(g_col), bspec(g_col),
                  pl.BlockSpec((1, w), lambda ct, p: (0, ct)), hspec, hspec,
                  cs(f1), cs(d3), cs(d3i), cs(f3), cs(twr), cs(twi)],
        out_specs=pl.BlockSpec((2, n, w), lambda ct, p: (p, 0, ct), pipeline_mode=one),
        out_shape=jax.ShapeDtypeStruct((b, n, HY_WIDTH), F32),
        scratch_shapes=[pltpu.VMEM((half * FFT_PITCH, w), F32), pltpu.VMEM((half * FFT_PITCH, w), F32),
                        pltpu.VMEM((n1c * FFT_PITCH, w), F32), pltpu.VMEM((n1c * FFT_PITCH, w), F32),
                        pltpu.VMEM((half * FFT_PITCH, w), F32), pltpu.VMEM((half * FFT_PITCH, w), F32)],
        compiler_params=_cparams(("parallel", "arbitrary")),
        name=f"hyena_conv{order + 1}",
    )(a, g, conv_w, conv_b, conv_w, conv_b, skip, hre, him, f1, d3, d3i, f3, twr, twi)


def _hy_filter_body(z_ref, w1_ref, b1_ref, f1_ref, w2_ref, b2_ref, f2_ref, w3_ref, b3_ref, dl_ref, o_ref):
    z = z_ref[...]
    h = jnp.sin(f1_ref[...] * (_hdot(z, w1_ref[...]) + b1_ref[...]))
    h = jnp.sin(f2_ref[...] * (_hdot(h, w2_ref[...]) + b2_ref[...]))
    h = _hdot(h, w3_ref[...]) + b3_ref[...]
    t = z[:, 0:1]
    o_ref[...] = h * (jnp.exp(-t * dl_ref[...]) + HY_WINDOW_SHIFT)


def _hy_filters(n, w1, b1, fr1, w2, b2, fr2, w3, b3):
    t = jnp.linspace(0.0, 1.0, n, dtype=F32)[:, None]
    wv = 2.0 * math.pi * jnp.arange(n, dtype=F32)[:, None] / n
    bands = jnp.linspace(1e-4, HY_BANDS - 1, HY_BANDS, dtype=F32)[None, :]
    z = jnp.concatenate([t, jnp.cos(bands * wv), -jnp.sin(bands * wv)], axis=-1)
    pos = z.shape[1]
    z = jnp.pad(z, ((0, 0), (0, HY_POS_PAD - pos)))
    w1p = jnp.pad(w1, ((0, HY_POS_PAD - pos), (0, 0)))
    min_decay = math.log(HY_DECAY_TARGET) / HY_SLOW_DECAY_PCT
    max_decay = math.log(HY_DECAY_TARGET) / HY_FAST_DECAY_PCT
    deltas = jnp.abs(jnp.linspace(min_decay, max_decay, HY_WIDTH, dtype=F32))
    n_out = w3.shape[1]
    dl = jnp.tile(deltas, n_out // HY_WIDTH)[None, :]
    tn = 512
    hid = w2.shape[0]
    full = lambda shape: pl.BlockSpec(shape, lambda i: (0, 0))
    return pl.pallas_call(
        _hy_filter_body,
        grid=(n // tn,),
        in_specs=[pl.BlockSpec((tn, HY_POS_PAD), lambda i: (i, 0)), full((HY_POS_PAD, hid)), full((1, hid)),
                  full((1, hid)), full((hid, hid)), full((1, hid)), full((1, hid)), full((hid, n_out)),
                  full((1, n_out)), full((1, n_out))],
        out_specs=pl.BlockSpec((tn, n_out), lambda i: (i, 0)),
        out_shape=jax.ShapeDtypeStruct((n, n_out), F32),
        compiler_params=_cparams(("parallel",)),
        name="hyena_filters",
    )(z, w1p, b1[None], fr1[None], w2, b2[None], fr2[None], w3, b3[None], dl)


def _hy_spectrum_body(f_ref, b_ref, f1_ref, d3_ref, twr_ref, twi_ref, hre_ref, him_ref, u_re, u_im, a_re, a_im):
    nblk = f_ref.shape[0] // FFT_N2
    n1c = hre_ref.shape[1]
    for part, out_ref in enumerate((hre_ref, him_ref)):
        for j in range(nblk):
            f = f_ref[j * FFT_N2:(j + 1) * FFT_N2, :]
            bw = b_ref[j * FFT_N2:(j + 1) * FFT_N2, :]
            if j == 0:
                row = lax.broadcasted_iota(jnp.int32, bw.shape, 0)
                bw = jnp.where(row == 0, 0.0, bw)
            rows = slice(j * FFT_PITCH, j * FFT_PITCH + FFT_N2)
            u_re[rows, :] = f + bw if part == 0 else f - bw
            u_im[rows, :] = jnp.zeros((FFT_N2, LANES), F32)
        _fft_stage1(u_re, u_im, f1_ref, a_re, a_im)

        def body(i, carry, part=part, out_ref=out_ref):
            _, ar, ai, twr, twi = _twiddled_pair(a_re, a_im, twr_ref, twi_ref, i)
            x = _dft_dot(d3_ref, jnp.concatenate([ar * twr - ai * twi, ar * twi + ai * twr], axis=0))
            x = x[:FFT_N2] if part == 0 else x[FFT_N2:]
            for s in range(2):
                out_ref[0, 2 * i + s] = x[:, s * LANES:(s + 1) * LANES]
            return carry
        lax.fori_loop(0, n1c // 2, body, 0, unroll=FFT_UNROLL)


def _hy_spectrum(h, consts):
    n = h.shape[0]
    f1, d3, _, _, twr, twi = consts
    n1c = f1.shape[1] // 2
    half = n1c // 2
    tiles = HY_WIDTH // LANES
    cs = lambda arr: pl.BlockSpec(arr.shape, lambda o, ct: (0,) * arr.ndim)
    out = jax.ShapeDtypeStruct((HY_ORDER, n1c, FFT_N2, HY_WIDTH), F32)
    ospec = pl.BlockSpec((1, n1c, FFT_N2, LANES), lambda o, ct: (o, 0, 0, ct))
    return pl.pallas_call(
        _hy_spectrum_body,
        grid=(HY_ORDER, tiles),
        in_specs=[pl.BlockSpec((n, LANES), lambda o, ct: (0, o * tiles + ct)),
                  pl.BlockSpec((n, LANES), lambda o, ct: (0, (HY_ORDER + o) * tiles + ct)),
                  cs(f1), cs(d3), cs(twr), cs(twi)],
        out_specs=[ospec, ospec],
        out_shape=[out, out],
        scratch_shapes=[pltpu.VMEM((half * FFT_PITCH, LANES), F32), pltpu.VMEM((half * FFT_PITCH, LANES), F32),
                        pltpu.VMEM((n1c * FFT_PITCH, LANES), F32), pltpu.VMEM((n1c * FFT_PITCH, LANES), F32)],
        compiler_params=_cparams(("parallel", "parallel")),
        name="hyena_spectrum",
    )(h, h, f1, d3, twr, twi)


def _hyena(p_lat, conv_w, conv_b, w1, b1, fr1, w2, b2, fr2, w3, b3, skip):
    n = p_lat.shape[1]
    consts = _dft_constants(n)
    h = _hy_filters(n, w1, b1, fr1, w2, b2, fr2, w3, b3)
    hre, him = _hy_spectrum(h, consts)
    tiles = HY_WIDTH // LANES
    col = COL_HY // LANES
    cw = jnp.pad(conv_w, ((0, 0), (COL_HY, 0)))
    cb = jnp.pad(conv_b[None], ((0, 0), (COL_HY, 0)))
    y = _hy_conv(p_lat, col, p_lat, col + tiles, cw, cb, skip[0][None], hre, him, 0, consts, True)
    return _hy_conv(y, 0, p_lat, col + 2 * tiles, cw, cb, skip[1][None], hre, him, 1, consts, False)


def _split_bf16(w):
    hi = w.astype(BF16)
    return jnp.stack([hi, (w - hi.astype(F32)).astype(BF16)])


def _pad_heads(w, heads):
    d = w.shape[0]
    w = w.reshape(d, heads, HEAD_DIM)
    w = jnp.pad(w, ((0, 0), (0, 0), (0, LANES - HEAD_DIM)))
    return w.reshape(d, heads * LANES)


def kernel(x, c, ctx, c_ctx, w_mod, b_mod, w_in, hy_conv_w, hy_conv_b, hy_f_w1, hy_f_b1, hy_f_freq1, hy_f_w2,
           hy_f_b2, hy_f_freq2, hy_f_w3, hy_f_b3, hy_skip, attn_sink, w_out, ln1_g, ln1_b, peer_wq, peer_keys1,
           peer_keys2, peer_u, peer_v, ln2_g, ln2_b):
    b, n, d = x.shape
    l = 0
    cc = jnp.concatenate([c, c_ctx[None], jnp.zeros((8 - b - 1, d), F32)], axis=0)
    mod = _modulation(cc, w_mod[l], b_mod[l][None])
    mod_lat = mod[:b].reshape(b, 6, 1, d)
    sh1, sc1, g1, sh2, sc2, g2 = (mod_lat[:, i] for i in range(6))
    mod_c = mod[b].reshape(6, 1, 1, d)
    csh1, csc1 = mod_c[0], mod_c[1]

    w = w_in[l]
    w_q = _pad_heads(w[:, PROJ_HY:KV_START], ATT_HEADS)
    w_k = _pad_heads(w[:, KV_START:KV_START + PROJ_KV], ATT_KV_HEADS)
    w_v = _pad_heads(w[:, KV_START + PROJ_KV:], ATT_KV_HEADS)
    w_pad = jnp.concatenate([w_q, w[:, :PROJ_HY], w_k, w_v], axis=1).astype(BF16)
    w_kv = jnp.concatenate([w_k, w_v], axis=1).astype(BF16)

    p_lat = _mod_matmul(x, sc1, sh1, w_pad, 512, "in_proj")
    kv_ctx = _mod_matmul(ctx, csc1, csh1, w_kv, ctx.shape[1], "ctx_kv_proj")

    cos_tab, sin_tab = _rope_tables(n)
    att = _attention(p_lat, kv_ctx, attn_sink[l], cos_tab, sin_tab)

    hy = _hyena(p_lat, hy_conv_w[l], hy_conv_b[l], hy_f_w1[l], hy_f_b1[l], hy_f_freq1[l], hy_f_w2[l], hy_f_b2[l],
                hy_f_freq2[l], hy_f_w3[l], hy_f_b3[l], hy_skip[l])

    wo = w_out[l]
    w_o_hy = wo[:HY_WIDTH].astype(BF16)
    w_o_att = jnp.pad(wo[HY_WIDTH:].reshape(ATT_HEADS, HEAD_DIM, d),
                      ((0, 0), (0, LANES - HEAD_DIM), (0, 0))).reshape(QPAD, d).astype(BF16)
    x1, hq = _outproj_ln(hy, att, x, w_o_hy, w_o_att, g1, sc2, sh2, ln1_g[l][None], ln1_b[l][None])

    hq2 = hq.reshape(b * n, d)
    a_idx, b_idx, gate = _peer_topk(hq2, peer_wq[l].astype(BF16), _split_bf16(peer_keys1[l]),
                                    _split_bf16(peer_keys2[l]))
    s_sel = _peer_scores(hq2, a_idx, b_idx, peer_u[l].astype(BF16))
    out = _peer_values(s_sel, a_idx, b_idx, gate, peer_v[l].astype(BF16), x1.reshape(b * n, d), g2,
                       ln2_g[l][None], ln2_b[l][None], n)
    return out.reshape(b, n, d)
```

```python
import functools
import math

import jax
import jax.numpy as jnp
import numpy as np
from jax import lax
from jax.experimental import pallas as pl
from jax.experimental.pallas import tpu as pltpu

F32 = jnp.float32
BF16 = jnp.bfloat16
HIGHEST = lax.Precision.HIGHEST

LANES = 128
VMEM_LIMIT = 60000 * 1024

D_MODEL = 1024
GRID_W = 64
HY_WIDTH = 512
HY_ORDER = 2
HY_BANDS = 8
HY_DECAY_TARGET = 1e-2
HY_FAST_DECAY_PCT = 0.3
HY_SLOW_DECAY_PCT = 1.5
HY_WINDOW_SHIFT = 0.05
ATT_HEADS = 8
ATT_KV_HEADS = 2
ATT_REP = ATT_HEADS // ATT_KV_HEADS
HEAD_DIM = 64
WINDOW = 128
BLOCK = 128
ROPE_BASE = 10000.0
ROPE_FREQS = HEAD_DIM // 4
PROJ_HY = (HY_ORDER + 1) * HY_WIDTH
PROJ_Q = ATT_HEADS * HEAD_DIM
PROJ_KV = ATT_KV_HEADS * HEAD_DIM
KV_START = PROJ_HY + PROJ_Q
PEER_KEYS = 128
PEER_HEADS = 8
PEER_QDIM = 256
PEER_TOPK = 16
LN_EPS = 1e-5
NEG_INF = -1e30
DEPTH = 1
DEEPNORM_ALPHA = (2.0 * DEPTH) ** 0.25

QPAD = ATT_HEADS * LANES
KVPAD = ATT_KV_HEADS * LANES
COL_Q = 0
COL_HY = QPAD
COL_K = QPAD + PROJ_HY
COL_V = COL_K + KVPAD
PROJ_PAD = COL_V + KVPAD


def _cparams(sem):
    return pltpu.CompilerParams(dimension_semantics=sem, vmem_limit_bytes=VMEM_LIMIT)


def _mod_body(c_ref, w_ref, b_ref, o_ref):
    c = c_ref[...]
    a = c * jax.nn.sigmoid(c)
    o_ref[...] = jnp.dot(a, w_ref[...], preferred_element_type=F32, precision=HIGHEST) + b_ref[...]


def _modulation(cc, w_mod, b_mod):
    rows, d = cc.shape
    n_out = w_mod.shape[1]
    tn = 1536
    return pl.pallas_call(
        _mod_body,
        grid=(n_out // tn,),
        in_specs=[
            pl.BlockSpec((rows, d), lambda j: (0, 0)),
            pl.BlockSpec((d, tn), lambda j: (0, j)),
            pl.BlockSpec((1, tn), lambda j: (0, j)),
        ],
        out_specs=pl.BlockSpec((rows, tn), lambda j: (0, j)),
        out_shape=jax.ShapeDtypeStruct((rows, n_out), F32),
        compiler_params=_cparams(("arbitrary",)),
        name="modulation",
    )(cc, w_mod, b_mod)


def _mod_matmul_body(x_ref, sc_ref, sh_ref, w_ref, o_ref):
    h = x_ref[0] * (1.0 + sc_ref[0]) + sh_ref[0]
    o_ref[0] = jnp.dot(h.astype(BF16), w_ref[...], preferred_element_type=F32)


def _mod_matmul(x, sc, sh, w, tm, name):
    b, n, d = x.shape
    n_out = w.shape[1]
    per_batch = sc.shape[0] == b
    mod_map = (lambda bi, i: (bi, 0, 0)) if per_batch else (lambda bi, i: (0, 0, 0))
    return pl.pallas_call(
        _mod_matmul_body,
        grid=(b, n // tm),
        in_specs=[
            pl.BlockSpec((1, tm, d), lambda bi, i: (bi, i, 0)),
            pl.BlockSpec((1, 1, d), mod_map),
            pl.BlockSpec((1, 1, d), mod_map),
            pl.BlockSpec((d, n_out), lambda bi, i: (0, 0)),
        ],
        out_specs=pl.BlockSpec((1, tm, n_out), lambda bi, i: (bi, i, 0)),
        out_shape=jax.ShapeDtypeStruct((b, n, n_out), F32),
        compiler_params=_cparams(("parallel", "parallel")),
        name=name,
    )(x, sc, sh, w)


def _rope_head(x, cos, sin_signed):
    lane = lax.broadcasted_iota(jnp.int32, x.shape, 1)
    first_half = (lane % 32) < 16
    partner = jnp.where(first_half, pltpu.roll(x, LANES - 16, 1), pltpu.roll(x, 16, 1))
    return x * cos + partner * sin_signed


ATT_TQ = 512
ROPE_CHUNK = 512


def _attn_body(sink_ref, q_ref, k_ref, v_ref, kvc_ref, cosq_ref, sinq_ref, cosk_ref, sink_tab_ref,
               o_ref, ks_ref, vs_ref, kcs_ref, vcs_ref):
    n = k_ref.shape[1]
    iq = pl.program_id(1)
    scale = HEAD_DIM ** -0.5

    @pl.when(iq == 0)
    def _prepare_keys():
        def chunk(ci, carry):
            r0 = pl.multiple_of(ci * ROPE_CHUNK, ROPE_CHUNK)
            cos = cosk_ref[pl.ds(r0, ROPE_CHUNK), :]
            sin = sink_tab_ref[pl.ds(r0, ROPE_CHUNK), :]
            for g in range(ATT_KV_HEADS):
                kg = k_ref[0, pl.ds(r0, ROPE_CHUNK), g * LANES:(g + 1) * LANES]
                ks_ref[pl.ds(r0, ROPE_CHUNK), g * LANES:(g + 1) * LANES] = _rope_head(kg, cos, sin).astype(BF16)
            vs_ref[pl.ds(r0, ROPE_CHUNK), :] = v_ref[0, pl.ds(r0, ROPE_CHUNK), :].astype(BF16)
            return carry
        lax.fori_loop(0, n // ROPE_CHUNK, chunk, 0)
        kcs_ref[...] = kvc_ref[0, :, 0:KVPAD].astype(BF16)
        vcs_ref[...] = kvc_ref[0, :, KVPAD:2 * KVPAD].astype(BF16)

    n_loc = 3 * BLOCK
    rows = ATT_REP * BLOCK
    row_i = lax.broadcasted_iota(jnp.int32, (rows, n_loc), 0)
    col_i = lax.broadcasted_iota(jnp.int32, (rows, n_loc), 1)
    rel = col_i - (row_i % BLOCK)
    head_of_row = lax.broadcasted_iota(jnp.int32, (rows, 1), 0) // BLOCK

    for j in range(ATT_TQ // BLOCK):
        blk = iq * (ATT_TQ // BLOCK) + j
        start = pl.multiple_of(jnp.clip((blk - 1) * BLOCK, 0, n - n_loc), BLOCK)
        qrows = slice(j * BLOCK, (j + 1) * BLOCK)
        cosq = cosq_ref[qrows, :]
        sinq = sinq_ref[qrows, :]
        delta = rel + (start - blk * BLOCK)
        in_window = jnp.abs(delta) <= WINDOW
        for g in range(ATT_KV_HEADS):
            heads = [ATT_REP * g + r for r in range(ATT_REP)]
            qg = jnp.concatenate(
                [(_rope_head(q_ref[0, qrows, h * LANES:(h + 1) * LANES], cosq, sinq) * scale).astype(BF16)
                 for h in heads], axis=0)
            kg = ks_ref[pl.ds(start, n_loc), g * LANES:(g + 1) * LANES]
            vg = vs_ref[pl.ds(start, n_loc), g * LANES:(g + 1) * LANES]
            kcg = kcs_ref[:, g * LANES:(g + 1) * LANES]
            vcg = vcs_ref[:, g * LANES:(g + 1) * LANES]
            nt = (((1,), (1,)), ((), ()))
            s_loc = lax.dot_general(qg, kg, nt, preferred_element_type=F32)
            s_ctx = lax.dot_general(qg, kcg, nt, preferred_element_type=F32)
            s_loc = jnp.where(in_window, s_loc, NEG_INF)
            sink_col = jnp.zeros((rows, 1), F32)
            for r, h in enumerate(heads):
                sink_col = jnp.where(head_of_row == r, sink_ref[h], sink_col)
            m = jnp.maximum(jnp.maximum(jnp.max(s_loc, axis=1, keepdims=True),
                                        jnp.max(s_ctx, axis=1, keepdims=True)), sink_col)
            p_loc = jnp.exp(s_loc - m)
            p_ctx = jnp.exp(s_ctx - m)
            den = (jnp.sum(p_loc, axis=1, keepdims=True) + jnp.sum(p_ctx, axis=1, keepdims=True)
                   + jnp.exp(sink_col - m))
            o = (jnp.dot(p_loc.astype(BF16), vg, preferred_element_type=F32)
                 + jnp.dot(p_ctx.astype(BF16), vcg, preferred_element_type=F32)) / den
            for r, h in enumerate(heads):
                o_ref[0, qrows, h * LANES:(h + 1) * LANES] = o[r * BLOCK:(r + 1) * BLOCK].astype(BF16)


def _attention(p_lat, kv_ctx, sink, cos_tab, sin_tab):
    b, n, _ = p_lat.shape
    n_ctx = kv_ctx.shape[1]
    grid_spec = pltpu.PrefetchScalarGridSpec(
        num_scalar_prefetch=1,
        grid=(b, n // ATT_TQ),
        in_specs=[
            pl.BlockSpec((1, ATT_TQ, QPAD), lambda bi, i, s: (bi, i, COL_Q // QPAD)),
            pl.BlockSpec((1, n, KVPAD), lambda bi, i, s: (bi, 0, COL_K // KVPAD)),
            pl.BlockSpec((1, n, KVPAD), lambda bi, i, s: (bi, 0, COL_V // KVPAD)),
            pl.BlockSpec((1, n_ctx, 2 * KVPAD), lambda bi, i, s: (bi, 0, 0)),
            pl.BlockSpec((ATT_TQ, LANES), lambda bi, i, s: (i, 0)),
            pl.BlockSpec((ATT_TQ, LANES), lambda bi, i, s: (i, 0)),
            pl.BlockSpec((n, LANES), lambda bi, i, s: (0, 0)),
            pl.BlockSpec((n, LANES), lambda bi, i, s: (0, 0)),
        ],
        out_specs=pl.BlockSpec((1, ATT_TQ, QPAD), lambda bi, i, s: (bi, i, 0)),
        scratch_shapes=[
            pltpu.VMEM((n, KVPAD), BF16),
            pltpu.VMEM((n, KVPAD), BF16),
            pltpu.VMEM((n_ctx, KVPAD), BF16),
            pltpu.VMEM((n_ctx, KVPAD), BF16),
        ],
    )
    return pl.pallas_call(
        _attn_body,
        grid_spec=grid_spec,
        out_shape=jax.ShapeDtypeStruct((b, n, QPAD), BF16),
        compiler_params=_cparams(("parallel", "arbitrary")),
        name="window_attention",
    )(sink, p_lat, p_lat, p_lat, kv_ctx, cos_tab, sin_tab, cos_tab, sin_tab)


def _rope_tables(n):
    rows = n // GRID_W
    row = jnp.repeat(jnp.arange(rows, dtype=F32), GRID_W)
    col = jnp.tile(jnp.arange(GRID_W, dtype=F32), rows)
    inv = ROPE_BASE ** (-jnp.arange(ROPE_FREQS, dtype=F32) / ROPE_FREQS)
    ang_r = row[:, None] * inv
    ang_c = col[:, None] * inv
    pad1 = jnp.ones((n, LANES - HEAD_DIM), F32)
    pad0 = jnp.zeros((n, LANES - HEAD_DIM), F32)
    cos = jnp.concatenate([jnp.cos(ang_r), jnp.cos(ang_r), jnp.cos(ang_c), jnp.cos(ang_c), pad1], axis=1)
    sin = jnp.concatenate([-jnp.sin(ang_r), jnp.sin(ang_r), -jnp.sin(ang_c), jnp.sin(ang_c), pad0], axis=1)
    return cos, sin


def _layer_norm(r, g, b):
    mu = jnp.mean(r, axis=-1, keepdims=True)
    var = jnp.mean(jnp.square(r - mu), axis=-1, keepdims=True)
    return (r - mu) * lax.rsqrt(var + LN_EPS) * g + b


def _outproj_body(hy_ref, att_ref, x_ref, wh_ref, wa_ref, g1_ref, sc2_ref, sh2_ref, lg_ref, lb_ref,
                  x1_ref, hq_ref):
    y = (jnp.dot(hy_ref[0].astype(BF16), wh_ref[...], preferred_element_type=F32)
         + jnp.dot(att_ref[0], wa_ref[...], preferred_element_type=F32))
    x1 = _layer_norm(DEEPNORM_ALPHA * x_ref[0] + g1_ref[0] * y, lg_ref[...], lb_ref[...])
    x1_ref[0] = x1
    hq_ref[0] = (x1 * (1.0 + sc2_ref[0]) + sh2_ref[0]).astype(BF16)


def _outproj_ln(hy, att, x, w_hy, w_att, g1, sc2, sh2, ln_g, ln_b, tm=512):
    b, n, d = x.shape
    modspec = pl.BlockSpec((1, 1, d), lambda bi, i: (bi, 0, 0))
    vecspec = pl.BlockSpec((1, d), lambda bi, i: (0, 0))
    return pl.pallas_call(
        _outproj_body,
        grid=(b, n // tm),
        in_specs=[
            pl.BlockSpec((1, tm, HY_WIDTH), lambda bi, i: (bi, i, 0)),
            pl.BlockSpec((1, tm, QPAD), lambda bi, i: (bi, i, 0)),
            pl.BlockSpec((1, tm, d), lambda bi, i: (bi, i, 0)),
            pl.BlockSpec(w_hy.shape, lambda bi, i: (0, 0)),
            pl.BlockSpec(w_att.shape, lambda bi, i: (0, 0)),
            modspec, modspec, modspec, vecspec, vecspec,
        ],
        out_specs=[
            pl.BlockSpec((1, tm, d), lambda bi, i: (bi, i, 0)),
            pl.BlockSpec((1, tm, d), lambda bi, i: (bi, i, 0)),
        ],
        out_shape=[jax.ShapeDtypeStruct((b, n, d), F32), jax.ShapeDtypeStruct((b, n, d), BF16)],
        compiler_params=_cparams(("parallel", "parallel")),
        name="outproj_ln1",
    )(hy, att, x, w_hy, w_att, g1, sc2, sh2, ln_g, ln_b)


PEER_TM = 256
TOPK_UNROLL = 4
_STAIR = sorted(((i, j) for i in range(PEER_TOPK) for j in range(PEER_TOPK) if (i + 1) * (j + 1) <= PEER_TOPK),
                key=lambda p: p[0] * PEER_TOPK + p[1])
_STAIR_ROWS = -(-len(_STAIR) // 8) * 8
_STAIR_COUNT = [PEER_TOPK // (i + 1) for i in range(PEER_TOPK)]
_STAIR_START = [sum(_STAIR_COUNT[:i]) for i in range(PEER_TOPK)]


def _stair_ids(tm):
    ids = np.full((_STAIR_ROWS,), float(PEER_TOPK * PEER_TOPK), np.float32)
    ids[:len(_STAIR)] = [i * PEER_TOPK + j for i, j in _STAIR]
    return np.repeat(ids[:, None], tm, axis=1)


def _select_topk(problems, write_row):
    tm = problems[0][0].shape[1]

    def step(k, prev):
        new = []
        for p, ((s_ref, ids, pad_id), prev_id) in enumerate(zip(problems, prev)):
            s = jnp.where(ids == prev_id, -jnp.inf, s_ref[...])
            s_ref[...] = s
            m = jnp.max(s, axis=0, keepdims=True)
            win = jnp.min(jnp.where(s == m, ids, pad_id), axis=0, keepdims=True)
            write_row(p, k, m, win)
            new.append(win)
        return tuple(new)

    lax.fori_loop(0, PEER_TOPK, step, tuple(jnp.full((1, tm), -1.0, F32) for _ in problems), unroll=TOPK_UNROLL)


def _rows_by_rank(rank, table_ref):
    out = jnp.zeros(rank.shape, F32)
    for p in range(PEER_TOPK):
        out = jnp.where(rank == float(p), table_ref[p:p + 1, :], out)
    return out


def _peer_topk_body(hq_ref, wq_ref, k1_ref, k2_ref, sid_ref, a_ref, b_ref, g_ref,
                    q_s, s1_s, s2_s, v1_s, i1_s, v2_s, i2_s, c_s, t_s, f_s, ao_s, bo_s, go_s):
    tm = hq_ref.shape[0]
    q = jnp.dot(hq_ref[...], wq_ref[...], preferred_element_type=F32)
    for c in range(2 * PEER_HEADS):
        q_s[c] = q[:, c * LANES:(c + 1) * LANES]
    nt = (((1,), (1,)), ((), ()))
    key_id = lax.broadcasted_iota(jnp.int32, (PEER_KEYS, tm), 0).astype(F32)

    def head(h, carry):
        for half, (kref, s_s) in enumerate(((k1_ref, s1_s), (k2_ref, s2_s))):
            qh = q_s[2 * h + half]
            q_hi = qh.astype(BF16)
            q_lo = (qh - q_hi.astype(F32)).astype(BF16)
            ntdot = functools.partial(lax.dot_general, dimension_numbers=nt, preferred_element_type=F32)
            s_s[...] = ntdot(kref[0], q_hi) + (ntdot(kref[0], q_lo) + ntdot(kref[1], q_hi))

        def write1(p, k, val, idx):
            vs, is_ = ((v1_s, i1_s), (v2_s, i2_s))[p]
            vs[pl.ds(k, 1), :] = val
            is_[pl.ds(k, 1), :] = idx
        _select_topk([(s1_s, key_id, float(PEER_KEYS)), (s2_s, key_id, float(PEER_KEYS))], write1)

        c_s[...] = jnp.full(c_s.shape, -jnp.inf, F32)
        for i in range(PEER_TOPK):
            r0, cnt = _STAIR_START[i], _STAIR_COUNT[i]
            c_s[r0:r0 + cnt, :] = v1_s[i:i + 1, :] + v2_s[0:cnt, :]

        def write2(p, k, val, idx):
            t_s[pl.ds(k, 1), :] = val
            f_s[pl.ds(k, 1), :] = idx
        _select_topk([(c_s, sid_ref[...], float(PEER_TOPK * PEER_TOPK))], write2)

        flat = f_s[...]
        rank1 = jnp.floor(flat * (1.0 / PEER_TOPK))
        rank2 = flat - rank1 * PEER_TOPK
        rows = pl.ds(pl.multiple_of(h * PEER_TOPK, PEER_TOPK), PEER_TOPK)
        ao_s[rows, :] = _rows_by_rank(rank1, i1_s)
        bo_s[rows, :] = _rows_by_rank(rank2, i2_s)
        t = t_s[...]
        e = jnp.exp(t - jnp.max(t, axis=0, keepdims=True))
        go_s[rows, :] = e / jnp.sum(e, axis=0, keepdims=True)
        return carry

    lax.fori_loop(0, PEER_HEADS, head, 0)
    a_ref[...] = ao_s[...].T
    b_ref[...] = bo_s[...].T
    g_ref[...] = go_s[...].T


def _peer_topk(hq, wq, keys1, keys2):
    t, d = hq.shape
    tm = PEER_TM
    hk = PEER_HEADS * PEER_TOPK
    out = jax.ShapeDtypeStruct((t, hk), F32)
    ospec = pl.BlockSpec((tm, hk), lambda i: (i, 0))
    sid = _stair_ids(tm)
    return pl.pallas_call(
        _peer_topk_body,
        grid=(t // tm,),
        in_specs=[
            pl.BlockSpec((tm, d), lambda i: (i, 0)),
            pl.BlockSpec(wq.shape, lambda i: (0, 0)),
            pl.BlockSpec(keys1.shape, lambda i: (0, 0, 0)),
            pl.BlockSpec(keys2.shape, lambda i: (0, 0, 0)),
            pl.BlockSpec(sid.shape, lambda i: (0, 0)),
        ],
        out_specs=[ospec, ospec, ospec],
        out_shape=[out, out, out],
        scratch_shapes=[
            pltpu.VMEM((2 * PEER_HEADS, tm, LANES), F32),
            pltpu.VMEM((PEER_KEYS, tm), F32), pltpu.VMEM((PEER_KEYS, tm), F32),
            pltpu.VMEM((PEER_TOPK, tm), F32), pltpu.VMEM((PEER_TOPK, tm), F32),
            pltpu.VMEM((PEER_TOPK, tm), F32), pltpu.VMEM((PEER_TOPK, tm), F32),
            pltpu.VMEM((_STAIR_ROWS, tm), F32),
            pltpu.VMEM((PEER_TOPK, tm), F32), pltpu.VMEM((PEER_TOPK, tm), F32),
            pltpu.VMEM((hk, tm), F32), pltpu.VMEM((hk, tm), F32), pltpu.VMEM((hk, tm), F32),
        ],
        compiler_params=_cparams(("parallel",)),
        name="peer_topk",
    )(hq, wq, keys1, keys2, sid)


PEER_TE = 2048
PEER_GROUPS = PEER_TE // PEER_KEYS
PEER_SCORE_TM = 512
PEER_VALUE_TM = 512
PEER_VALUE_TE = 1024
GATE_PITCH = PEER_KEYS + 8
GATE_UNROLL = 64


def _peer_scores_body(hq_ref, a_ref, b_ref, u_ref, o_ref):
    e = pl.program_id(0)
    s = lax.dot_general(hq_ref[...], u_ref[...], (((1,), (1,)), ((), ())),
                        preferred_element_type=F32)
    b_idx = b_ref[...].astype(jnp.int32)
    a_val = a_ref[...]
    cur = jnp.zeros(a_val.shape, F32)
    for jj in range(PEER_GROUPS):
        cand = jnp.take_along_axis(s[:, jj * LANES:(jj + 1) * LANES], b_idx, axis=1)
        cur = jnp.where(a_val == (e * PEER_GROUPS + jj).astype(F32), cand, cur)
    o_ref[0] = cur


def _peer_scores(hq, a, b, u_tab):
    t, d = hq.shape
    tm = PEER_SCORE_TM
    hk = a.shape[1]
    n_e = u_tab.shape[0] // PEER_TE
    tok = lambda e, i: (i, 0)
    return pl.pallas_call(
        _peer_scores_body,
        grid=(n_e, t // tm),
        in_specs=[
            pl.BlockSpec((tm, d), tok),
            pl.BlockSpec((tm, hk), tok), pl.BlockSpec((tm, hk), tok),
            pl.BlockSpec((PEER_TE, d), lambda e, i: (e, 0)),
        ],
        out_specs=pl.BlockSpec((1, tm, hk), lambda e, i: (e, i, 0)),
        out_shape=jax.ShapeDtypeStruct((n_e, t, hk), F32),
        compiler_params=_cparams(("arbitrary", "arbitrary")),
        name="peer_scores",
    )(hq, a, b, u_tab)


def _peer_values_body(ss_ref, a_ref, b_ref, g_ref, v_ref, x1_ref, g2_ref, lg_ref, lb_ref, o_ref,
                      w_s, hd_s, acc_s):
    tm = a_ref.shape[0]
    n_e = pl.num_programs(1)
    e = pl.program_id(1)
    nt = (((1,), (1,)), ((), ()))

    @pl.when(e == 0)
    def _gates():
        s_sel = jnp.sum(ss_ref[...], axis=0)
        act = 0.5 * s_sel * (1.0 + lax.erf(s_sel * (2.0 ** -0.5)))
        w_s[...] = g_ref[...] * act
        sub = lax.broadcasted_iota(jnp.int32, (PEER_KEYS, LANES), 0).astype(F32)

        def token(t, carry):
            a_row = a_ref[pl.ds(t, 1), :]
            b_row = b_ref[pl.ds(t, 1), :]
            w_row = w_s[pl.ds(t, 1), :]
            lhs = jnp.where(sub == a_row, w_row, 0.0).astype(BF16)
            rhs = jnp.where(sub == b_row, 1.0, 0.0).astype(BF16)
            tile = lax.dot_general(lhs, rhs, nt, preferred_element_type=F32)
            hd_s[pl.ds(pl.multiple_of(t * GATE_PITCH, 8), PEER_KEYS), :] = tile
            return carry
        lax.fori_loop(0, tm, token, 0, unroll=GATE_UNROLL)
        acc_s[...] = jnp.zeros(acc_s.shape, F32)

    groups = v_ref.shape[0] // PEER_KEYS
    j0 = e * groups
    lhs = jnp.concatenate(
        [hd_s[pl.ds(j0 + jj, tm, stride=GATE_PITCH), :].astype(BF16) for jj in range(groups)], axis=1)
    acc_s[...] += jnp.dot(lhs, v_ref[...], preferred_element_type=F32)

    @pl.when(e == n_e - 1)
    def _finish():
        o_ref[...] = _layer_norm(DEEPNORM_ALPHA * x1_ref[...] + g2_ref[0] * acc_s[...], lg_ref[...], lb_ref[...])


def _peer_values(ssel, a, b, g, v_tab, x1, g2, ln_g, ln_b, tokens_per_batch):
    t, d = x1.shape
    tm = PEER_VALUE_TM
    hk = a.shape[1]
    n_e = v_tab.shape[0] // PEER_VALUE_TE
    tiles_per_batch = tokens_per_batch // tm
    tok = lambda i, e: (i, 0)
    one = pl.Buffered(1)
    return pl.pallas_call(
        _peer_values_body,
        grid=(t // tm, n_e),
        in_specs=[
            pl.BlockSpec((ssel.shape[0], tm, hk), lambda i, e: (0, i, 0)),
            pl.BlockSpec((tm, hk), tok), pl.BlockSpec((tm, hk), tok), pl.BlockSpec((tm, hk), tok),
            pl.BlockSpec((PEER_VALUE_TE, d), lambda i, e: (e, 0)),
            pl.BlockSpec((tm, d), tok),
            pl.BlockSpec((1, 1, d), lambda i, e: (i // tiles_per_batch, 0, 0)),
            pl.BlockSpec((1, d), lambda i, e: (0, 0)),
            pl.BlockSpec((1, d), lambda i, e: (0, 0)),
        ],
        out_specs=pl.BlockSpec((tm, d), tok, pipeline_mode=one),
        out_shape=jax.ShapeDtypeStruct((t, d), F32),
        scratch_shapes=[
            pltpu.VMEM((tm, hk), F32),
            pltpu.VMEM((tm * GATE_PITCH, LANES), F32),
            pltpu.VMEM((tm, d), F32),
        ],
        compiler_params=_cparams(("parallel", "arbitrary")),
        name="peer_values",
    )(ssel, a, b, g, v_tab, x1, g2, ln_g, ln_b)


FFT_N2 = 128
FFT_PITCH = FFT_N2 + 8
HY_POS_PAD = 32
FFT_UNROLL = 4


def _dft_constants(n):
    big = 2 * n
    n1c = big // FFT_N2
    half = n1c // 2
    k1 = np.arange(n1c)[:, None]
    n1 = np.arange(half)[None, :]
    ang = 2 * np.pi * k1 * n1 / n1c
    c, s = np.cos(ang), np.sin(ang)
    f1 = np.block([[c, s], [-s, c]])
    k2 = np.arange(FFT_N2)[:, None]
    n2 = np.arange(FFT_N2)[None, :]
    ang = 2 * np.pi * k2 * n2 / FFT_N2
    c, s = np.cos(ang), np.sin(ang)
    d3 = np.block([[c, s], [-s, c]])
    d3i = np.block([[c, -s], [s, c]])
    ang = 2 * np.pi * n1.T * k1.T / n1c
    c, s = np.cos(ang), np.sin(ang)
    f3 = np.block([[c, -s], [s, c]]) / big
    ang = 2 * np.pi * (np.arange(n1c)[:, None] * np.arange(FFT_N2)[None, :]) / big
    twr = np.repeat(np.cos(ang).reshape(-1, 1), LANES, axis=1)
    twi = np.repeat(-np.sin(ang).reshape(-1, 1), LANES, axis=1)
    as32 = lambda a: np.asarray(a, np.float32)

    def split(a):
        parts, rest = [], np.asarray(a, np.float64)
        for _ in range(3):
            p = np.asarray(rest, BF16)
            parts.append(p)
            rest = rest - p.astype(np.float64)
        return np.stack(parts)
    return split(f1), split(d3), split(d3i), split(f3), as32(twr), as32(twi)


def _hdot(a, b):
    return jnp.dot(a, b, preferred_element_type=F32, precision=HIGHEST)


def _dft_dot(m_ref, x):
    x_hi = x.astype(BF16)
    x_lo = (x - x_hi.astype(F32)).astype(BF16)
    m_hi = m_ref[0]
    dot = functools.partial(jnp.dot, preferred_element_type=F32)
    return (dot(m_hi, x_hi) + dot(m_hi, x_lo)) + (dot(m_ref[1], x_hi) + dot(m_ref[2], x_hi))


def _fft_stage1(u_re, u_im, f1_ref, a_re, a_im):
    half = f1_ref.shape[2] // 2
    n1c = f1_ref.shape[1] // 2

    def column(n2):
        return jnp.concatenate([u_re[pl.ds(n2, half, stride=FFT_PITCH), :],
                                u_im[pl.ds(n2, half, stride=FFT_PITCH), :]], axis=0)

    def body(i, carry):
        a = _dft_dot(f1_ref, jnp.concatenate([column(2 * i), column(2 * i + 1)], axis=1))
        for s in range(2):
            a_re[pl.ds(2 * i + s, n1c, stride=FFT_PITCH), :] = a[:n1c, s * LANES:(s + 1) * LANES]
            a_im[pl.ds(2 * i + s, n1c, stride=FFT_PITCH), :] = a[n1c:, s * LANES:(s + 1) * LANES]
        return carry
    lax.fori_loop(0, FFT_N2 // 2, body, 0, unroll=FFT_UNROLL)


def _twiddled_pair(a_re, a_im, twr_ref, twi_ref, i):
    t0 = pl.multiple_of(2 * i * FFT_N2, 2 * FFT_N2)
    rows = [pl.multiple_of((2 * i + s) * FFT_PITCH, 8) for s in range(2)]
    side = lambda ref, r: jnp.concatenate([ref[pl.ds(r[0], FFT_N2), :], ref[pl.ds(r[1], FFT_N2), :]], axis=1)
    tws = [pl.multiple_of(t0 + s * FFT_N2, FFT_N2) for s in range(2)]
    return rows, side(a_re, rows), side(a_im, rows), side(twr_ref, tws), side(twi_ref, tws)


def _fft_conv_middle(a_re, a_im, d3_ref, d3i_ref, twr_ref, twi_ref, hre_ref, him_ref):
    n1c = hre_ref.shape[1]

    def body(i, carry):
        rows, ar, ai, twr, twi = _twiddled_pair(a_re, a_im, twr_ref, twi_ref, i)
        x = _dft_dot(d3_ref, jnp.concatenate([ar * twr - ai * twi, ar * twi + ai * twr], axis=0))
        xr, xi = x[:FFT_N2], x[FFT_N2:]
        hr = jnp.concatenate([hre_ref[0, 2 * i], hre_ref[0, 2 * i + 1]], axis=1)
        hi = jnp.concatenate([him_ref[0, 2 * i], him_ref[0, 2 * i + 1]], axis=1)
        y = _dft_dot(d3i_ref, jnp.concatenate([xr * hr - xi * hi, xr * hi + xi * hr], axis=0))
        br, bi = y[:FFT_N2], y[FFT_N2:]
        out_re = br * twr + bi * twi
        out_im = bi * twr - br * twi
        for s in range(2):
            a_re[pl.ds(rows[s], FFT_N2), :] = out_re[:, s * LANES:(s + 1) * LANES]
            a_im[pl.ds(rows[s], FFT_N2), :] = out_im[:, s * LANES:(s + 1) * LANES]
        return carry
    lax.fori_loop(0, n1c // 2, body, 0, unroll=FFT_UNROLL)


def _fft_stage_inv(a_re, a_im, f3_ref, y_re, y_im):
    half = f3_ref.shape[1] // 2
    n1c = f3_ref.shape[2] // 2

    def column(n2):
        return jnp.concatenate([a_re[pl.ds(n2, n1c, stride=FFT_PITCH), :],
                                a_im[pl.ds(n2, n1c, stride=FFT_PITCH), :]], axis=0)

    def body(i, carry):
        y = _dft_dot(f3_ref, jnp.concatenate([column(2 * i), column(2 * i + 1)], axis=1))
        for s in range(2):
            y_re[pl.ds(2 * i + s, half, stride=FFT_PITCH), :] = y[:half, s * LANES:(s + 1) * LANES]
            y_im[pl.ds(2 * i + s, half, stride=FFT_PITCH), :] = y[half:, s * LANES:(s + 1) * LANES]
        return carry
    lax.fori_loop(0, FFT_N2 // 2, body, 0, unroll=FFT_UNROLL)


def _short_conv_chunk(x_ref, bi, j, nblk, w_ref, b_ref):
    r0 = j * FFT_N2
    cur = x_ref[bi, r0:r0 + FFT_N2, :]
    row = lax.broadcasted_iota(jnp.int32, cur.shape, 0)
    if j == 0:
        prev = jnp.where(row == 0, 0.0, pltpu.roll(cur, 1, 0))
    else:
        prev = x_ref[bi, r0 - 1:r0 + FFT_N2 - 1, :]
    if j == nblk - 1:
        nxt = jnp.where(row == FFT_N2 - 1, 0.0, pltpu.roll(cur, FFT_N2 - 1, 0))
    else:
        nxt = x_ref[bi, r0 + 1:r0 + FFT_N2 + 1, :]
    return prev * w_ref[0:1, :] + cur * w_ref[1:2, :] + nxt * w_ref[2:3, :] + b_ref[...]


def _hy_conv_body(conv_a, a_ref, g_ref, wa_ref, ba_ref, wg_ref, bg_ref, skip_ref, hre_ref, him_ref,
                  f1_ref, d3_ref, d3i_ref, f3_ref, twr_ref, twi_ref, o_ref,
                  u_re, u_im, a_re, a_im, y_re, y_im):
    nblk = a_ref.shape[1] // FFT_N2
    for bi, dst in ((0, u_re), (1, u_im)):
        for j in range(nblk):
            if conv_a:
                blk = _short_conv_chunk(a_ref, bi, j, nblk, wa_ref, ba_ref)
            else:
                blk = a_ref[bi, j * FFT_N2:(j + 1) * FFT_N2, :]
            dst[j * FFT_PITCH:j * FFT_PITCH + FFT_N2, :] = blk
    _fft_stage1(u_re, u_im, f1_ref, a_re, a_im)
    _fft_conv_middle(a_re, a_im, d3_ref, d3i_ref, twr_ref, twi_ref, hre_ref, him_ref)
    _fft_stage_inv(a_re, a_im, f3_ref, y_re, y_im)
    skip = skip_ref[...]
    for bi, (ysrc, usrc) in enumerate(((y_re, u_re), (y_im, u_im))):
        for j in range(nblk):
            rows = slice(j * FFT_PITCH, j * FFT_PITCH + FFT_N2)
            gate = _short_conv_chunk(g_ref, bi, j, nblk, wg_ref, bg_ref)
            o_ref[bi, j * FFT_N2:(j + 1) * FFT_N2, :] = gate * (ysrc[rows, :] + usrc[rows, :] * skip)


def _hy_conv(a, a_col, g, g_col, conv_w, conv_b, skip, hre, him, order, consts, conv_a):
    b, n, _ = a.shape
    f1, d3, d3i, f3, twr, twi = consts
    n1c = f1.shape[1] // 2
    half = n1c // 2
    w = LANES
    tiles = HY_WIDTH // w
    one = pl.Buffered(1)
    data = lambda col: pl.BlockSpec((2, n, w), lambda ct, p: (p, 0, col + ct))
    wspec = lambda col: pl.BlockSpec((3, w), lambda ct, p: (0, col + ct))
    bspec = lambda col: pl.BlockSpec((1, w), lambda ct, p: (0, col + ct))
    hspec = pl.BlockSpec((1, n1c, FFT_N2, w), lambda ct, p: (order, 0, 0, ct), pipeline_mode=one)
    cs = lambda arr: pl.BlockSpec(arr.shape, lambda ct, p: (0,) * arr.ndim, pipeline_mode=one)
    a_wcol = a_col if conv_a else g_col
    return pl.pallas_call(
        functools.partial(_hy_conv_body, conv_a),
        grid=(tiles, b // 2),
        in_specs=[data(a_col), data(g_col), wspec(a_wcol), bspec(a_wcol), wspec(g_col), bspec(g_col),
                  pl.BlockSpec((1, w), lambda ct, p: (0, ct)), hspec, hspec,
                  cs(f1), cs(d3), cs(d3i), cs(f3), cs(twr), cs(twi)],
        out_specs=pl.BlockSpec((2, n, w), lambda ct, p: (p, 0, ct), pipeline_mode=one),
        out_shape=jax.ShapeDtypeStruct((b, n, HY_WIDTH), F32),
        scratch_shapes=[pltpu.VMEM((half * FFT_PITCH, w), F32), pltpu.VMEM((half * FFT_PITCH, w), F32),
                        pltpu.VMEM((n1c * FFT_PITCH, w), F32), pltpu.VMEM((n1c * FFT_PITCH, w), F32),
                        pltpu.VMEM((half * FFT_PITCH, w), F32), pltpu.VMEM((half * FFT_PITCH, w), F32)],
        compiler_params=_cparams(("parallel", "arbitrary")),
        name=f"hyena_conv{order + 1}",
    )(a, g, conv_w, conv_b, conv_w, conv_b, skip, hre, him, f1, d3, d3i, f3, twr, twi)


def _hy_filter_body(z_ref, w1_ref, b1_ref, f1_ref, w2_ref, b2_ref, f2_ref, w3_ref, b3_ref, dl_ref, o_ref):
    z = z_ref[...]
    h = jnp.sin(f1_ref[...] * (_hdot(z, w1_ref[...]) + b1_ref[...]))
    h = jnp.sin(f2_ref[...] * (_hdot(h, w2_ref[...]) + b2_ref[...]))
    h = _hdot(h, w3_ref[...]) + b3_ref[...]
    t = z[:, 0:1]
    o_ref[...] = h * (jnp.exp(-t * dl_ref[...]) + HY_WINDOW_SHIFT)


def _hy_filters(n, w1, b1, fr1, w2, b2, fr2, w3, b3):
    t = jnp.linspace(0.0, 1.0, n, dtype=F32)[:, None]
    wv = 2.0 * math.pi * jnp.arange(n, dtype=F32)[:, None] / n
    bands = jnp.linspace(1e-4, HY_BANDS - 1, HY_BANDS, dtype=F32)[None, :]
    z = jnp.concatenate([t, jnp.cos(bands * wv), -jnp.sin(bands * wv)], axis=-1)
    pos = z.shape[1]
    z = jnp.pad(z, ((0, 0), (0, HY_POS_PAD - pos)))
    w1p = jnp.pad(w1, ((0, HY_POS_PAD - pos), (0, 0)))
    min_decay = math.log(HY_DECAY_TARGET) / HY_SLOW_DECAY_PCT
    max_decay = math.log(HY_DECAY_TARGET) / HY_FAST_DECAY_PCT
    deltas = jnp.abs(jnp.linspace(min_decay, max_decay, HY_WIDTH, dtype=F32))
    n_out = w3.shape[1]
    dl = jnp.tile(deltas, n_out // HY_WIDTH)[None, :]
    tn = 512
    hid = w2.shape[0]
    full = lambda shape: pl.BlockSpec(shape, lambda i: (0, 0))
    return pl.pallas_call(
        _hy_filter_body,
        grid=(n // tn,),
        in_specs=[pl.BlockSpec((tn, HY_POS_PAD), lambda i: (i, 0)), full((HY_POS_PAD, hid)), full((1, hid)),
                  full((1, hid)), full((hid, hid)), full((1, hid)), full((1, hid)), full((hid, n_out)),
                  full((1, n_out)), full((1, n_out))],
        out_specs=pl.BlockSpec((tn, n_out), lambda i: (i, 0)),
        out_shape=jax.ShapeDtypeStruct((n, n_out), F32),
        compiler_params=_cparams(("parallel",)),
        name="hyena_filters",
    )(z, w1p, b1[None], fr1[None], w2, b2[None], fr2[None], w3, b3[None], dl)


def _hy_spectrum_body(f_ref, b_ref, f1_ref, d3_ref, twr_ref, twi_ref, hre_ref, him_ref, u_re, u_im, a_re, a_im):
    nblk = f_ref.shape[0] // FFT_N2
    n1c = hre_ref.shape[1]
    for part, out_ref in enumerate((hre_ref, him_ref)):
        for j in range(nblk):
            f = f_ref[j * FFT_N2:(j + 1) * FFT_N2, :]
            bw = b_ref[j * FFT_N2:(j + 1) * FFT_N2, :]
            if j == 0:
                row = lax.broadcasted_iota(jnp.int32, bw.shape, 0)
                bw = jnp.where(row == 0, 0.0, bw)
            rows = slice(j * FFT_PITCH, j * FFT_PITCH + FFT_N2)
            u_re[rows, :] = f + bw if part == 0 else f - bw
            u_im[rows, :] = jnp.zeros((FFT_N2, LANES), F32)
        _fft_stage1(u_re, u_im, f1_ref, a_re, a_im)

        def body(i, carry, part=part, out_ref=out_ref):
            _, ar, ai, twr, twi = _twiddled_pair(a_re, a_im, twr_ref, twi_ref, i)
            x = _dft_dot(d3_ref, jnp.concatenate([ar * twr - ai * twi, ar * twi + ai * twr], axis=0))
            x = x[:FFT_N2] if part == 0 else x[FFT_N2:]
            for s in range(2):
                out_ref[0, 2 * i + s] = x[:, s * LANES:(s + 1) * LANES]
            return carry
        lax.fori_loop(0, n1c // 2, body, 0, unroll=FFT_UNROLL)


def _hy_spectrum(h, consts):
    n = h.shape[0]
    f1, d3, _, _, twr, twi = consts
    n1c = f1.shape[1] // 2
    half = n1c // 2
    tiles = HY_WIDTH // LANES
    cs = lambda arr: pl.BlockSpec(arr.shape, lambda o, ct: (0,) * arr.ndim)
    out = jax.ShapeDtypeStruct((HY_ORDER, n1c, FFT_N2, HY_WIDTH), F32)
    ospec = pl.BlockSpec((1, n1c, FFT_N2, LANES), lambda o, ct: (o, 0, 0, ct))
    return pl.pallas_call(
        _hy_spectrum_body,
        grid=(HY_ORDER, tiles),
        in_specs=[pl.BlockSpec((n, LANES), lambda o, ct: (0, o * tiles + ct)),
                  pl.BlockSpec((n, LANES), lambda o, ct: (0, (HY_ORDER + o) * tiles + ct)),
                  cs(f1), cs(d3), cs(twr), cs(twi)],
        out_specs=[ospec, ospec],
        out_shape=[out, out],
        scratch_shapes=[pltpu.VMEM((half * FFT_PITCH, LANES), F32), pltpu.VMEM((half * FFT_PITCH, LANES), F32),
                        pltpu.VMEM((n1c * FFT_PITCH, LANES), F32), pltpu.VMEM((n1c * FFT_PITCH, LANES), F32)],
        compiler_params=_cparams(("parallel", "parallel")),
        name="hyena_spectrum",
    )(h, h, f1, d3, twr, twi)


def _hyena(p_lat, conv_w, conv_b, w1, b1, fr1, w2, b2, fr2, w3, b3, skip):
    n = p_lat.shape[1]
    consts = _dft_constants(n)
    h = _hy_filters(n, w1, b1, fr1, w2, b2, fr2, w3, b3)
    hre, him = _hy_spectrum(h, consts)
    tiles = HY_WIDTH // LANES
    col = COL_HY // LANES
    cw = jnp.pad(conv_w, ((0, 0), (COL_HY, 0)))
    cb = jnp.pad(conv_b[None], ((0, 0), (COL_HY, 0)))
    y = _hy_conv(p_lat, col, p_lat, col + tiles, cw, cb, skip[0][None], hre, him, 0, consts, True)
    return _hy_conv(y, 0, p_lat, col + 2 * tiles, cw, cb, skip[1][None], hre, him, 1, consts, False)


def _split_bf16(w):
    hi = w.astype(BF16)
    return jnp.stack([hi, (w - hi.astype(F32)).astype(BF16)])


def _pad_heads(w, heads):
    d = w.shape[0]
    w = w.reshape(d, heads, HEAD_DIM)
    w = jnp.pad(w, ((0, 0), (0, 0), (0, LANES - HEAD_DIM)))
    return w.reshape(d, heads * LANES)


def kernel(x, c, ctx, c_ctx, w_mod, b_mod, w_in, hy_conv_w, hy_conv_b, hy_f_w1, hy_f_b1, hy_f_freq1, hy_f_w2,
           hy_f_b2, hy_f_freq2, hy_f_w3, hy_f_b3, hy_skip, attn_sink, w_out, ln1_g, ln1_b, peer_wq, peer_keys1,
           peer_keys2, peer_u, peer_v, ln2_g, ln2_b):
    b, n, d = x.shape
    l = 0
    cc = jnp.concatenate([c, c_ctx[None], jnp.zeros((8 - b - 1, d), F32)], axis=0)
    mod = _modulation(cc, w_mod[l], b_mod[l][None])
    mod_lat = mod[:b].reshape(b, 6, 1, d)
    sh1, sc1, g1, sh2, sc2, g2 = (mod_lat[:, i] for i in range(6))
    mod_c = mod[b].reshape(6, 1, 1, d)
    csh1, csc1 = mod_c[0], mod_c[1]

    w = w_in[l]
    w_q = _pad_heads(w[:, PROJ_HY:KV_START], ATT_HEADS)
    w_k = _pad_heads(w[:, KV_START:KV_START + PROJ_KV], ATT_KV_HEADS)
    w_v = _pad_heads(w[:, KV_START + PROJ_KV:], ATT_KV_HEADS)
    w_pad = jnp.concatenate([w_q, w[:, :PROJ_HY], w_k, w_v], axis=1).astype(BF16)
    w_kv = jnp.concatenate([w_k, w_v], axis=1).astype(BF16)

    p_lat = _mod_matmul(x, sc1, sh1, w_pad, 512, "in_proj")
    kv_ctx = _mod_matmul(ctx, csc1, csh1, w_kv, ctx.shape[1], "ctx_kv_proj")

    cos_tab, sin_tab = _rope_tables(n)
    att = _attention(p_lat, kv_ctx, attn_sink[l], cos_tab, sin_tab)

    hy = _hyena(p_lat, hy_conv_w[l], hy_conv_b[l], hy_f_w1[l], hy_f_b1[l], hy_f_freq1[l], hy_f_w2[l], hy_f_b2[l],
                hy_f_freq2[l], hy_f_w3[l], hy_f_b3[l], hy_skip[l])

    wo = w_out[l]
    w_o_hy = wo[:HY_WIDTH].astype(BF16)
    w_o_att = jnp.pad(wo[HY_WIDTH:].reshape(ATT_HEADS, HEAD_DIM, d),
                      ((0, 0), (0, LANES - HEAD_DIM), (0, 0))).reshape(QPAD, d).astype(BF16)
    x1, hq = _outproj_ln(hy, att, x, w_o_hy, w_o_att, g1, sc2, sh2, ln1_g[l][None], ln1_b[l][None])

    hq2 = hq.reshape(b * n, d)
    a_idx, b_idx, gate = _peer_topk(hq2, peer_wq[l].astype(BF16), _split_bf16(peer_keys1[l]),
                                    _split_bf16(peer_keys2[l]))
    s_sel = _peer_scores(hq2, a_idx, b_idx, peer_u[l].astype(BF16))
    out = _peer_values(s_sel, a_idx, b_idx, gate, peer_v[l].astype(BF16), x1.reshape(b * n, d), g2,
                       ln2_g[l][None], ln2_b[l][None], n)
    return out.reshape(b, n, d)
```

```python
import functools
import math

import jax
import jax.numpy as jnp
import numpy as np
from jax import lax
from jax.experimental import pallas as pl
from jax.experimental.pallas import tpu as pltpu

F32 = jnp.float32
BF16 = jnp.bfloat16
HIGHEST = lax.Precision.HIGHEST

LANES = 128
VMEM_LIMIT = 60000 * 1024

D_MODEL = 1024
GRID_W = 64
HY_WIDTH = 512
HY_ORDER = 2
HY_BANDS = 8
HY_DECAY_TARGET = 1e-2
HY_FAST_DECAY_PCT = 0.3
HY_SLOW_DECAY_PCT = 1.5
HY_WINDOW_SHIFT = 0.05
ATT_HEADS = 8
ATT_KV_HEADS = 2
ATT_REP = ATT_HEADS // ATT_KV_HEADS
HEAD_DIM = 64
WINDOW = 128
BLOCK = 128
ROPE_BASE = 10000.0
ROPE_FREQS = HEAD_DIM // 4
PROJ_HY = (HY_ORDER + 1) * HY_WIDTH
PROJ_Q = ATT_HEADS * HEAD_DIM
PROJ_KV = ATT_KV_HEADS * HEAD_DIM
KV_START = PROJ_HY + PROJ_Q
PEER_KEYS = 128
PEER_HEADS = 8
PEER_QDIM = 256
PEER_TOPK = 16
LN_EPS = 1e-5
NEG_INF = -1e30
DEPTH = 1
DEEPNORM_ALPHA = (2.0 * DEPTH) ** 0.25

QPAD = ATT_HEADS * LANES
KVPAD = ATT_KV_HEADS * LANES
COL_Q = 0
COL_HY = QPAD
COL_K = QPAD + PROJ_HY
COL_V = COL_K + KVPAD
PROJ_PAD = COL_V + KVPAD


def _cparams(sem):
    return pltpu.CompilerParams(dimension_semantics=sem, vmem_limit_bytes=VMEM_LIMIT)


def _mod_body(c_ref, w_ref, b_ref, o_ref):
    c = c_ref[...]
    a = c * jax.nn.sigmoid(c)
    o_ref[...] = jnp.dot(a, w_ref[...], preferred_element_type=F32, precision=HIGHEST) + b_ref[...]


def _modulation(cc, w_mod, b_mod):
    rows, d = cc.shape
    n_out = w_mod.shape[1]
    tn = 1536
    return pl.pallas_call(
        _mod_body,
        grid=(n_out // tn,),
        in_specs=[
            pl.BlockSpec((rows, d), lambda j: (0, 0)),
            pl.BlockSpec((d, tn), lambda j: (0, j)),
            pl.BlockSpec((1, tn), lambda j: (0, j)),
        ],
        out_specs=pl.BlockSpec((rows, tn), lambda j: (0, j)),
        out_shape=jax.ShapeDtypeStruct((rows, n_out), F32),
        compiler_params=_cparams(("arbitrary",)),
        name="modulation",
    )(cc, w_mod, b_mod)


def _mod_matmul_body(x_ref, sc_ref, sh_ref, w_ref, o_ref):
    h = x_ref[0] * (1.0 + sc_ref[0]) + sh_ref[0]
    o_ref[0] = jnp.dot(h.astype(BF16), w_ref[...], preferred_element_type=F32)


def _mod_matmul(x, sc, sh, w, tm, name):
    b, n, d = x.shape
    n_out = w.shape[1]
    per_batch = sc.shape[0] == b
    mod_map = (lambda bi, i: (bi, 0, 0)) if per_batch else (lambda bi, i: (0, 0, 0))
    return pl.pallas_call(
        _mod_matmul_body,
        grid=(b, n // tm),
        in_specs=[
            pl.BlockSpec((1, tm, d), lambda bi, i: (bi, i, 0)),
            pl.BlockSpec((1, 1, d), mod_map),
            pl.BlockSpec((1, 1, d), mod_map),
            pl.BlockSpec((d, n_out), lambda bi, i: (0, 0)),
        ],
        out_specs=pl.BlockSpec((1, tm, n_out), lambda bi, i: (bi, i, 0)),
        out_shape=jax.ShapeDtypeStruct((b, n, n_out), F32),
        compiler_params=_cparams(("parallel", "parallel")),
        name=name,
    )(x, sc, sh, w)


def _rope_head(x, cos, sin_signed):
    lane = lax.broadcasted_iota(jnp.int32, x.shape, 1)
    first_half = (lane % 32) < 16
    partner = jnp.where(first_half, pltpu.roll(x, LANES - 16, 1), pltpu.roll(x, 16, 1))
    return x * cos + partner * sin_signed


ATT_TQ = 512
ROPE_CHUNK = 512


def _attn_body(sink_ref, q_ref, k_ref, v_ref, kvc_ref, cosq_ref, sinq_ref, cosk_ref, sink_tab_ref,
               o_ref, ks_ref, vs_ref, kcs_ref, vcs_ref):
    n = k_ref.shape[1]
    iq = pl.program_id(1)
    scale = HEAD_DIM ** -0.5

    @pl.when(iq == 0)
    def _prepare_keys():
        def chunk(ci, carry):
            r0 = pl.multiple_of(ci * ROPE_CHUNK, ROPE_CHUNK)
            cos = cosk_ref[pl.ds(r0, ROPE_CHUNK), :]
            sin = sink_tab_ref[pl.ds(r0, ROPE_CHUNK), :]
            for g in range(ATT_KV_HEADS):
                kg = k_ref[0, pl.ds(r0, ROPE_CHUNK), g * LANES:(g + 1) * LANES]
                ks_ref[pl.ds(r0, ROPE_CHUNK), g * LANES:(g + 1) * LANES] = _rope_head(kg, cos, sin).astype(BF16)
            vs_ref[pl.ds(r0, ROPE_CHUNK), :] = v_ref[0, pl.ds(r0, ROPE_CHUNK), :].astype(BF16)
            return carry
        lax.fori_loop(0, n // ROPE_CHUNK, chunk, 0)
        kcs_ref[...] = kvc_ref[0, :, 0:KVPAD].astype(BF16)
        vcs_ref[...] = kvc_ref[0, :, KVPAD:2 * KVPAD].astype(BF16)

    n_loc = 3 * BLOCK
    rows = ATT_REP * BLOCK
    row_i = lax.broadcasted_iota(jnp.int32, (rows, n_loc), 0)
    col_i = lax.broadcasted_iota(jnp.int32, (rows, n_loc), 1)
    rel = col_i - (row_i % BLOCK)
    head_of_row = lax.broadcasted_iota(jnp.int32, (rows, 1), 0) // BLOCK

    for j in range(ATT_TQ // BLOCK):
        blk = iq * (ATT_TQ // BLOCK) + j
        start = pl.multiple_of(jnp.clip((blk - 1) * BLOCK, 0, n - n_loc), BLOCK)
        qrows = slice(j * BLOCK, (j + 1) * BLOCK)
        cosq = cosq_ref[qrows, :]
        sinq = sinq_ref[qrows, :]
        delta = rel + (start - blk * BLOCK)
        in_window = jnp.abs(delta) <= WINDOW
        for g in range(ATT_KV_HEADS):
            heads = [ATT_REP * g + r for r in range(ATT_REP)]
            qg = jnp.concatenate(
                [(_rope_head(q_ref[0, qrows, h * LANES:(h + 1) * LANES], cosq, sinq) * scale).astype(BF16)
                 for h in heads], axis=0)
            kg = ks_ref[pl.ds(start, n_loc), g * LANES:(g + 1) * LANES]
            vg = vs_ref[pl.ds(start, n_loc), g * LANES:(g + 1) * LANES]
            kcg = kcs_ref[:, g * LANES:(g + 1) * LANES]
            vcg = vcs_ref[:, g * LANES:(g + 1) * LANES]
            nt = (((1,), (1,)), ((), ()))
            s_loc = lax.dot_general(qg, kg, nt, preferred_element_type=F32)
            s_ctx = lax.dot_general(qg, kcg, nt, preferred_element_type=F32)
            s_loc = jnp.where(in_window, s_loc, NEG_INF)
            sink_col = jnp.zeros((rows, 1), F32)
            for r, h in enumerate(heads):
                sink_col = jnp.where(head_of_row == r, sink_ref[h], sink_col)
            m = jnp.maximum(jnp.maximum(jnp.max(s_loc, axis=1, keepdims=True),
                                        jnp.max(s_ctx, axis=1, keepdims=True)), sink_col)
            p_loc = jnp.exp(s_loc - m)
            p_ctx = jnp.exp(s_ctx - m)
            den = (jnp.sum(p_loc, axis=1, keepdims=True) + jnp.sum(p_ctx, axis=1, keepdims=True)
                   + jnp.exp(sink_col - m))
            o = (jnp.dot(p_loc.astype(BF16), vg, preferred_element_type=F32)
                 + jnp.dot(p_ctx.astype(BF16), vcg, preferred_element_type=F32)) / den
            for r, h in enumerate(heads):
                o_ref[0, qrows, h * LANES:(h + 1) * LANES] = o[r * BLOCK:(r + 1) * BLOCK].astype(BF16)


def _attention(p_lat, kv_ctx, sink, cos_tab, sin_tab):
    b, n, _ = p_lat.shape
    n_ctx = kv_ctx.shape[1]
    grid_spec = pltpu.PrefetchScalarGridSpec(
        num_scalar_prefetch=1,
        grid=(b, n // ATT_TQ),
        in_specs=[
            pl.BlockSpec((1, ATT_TQ, QPAD), lambda bi, i, s: (bi, i, COL_Q // QPAD)),
            pl.BlockSpec((1, n, KVPAD), lambda bi, i, s: (bi, 0, COL_K // KVPAD)),
            pl.BlockSpec((1, n, KVPAD), lambda bi, i, s: (bi, 0, COL_V // KVPAD)),
            pl.BlockSpec((1, n_ctx, 2 * KVPAD), lambda bi, i, s: (bi, 0, 0)),
            pl.BlockSpec((ATT_TQ, LANES), lambda bi, i, s: (i, 0)),
            pl.BlockSpec((ATT_TQ, LANES), lambda bi, i, s: (i, 0)),
            pl.BlockSpec((n, LANES), lambda bi, i, s: (0, 0)),
            pl.BlockSpec((n, LANES), lambda bi, i, s: (0, 0)),
        ],
        out_specs=pl.BlockSpec((1, ATT_TQ, QPAD), lambda bi, i, s: (bi, i, 0)),
        scratch_shapes=[
            pltpu.VMEM((n, KVPAD), BF16),
            pltpu.VMEM((n, KVPAD), BF16),
            pltpu.VMEM((n_ctx, KVPAD), BF16),
            pltpu.VMEM((n_ctx, KVPAD), BF16),
        ],
    )
    return pl.pallas_call(
        _attn_body,
        grid_spec=grid_spec,
        out_shape=jax.ShapeDtypeStruct((b, n, QPAD), BF16),
        compiler_params=_cparams(("parallel", "arbitrary")),
        name="window_attention",
    )(sink, p_lat, p_lat, p_lat, kv_ctx, cos_tab, sin_tab, cos_tab, sin_tab)


def _rope_tables(n):
    rows = n // GRID_W
    row = jnp.repeat(jnp.arange(rows, dtype=F32), GRID_W)
    col = jnp.tile(jnp.arange(GRID_W, dtype=F32), rows)
    inv = ROPE_BASE ** (-jnp.arange(ROPE_FREQS, dtype=F32) / ROPE_FREQS)
    ang_r = row[:, None] * inv
    ang_c = col[:, None] * inv
    pad1 = jnp.ones((n, LANES - HEAD_DIM), F32)
    pad0 = jnp.zeros((n, LANES - HEAD_DIM), F32)
    cos = jnp.concatenate([jnp.cos(ang_r), jnp.cos(ang_r), jnp.cos(ang_c), jnp.cos(ang_c), pad1], axis=1)
    sin = jnp.concatenate([-jnp.sin(ang_r), jnp.sin(ang_r), -jnp.sin(ang_c), jnp.sin(ang_c), pad0], axis=1)
    return cos, sin


def _layer_norm(r, g, b):
    mu = jnp.mean(r, axis=-1, keepdims=True)
    var = jnp.mean(jnp.square(r - mu), axis=-1, keepdims=True)
    return (r - mu) * lax.rsqrt(var + LN_EPS) * g + b


def _outproj_body(hy_ref, att_ref, x_ref, wh_ref, wa_ref, g1_ref, sc2_ref, sh2_ref, lg_ref, lb_ref,
                  x1_ref, hq_ref):
    y = (jnp.dot(hy_ref[0].astype(BF16), wh_ref[...], preferred_element_type=F32)
         + jnp.dot(att_ref[0], wa_ref[...], preferred_element_type=F32))
    x1 = _layer_norm(DEEPNORM_ALPHA * x_ref[0] + g1_ref[0] * y, lg_ref[...], lb_ref[...])
    x1_ref[0] = x1
    hq_ref[0] = (x1 * (1.0 + sc2_ref[0]) + sh2_ref[0]).astype(BF16)


def _outproj_ln(hy, att, x, w_hy, w_att, g1, sc2, sh2, ln_g, ln_b, tm=512):
    b, n, d = x.shape
    modspec = pl.BlockSpec((1, 1, d), lambda bi, i: (bi, 0, 0))
    vecspec = pl.BlockSpec((1, d), lambda bi, i: (0, 0))
    return pl.pallas_call(
        _outproj_body,
        grid=(b, n // tm),
        in_specs=[
            pl.BlockSpec((1, tm, HY_WIDTH), lambda bi, i: (bi, i, 0)),
            pl.BlockSpec((1, tm, QPAD), lambda bi, i: (bi, i, 0)),
            pl.BlockSpec((1, tm, d), lambda bi, i: (bi, i, 0)),
            pl.BlockSpec(w_hy.shape, lambda bi, i: (0, 0)),
            pl.BlockSpec(w_att.shape, lambda bi, i: (0, 0)),
            modspec, modspec, modspec, vecspec, vecspec,
        ],
        out_specs=[
            pl.BlockSpec((1, tm, d), lambda bi, i: (bi, i, 0)),
            pl.BlockSpec((1, tm, d), lambda bi, i: (bi, i, 0)),
        ],
        out_shape=[jax.ShapeDtypeStruct((b, n, d), F32), jax.ShapeDtypeStruct((b, n, d), BF16)],
        compiler_params=_cparams(("parallel", "parallel")),
        name="outproj_ln1",
    )(hy, att, x, w_hy, w_att, g1, sc2, sh2, ln_g, ln_b)


PEER_TM = 256
TOPK_UNROLL = 4
_STAIR = sorted(((i, j) for i in range(PEER_TOPK) for j in range(PEER_TOPK) if (i + 1) * (j + 1) <= PEER_TOPK),
                key=lambda p: p[0] * PEER_TOPK + p[1])
_STAIR_ROWS = -(-len(_STAIR) // 8) * 8
_STAIR_COUNT = [PEER_TOPK // (i + 1) for i in range(PEER_TOPK)]
_STAIR_START = [sum(_STAIR_COUNT[:i]) for i in range(PEER_TOPK)]


def _stair_ids(tm):
    ids = np.full((_STAIR_ROWS,), float(PEER_TOPK * PEER_TOPK), np.float32)
    ids[:len(_STAIR)] = [i * PEER_TOPK + j for i, j in _STAIR]
    return np.repeat(ids[:, None], tm, axis=1)


def _select_topk(problems, write_row):
    tm = problems[0][0].shape[1]

    def step(k, prev):
        new = []
        for p, ((s_ref, ids, pad_id), prev_id) in enumerate(zip(problems, prev)):
            s = jnp.where(ids == prev_id, -jnp.inf, s_ref[...])
            s_ref[...] = s
            m = jnp.max(s, axis=0, keepdims=True)
            win = jnp.min(jnp.where(s == m, ids, pad_id), axis=0, keepdims=True)
            write_row(p, k, m, win)
            new.append(win)
        return tuple(new)

    lax.fori_loop(0, PEER_TOPK, step, tuple(jnp.full((1, tm), -1.0, F32) for _ in problems), unroll=TOPK_UNROLL)


def _rows_by_rank(rank, table_ref):
    out = jnp.zeros(rank.shape, F32)
    for p in range(PEER_TOPK):
        out = jnp.where(rank == float(p), table_ref[p:p + 1, :], out)
    return out


def _peer_topk_body(hq_ref, wq_ref, k1_ref, k2_ref, sid_ref, a_ref, b_ref, g_ref,
                    q_s, s1_s, s2_s, v1_s, i1_s, v2_s, i2_s, c_s, t_s, f_s, ao_s, bo_s, go_s):
    tm = hq_ref.shape[0]
    q = jnp.dot(hq_ref[...], wq_ref[...], preferred_element_type=F32)
    for c in range(2 * PEER_HEADS):
        q_s[c] = q[:, c * LANES:(c + 1) * LANES]
    nt = (((1,), (1,)), ((), ()))
    key_id = lax.broadcasted_iota(jnp.int32, (PEER_KEYS, tm), 0).astype(F32)

    def head(h, carry):
        for half, (kref, s_s) in enumerate(((k1_ref, s1_s), (k2_ref, s2_s))):
            qh = q_s[2 * h + half]
            q_hi = qh.astype(BF16)
            q_lo = (qh - q_hi.astype(F32)).astype(BF16)
            ntdot = functools.partial(lax.dot_general, dimension_numbers=nt, preferred_element_type=F32)
            s_s[...] = ntdot(kref[0], q_hi) + (ntdot(kref[0], q_lo) + ntdot(kref[1], q_hi))

        def write1(p, k, val, idx):
            vs, is_ = ((v1_s, i1_s), (v2_s, i2_s))[p]
            vs[pl.ds(k, 1), :] = val
            is_[pl.ds(k, 1), :] = idx
        _select_topk([(s1_s, key_id, float(PEER_KEYS)), (s2_s, key_id, float(PEER_KEYS))], write1)

        c_s[...] = jnp.full(c_s.shape, -jnp.inf, F32)
        for i in range(PEER_TOPK):
            r0, cnt = _STAIR_START[i], _STAIR_COUNT[i]
            c_s[r0:r0 + cnt, :] = v1_s[i:i + 1, :] + v2_s[0:cnt, :]

        def write2(p, k, val, idx):
            t_s[pl.ds(k, 1), :] = val
            f_s[pl.ds(k, 1), :] = idx
        _select_topk([(c_s, sid_ref[...], float(PEER_TOPK * PEER_TOPK))], write2)

        flat = f_s[...]
        rank1 = jnp.floor(flat * (1.0 / PEER_TOPK))
        rank2 = flat - rank1 * PEER_TOPK
        rows = pl.ds(pl.multiple_of(h * PEER_TOPK, PEER_TOPK), PEER_TOPK)
        ao_s[rows, :] = _rows_by_rank(rank1, i1_s)
        bo_s[rows, :] = _rows_by_rank(rank2, i2_s)
        t = t_s[...]
        e = jnp.exp(t - jnp.max(t, axis=0, keepdims=True))
        go_s[rows, :] = e / jnp.sum(e, axis=0, keepdims=True)
        return carry

    lax.fori_loop(0, PEER_HEADS, head, 0)
    a_ref[...] = ao_s[...].T
    b_ref[...] = bo_s[...].T
    g_ref[...] = go_s[...].T


def _peer_topk(hq, wq, keys1, keys2):
    t, d = hq.shape
    tm = PEER_TM
    hk = PEER_HEADS * PEER_TOPK
    out = jax.ShapeDtypeStruct((t, hk), F32)
    ospec = pl.BlockSpec((tm, hk), lambda i: (i, 0))
    sid = _stair_ids(tm)
    return pl.pallas_call(
        _peer_topk_body,
        grid=(t // tm,),
        in_specs=[
            pl.BlockSpec((tm, d), lambda i: (i, 0)),
            pl.BlockSpec(wq.shape, lambda i: (0, 0)),
            pl.BlockSpec(keys1.shape, lambda i: (0, 0, 0)),
            pl.BlockSpec(keys2.shape, lambda i: (0, 0, 0)),
            pl.BlockSpec(sid.shape, lambda i: (0, 0)),
        ],
        out_specs=[ospec, ospec, ospec],
        out_shape=[out, out, out],
        scratch_shapes=[
            pltpu.VMEM((2 * PEER_HEADS, tm, LANES), F32),
            pltpu.VMEM((PEER_KEYS, tm), F32), pltpu.VMEM((PEER_KEYS, tm), F32),
            pltpu.VMEM((PEER_TOPK, tm), F32), pltpu.VMEM((PEER_TOPK, tm), F32),
            pltpu.VMEM((PEER_TOPK, tm), F32), pltpu.VMEM((PEER_TOPK, tm), F32),
            pltpu.VMEM((_STAIR_ROWS, tm), F32),
            pltpu.VMEM((PEER_TOPK, tm), F32), pltpu.VMEM((PEER_TOPK, tm), F32),
            pltpu.VMEM((hk, tm), F32), pltpu.VMEM((hk, tm), F32), pltpu.VMEM((hk, tm), F32),
        ],
        compiler_params=_cparams(("parallel",)),
        name="peer_topk",
    )(hq, wq, keys1, keys2, sid)


PEER_TE = 8192
PEER_GROUPS = PEER_TE // PEER_KEYS
PEER_SCORE_TM = 512
PEER_VALUE_TM = 512
PEER_VALUE_TE = 1024
GATE_PITCH = PEER_KEYS + 8
GATE_UNROLL = 64


def _peer_scores_body(hq_ref, a_ref, b_ref, u_ref, o_ref):
    e = pl.program_id(0)
    s = lax.dot_general(hq_ref[...], u_ref[...], (((1,), (1,)), ((), ())),
                        preferred_element_type=F32)
    b_idx = b_ref[...].astype(jnp.int32)
    a_val = a_ref[...]
    cur = jnp.zeros(a_val.shape, F32)
    for jj in range(PEER_GROUPS):
        cand = jnp.take_along_axis(s[:, jj * LANES:(jj + 1) * LANES], b_idx, axis=1)
        cur = jnp.where(a_val == (e * PEER_GROUPS + jj).astype(F32), cand, cur)
    o_ref[0] = cur


def _peer_scores(hq, a, b, u_tab):
    t, d = hq.shape
    tm = PEER_SCORE_TM
    hk = a.shape[1]
    n_e = u_tab.shape[0] // PEER_TE
    tok = lambda e, i: (i, 0)
    return pl.pallas_call(
        _peer_scores_body,
        grid=(n_e, t // tm),
        in_specs=[
            pl.BlockSpec((tm, d), tok),
            pl.BlockSpec((tm, hk), tok), pl.BlockSpec((tm, hk), tok),
            pl.BlockSpec((PEER_TE, d), lambda e, i: (e, 0)),
        ],
        out_specs=pl.BlockSpec((1, tm, hk), lambda e, i: (e, i, 0)),
        out_shape=jax.ShapeDtypeStruct((n_e, t, hk), F32),
        compiler_params=_cparams(("arbitrary", "arbitrary")),
        name="peer_scores",
    )(hq, a, b, u_tab)


def _peer_values_body(ss_ref, a_ref, b_ref, g_ref, v_ref, x1_ref, g2_ref, lg_ref, lb_ref, o_ref,
                      w_s, hd_s, acc_s):
    tm = a_ref.shape[0]
    n_e = pl.num_programs(1)
    e = pl.program_id(1)
    nt = (((1,), (1,)), ((), ()))

    @pl.when(e == 0)
    def _gates():
        s_sel = jnp.sum(ss_ref[...], axis=0)
        act = 0.5 * s_sel * (1.0 + lax.erf(s_sel * (2.0 ** -0.5)))
        w_s[...] = g_ref[...] * act
        sub = lax.broadcasted_iota(jnp.int32, (PEER_KEYS, LANES), 0).astype(F32)

        def token(t, carry):
            a_row = a_ref[pl.ds(t, 1), :]
            b_row = b_ref[pl.ds(t, 1), :]
            w_row = w_s[pl.ds(t, 1), :]
            lhs = jnp.where(sub == a_row, w_row, 0.0).astype(BF16)
            rhs = jnp.where(sub == b_row, 1.0, 0.0).astype(BF16)
            tile = lax.dot_general(lhs, rhs, nt, preferred_element_type=F32)
            hd_s[pl.ds(pl.multiple_of(t * GATE_PITCH, 8), PEER_KEYS), :] = tile
            return carry
        lax.fori_loop(0, tm, token, 0, unroll=GATE_UNROLL)
        acc_s[...] = jnp.zeros(acc_s.shape, F32)

    groups = v_ref.shape[0] // PEER_KEYS
    j0 = e * groups
    lhs = jnp.concatenate(
        [hd_s[pl.ds(j0 + jj, tm, stride=GATE_PITCH), :].astype(BF16) for jj in range(groups)], axis=1)
    acc_s[...] += jnp.dot(lhs, v_ref[...], preferred_element_type=F32)

    @pl.when(e == n_e - 1)
    def _finish():
        o_ref[...] = _layer_norm(DEEPNORM_ALPHA * x1_ref[...] + g2_ref[0] * acc_s[...], lg_ref[...], lb_ref[...])


def _peer_values(ssel, a, b, g, v_tab, x1, g2, ln_g, ln_b, tokens_per_batch):
    t, d = x1.shape
    tm = PEER_VALUE_TM
    hk = a.shape[1]
    n_e = v_tab.shape[0] // PEER_VALUE_TE
    tiles_per_batch = tokens_per_batch // tm
    tok = lambda i, e: (i, 0)
    one = pl.Buffered(1)
    return pl.pallas_call(
        _peer_values_body,
        grid=(t // tm, n_e),
        in_specs=[
            pl.BlockSpec((ssel.shape[0], tm, hk), lambda i, e: (0, i, 0)),
            pl.BlockSpec((tm, hk), tok), pl.BlockSpec((tm, hk), tok), pl.BlockSpec((tm, hk), tok),
            pl.BlockSpec((PEER_VALUE_TE, d), lambda i, e: (e, 0)),
            pl.BlockSpec((tm, d), tok),
            pl.BlockSpec((1, 1, d), lambda i, e: (i // tiles_per_batch, 0, 0)),
            pl.BlockSpec((1, d), lambda i, e: (0, 0)),
            pl.BlockSpec((1, d), lambda i, e: (0, 0)),
        ],
        out_specs=pl.BlockSpec((tm, d), tok, pipeline_mode=one),
        out_shape=jax.ShapeDtypeStruct((t, d), F32),
        scratch_shapes=[
            pltpu.VMEM((tm, hk), F32),
            pltpu.VMEM((tm * GATE_PITCH, LANES), F32),
            pltpu.VMEM((tm, d), F32),
        ],
        compiler_params=_cparams(("parallel", "arbitrary")),
        name="peer_values",
    )(ssel, a, b, g, v_tab, x1, g2, ln_g, ln_b)


FFT_N2 = 128
FFT_PITCH = FFT_N2 + 8
HY_POS_PAD = 32
FFT_UNROLL = 4


def _dft_constants(n):
    big = 2 * n
    n1c = big // FFT_N2
    half = n1c // 2
    k1 = np.arange(n1c)[:, None]
    n1 = np.arange(half)[None, :]
    ang = 2 * np.pi * k1 * n1 / n1c
    c, s = np.cos(ang), np.sin(ang)
    f1 = np.block([[c, s], [-s, c]])
    k2 = np.arange(FFT_N2)[:, None]
    n2 = np.arange(FFT_N2)[None, :]
    ang = 2 * np.pi * k2 * n2 / FFT_N2
    c, s = np.cos(ang), np.sin(ang)
    d3 = np.block([[c, s], [-s, c]])
    d3i = np.block([[c, -s], [s, c]])
    ang = 2 * np.pi * n1.T * k1.T / n1c
    c, s = np.cos(ang), np.sin(ang)
    f3 = np.block([[c, -s], [s, c]]) / big
    ang = 2 * np.pi * (np.arange(n1c)[:, None] * np.arange(FFT_N2)[None, :]) / big
    twr = np.repeat(np.cos(ang).reshape(-1, 1), LANES, axis=1)
    twi = np.repeat(-np.sin(ang).reshape(-1, 1), LANES, axis=1)
    as32 = lambda a: np.asarray(a, np.float32)

    def split(a):
        parts, rest = [], np.asarray(a, np.float64)
        for _ in range(3):
            p = np.asarray(rest, BF16)
            parts.append(p)
            rest = rest - p.astype(np.float64)
        return np.stack(parts)
    return split(f1), split(d3), split(d3i), split(f3), as32(twr), as32(twi)


def _hdot(a, b):
    return jnp.dot(a, b, preferred_element_type=F32, precision=HIGHEST)


def _dft_dot(m_ref, x):
    x_hi = x.astype(BF16)
    x_lo = (x - x_hi.astype(F32)).astype(BF16)
    m_hi = m_ref[0]
    dot = functools.partial(jnp.dot, preferred_element_type=F32)
    return (dot(m_hi, x_hi) + dot(m_hi, x_lo)) + (dot(m_ref[1], x_hi) + dot(m_ref[2], x_hi))


def _fft_stage1(u_re, u_im, f1_ref, a_re, a_im):
    half = f1_ref.shape[2] // 2
    n1c = f1_ref.shape[1] // 2

    def column(n2):
        return jnp.concatenate([u_re[pl.ds(n2, half, stride=FFT_PITCH), :],
                                u_im[pl.ds(n2, half, stride=FFT_PITCH), :]], axis=0)

    def body(i, carry):
        a = _dft_dot(f1_ref, jnp.concatenate([column(2 * i), column(2 * i + 1)], axis=1))
        for s in range(2):
            a_re[pl.ds(2 * i + s, n1c, stride=FFT_PITCH), :] = a[:n1c, s * LANES:(s + 1) * LANES]
            a_im[pl.ds(2 * i + s, n1c, stride=FFT_PITCH), :] = a[n1c:, s * LANES:(s + 1) * LANES]
        return carry
    lax.fori_loop(0, FFT_N2 // 2, body, 0, unroll=FFT_UNROLL)


def _twiddled_pair(a_re, a_im, twr_ref, twi_ref, i):
    t0 = pl.multiple_of(2 * i * FFT_N2, 2 * FFT_N2)
    rows = [pl.multiple_of((2 * i + s) * FFT_PITCH, 8) for s in range(2)]
    side = lambda ref, r: jnp.concatenate([ref[pl.ds(r[0], FFT_N2), :], ref[pl.ds(r[1], FFT_N2), :]], axis=1)
    tws = [pl.multiple_of(t0 + s * FFT_N2, FFT_N2) for s in range(2)]
    return rows, side(a_re, rows), side(a_im, rows), side(twr_ref, tws), side(twi_ref, tws)


def _fft_conv_middle(a_re, a_im, d3_ref, d3i_ref, twr_ref, twi_ref, hre_ref, him_ref):
    n1c = hre_ref.shape[1]

    def body(i, carry):
        rows, ar, ai, twr, twi = _twiddled_pair(a_re, a_im, twr_ref, twi_ref, i)
        x = _dft_dot(d3_ref, jnp.concatenate([ar * twr - ai * twi, ar * twi + ai * twr], axis=0))
        xr, xi = x[:FFT_N2], x[FFT_N2:]
        hr = jnp.concatenate([hre_ref[0, 2 * i], hre_ref[0, 2 * i + 1]], axis=1)
        hi = jnp.concatenate([him_ref[0, 2 * i], him_ref[0, 2 * i + 1]], axis=1)
        y = _dft_dot(d3i_ref, jnp.concatenate([xr * hr - xi * hi, xr * hi + xi * hr], axis=0))
        br, bi = y[:FFT_N2], y[FFT_N2:]
        out_re = br * twr + bi * twi
        out_im = bi * twr - br * twi
        for s in range(2):
            a_re[pl.ds(rows[s], FFT_N2), :] = out_re[:, s * LANES:(s + 1) * LANES]
            a_im[pl.ds(rows[s], FFT_N2), :] = out_im[:, s * LANES:(s + 1) * LANES]
        return carry
    lax.fori_loop(0, n1c // 2, body, 0, unroll=FFT_UNROLL)


def _fft_stage_inv(a_re, a_im, f3_ref, y_re, y_im):
    half = f3_ref.shape[1] // 2
    n1c = f3_ref.shape[2] // 2

    def column(n2):
        return jnp.concatenate([a_re[pl.ds(n2, n1c, stride=FFT_PITCH), :],
                                a_im[pl.ds(n2, n1c, stride=FFT_PITCH), :]], axis=0)

    def body(i, carry):
        y = _dft_dot(f3_ref, jnp.concatenate([column(2 * i), column(2 * i + 1)], axis=1))
        for s in range(2):
            y_re[pl.ds(2 * i + s, half, stride=FFT_PITCH), :] = y[:half, s * LANES:(s + 1) * LANES]
            y_im[pl.ds(2 * i + s, half, stride=FFT_PITCH), :] = y[half:, s * LANES:(s + 1) * LANES]
        return carry
    lax.fori_loop(0, FFT_N2 // 2, body, 0, unroll=FFT_UNROLL)


def _short_conv_chunk(x_ref, bi, j, nblk, w_ref, b_ref):
    r0 = j * FFT_N2
    cur = x_ref[bi, r0:r0 + FFT_N2, :]
    row = lax.broadcasted_iota(jnp.int32, cur.shape, 0)
    if j == 0:
        prev = jnp.where(row == 0, 0.0, pltpu.roll(cur, 1, 0))
    else:
        prev = x_ref[bi, r0 - 1:r0 + FFT_N2 - 1, :]
    if j == nblk - 1:
        nxt = jnp.where(row == FFT_N2 - 1, 0.0, pltpu.roll(cur, FFT_N2 - 1, 0))
    else:
        nxt = x_ref[bi, r0 + 1:r0 + FFT_N2 + 1, :]
    return prev * w_ref[0:1, :] + cur * w_ref[1:2, :] + nxt * w_ref[2:3, :] + b_ref[...]


def _hy_conv_body(conv_a, a_ref, g_ref, wa_ref, ba_ref, wg_ref, bg_ref, skip_ref, hre_ref, him_ref,
                  f1_ref, d3_ref, d3i_ref, f3_ref, twr_ref, twi_ref, o_ref,
                  u_re, u_im, a_re, a_im, y_re, y_im):
    nblk = a_ref.shape[1] // FFT_N2
    for bi, dst in ((0, u_re), (1, u_im)):
        for j in range(nblk):
            if conv_a:
                blk = _short_conv_chunk(a_ref, bi, j, nblk, wa_ref, ba_ref)
            else:
                blk = a_ref[bi, j * FFT_N2:(j + 1) * FFT_N2, :]
            dst[j * FFT_PITCH:j * FFT_PITCH + FFT_N2, :] = blk
    _fft_stage1(u_re, u_im, f1_ref, a_re, a_im)
    _fft_conv_middle(a_re, a_im, d3_ref, d3i_ref, twr_ref, twi_ref, hre_ref, him_ref)
    _fft_stage_inv(a_re, a_im, f3_ref, y_re, y_im)
    skip = skip_ref[...]
    for bi, (ysrc, usrc) in enumerate(((y_re, u_re), (y_im, u_im))):
        for j in range(nblk):
            rows = slice(j * FFT_PITCH, j * FFT_PITCH + FFT_N2)
            gate = _short_conv_chunk(g_ref, bi, j, nblk, wg_ref, bg_ref)
            o_ref[bi, j * FFT_N2:(j + 1) * FFT_N2, :] = gate * (ysrc[rows, :] + usrc[rows, :] * skip)


def _hy_conv(a, a_col, g, g_col, conv_w, conv_b, skip, hre, him, order, consts, conv_a):
    b, n, _ = a.shape
    f1, d3, d3i, f3, twr, twi = consts
    n1c = f1.shape[1] // 2
    half = n1c // 2
    w = LANES
    tiles = HY_WIDTH // w
    one = pl.Buffered(1)
    data = lambda col: pl.BlockSpec((2, n, w), lambda ct, p: (p, 0, col + ct))
    wspec = lambda col: pl.BlockSpec((3, w), lambda ct, p: (0, col + ct))
    bspec = lambda col: pl.BlockSpec((1, w), lambda ct, p: (0, col + ct))
    hspec = pl.BlockSpec((1, n1c, FFT_N2, w), lambda ct, p: (order, 0, 0, ct), pipeline_mode=one)
    cs = lambda arr: pl.BlockSpec(arr.shape, lambda ct, p: (0,) * arr.ndim, pipeline_mode=one)
    a_wcol = a_col if conv_a else g_col
    return pl.pallas_call(
        functools.partial(_hy_conv_body, conv_a),
        grid=(tiles, b // 2),
        in_specs=[data(a_col), data(g_col), wspec(a_wcol), bspec(a_wcol), wspec(g_col), bspec(g_col),
                  pl.BlockSpec((1, w), lambda ct, p: (0, ct)), hspec, hspec,
                  cs(f1), cs(d3), cs(d3i), cs(f3), cs(twr), cs(twi)],
        out_specs=pl.BlockSpec((2, n, w), lambda ct, p: (p, 0, ct), pipeline_mode=one),
        out_shape=jax.ShapeDtypeStruct((b, n, HY_WIDTH), F32),
        scratch_shapes=[pltpu.VMEM((half * FFT_PITCH, w), F32), pltpu.VMEM((half * FFT_PITCH, w), F32),
                        pltpu.VMEM((n1c * FFT_PITCH, w), F32), pltpu.VMEM((n1c * FFT_PITCH, w), F32),
                        pltpu.VMEM((half * FFT_PITCH, w), F32), pltpu.VMEM((half * FFT_PITCH, w), F32)],
        compiler_params=_cparams(("parallel", "arbitrary")),
        name=f"hyena_conv{order + 1}",
    )(a, g, conv_w, conv_b, conv_w, conv_b, skip, hre, him, f1, d3, d3i, f3, twr, twi)


def _hy_filter_body(z_ref, w1_ref, b1_ref, f1_ref, w2_ref, b2_ref, f2_ref, w3_ref, b3_ref, dl_ref, o_ref):
    z = z_ref[...]
    h = jnp.sin(f1_ref[...] * (_hdot(z, w1_ref[...]) + b1_ref[...]))
    h = jnp.sin(f2_ref[...] * (_hdot(h, w2_ref[...]) + b2_ref[...]))
    h = _hdot(h, w3_ref[...]) + b3_ref[...]
    t = z[:, 0:1]
    o_ref[...] = h * (jnp.exp(-t * dl_ref[...]) + HY_WINDOW_SHIFT)


def _hy_filters(n, w1, b1, fr1, w2, b2, fr2, w3, b3):
    t = jnp.linspace(0.0, 1.0, n, dtype=F32)[:, None]
    wv = 2.0 * math.pi * jnp.arange(n, dtype=F32)[:, None] / n
    bands = jnp.linspace(1e-4, HY_BANDS - 1, HY_BANDS, dtype=F32)[None, :]
    z = jnp.concatenate([t, jnp.cos(bands * wv), -jnp.sin(bands * wv)], axis=-1)
    pos = z.shape[1]
    z = jnp.pad(z, ((0, 0), (0, HY_POS_PAD - pos)))
    w1p = jnp.pad(w1, ((0, HY_POS_PAD - pos), (0, 0)))
    min_decay = math.log(HY_DECAY_TARGET) / HY_SLOW_DECAY_PCT
    max_decay = math.log(HY_DECAY_TARGET) / HY_FAST_DECAY_PCT
    deltas = jnp.abs(jnp.linspace(min_decay, max_decay, HY_WIDTH, dtype=F32))
    n_out = w3.shape[1]
    dl = jnp.tile(deltas, n_out // HY_WIDTH)[None, :]
    tn = 512
    hid = w2.shape[0]
    full = lambda shape: pl.BlockSpec(shape, lambda i: (0, 0))
    return pl.pallas_call(
        _hy_filter_body,
        grid=(n // tn,),
        in_specs=[pl.BlockSpec((tn, HY_POS_PAD), lambda i: (i, 0)), full((HY_POS_PAD, hid)), full((1, hid)),
                  full((1, hid)), full((hid, hid)), full((1, hid)), full((1, hid)), full((hid, n_out)),
                  full((1, n_out)), full((1, n_out))],
        out_specs=pl.BlockSpec((tn, n_out), lambda i: (i, 0)),
        out_shape=jax.ShapeDtypeStruct((n, n_out), F32),
        compiler_params=_cparams(("parallel",)),
        name="hyena_filters",
    )(z, w1p, b1[None], fr1[None], w2, b2[None], fr2[None], w3, b3[None], dl)


def _hy_spectrum_body(f_ref, b_ref, f1_ref, d3_ref, twr_ref, twi_ref, hre_ref, him_ref, u_re, u_im, a_re, a_im):
    nblk = f_ref.shape[0] // FFT_N2
    n1c = hre_ref.shape[1]
    for part, out_ref in enumerate((hre_ref, him_ref)):
        for j in range(nblk):
            f = f_ref[j * FFT_N2:(j + 1) * FFT_N2, :]
            bw = b_ref[j * FFT_N2:(j + 1) * FFT_N2, :]
            if j == 0:
                row = lax.broadcasted_iota(jnp.int32, bw.shape, 0)
                bw = jnp.where(row == 0, 0.0, bw)
            rows = slice(j * FFT_PITCH, j * FFT_PITCH + FFT_N2)
            u_re[rows, :] = f + bw if part == 0 else f - bw
            u_im[rows, :] = jnp.zeros((FFT_N2, LANES), F32)
        _fft_stage1(u_re, u_im, f1_ref, a_re, a_im)

        def body(i, carry, part=part, out_ref=out_ref):
            _, ar, ai, twr, twi = _twiddled_pair(a_re, a_im, twr_ref, twi_ref, i)
            x = _dft_dot(d3_ref, jnp.concatenate([ar * twr - ai * twi, ar * twi + ai * twr], axis=0))
            x = x[:FFT_N2] if part == 0 else x[FFT_N2:]
            for s in range(2):
                out_ref[0, 2 * i + s] = x[:, s * LANES:(s + 1) * LANES]
            return carry
        lax.fori_loop(0, n1c // 2, body, 0, unroll=FFT_UNROLL)


def _hy_spectrum(h, consts):
    n = h.shape[0]
    f1, d3, _, _, twr, twi = consts
    n1c = f1.shape[1] // 2
    half = n1c // 2
    tiles = HY_WIDTH // LANES
    cs = lambda arr: pl.BlockSpec(arr.shape, lambda o, ct: (0,) * arr.ndim)
    out = jax.ShapeDtypeStruct((HY_ORDER, n1c, FFT_N2, HY_WIDTH), F32)
    ospec = pl.BlockSpec((1, n1c, FFT_N2, LANES), lambda o, ct: (o, 0, 0, ct))
    return pl.pallas_call(
        _hy_spectrum_body,
        grid=(HY_ORDER, tiles),
        in_specs=[pl.BlockSpec((n, LANES), lambda o, ct: (0, o * tiles + ct)),
                  pl.BlockSpec((n, LANES), lambda o, ct: (0, (HY_ORDER + o) * tiles + ct)),
                  cs(f1), cs(d3), cs(twr), cs(twi)],
        out_specs=[ospec, ospec],
        out_shape=[out, out],
        scratch_shapes=[pltpu.VMEM((half * FFT_PITCH, LANES), F32), pltpu.VMEM((half * FFT_PITCH, LANES), F32),
                        pltpu.VMEM((n1c * FFT_PITCH, LANES), F32), pltpu.VMEM((n1c * FFT_PITCH, LANES), F32)],
        compiler_params=_cparams(("parallel", "parallel")),
        name="hyena_spectrum",
    )(h, h, f1, d3, twr, twi)


def _hyena(p_lat, conv_w, conv_b, w1, b1, fr1, w2, b2, fr2, w3, b3, skip):
    n = p_lat.shape[1]
    consts = _dft_constants(n)
    h = _hy_filters(n, w1, b1, fr1, w2, b2, fr2, w3, b3)
    hre, him = _hy_spectrum(h, consts)
    tiles = HY_WIDTH // LANES
    col = COL_HY // LANES
    cw = jnp.pad(conv_w, ((0, 0), (COL_HY, 0)))
    cb = jnp.pad(conv_b[None], ((0, 0), (COL_HY, 0)))
    y = _hy_conv(p_lat, col, p_lat, col + tiles, cw, cb, skip[0][None], hre, him, 0, consts, True)
    return _hy_conv(y, 0, p_lat, col + 2 * tiles, cw, cb, skip[1][None], hre, him, 1, consts, False)


def _split_bf16(w):
    hi = w.astype(BF16)
    return jnp.stack([hi, (w - hi.astype(F32)).astype(BF16)])


def _pad_heads(w, heads):
    d = w.shape[0]
    w = w.reshape(d, heads, HEAD_DIM)
    w = jnp.pad(w, ((0, 0), (0, 0), (0, LANES - HEAD_DIM)))
    return w.reshape(d, heads * LANES)


def kernel(x, c, ctx, c_ctx, w_mod, b_mod, w_in, hy_conv_w, hy_conv_b, hy_f_w1, hy_f_b1, hy_f_freq1, hy_f_w2,
           hy_f_b2, hy_f_freq2, hy_f_w3, hy_f_b3, hy_skip, attn_sink, w_out, ln1_g, ln1_b, peer_wq, peer_keys1,
           peer_keys2, peer_u, peer_v, ln2_g, ln2_b):
    b, n, d = x.shape
    l = 0
    cc = jnp.concatenate([c, c_ctx[None], jnp.zeros((8 - b - 1, d), F32)], axis=0)
    mod = _modulation(cc, w_mod[l], b_mod[l][None])
    mod_lat = mod[:b].reshape(b, 6, 1, d)
    sh1, sc1, g1, sh2, sc2, g2 = (mod_lat[:, i] for i in range(6))
    mod_c = mod[b].reshape(6, 1, 1, d)
    csh1, csc1 = mod_c[0], mod_c[1]

    w = w_in[l]
    w_q = _pad_heads(w[:, PROJ_HY:KV_START], ATT_HEADS)
    w_k = _pad_heads(w[:, KV_START:KV_START + PROJ_KV], ATT_KV_HEADS)
    w_v = _pad_heads(w[:, KV_START + PROJ_KV:], ATT_KV_HEADS)
    w_pad = jnp.concatenate([w_q, w[:, :PROJ_HY], w_k, w_v], axis=1).astype(BF16)
    w_kv = jnp.concatenate([w_k, w_v], axis=1).astype(BF16)

    p_lat = _mod_matmul(x, sc1, sh1, w_pad, 512, "in_proj")
    kv_ctx = _mod_matmul(ctx, csc1, csh1, w_kv, ctx.shape[1], "ctx_kv_proj")

    cos_tab, sin_tab = _rope_tables(n)
    att = _attention(p_lat, kv_ctx, attn_sink[l], cos_tab, sin_tab)

    hy = _hyena(p_lat, hy_conv_w[l], hy_conv_b[l], hy_f_w1[l], hy_f_b1[l], hy_f_freq1[l], hy_f_w2[l], hy_f_b2[l],
                hy_f_freq2[l], hy_f_w3[l], hy_f_b3[l], hy_skip[l])

    wo = w_out[l]
    w_o_hy = wo[:HY_WIDTH].astype(BF16)
    w_o_att = jnp.pad(wo[HY_WIDTH:].reshape(ATT_HEADS, HEAD_DIM, d),
                      ((0, 0), (0, LANES - HEAD_DIM), (0, 0))).reshape(QPAD, d).astype(BF16)
    x1, hq = _outproj_ln(hy, att, x, w_o_hy, w_o_att, g1, sc2, sh2, ln1_g[l][None], ln1_b[l][None])

    hq2 = hq.reshape(b * n, d)
    a_idx, b_idx, gate = _peer_topk(hq2, peer_wq[l].astype(BF16), _split_bf16(peer_keys1[l]),
                                    _split_bf16(peer_keys2[l]))
    s_sel = _peer_scores(hq2, a_idx, b_idx, peer_u[l].astype(BF16))
    out = _peer_values(s_sel, a_idx, b_idx, gate, peer_v[l].astype(BF16), x1.reshape(b * n, d), g2,
                       ln2_g[l][None], ln2_b[l][None], n)
    return out.reshape(b, n, d)
```

```python
import functools
import math

import jax
import jax.numpy as jnp
import numpy as np
from jax import lax
from jax.experimental import pallas as pl
from jax.experimental.pallas import tpu as pltpu

F32 = jnp.float32
BF16 = jnp.bfloat16
HIGHEST = lax.Precision.HIGHEST

LANES = 128
VMEM_LIMIT = 60000 * 1024

D_MODEL = 1024
GRID_W = 64
HY_WIDTH = 512
HY_ORDER = 2
HY_BANDS = 8
HY_DECAY_TARGET = 1e-2
HY_FAST_DECAY_PCT = 0.3
HY_SLOW_DECAY_PCT = 1.5
HY_WINDOW_SHIFT = 0.05
ATT_HEADS = 8
ATT_KV_HEADS = 2
ATT_REP = ATT_HEADS // ATT_KV_HEADS
HEAD_DIM = 64
WINDOW = 128
BLOCK = 128
ROPE_BASE = 10000.0
ROPE_FREQS = HEAD_DIM // 4
PROJ_HY = (HY_ORDER + 1) * HY_WIDTH
PROJ_Q = ATT_HEADS * HEAD_DIM
PROJ_KV = ATT_KV_HEADS * HEAD_DIM
KV_START = PROJ_HY + PROJ_Q
PEER_KEYS = 128
PEER_HEADS = 8
PEER_QDIM = 256
PEER_TOPK = 16
LN_EPS = 1e-5
NEG_INF = -1e30
DEPTH = 1
DEEPNORM_ALPHA = (2.0 * DEPTH) ** 0.25

QPAD = ATT_HEADS * LANES
KVPAD = ATT_KV_HEADS * LANES
COL_Q = 0
COL_HY = QPAD
COL_K = QPAD + PROJ_HY
COL_V = COL_K + KVPAD
PROJ_PAD = COL_V + KVPAD


def _cparams(sem):
    return pltpu.CompilerParams(dimension_semantics=sem, vmem_limit_bytes=VMEM_LIMIT)


def _mod_body(c_ref, w_ref, b_ref, o_ref):
    c = c_ref[...]
    a = c * jax.nn.sigmoid(c)
    o_ref[...] = jnp.dot(a, w_ref[...], preferred_element_type=F32, precision=HIGHEST) + b_ref[...]


def _modulation(cc, w_mod, b_mod):
    rows, d = cc.shape
    n_out = w_mod.shape[1]
    tn = 1536
    return pl.pallas_call(
        _mod_body,
        grid=(n_out // tn,),
        in_specs=[
            pl.BlockSpec((rows, d), lambda j: (0, 0)),
            pl.BlockSpec((d, tn), lambda j: (0, j)),
            pl.BlockSpec((1, tn), lambda j: (0, j)),
        ],
        out_specs=pl.BlockSpec((rows, tn), lambda j: (0, j)),
        out_shape=jax.ShapeDtypeStruct((rows, n_out), F32),
        compiler_params=_cparams(("arbitrary",)),
        name="modulation",
    )(cc, w_mod, b_mod)


def _mod_matmul_body(x_ref, sc_ref, sh_ref, w_ref, o_ref):
    h = x_ref[0] * (1.0 + sc_ref[0]) + sh_ref[0]
    o_ref[0] = jnp.dot(h.astype(BF16), w_ref[...], preferred_element_type=F32)


def _mod_matmul(x, sc, sh, w, tm, name):
    b, n, d = x.shape
    n_out = w.shape[1]
    per_batch = sc.shape[0] == b
    mod_map = (lambda bi, i: (bi, 0, 0)) if per_batch else (lambda bi, i: (0, 0, 0))
    return pl.pallas_call(
        _mod_matmul_body,
        grid=(b, n // tm),
        in_specs=[
            pl.BlockSpec((1, tm, d), lambda bi, i: (bi, i, 0)),
            pl.BlockSpec((1, 1, d), mod_map),
            pl.BlockSpec((1, 1, d), mod_map),
            pl.BlockSpec((d, n_out), lambda bi, i: (0, 0)),
        ],
        out_specs=pl.BlockSpec((1, tm, n_out), lambda bi, i: (bi, i, 0)),
        out_shape=jax.ShapeDtypeStruct((b, n, n_out), F32),
        compiler_params=_cparams(("parallel", "parallel")),
        name=name,
    )(x, sc, sh, w)


def _rope_head(x, cos, sin_signed):
    lane = lax.broadcasted_iota(jnp.int32, x.shape, 1)
    first_half = (lane % 32) < 16
    partner = jnp.where(first_half, pltpu.roll(x, LANES - 16, 1), pltpu.roll(x, 16, 1))
    return x * cos + partner * sin_signed


ATT_TQ = 512
ROPE_CHUNK = 512


def _attn_body(sink_ref, q_ref, k_ref, v_ref, kvc_ref, cosq_ref, sinq_ref, cosk_ref, sink_tab_ref,
               o_ref, ks_ref, vs_ref, kcs_ref, vcs_ref):
    n = k_ref.shape[1]
    iq = pl.program_id(1)
    scale = HEAD_DIM ** -0.5

    @pl.when(iq == 0)
    def _prepare_keys():
        def chunk(ci, carry):
            r0 = pl.multiple_of(ci * ROPE_CHUNK, ROPE_CHUNK)
            cos = cosk_ref[pl.ds(r0, ROPE_CHUNK), :]
            sin = sink_tab_ref[pl.ds(r0, ROPE_CHUNK), :]
            for g in range(ATT_KV_HEADS):
                kg = k_ref[0, pl.ds(r0, ROPE_CHUNK), g * LANES:(g + 1) * LANES]
                ks_ref[pl.ds(r0, ROPE_CHUNK), g * LANES:(g + 1) * LANES] = _rope_head(kg, cos, sin).astype(BF16)
            vs_ref[pl.ds(r0, ROPE_CHUNK), :] = v_ref[0, pl.ds(r0, ROPE_CHUNK), :].astype(BF16)
            return carry
        lax.fori_loop(0, n // ROPE_CHUNK, chunk, 0)
        kcs_ref[...] = kvc_ref[0, :, 0:KVPAD].astype(BF16)
        vcs_ref[...] = kvc_ref[0, :, KVPAD:2 * KVPAD].astype(BF16)

    n_loc = 3 * BLOCK
    rows = ATT_REP * BLOCK
    row_i = lax.broadcasted_iota(jnp.int32, (rows, n_loc), 0)
    col_i = lax.broadcasted_iota(jnp.int32, (rows, n_loc), 1)
    rel = col_i - (row_i % BLOCK)
    head_of_row = lax.broadcasted_iota(jnp.int32, (rows, 1), 0) // BLOCK

    for j in range(ATT_TQ // BLOCK):
        blk = iq * (ATT_TQ // BLOCK) + j
        start = pl.multiple_of(jnp.clip((blk - 1) * BLOCK, 0, n - n_loc), BLOCK)
        qrows = slice(j * BLOCK, (j + 1) * BLOCK)
        cosq = cosq_ref[qrows, :]
        sinq = sinq_ref[qrows, :]
        delta = rel + (start - blk * BLOCK)
        in_window = jnp.abs(delta) <= WINDOW
        for g in range(ATT_KV_HEADS):
            heads = [ATT_REP * g + r for r in range(ATT_REP)]
            qg = jnp.concatenate(
                [(_rope_head(q_ref[0, qrows, h * LANES:(h + 1) * LANES], cosq, sinq) * scale).astype(BF16)
                 for h in heads], axis=0)
            kg = ks_ref[pl.ds(start, n_loc), g * LANES:(g + 1) * LANES]
            vg = vs_ref[pl.ds(start, n_loc), g * LANES:(g + 1) * LANES]
            kcg = kcs_ref[:, g * LANES:(g + 1) * LANES]
            vcg = vcs_ref[:, g * LANES:(g + 1) * LANES]
            nt = (((1,), (1,)), ((), ()))
            s_loc = lax.dot_general(qg, kg, nt, preferred_element_type=F32)
            s_ctx = lax.dot_general(qg, kcg, nt, preferred_element_type=F32)
            s_loc = jnp.where(in_window, s_loc, NEG_INF)
            sink_col = jnp.zeros((rows, 1), F32)
            for r, h in enumerate(heads):
                sink_col = jnp.where(head_of_row == r, sink_ref[h], sink_col)
            m = jnp.maximum(jnp.maximum(jnp.max(s_loc, axis=1, keepdims=True),
                                        jnp.max(s_ctx, axis=1, keepdims=True)), sink_col)
            p_loc = jnp.exp(s_loc - m)
            p_ctx = jnp.exp(s_ctx - m)
            den = (jnp.sum(p_loc, axis=1, keepdims=True) + jnp.sum(p_ctx, axis=1, keepdims=True)
                   + jnp.exp(sink_col - m))
            o = (jnp.dot(p_loc.astype(BF16), vg, preferred_element_type=F32)
                 + jnp.dot(p_ctx.astype(BF16), vcg, preferred_element_type=F32)) / den
            for r, h in enumerate(heads):
                o_ref[0, qrows, h * LANES:(h + 1) * LANES] = o[r * BLOCK:(r + 1) * BLOCK].astype(BF16)


def _attention(p_lat, kv_ctx, sink, cos_tab, sin_tab):
    b, n, _ = p_lat.shape
    n_ctx = kv_ctx.shape[1]
    grid_spec = pltpu.PrefetchScalarGridSpec(
        num_scalar_prefetch=1,
        grid=(b, n // ATT_TQ),
        in_specs=[
            pl.BlockSpec((1, ATT_TQ, QPAD), lambda bi, i, s: (bi, i, COL_Q // QPAD)),
            pl.BlockSpec((1, n, KVPAD), lambda bi, i, s: (bi, 0, COL_K // KVPAD)),
            pl.BlockSpec((1, n, KVPAD), lambda bi, i, s: (bi, 0, COL_V // KVPAD)),
            pl.BlockSpec((1, n_ctx, 2 * KVPAD), lambda bi, i, s: (bi, 0, 0)),
            pl.BlockSpec((ATT_TQ, LANES), lambda bi, i, s: (i, 0)),
            pl.BlockSpec((ATT_TQ, LANES), lambda bi, i, s: (i, 0)),
            pl.BlockSpec((n, LANES), lambda bi, i, s: (0, 0)),
            pl.BlockSpec((n, LANES), lambda bi, i, s: (0, 0)),
        ],
        out_specs=pl.BlockSpec((1, ATT_TQ, QPAD), lambda bi, i, s: (bi, i, 0)),
        scratch_shapes=[
            pltpu.VMEM((n, KVPAD), BF16),
            pltpu.VMEM((n, KVPAD), BF16),
            pltpu.VMEM((n_ctx, KVPAD), BF16),
            pltpu.VMEM((n_ctx, KVPAD), BF16),
        ],
    )
    return pl.pallas_call(
        _attn_body,
        grid_spec=grid_spec,
        out_shape=jax.ShapeDtypeStruct((b, n, QPAD), BF16),
        compiler_params=_cparams(("parallel", "arbitrary")),
        name="window_attention",
    )(sink, p_lat, p_lat, p_lat, kv_ctx, cos_tab, sin_tab, cos_tab, sin_tab)


def _rope_tables(n):
    rows = n // GRID_W
    row = jnp.repeat(jnp.arange(rows, dtype=F32), GRID_W)
    col = jnp.tile(jnp.arange(GRID_W, dtype=F32), rows)
    inv = ROPE_BASE ** (-jnp.arange(ROPE_FREQS, dtype=F32) / ROPE_FREQS)
    ang_r = row[:, None] * inv
    ang_c = col[:, None] * inv
    pad1 = jnp.ones((n, LANES - HEAD_DIM), F32)
    pad0 = jnp.zeros((n, LANES - HEAD_DIM), F32)
    cos = jnp.concatenate([jnp.cos(ang_r), jnp.cos(ang_r), jnp.cos(ang_c), jnp.cos(ang_c), pad1], axis=1)
    sin = jnp.concatenate([-jnp.sin(ang_r), jnp.sin(ang_r), -jnp.sin(ang_c), jnp.sin(ang_c), pad0], axis=1)
    return cos, sin


def _layer_norm(r, g, b):
    mu = jnp.mean(r, axis=-1, keepdims=True)
    var = jnp.mean(jnp.square(r - mu), axis=-1, keepdims=True)
    return (r - mu) * lax.rsqrt(var + LN_EPS) * g + b


def _outproj_body(hy_ref, att_ref, x_ref, wh_ref, wa_ref, g1_ref, sc2_ref, sh2_ref, lg_ref, lb_ref,
                  x1_ref, hq_ref):
    y = (jnp.dot(hy_ref[0].astype(BF16), wh_ref[...], preferred_element_type=F32)
         + jnp.dot(att_ref[0], wa_ref[...], preferred_element_type=F32))
    x1 = _layer_norm(DEEPNORM_ALPHA * x_ref[0] + g1_ref[0] * y, lg_ref[...], lb_ref[...])
    x1_ref[0] = x1
    hq_ref[0] = (x1 * (1.0 + sc2_ref[0]) + sh2_ref[0]).astype(BF16)


def _outproj_ln(hy, att, x, w_hy, w_att, g1, sc2, sh2, ln_g, ln_b, tm=512):
    b, n, d = x.shape
    modspec = pl.BlockSpec((1, 1, d), lambda bi, i: (bi, 0, 0))
    vecspec = pl.BlockSpec((1, d), lambda bi, i: (0, 0))
    return pl.pallas_call(
        _outproj_body,
        grid=(b, n // tm),
        in_specs=[
            pl.BlockSpec((1, tm, HY_WIDTH), lambda bi, i: (bi, i, 0)),
            pl.BlockSpec((1, tm, QPAD), lambda bi, i: (bi, i, 0)),
            pl.BlockSpec((1, tm, d), lambda bi, i: (bi, i, 0)),
            pl.BlockSpec(w_hy.shape, lambda bi, i: (0, 0)),
            pl.BlockSpec(w_att.shape, lambda bi, i: (0, 0)),
            modspec, modspec, modspec, vecspec, vecspec,
        ],
        out_specs=[
            pl.BlockSpec((1, tm, d), lambda bi, i: (bi, i, 0)),
            pl.BlockSpec((1, tm, d), lambda bi, i: (bi, i, 0)),
        ],
        out_shape=[jax.ShapeDtypeStruct((b, n, d), F32), jax.ShapeDtypeStruct((b, n, d), BF16)],
        compiler_params=_cparams(("parallel", "parallel")),
        name="outproj_ln1",
    )(hy, att, x, w_hy, w_att, g1, sc2, sh2, ln_g, ln_b)


PEER_TM = 256
TOPK_UNROLL = 4
_STAIR = sorted(((i, j) for i in range(PEER_TOPK) for j in range(PEER_TOPK) if (i + 1) * (j + 1) <= PEER_TOPK),
                key=lambda p: p[0] * PEER_TOPK + p[1])
_STAIR_ROWS = -(-len(_STAIR) // 8) * 8
_STAIR_COUNT = [PEER_TOPK // (i + 1) for i in range(PEER_TOPK)]
_STAIR_START = [sum(_STAIR_COUNT[:i]) for i in range(PEER_TOPK)]


def _stair_ids(tm):
    ids = np.full((_STAIR_ROWS,), float(PEER_TOPK * PEER_TOPK), np.float32)
    ids[:len(_STAIR)] = [i * PEER_TOPK + j for i, j in _STAIR]
    return np.repeat(ids[:, None], tm, axis=1)


def _select_topk(problems, write_row):
    tm = problems[0][0].shape[1]

    def step(k, prev):
        new = []
        for p, ((s_ref, ids, pad_id), prev_id) in enumerate(zip(problems, prev)):
            s = jnp.where(ids == prev_id, -jnp.inf, s_ref[...])
            s_ref[...] = s
            m = jnp.max(s, axis=0, keepdims=True)
            win = jnp.min(jnp.where(s == m, ids, pad_id), axis=0, keepdims=True)
            write_row(p, k, m, win)
            new.append(win)
        return tuple(new)

    lax.fori_loop(0, PEER_TOPK, step, tuple(jnp.full((1, tm), -1.0, F32) for _ in problems), unroll=TOPK_UNROLL)


def _rows_by_rank(rank, table_ref):
    out = jnp.zeros(rank.shape, F32)
    for p in range(PEER_TOPK):
        out = jnp.where(rank == float(p), table_ref[p:p + 1, :], out)
    return out


def _peer_topk_body(hq_ref, wq_ref, k1_ref, k2_ref, sid_ref, a_ref, b_ref, g_ref,
                    q_s, s1_s, s2_s, v1_s, i1_s, v2_s, i2_s, c_s, t_s, f_s, ao_s, bo_s, go_s):
    tm = hq_ref.shape[0]
    q = jnp.dot(hq_ref[...], wq_ref[...], preferred_element_type=F32)
    for c in range(2 * PEER_HEADS):
        q_s[c] = q[:, c * LANES:(c + 1) * LANES]
    nt = (((1,), (1,)), ((), ()))
    key_id = lax.broadcasted_iota(jnp.int32, (PEER_KEYS, tm), 0).astype(F32)

    def head(h, carry):
        for half, (kref, s_s) in enumerate(((k1_ref, s1_s), (k2_ref, s2_s))):
            qh = q_s[2 * h + half]
            q_hi = qh.astype(BF16)
            q_lo = (qh - q_hi.astype(F32)).astype(BF16)
            ntdot = functools.partial(lax.dot_general, dimension_numbers=nt, preferred_element_type=F32)
            s_s[...] = ntdot(kref[0], q_hi) + (ntdot(kref[0], q_lo) + ntdot(kref[1], q_hi))

        def write1(p, k, val, idx):
            vs, is_ = ((v1_s, i1_s), (v2_s, i2_s))[p]
            vs[pl.ds(k, 1), :] = val
            is_[pl.ds(k, 1), :] = idx
        _select_topk([(s1_s, key_id, float(PEER_KEYS)), (s2_s, key_id, float(PEER_KEYS))], write1)

        c_s[...] = jnp.full(c_s.shape, -jnp.inf, F32)
        for i in range(PEER_TOPK):
            r0, cnt = _STAIR_START[i], _STAIR_COUNT[i]
            c_s[r0:r0 + cnt, :] = v1_s[i:i + 1, :] + v2_s[0:cnt, :]

        def write2(p, k, val, idx):
            t_s[pl.ds(k, 1), :] = val
            f_s[pl.ds(k, 1), :] = idx
        _select_topk([(c_s, sid_ref[...], float(PEER_TOPK * PEER_TOPK))], write2)

        flat = f_s[...]
        rank1 = jnp.floor(flat * (1.0 / PEER_TOPK))
        rank2 = flat - rank1 * PEER_TOPK
        rows = pl.ds(pl.multiple_of(h * PEER_TOPK, PEER_TOPK), PEER_TOPK)
        ao_s[rows, :] = _rows_by_rank(rank1, i1_s)
        bo_s[rows, :] = _rows_by_rank(rank2, i2_s)
        t = t_s[...]
        e = jnp.exp(t - jnp.max(t, axis=0, keepdims=True))
        go_s[rows, :] = e / jnp.sum(e, axis=0, keepdims=True)
        return carry

    lax.fori_loop(0, PEER_HEADS, head, 0)
    a_ref[...] = ao_s[...].T
    b_ref[...] = bo_s[...].T
    g_ref[...] = go_s[...].T


def _peer_topk(hq, wq, keys1, keys2):
    t, d = hq.shape
    tm = PEER_TM
    hk = PEER_HEADS * PEER_TOPK
    out = jax.ShapeDtypeStruct((t, hk), F32)
    ospec = pl.BlockSpec((tm, hk), lambda i: (i, 0))
    sid = _stair_ids(tm)
    return pl.pallas_call(
        _peer_topk_body,
        grid=(t // tm,),
        in_specs=[
            pl.BlockSpec((tm, d), lambda i: (i, 0)),
            pl.BlockSpec(wq.shape, lambda i: (0, 0)),
            pl.BlockSpec(keys1.shape, lambda i: (0, 0, 0)),
            pl.BlockSpec(keys2.shape, lambda i: (0, 0, 0)),
            pl.BlockSpec(sid.shape, lambda i: (0, 0)),
        ],
        out_specs=[ospec, ospec, ospec],
        out_shape=[out, out, out],
        scratch_shapes=[
            pltpu.VMEM((2 * PEER_HEADS, tm, LANES), F32),
            pltpu.VMEM((PEER_KEYS, tm), F32), pltpu.VMEM((PEER_KEYS, tm), F32),
            pltpu.VMEM((PEER_TOPK, tm), F32), pltpu.VMEM((PEER_TOPK, tm), F32),
            pltpu.VMEM((PEER_TOPK, tm), F32), pltpu.VMEM((PEER_TOPK, tm), F32),
            pltpu.VMEM((_STAIR_ROWS, tm), F32),
            pltpu.VMEM((PEER_TOPK, tm), F32), pltpu.VMEM((PEER_TOPK, tm), F32),
            pltpu.VMEM((hk, tm), F32), pltpu.VMEM((hk, tm), F32), pltpu.VMEM((hk, tm), F32),
        ],
        compiler_params=_cparams(("parallel",)),
        name="peer_topk",
    )(hq, wq, keys1, keys2, sid)


PEER_TE = 8192
PEER_GROUPS = PEER_TE // PEER_KEYS
PEER_SCORE_TM = 512
PEER_VALUE_TM = 512
PEER_VALUE_TE = 2048
GATE_PITCH = PEER_KEYS + 8
GATE_UNROLL = 64


def _peer_scores_body(hq_ref, a_ref, b_ref, u_ref, o_ref):
    e = pl.program_id(0)
    s = lax.dot_general(hq_ref[...], u_ref[...], (((1,), (1,)), ((), ())),
                        preferred_element_type=F32)
    b_idx = b_ref[...].astype(jnp.int32)
    a_val = a_ref[...]
    cur = jnp.zeros(a_val.shape, F32)
    for jj in range(PEER_GROUPS):
        cand = jnp.take_along_axis(s[:, jj * LANES:(jj + 1) * LANES], b_idx, axis=1)
        cur = jnp.where(a_val == (e * PEER_GROUPS + jj).astype(F32), cand, cur)
    o_ref[0] = cur


def _peer_scores(hq, a, b, u_tab):
    t, d = hq.shape
    tm = PEER_SCORE_TM
    hk = a.shape[1]
    n_e = u_tab.shape[0] // PEER_TE
    tok = lambda e, i: (i, 0)
    return pl.pallas_call(
        _peer_scores_body,
        grid=(n_e, t // tm),
        in_specs=[
            pl.BlockSpec((tm, d), tok),
            pl.BlockSpec((tm, hk), tok), pl.BlockSpec((tm, hk), tok),
            pl.BlockSpec((PEER_TE, d), lambda e, i: (e, 0)),
        ],
        out_specs=pl.BlockSpec((1, tm, hk), lambda e, i: (e, i, 0)),
        out_shape=jax.ShapeDtypeStruct((n_e, t, hk), F32),
        compiler_params=_cparams(("arbitrary", "arbitrary")),
        name="peer_scores",
    )(hq, a, b, u_tab)


def _peer_values_body(ss_ref, a_ref, b_ref, g_ref, v_ref, x1_ref, g2_ref, lg_ref, lb_ref, o_ref,
                      w_s, hd_s, acc_s):
    tm = a_ref.shape[0]
    n_e = pl.num_programs(1)
    e = pl.program_id(1)
    nt = (((1,), (1,)), ((), ()))

    @pl.when(e == 0)
    def _gates():
        s_sel = jnp.sum(ss_ref[...], axis=0)
        act = 0.5 * s_sel * (1.0 + lax.erf(s_sel * (2.0 ** -0.5)))
        w_s[...] = g_ref[...] * act
        sub = lax.broadcasted_iota(jnp.int32, (PEER_KEYS, LANES), 0).astype(F32)

        def token(t, carry):
            a_row = a_ref[pl.ds(t, 1), :]
            b_row = b_ref[pl.ds(t, 1), :]
            w_row = w_s[pl.ds(t, 1), :]
            lhs = jnp.where(sub == a_row, w_row, 0.0).astype(BF16)
            rhs = jnp.where(sub == b_row, 1.0, 0.0).astype(BF16)
            tile = lax.dot_general(lhs, rhs, nt, preferred_element_type=F32)
            hd_s[pl.ds(pl.multiple_of(t * GATE_PITCH, 8), PEER_KEYS), :] = tile
            return carry
        lax.fori_loop(0, tm, token, 0, unroll=GATE_UNROLL)
        acc_s[...] = jnp.zeros(acc_s.shape, F32)

    groups = v_ref.shape[0] // PEER_KEYS
    j0 = e * groups
    lhs = jnp.concatenate(
        [hd_s[pl.ds(j0 + jj, tm, stride=GATE_PITCH), :].astype(BF16) for jj in range(groups)], axis=1)
    acc_s[...] += jnp.dot(lhs, v_ref[...], preferred_element_type=F32)

    @pl.when(e == n_e - 1)
    def _finish():
        o_ref[...] = _layer_norm(DEEPNORM_ALPHA * x1_ref[...] + g2_ref[0] * acc_s[...], lg_ref[...], lb_ref[...])


def _peer_values(ssel, a, b, g, v_tab, x1, g2, ln_g, ln_b, tokens_per_batch):
    t, d = x1.shape
    tm = PEER_VALUE_TM
    hk = a.shape[1]
    n_e = v_tab.shape[0] // PEER_VALUE_TE
    tiles_per_batch = tokens_per_batch // tm
    tok = lambda i, e: (i, 0)
    one = pl.Buffered(1)
    return pl.pallas_call(
        _peer_values_body,
        grid=(t // tm, n_e),
        in_specs=[
            pl.BlockSpec((ssel.shape[0], tm, hk), lambda i, e: (0, i, 0)),
            pl.BlockSpec((tm, hk), tok), pl.BlockSpec((tm, hk), tok), pl.BlockSpec((tm, hk), tok),
            pl.BlockSpec((PEER_VALUE_TE, d), lambda i, e: (e, 0)),
            pl.BlockSpec((tm, d), tok),
            pl.BlockSpec((1, 1, d), lambda i, e: (i // tiles_per_batch, 0, 0)),
            pl.BlockSpec((1, d), lambda i, e: (0, 0)),
            pl.BlockSpec((1, d), lambda i, e: (0, 0)),
        ],
        out_specs=pl.BlockSpec((tm, d), tok, pipeline_mode=one),
        out_shape=jax.ShapeDtypeStruct((t, d), F32),
        scratch_shapes=[
            pltpu.VMEM((tm, hk), F32),
            pltpu.VMEM((tm * GATE_PITCH, LANES), F32),
            pltpu.VMEM((tm, d), F32),
        ],
        compiler_params=_cparams(("parallel", "arbitrary")),
        name="peer_values",
    )(ssel, a, b, g, v_tab, x1, g2, ln_g, ln_b)


FFT_N2 = 128
FFT_PITCH = FFT_N2 + 8
HY_POS_PAD = 32
FFT_UNROLL = 4


def _dft_constants(n):
    big = 2 * n
    n1c = big // FFT_N2
    half = n1c // 2
    k1 = np.arange(n1c)[:, None]
    n1 = np.arange(half)[None, :]
    ang = 2 * np.pi * k1 * n1 / n1c
    c, s = np.cos(ang), np.sin(ang)
    f1 = np.block([[c, s], [-s, c]])
    k2 = np.arange(FFT_N2)[:, None]
    n2 = np.arange(FFT_N2)[None, :]
    ang = 2 * np.pi * k2 * n2 / FFT_N2
    c, s = np.cos(ang), np.sin(ang)
    d3 = np.block([[c, s], [-s, c]])
    d3i = np.block([[c, -s], [s, c]])
    ang = 2 * np.pi * n1.T * k1.T / n1c
    c, s = np.cos(ang), np.sin(ang)
    f3 = np.block([[c, -s], [s, c]]) / big
    ang = 2 * np.pi * (np.arange(n1c)[:, None] * np.arange(FFT_N2)[None, :]) / big
    twr = np.repeat(np.cos(ang).reshape(-1, 1), LANES, axis=1)
    twi = np.repeat(-np.sin(ang).reshape(-1, 1), LANES, axis=1)
    as32 = lambda a: np.asarray(a, np.float32)

    def split(a):
        parts, rest = [], np.asarray(a, np.float64)
        for _ in range(3):
            p = np.asarray(rest, BF16)
            parts.append(p)
            rest = rest - p.astype(np.float64)
        return np.stack(parts)
    return split(f1), split(d3), split(d3i), split(f3), as32(twr), as32(twi)


def _hdot(a, b):
    return jnp.dot(a, b, preferred_element_type=F32, precision=HIGHEST)


def _dft_dot(m_ref, x):
    x_hi = x.astype(BF16)
    x_lo = (x - x_hi.astype(F32)).astype(BF16)
    m_hi = m_ref[0]
    dot = functools.partial(jnp.dot, preferred_element_type=F32)
    return (dot(m_hi, x_hi) + dot(m_hi, x_lo)) + (dot(m_ref[1], x_hi) + dot(m_ref[2], x_hi))


def _fft_stage1(u_re, u_im, f1_ref, a_re, a_im):
    half = f1_ref.shape[2] // 2
    n1c = f1_ref.shape[1] // 2

    def column(n2):
        return jnp.concatenate([u_re[pl.ds(n2, half, stride=FFT_PITCH), :],
                                u_im[pl.ds(n2, half, stride=FFT_PITCH), :]], axis=0)

    def body(i, carry):
        a = _dft_dot(f1_ref, jnp.concatenate([column(2 * i), column(2 * i + 1)], axis=1))
        for s in range(2):
            a_re[pl.ds(2 * i + s, n1c, stride=FFT_PITCH), :] = a[:n1c, s * LANES:(s + 1) * LANES]
            a_im[pl.ds(2 * i + s, n1c, stride=FFT_PITCH), :] = a[n1c:, s * LANES:(s + 1) * LANES]
        return carry
    lax.fori_loop(0, FFT_N2 // 2, body, 0, unroll=FFT_UNROLL)


def _twiddled_pair(a_re, a_im, twr_ref, twi_ref, i):
    t0 = pl.multiple_of(2 * i * FFT_N2, 2 * FFT_N2)
    rows = [pl.multiple_of((2 * i + s) * FFT_PITCH, 8) for s in range(2)]
    side = lambda ref, r: jnp.concatenate([ref[pl.ds(r[0], FFT_N2), :], ref[pl.ds(r[1], FFT_N2), :]], axis=1)
    tws = [pl.multiple_of(t0 + s * FFT_N2, FFT_N2) for s in range(2)]
    return rows, side(a_re, rows), side(a_im, rows), side(twr_ref, tws), side(twi_ref, tws)


def _fft_conv_middle(a_re, a_im, d3_ref, d3i_ref, twr_ref, twi_ref, hre_ref, him_ref):
    n1c = hre_ref.shape[1]

    def body(i, carry):
        rows, ar, ai, twr, twi = _twiddled_pair(a_re, a_im, twr_ref, twi_ref, i)
        x = _dft_dot(d3_ref, jnp.concatenate([ar * twr - ai * twi, ar * twi + ai * twr], axis=0))
        xr, xi = x[:FFT_N2], x[FFT_N2:]
        hr = jnp.concatenate([hre_ref[0, 2 * i], hre_ref[0, 2 * i + 1]], axis=1)
        hi = jnp.concatenate([him_ref[0, 2 * i], him_ref[0, 2 * i + 1]], axis=1)
        y = _dft_dot(d3i_ref, jnp.concatenate([xr * hr - xi * hi, xr * hi + xi * hr], axis=0))
        br, bi = y[:FFT_N2], y[FFT_N2:]
        out_re = br * twr + bi * twi
        out_im = bi * twr - br * twi
        for s in range(2):
            a_re[pl.ds(rows[s], FFT_N2), :] = out_re[:, s * LANES:(s + 1) * LANES]
            a_im[pl.ds(rows[s], FFT_N2), :] = out_im[:, s * LANES:(s + 1) * LANES]
        return carry
    lax.fori_loop(0, n1c // 2, body, 0, unroll=FFT_UNROLL)


def _fft_stage_inv(a_re, a_im, f3_ref, y_re, y_im):
    half = f3_ref.shape[1] // 2
    n1c = f3_ref.shape[2] // 2

    def column(n2):
        return jnp.concatenate([a_re[pl.ds(n2, n1c, stride=FFT_PITCH), :],
                                a_im[pl.ds(n2, n1c, stride=FFT_PITCH), :]], axis=0)

    def body(i, carry):
        y = _dft_dot(f3_ref, jnp.concatenate([column(2 * i), column(2 * i + 1)], axis=1))
        for s in range(2):
            y_re[pl.ds(2 * i + s, half, stride=FFT_PITCH), :] = y[:half, s * LANES:(s + 1) * LANES]
            y_im[pl.ds(2 * i + s, half, stride=FFT_PITCH), :] = y[half:, s * LANES:(s + 1) * LANES]
        return carry
    lax.fori_loop(0, FFT_N2 // 2, body, 0, unroll=FFT_UNROLL)


def _short_conv_chunk(x_ref, bi, j, nblk, w_ref, b_ref):
    r0 = j * FFT_N2
    cur = x_ref[bi, r0:r0 + FFT_N2, :]
    row = lax.broadcasted_iota(jnp.int32, cur.shape, 0)
    if j == 0:
        prev = jnp.where(row == 0, 0.0, pltpu.roll(cur, 1, 0))
    else:
        prev = x_ref[bi, r0 - 1:r0 + FFT_N2 - 1, :]
    if j == nblk - 1:
        nxt = jnp.where(row == FFT_N2 - 1, 0.0, pltpu.roll(cur, FFT_N2 - 1, 0))
    else:
        nxt = x_ref[bi, r0 + 1:r0 + FFT_N2 + 1, :]
    return prev * w_ref[0:1, :] + cur * w_ref[1:2, :] + nxt * w_ref[2:3, :] + b_ref[...]


def _hy_conv_body(conv_a, a_ref, g_ref, wa_ref, ba_ref, wg_ref, bg_ref, skip_ref, hre_ref, him_ref,
                  f1_ref, d3_ref, d3i_ref, f3_ref, twr_ref, twi_ref, o_ref,
                  u_re, u_im, a_re, a_im, y_re, y_im):
    nblk = a_ref.shape[1] // FFT_N2
    for bi, dst in ((0, u_re), (1, u_im)):
        for j in range(nblk):
            if conv_a:
                blk = _short_conv_chunk(a_ref, bi, j, nblk, wa_ref, ba_ref)
            else:
                blk = a_ref[bi, j * FFT_N2:(j + 1) * FFT_N2, :]
            dst[j * FFT_PITCH:j * FFT_PITCH + FFT_N2, :] = blk
    _fft_stage1(u_re, u_im, f1_ref, a_re, a_im)
    _fft_conv_middle(a_re, a_im, d3_ref, d3i_ref, twr_ref, twi_ref, hre_ref, him_ref)
    _fft_stage_inv(a_re, a_im, f3_ref, y_re, y_im)
    skip = skip_ref[...]
    for bi, (ysrc, usrc) in enumerate(((y_re, u_re), (y_im, u_im))):
        for j in range(nblk):
            rows = slice(j * FFT_PITCH, j * FFT_PITCH + FFT_N2)
            gate = _short_conv_chunk(g_ref, bi, j, nblk, wg_ref, bg_ref)
            o_ref[bi, j * FFT_N2:(j + 1) * FFT_N2, :] = gate * (ysrc[rows, :] + usrc[rows, :] * skip)


def _hy_conv(a, a_col, g, g_col, conv_w, conv_b, skip, hre, him, order, consts, conv_a):
    b, n, _ = a.shape
    f1, d3, d3i, f3, twr, twi = consts
    n1c = f1.shape[1] // 2
    half = n1c // 2
    w = LANES
    tiles = HY_WIDTH // w
    one = pl.Buffered(1)
    data = lambda col: pl.BlockSpec((2, n, w), lambda ct, p: (p, 0, col + ct))
    wspec = lambda col: pl.BlockSpec((3, w), lambda ct, p: (0, col + ct))
    bspec = lambda col: pl.BlockSpec((1, w), lambda ct, p: (0, col + ct))
    hspec = pl.BlockSpec((1, n1c, FFT_N2, w), lambda ct, p: (order, 0, 0, ct), pipeline_mode=one)
    cs = lambda arr: pl.BlockSpec(arr.shape, lambda ct, p: (0,) * arr.ndim, pipeline_mode=one)
    a_wcol = a_col if conv_a else g_col
    return pl.pallas_call(
        functools.partial(_hy_conv_body, conv_a),
        grid=(tiles, b // 2),
        in_specs=[data(a_col), data(g_col), wspec(a_wcol), bspec(a_wcol), wspec(g_col), bspec(g_col),
                  pl.BlockSpec((1, w), lambda ct, p: (0, ct)), hspec, hspec,
                  cs(f1), cs(d3), cs(d3i), cs(f3), cs(twr), cs(twi)],
        out_specs=pl.BlockSpec((2, n, w), lambda ct, p: (p, 0, ct), pipeline_mode=one),
        out_shape=jax.ShapeDtypeStruct((b, n, HY_WIDTH), F32),
        scratch_shapes=[pltpu.VMEM((half * FFT_PITCH, w), F32), pltpu.VMEM((half * FFT_PITCH, w), F32),
                        pltpu.VMEM((n1c * FFT_PITCH, w), F32), pltpu.VMEM((n1c * FFT_PITCH, w), F32),
                        pltpu.VMEM((half * FFT_PITCH, w), F32), pltpu.VMEM((half * FFT_PITCH, w), F32)],
        compiler_params=_cparams(("parallel", "arbitrary")),
        name=f"hyena_conv{order + 1}",
    )(a, g, conv_w, conv_b, conv_w, conv_b, skip, hre, him, f1, d3, d3i, f3, twr, twi)


def _hy_filter_body(z_ref, w1_ref, b1_ref, f1_ref, w2_ref, b2_ref, f2_ref, w3_ref, b3_ref, dl_ref, o_ref):
    z = z_ref[...]
    h = jnp.sin(f1_ref[...] * (_hdot(z, w1_ref[...]) + b1_ref[...]))
    h = jnp.sin(f2_ref[...] * (_hdot(h, w2_ref[...]) + b2_ref[...]))
    h = _hdot(h, w3_ref[...]) + b3_ref[...]
    t = z[:, 0:1]
    o_ref[...] = h * (jnp.exp(-t * dl_ref[...]) + HY_WINDOW_SHIFT)


def _hy_filters(n, w1, b1, fr1, w2, b2, fr2, w3, b3):
    t = jnp.linspace(0.0, 1.0, n, dtype=F32)[:, None]
    wv = 2.0 * math.pi * jnp.arange(n, dtype=F32)[:, None] / n
    bands = jnp.linspace(1e-4, HY_BANDS - 1, HY_BANDS, dtype=F32)[None, :]
    z = jnp.concatenate([t, jnp.cos(bands * wv), -jnp.sin(bands * wv)], axis=-1)
    pos = z.shape[1]
    z = jnp.pad(z, ((0, 0), (0, HY_POS_PAD - pos)))
    w1p = jnp.pad(w1, ((0, HY_POS_PAD - pos), (0, 0)))
    min_decay = math.log(HY_DECAY_TARGET) / HY_SLOW_DECAY_PCT
    max_decay = math.log(HY_DECAY_TARGET) / HY_FAST_DECAY_PCT
    deltas = jnp.abs(jnp.linspace(min_decay, max_decay, HY_WIDTH, dtype=F32))
    n_out = w3.shape[1]
    dl = jnp.tile(deltas, n_out // HY_WIDTH)[None, :]
    tn = 512
    hid = w2.shape[0]
    full = lambda shape: pl.BlockSpec(shape, lambda i: (0, 0))
    return pl.pallas_call(
        _hy_filter_body,
        grid=(n // tn,),
        in_specs=[pl.BlockSpec((tn, HY_POS_PAD), lambda i: (i, 0)), full((HY_POS_PAD, hid)), full((1, hid)),
                  full((1, hid)), full((hid, hid)), full((1, hid)), full((1, hid)), full((hid, n_out)),
                  full((1, n_out)), full((1, n_out))],
        out_specs=pl.BlockSpec((tn, n_out), lambda i: (i, 0)),
        out_shape=jax.ShapeDtypeStruct((n, n_out), F32),
        compiler_params=_cparams(("parallel",)),
        name="hyena_filters",
    )(z, w1p, b1[None], fr1[None], w2, b2[None], fr2[None], w3, b3[None], dl)


def _hy_spectrum_body(f_ref, b_ref, f1_ref, d3_ref, twr_ref, twi_ref, hre_ref, him_ref, u_re, u_im, a_re, a_im):
    nblk = f_ref.shape[0] // FFT_N2
    n1c = hre_ref.shape[1]
    for part, out_ref in enumerate((hre_ref, him_ref)):
        for j in range(nblk):
            f = f_ref[j * FFT_N2:(j + 1) * FFT_N2, :]
            bw = b_ref[j * FFT_N2:(j + 1) * FFT_N2, :]
            if j == 0:
                row = lax.broadcasted_iota(jnp.int32, bw.shape, 0)
                bw = jnp.where(row == 0, 0.0, bw)
            rows = slice(j * FFT_PITCH, j * FFT_PITCH + FFT_N2)
            u_re[rows, :] = f + bw if part == 0 else f - bw
            u_im[rows, :] = jnp.zeros((FFT_N2, LANES), F32)
        _fft_stage1(u_re, u_im, f1_ref, a_re, a_im)

        def body(i, carry, part=part, out_ref=out_ref):
            _, ar, ai, twr, twi = _twiddled_pair(a_re, a_im, twr_ref, twi_ref, i)
            x = _dft_dot(d3_ref, jnp.concatenate([ar * twr - ai * twi, ar * twi + ai * twr], axis=0))
            x = x[:FFT_N2] if part == 0 else x[FFT_N2:]
            for s in range(2):
                out_ref[0, 2 * i + s] = x[:, s * LANES:(s + 1) * LANES]
            return carry
        lax.fori_loop(0, n1c // 2, body, 0, unroll=FFT_UNROLL)


def _hy_spectrum(h, consts):
    n = h.shape[0]
    f1, d3, _, _, twr, twi = consts
    n1c = f1.shape[1] // 2
    half = n1c // 2
    tiles = HY_WIDTH // LANES
    cs = lambda arr: pl.BlockSpec(arr.shape, lambda o, ct: (0,) * arr.ndim)
    out = jax.ShapeDtypeStruct((HY_ORDER, n1c, FFT_N2, HY_WIDTH), F32)
    ospec = pl.BlockSpec((1, n1c, FFT_N2, LANES), lambda o, ct: (o, 0, 0, ct))
    return pl.pallas_call(
        _hy_spectrum_body,
        grid=(HY_ORDER, tiles),
        in_specs=[pl.BlockSpec((n, LANES), lambda o, ct: (0, o * tiles + ct)),
                  pl.BlockSpec((n, LANES), lambda o, ct: (0, (HY_ORDER + o) * tiles + ct)),
                  cs(f1), cs(d3), cs(twr), cs(twi)],
        out_specs=[ospec, ospec],
        out_shape=[out, out],
        scratch_shapes=[pltpu.VMEM((half * FFT_PITCH, LANES), F32), pltpu.VMEM((half * FFT_PITCH, LANES), F32),
                        pltpu.VMEM((n1c * FFT_PITCH, LANES), F32), pltpu.VMEM((n1c * FFT_PITCH, LANES), F32)],
        compiler_params=_cparams(("parallel", "parallel")),
        name="hyena_spectrum",
    )(h, h, f1, d3, twr, twi)


def _hyena(p_lat, conv_w, conv_b, w1, b1, fr1, w2, b2, fr2, w3, b3, skip):
    n = p_lat.shape[1]
    consts = _dft_constants(n)
    h = _hy_filters(n, w1, b1, fr1, w2, b2, fr2, w3, b3)
    hre, him = _hy_spectrum(h, consts)
    tiles = HY_WIDTH // LANES
    col = COL_HY // LANES
    cw = jnp.pad(conv_w, ((0, 0), (COL_HY, 0)))
    cb = jnp.pad(conv_b[None], ((0, 0), (COL_HY, 0)))
    y = _hy_conv(p_lat, col, p_lat, col + tiles, cw, cb, skip[0][None], hre, him, 0, consts, True)
    return _hy_conv(y, 0, p_lat, col + 2 * tiles, cw, cb, skip[1][None], hre, him, 1, consts, False)


def _split_bf16(w):
    hi = w.astype(BF16)
    return jnp.stack([hi, (w - hi.astype(F32)).astype(BF16)])


def _pad_heads(w, heads):
    d = w.shape[0]
    w = w.reshape(d, heads, HEAD_DIM)
    w = jnp.pad(w, ((0, 0), (0, 0), (0, LANES - HEAD_DIM)))
    return w.reshape(d, heads * LANES)


def kernel(x, c, ctx, c_ctx, w_mod, b_mod, w_in, hy_conv_w, hy_conv_b, hy_f_w1, hy_f_b1, hy_f_freq1, hy_f_w2,
           hy_f_b2, hy_f_freq2, hy_f_w3, hy_f_b3, hy_skip, attn_sink, w_out, ln1_g, ln1_b, peer_wq, peer_keys1,
           peer_keys2, peer_u, peer_v, ln2_g, ln2_b):
    b, n, d = x.shape
    l = 0
    cc = jnp.concatenate([c, c_ctx[None], jnp.zeros((8 - b - 1, d), F32)], axis=0)
    mod = _modulation(cc, w_mod[l], b_mod[l][None])
    mod_lat = mod[:b].reshape(b, 6, 1, d)
    sh1, sc1, g1, sh2, sc2, g2 = (mod_lat[:, i] for i in range(6))
    mod_c = mod[b].reshape(6, 1, 1, d)
    csh1, csc1 = mod_c[0], mod_c[1]

    w = w_in[l]
    w_q = _pad_heads(w[:, PROJ_HY:KV_START], ATT_HEADS)
    w_k = _pad_heads(w[:, KV_START:KV_START + PROJ_KV], ATT_KV_HEADS)
    w_v = _pad_heads(w[:, KV_START + PROJ_KV:], ATT_KV_HEADS)
    w_pad = jnp.concatenate([w_q, w[:, :PROJ_HY], w_k, w_v], axis=1).astype(BF16)
    w_kv = jnp.concatenate([w_k, w_v], axis=1).astype(BF16)

    p_lat = _mod_matmul(x, sc1, sh1, w_pad, 512, "in_proj")
    kv_ctx = _mod_matmul(ctx, csc1, csh1, w_kv, ctx.shape[1], "ctx_kv_proj")

    cos_tab, sin_tab = _rope_tables(n)
    att = _attention(p_lat, kv_ctx, attn_sink[l], cos_tab, sin_tab)

    hy = _hyena(p_lat, hy_conv_w[l], hy_conv_b[l], hy_f_w1[l], hy_f_b1[l], hy_f_freq1[l], hy_f_w2[l], hy_f_b2[l],
                hy_f_freq2[l], hy_f_w3[l], hy_f_b3[l], hy_skip[l])

    wo = w_out[l]
    w_o_hy = wo[:HY_WIDTH].astype(BF16)
    w_o_att = jnp.pad(wo[HY_WIDTH:].reshape(ATT_HEADS, HEAD_DIM, d),
                      ((0, 0), (0, LANES - HEAD_DIM), (0, 0))).reshape(QPAD, d).astype(BF16)
    x1, hq = _outproj_ln(hy, att, x, w_o_hy, w_o_att, g1, sc2, sh2, ln1_g[l][None], ln1_b[l][None])

    hq2 = hq.reshape(b * n, d)
    a_idx, b_idx, gate = _peer_topk(hq2, peer_wq[l].astype(BF16), _split_bf16(peer_keys1[l]),
                                    _split_bf16(peer_keys2[l]))
    s_sel = _peer_scores(hq2, a_idx, b_idx, peer_u[l].astype(BF16))
    out = _peer_values(s_sel, a_idx, b_idx, gate, peer_v[l].astype(BF16), x1.reshape(b * n, d), g2,
                       ln2_g[l][None], ln2_b[l][None], n)
    return out.reshape(b, n, d)
```

```python
import functools
import math

import jax
import jax.numpy as jnp
import numpy as np
from jax import lax
from jax.experimental import pallas as pl
from jax.experimental.pallas import tpu as pltpu

F32 = jnp.float32
BF16 = jnp.bfloat16
HIGHEST = lax.Precision.HIGHEST

LANES = 128
VMEM_LIMIT = 60000 * 1024

D_MODEL = 1024
GRID_W = 64
HY_WIDTH = 512
HY_ORDER = 2
HY_BANDS = 8
HY_DECAY_TARGET = 1e-2
HY_FAST_DECAY_PCT = 0.3
HY_SLOW_DECAY_PCT = 1.5
HY_WINDOW_SHIFT = 0.05
ATT_HEADS = 8
ATT_KV_HEADS = 2
ATT_REP = ATT_HEADS // ATT_KV_HEADS
HEAD_DIM = 64
WINDOW = 128
BLOCK = 128
ROPE_BASE = 10000.0
ROPE_FREQS = HEAD_DIM // 4
PROJ_HY = (HY_ORDER + 1) * HY_WIDTH
PROJ_Q = ATT_HEADS * HEAD_DIM
PROJ_KV = ATT_KV_HEADS * HEAD_DIM
KV_START = PROJ_HY + PROJ_Q
PEER_KEYS = 128
PEER_HEADS = 8
PEER_QDIM = 256
PEER_TOPK = 16
LN_EPS = 1e-5
NEG_INF = -1e30
DEPTH = 1
DEEPNORM_ALPHA = (2.0 * DEPTH) ** 0.25

QPAD = ATT_HEADS * LANES
KVPAD = ATT_KV_HEADS * LANES
COL_Q = 0
COL_HY = QPAD
COL_K = QPAD + PROJ_HY
COL_V = COL_K + KVPAD
PROJ_PAD = COL_V + KVPAD


def _cparams(sem):
    return pltpu.CompilerParams(dimension_semantics=sem, vmem_limit_bytes=VMEM_LIMIT)


def _mod_body(c_ref, w_ref, b_ref, o_ref):
    c = c_ref[...]
    a = c * jax.nn.sigmoid(c)
    o_ref[...] = jnp.dot(a, w_ref[...], preferred_element_type=F32, precision=HIGHEST) + b_ref[...]


def _modulation(cc, w_mod, b_mod):
    rows, d = cc.shape
    n_out = w_mod.shape[1]
    tn = 1536
    return pl.pallas_call(
        _mod_body,
        grid=(n_out // tn,),
        in_specs=[
            pl.BlockSpec((rows, d), lambda j: (0, 0)),
            pl.BlockSpec((d, tn), lambda j: (0, j)),
            pl.BlockSpec((1, tn), lambda j: (0, j)),
        ],
        out_specs=pl.BlockSpec((rows, tn), lambda j: (0, j)),
        out_shape=jax.ShapeDtypeStruct((rows, n_out), F32),
        compiler_params=_cparams(("arbitrary",)),
        name="modulation",
    )(cc, w_mod, b_mod)


def _mod_matmul_body(x_ref, sc_ref, sh_ref, w_ref, o_ref):
    h = x_ref[0] * (1.0 + sc_ref[0]) + sh_ref[0]
    o_ref[0] = jnp.dot(h.astype(BF16), w_ref[...], preferred_element_type=F32)


def _mod_matmul(x, sc, sh, w, tm, name):
    b, n, d = x.shape
    n_out = w.shape[1]
    per_batch = sc.shape[0] == b
    mod_map = (lambda bi, i: (bi, 0, 0)) if per_batch else (lambda bi, i: (0, 0, 0))
    return pl.pallas_call(
        _mod_matmul_body,
        grid=(b, n // tm),
        in_specs=[
            pl.BlockSpec((1, tm, d), lambda bi, i: (bi, i, 0)),
            pl.BlockSpec((1, 1, d), mod_map),
            pl.BlockSpec((1, 1, d), mod_map),
            pl.BlockSpec((d, n_out), lambda bi, i: (0, 0)),
        ],
        out_specs=pl.BlockSpec((1, tm, n_out), lambda bi, i: (bi, i, 0)),
        out_shape=jax.ShapeDtypeStruct((b, n, n_out), F32),
        compiler_params=_cparams(("parallel", "parallel")),
        name=name,
    )(x, sc, sh, w)


def _rope_head(x, cos, sin_signed):
    lane = lax.broadcasted_iota(jnp.int32, x.shape, 1)
    first_half = (lane % 32) < 16
    partner = jnp.where(first_half, pltpu.roll(x, LANES - 16, 1), pltpu.roll(x, 16, 1))
    return x * cos + partner * sin_signed


ATT_TQ = 512
ROPE_CHUNK = 512


def _attn_body(sink_ref, q_ref, k_ref, v_ref, kvc_ref, cosq_ref, sinq_ref, cosk_ref, sink_tab_ref,
               o_ref, ks_ref, vs_ref, kcs_ref, vcs_ref):
    n = k_ref.shape[1]
    iq = pl.program_id(1)
    scale = HEAD_DIM ** -0.5

    @pl.when(iq == 0)
    def _prepare_keys():
        def chunk(ci, carry):
            r0 = pl.multiple_of(ci * ROPE_CHUNK, ROPE_CHUNK)
            cos = cosk_ref[pl.ds(r0, ROPE_CHUNK), :]
            sin = sink_tab_ref[pl.ds(r0, ROPE_CHUNK), :]
            for g in range(ATT_KV_HEADS):
                kg = k_ref[0, pl.ds(r0, ROPE_CHUNK), g * LANES:(g + 1) * LANES]
                ks_ref[pl.ds(r0, ROPE_CHUNK), g * LANES:(g + 1) * LANES] = _rope_head(kg, cos, sin).astype(BF16)
            vs_ref[pl.ds(r0, ROPE_CHUNK), :] = v_ref[0, pl.ds(r0, ROPE_CHUNK), :].astype(BF16)
            return carry
        lax.fori_loop(0, n // ROPE_CHUNK, chunk, 0)
        kcs_ref[...] = kvc_ref[0, :, 0:KVPAD].astype(BF16)
        vcs_ref[...] = kvc_ref[0, :, KVPAD:2 * KVPAD].astype(BF16)

    n_loc = 3 * BLOCK
    rows = ATT_REP * BLOCK
    row_i = lax.broadcasted_iota(jnp.int32, (rows, n_loc), 0)
    col_i = lax.broadcasted_iota(jnp.int32, (rows, n_loc), 1)
    rel = col_i - (row_i % BLOCK)
    head_of_row = lax.broadcasted_iota(jnp.int32, (rows, 1), 0) // BLOCK

    for j in range(ATT_TQ // BLOCK):
        blk = iq * (ATT_TQ // BLOCK) + j
        start = pl.multiple_of(jnp.clip((blk - 1) * BLOCK, 0, n - n_loc), BLOCK)
        qrows = slice(j * BLOCK, (j + 1) * BLOCK)
        cosq = cosq_ref[qrows, :]
        sinq = sinq_ref[qrows, :]
        delta = rel + (start - blk * BLOCK)
        in_window = jnp.abs(delta) <= WINDOW
        for g in range(ATT_KV_HEADS):
            heads = [ATT_REP * g + r for r in range(ATT_REP)]
            qg = jnp.concatenate(
                [(_rope_head(q_ref[0, qrows, h * LANES:(h + 1) * LANES], cosq, sinq) * scale).astype(BF16)
                 for h in heads], axis=0)
            kg = ks_ref[pl.ds(start, n_loc), g * LANES:(g + 1) * LANES]
            vg = vs_ref[pl.ds(start, n_loc), g * LANES:(g + 1) * LANES]
            kcg = kcs_ref[:, g * LANES:(g + 1) * LANES]
            vcg = vcs_ref[:, g * LANES:(g + 1) * LANES]
            nt = (((1,), (1,)), ((), ()))
            s_loc = lax.dot_general(qg, kg, nt, preferred_element_type=F32)
            s_ctx = lax.dot_general(qg, kcg, nt, preferred_element_type=F32)
            s_loc = jnp.where(in_window, s_loc, NEG_INF)
            sink_col = jnp.zeros((rows, 1), F32)
            for r, h in enumerate(heads):
                sink_col = jnp.where(head_of_row == r, sink_ref[h], sink_col)
            m = jnp.maximum(jnp.maximum(jnp.max(s_loc, axis=1, keepdims=True),
                                        jnp.max(s_ctx, axis=1, keepdims=True)), sink_col)
            p_loc = jnp.exp(s_loc - m)
            p_ctx = jnp.exp(s_ctx - m)
            den = (jnp.sum(p_loc, axis=1, keepdims=True) + jnp.sum(p_ctx, axis=1, keepdims=True)
                   + jnp.exp(sink_col - m))
            o = (jnp.dot(p_loc.astype(BF16), vg, preferred_element_type=F32)
                 + jnp.dot(p_ctx.astype(BF16), vcg, preferred_element_type=F32)) / den
            for r, h in enumerate(heads):
                o_ref[0, qrows, h * LANES:(h + 1) * LANES] = o[r * BLOCK:(r + 1) * BLOCK].astype(BF16)


def _attention(p_lat, kv_ctx, sink, cos_tab, sin_tab):
    b, n, _ = p_lat.shape
    n_ctx = kv_ctx.shape[1]
    grid_spec = pltpu.PrefetchScalarGridSpec(
        num_scalar_prefetch=1,
        grid=(b, n // ATT_TQ),
        in_specs=[
            pl.BlockSpec((1, ATT_TQ, QPAD), lambda bi, i, s: (bi, i, COL_Q // QPAD)),
            pl.BlockSpec((1, n, KVPAD), lambda bi, i, s: (bi, 0, COL_K // KVPAD)),
            pl.BlockSpec((1, n, KVPAD), lambda bi, i, s: (bi, 0, COL_V // KVPAD)),
            pl.BlockSpec((1, n_ctx, 2 * KVPAD), lambda bi, i, s: (bi, 0, 0)),
            pl.BlockSpec((ATT_TQ, LANES), lambda bi, i, s: (i, 0)),
            pl.BlockSpec((ATT_TQ, LANES), lambda bi, i, s: (i, 0)),
            pl.BlockSpec((n, LANES), lambda bi, i, s: (0, 0)),
            pl.BlockSpec((n, LANES), lambda bi, i, s: (0, 0)),
        ],
        out_specs=pl.BlockSpec((1, ATT_TQ, QPAD), lambda bi, i, s: (bi, i, 0)),
        scratch_shapes=[
            pltpu.VMEM((n, KVPAD), BF16),
            pltpu.VMEM((n, KVPAD), BF16),
            pltpu.VMEM((n_ctx, KVPAD), BF16),
            pltpu.VMEM((n_ctx, KVPAD), BF16),
        ],
    )
    return pl.pallas_call(
        _attn_body,
        grid_spec=grid_spec,
        out_shape=jax.ShapeDtypeStruct((b, n, QPAD), BF16),
        compiler_params=_cparams(("parallel", "arbitrary")),
        name="window_attention",
    )(sink, p_lat, p_lat, p_lat, kv_ctx, cos_tab, sin_tab, cos_tab, sin_tab)


def _rope_tables(n):
    rows = n // GRID_W
    row = jnp.repeat(jnp.arange(rows, dtype=F32), GRID_W)
    col = jnp.tile(jnp.arange(GRID_W, dtype=F32), rows)
    inv = ROPE_BASE ** (-jnp.arange(ROPE_FREQS, dtype=F32) / ROPE_FREQS)
    ang_r = row[:, None] * inv
    ang_c = col[:, None] * inv
    pad1 = jnp.ones((n, LANES - HEAD_DIM), F32)
    pad0 = jnp.zeros((n, LANES - HEAD_DIM), F32)
    cos = jnp.concatenate([jnp.cos(ang_r), jnp.cos(ang_r), jnp.cos(ang_c), jnp.cos(ang_c), pad1], axis=1)
    sin = jnp.concatenate([-jnp.sin(ang_r), jnp.sin(ang_r), -jnp.sin(ang_c), jnp.sin(ang_c), pad0], axis=1)
    return cos, sin


def _layer_norm(r, g, b):
    mu = jnp.mean(r, axis=-1, keepdims=True)
    var = jnp.mean(jnp.square(r - mu), axis=-1, keepdims=True)
    return (r - mu) * lax.rsqrt(var + LN_EPS) * g + b


def _outproj_body(hy_ref, att_ref, x_ref, wh_ref, wa_ref, g1_ref, sc2_ref, sh2_ref, lg_ref, lb_ref,
                  x1_ref, hq_ref):
    y = (jnp.dot(hy_ref[0].astype(BF16), wh_ref[...], preferred_element_type=F32)
         + jnp.dot(att_ref[0], wa_ref[...], preferred_element_type=F32))
    x1 = _layer_norm(DEEPNORM_ALPHA * x_ref[0] + g1_ref[0] * y, lg_ref[...], lb_ref[...])
    x1_ref[0] = x1
    hq_ref[0] = (x1 * (1.0 + sc2_ref[0]) + sh2_ref[0]).astype(BF16)


def _outproj_ln(hy, att, x, w_hy, w_att, g1, sc2, sh2, ln_g, ln_b, tm=512):
    b, n, d = x.shape
    modspec = pl.BlockSpec((1, 1, d), lambda bi, i: (bi, 0, 0))
    vecspec = pl.BlockSpec((1, d), lambda bi, i: (0, 0))
    return pl.pallas_call(
        _outproj_body,
        grid=(b, n // tm),
        in_specs=[
            pl.BlockSpec((1, tm, HY_WIDTH), lambda bi, i: (bi, i, 0)),
            pl.BlockSpec((1, tm, QPAD), lambda bi, i: (bi, i, 0)),
            pl.BlockSpec((1, tm, d), lambda bi, i: (bi, i, 0)),
            pl.BlockSpec(w_hy.shape, lambda bi, i: (0, 0)),
            pl.BlockSpec(w_att.shape, lambda bi, i: (0, 0)),
            modspec, modspec, modspec, vecspec, vecspec,
        ],
        out_specs=[
            pl.BlockSpec((1, tm, d), lambda bi, i: (bi, i, 0)),
            pl.BlockSpec((1, tm, d), lambda bi, i: (bi, i, 0)),
        ],
        out_shape=[jax.ShapeDtypeStruct((b, n, d), F32), jax.ShapeDtypeStruct((b, n, d), BF16)],
        compiler_params=_cparams(("parallel", "parallel")),
        name="outproj_ln1",
    )(hy, att, x, w_hy, w_att, g1, sc2, sh2, ln_g, ln_b)


PEER_TM = 512
TOPK_UNROLL = 4
_STAIR = sorted(((i, j) for i in range(PEER_TOPK) for j in range(PEER_TOPK) if (i + 1) * (j + 1) <= PEER_TOPK),
                key=lambda p: p[0] * PEER_TOPK + p[1])
_STAIR_ROWS = -(-len(_STAIR) // 8) * 8
_STAIR_COUNT = [PEER_TOPK // (i + 1) for i in range(PEER_TOPK)]
_STAIR_START = [sum(_STAIR_COUNT[:i]) for i in range(PEER_TOPK)]


def _stair_ids(tm):
    ids = np.full((_STAIR_ROWS,), float(PEER_TOPK * PEER_TOPK), np.float32)
    ids[:len(_STAIR)] = [i * PEER_TOPK + j for i, j in _STAIR]
    return np.repeat(ids[:, None], tm, axis=1)


def _select_topk(problems, write_row):
    tm = problems[0][0].shape[1]

    def step(k, prev):
        new = []
        for p, ((s_ref, ids, pad_id), prev_id) in enumerate(zip(problems, prev)):
            s = jnp.where(ids == prev_id, -jnp.inf, s_ref[...])
            s_ref[...] = s
            m = jnp.max(s, axis=0, keepdims=True)
            win = jnp.min(jnp.where(s == m, ids, pad_id), axis=0, keepdims=True)
            write_row(p, k, m, win)
            new.append(win)
        return tuple(new)

    lax.fori_loop(0, PEER_TOPK, step, tuple(jnp.full((1, tm), -1.0, F32) for _ in problems), unroll=TOPK_UNROLL)


def _rows_by_rank(rank, table_ref):
    out = jnp.zeros(rank.shape, F32)
    for p in range(PEER_TOPK):
        out = jnp.where(rank == float(p), table_ref[p:p + 1, :], out)
    return out


def _peer_topk_body(hq_ref, wq_ref, k1_ref, k2_ref, sid_ref, a_ref, b_ref, g_ref,
                    q_s, s1_s, s2_s, v1_s, i1_s, v2_s, i2_s, c_s, t_s, f_s, ao_s, bo_s, go_s):
    tm = hq_ref.shape[0]
    q = jnp.dot(hq_ref[...], wq_ref[...], preferred_element_type=F32)
    for c in range(2 * PEER_HEADS):
        q_s[c] = q[:, c * LANES:(c + 1) * LANES]
    nt = (((1,), (1,)), ((), ()))
    key_id = lax.broadcasted_iota(jnp.int32, (PEER_KEYS, tm), 0).astype(F32)

    def head(h, carry):
        for half, (kref, s_s) in enumerate(((k1_ref, s1_s), (k2_ref, s2_s))):
            qh = q_s[2 * h + half]
            q_hi = qh.astype(BF16)
            q_lo = (qh - q_hi.astype(F32)).astype(BF16)
            ntdot = functools.partial(lax.dot_general, dimension_numbers=nt, preferred_element_type=F32)
            s_s[...] = ntdot(kref[0], q_hi) + (ntdot(kref[0], q_lo) + ntdot(kref[1], q_hi))

        def write1(p, k, val, idx):
            vs, is_ = ((v1_s, i1_s), (v2_s, i2_s))[p]
            vs[pl.ds(k, 1), :] = val
            is_[pl.ds(k, 1), :] = idx
        _select_topk([(s1_s, key_id, float(PEER_KEYS)), (s2_s, key_id, float(PEER_KEYS))], write1)

        c_s[...] = jnp.full(c_s.shape, -jnp.inf, F32)
        for i in range(PEER_TOPK):
            r0, cnt = _STAIR_START[i], _STAIR_COUNT[i]
            c_s[r0:r0 + cnt, :] = v1_s[i:i + 1, :] + v2_s[0:cnt, :]

        def write2(p, k, val, idx):
            t_s[pl.ds(k, 1), :] = val
            f_s[pl.ds(k, 1), :] = idx
        _select_topk([(c_s, sid_ref[...], float(PEER_TOPK * PEER_TOPK))], write2)

        flat = f_s[...]
        rank1 = jnp.floor(flat * (1.0 / PEER_TOPK))
        rank2 = flat - rank1 * PEER_TOPK
        rows = pl.ds(pl.multiple_of(h * PEER_TOPK, PEER_TOPK), PEER_TOPK)
        ao_s[rows, :] = _rows_by_rank(rank1, i1_s)
        bo_s[rows, :] = _rows_by_rank(rank2, i2_s)
        t = t_s[...]
        e = jnp.exp(t - jnp.max(t, axis=0, keepdims=True))
        go_s[rows, :] = e / jnp.sum(e, axis=0, keepdims=True)
        return carry

    lax.fori_loop(0, PEER_HEADS, head, 0)
    a_ref[...] = ao_s[...].T
    b_ref[...] = bo_s[...].T
    g_ref[...] = go_s[...].T


def _peer_topk(hq, wq, keys1, keys2):
    t, d = hq.shape
    tm = PEER_TM
    hk = PEER_HEADS * PEER_TOPK
    out = jax.ShapeDtypeStruct((t, hk), F32)
    ospec = pl.BlockSpec((tm, hk), lambda i: (i, 0))
    sid = _stair_ids(tm)
    return pl.pallas_call(
        _peer_topk_body,
        grid=(t // tm,),
        in_specs=[
            pl.BlockSpec((tm, d), lambda i: (i, 0)),
            pl.BlockSpec(wq.shape, lambda i: (0, 0)),
            pl.BlockSpec(keys1.shape, lambda i: (0, 0, 0)),
            pl.BlockSpec(keys2.shape, lambda i: (0, 0, 0)),
            pl.BlockSpec(sid.shape, lambda i: (0, 0)),
        ],
        out_specs=[ospec, ospec, ospec],
        out_shape=[out, out, out],
        scratch_shapes=[
            pltpu.VMEM((2 * PEER_HEADS, tm, LANES), F32),
            pltpu.VMEM((PEER_KEYS, tm), F32), pltpu.VMEM((PEER_KEYS, tm), F32),
            pltpu.VMEM((PEER_TOPK, tm), F32), pltpu.VMEM((PEER_TOPK, tm), F32),
            pltpu.VMEM((PEER_TOPK, tm), F32), pltpu.VMEM((PEER_TOPK, tm), F32),
            pltpu.VMEM((_STAIR_ROWS, tm), F32),
            pltpu.VMEM((PEER_TOPK, tm), F32), pltpu.VMEM((PEER_TOPK, tm), F32),
            pltpu.VMEM((hk, tm), F32), pltpu.VMEM((hk, tm), F32), pltpu.VMEM((hk, tm), F32),
        ],
        compiler_params=_cparams(("parallel",)),
        name="peer_topk",
    )(hq, wq, keys1, keys2, sid)


PEER_TE = 8192
PEER_GROUPS = PEER_TE // PEER_KEYS
PEER_SCORE_TM = 512
PEER_VALUE_TM = 512
PEER_VALUE_TE = 2048
GATE_PITCH = PEER_KEYS + 8
GATE_UNROLL = 64


def _peer_scores_body(hq_ref, a_ref, b_ref, u_ref, o_ref):
    e = pl.program_id(0)
    s = lax.dot_general(hq_ref[...], u_ref[...], (((1,), (1,)), ((), ())),
                        preferred_element_type=F32)
    b_idx = b_ref[...].astype(jnp.int32)
    a_val = a_ref[...]
    cur = jnp.zeros(a_val.shape, F32)
    for jj in range(PEER_GROUPS):
        cand = jnp.take_along_axis(s[:, jj * LANES:(jj + 1) * LANES], b_idx, axis=1)
        cur = jnp.where(a_val == (e * PEER_GROUPS + jj).astype(F32), cand, cur)
    o_ref[0] = cur


def _peer_scores(hq, a, b, u_tab):
    t, d = hq.shape
    tm = PEER_SCORE_TM
    hk = a.shape[1]
    n_e = u_tab.shape[0] // PEER_TE
    tok = lambda e, i: (i, 0)
    return pl.pallas_call(
        _peer_scores_body,
        grid=(n_e, t // tm),
        in_specs=[
            pl.BlockSpec((tm, d), tok),
            pl.BlockSpec((tm, hk), tok), pl.BlockSpec((tm, hk), tok),
            pl.BlockSpec((PEER_TE, d), lambda e, i: (e, 0)),
        ],
        out_specs=pl.BlockSpec((1, tm, hk), lambda e, i: (e, i, 0)),
        out_shape=jax.ShapeDtypeStruct((n_e, t, hk), F32),
        compiler_params=_cparams(("arbitrary", "arbitrary")),
        name="peer_scores",
    )(hq, a, b, u_tab)


def _peer_values_body(ss_ref, a_ref, b_ref, g_ref, v_ref, x1_ref, g2_ref, lg_ref, lb_ref, o_ref,
                      w_s, hd_s, acc_s):
    tm = a_ref.shape[0]
    n_e = pl.num_programs(1)
    e = pl.program_id(1)
    nt = (((1,), (1,)), ((), ()))

    @pl.when(e == 0)
    def _gates():
        s_sel = jnp.sum(ss_ref[...], axis=0)
        act = 0.5 * s_sel * (1.0 + lax.erf(s_sel * (2.0 ** -0.5)))
        w_s[...] = g_ref[...] * act
        sub = lax.broadcasted_iota(jnp.int32, (PEER_KEYS, LANES), 0).astype(F32)

        def token(t, carry):
            a_row = a_ref[pl.ds(t, 1), :]
            b_row = b_ref[pl.ds(t, 1), :]
            w_row = w_s[pl.ds(t, 1), :]
            lhs = jnp.where(sub == a_row, w_row, 0.0).astype(BF16)
            rhs = jnp.where(sub == b_row, 1.0, 0.0).astype(BF16)
            tile = lax.dot_general(lhs, rhs, nt, preferred_element_type=F32)
            hd_s[pl.ds(pl.multiple_of(t * GATE_PITCH, 8), PEER_KEYS), :] = tile
            return carry
        lax.fori_loop(0, tm, token, 0, unroll=GATE_UNROLL)
        acc_s[...] = jnp.zeros(acc_s.shape, F32)

    groups = v_ref.shape[0] // PEER_KEYS
    j0 = e * groups
    lhs = jnp.concatenate(
        [hd_s[pl.ds(j0 + jj, tm, stride=GATE_PITCH), :].astype(BF16) for jj in range(groups)], axis=1)
    acc_s[...] += jnp.dot(lhs, v_ref[...], preferred_element_type=F32)

    @pl.when(e == n_e - 1)
    def _finish():
        o_ref[...] = _layer_norm(DEEPNORM_ALPHA * x1_ref[...] + g2_ref[0] * acc_s[...], lg_ref[...], lb_ref[...])


def _peer_values(ssel, a, b, g, v_tab, x1, g2, ln_g, ln_b, tokens_per_batch):
    t, d = x1.shape
    tm = PEER_VALUE_TM
    hk = a.shape[1]
    n_e = v_tab.shape[0] // PEER_VALUE_TE
    tiles_per_batch = tokens_per_batch // tm
    tok = lambda i, e: (i, 0)
    one = pl.Buffered(1)
    return pl.pallas_call(
        _peer_values_body,
        grid=(t // tm, n_e),
        in_specs=[
            pl.BlockSpec((ssel.shape[0], tm, hk), lambda i, e: (0, i, 0)),
            pl.BlockSpec((tm, hk), tok), pl.BlockSpec((tm, hk), tok), pl.BlockSpec((tm, hk), tok),
            pl.BlockSpec((PEER_VALUE_TE, d), lambda i, e: (e, 0)),
            pl.BlockSpec((tm, d), tok),
            pl.BlockSpec((1, 1, d), lambda i, e: (i // tiles_per_batch, 0, 0)),
            pl.BlockSpec((1, d), lambda i, e: (0, 0)),
            pl.BlockSpec((1, d), lambda i, e: (0, 0)),
        ],
        out_specs=pl.BlockSpec((tm, d), tok, pipeline_mode=one),
        out_shape=jax.ShapeDtypeStruct((t, d), F32),
        scratch_shapes=[
            pltpu.VMEM((tm, hk), F32),
            pltpu.VMEM((tm * GATE_PITCH, LANES), F32),
            pltpu.VMEM((tm, d), F32),
        ],
        compiler_params=_cparams(("parallel", "arbitrary")),
        name="peer_values",
    )(ssel, a, b, g, v_tab, x1, g2, ln_g, ln_b)


FFT_N2 = 128
FFT_PITCH = FFT_N2 + 8
HY_POS_PAD = 32
FFT_UNROLL = 4


def _dft_constants(n):
    big = 2 * n
    n1c = big // FFT_N2
    half = n1c // 2
    k1 = np.arange(n1c)[:, None]
    n1 = np.arange(half)[None, :]
    ang = 2 * np.pi * k1 * n1 / n1c
    c, s = np.cos(ang), np.sin(ang)
    f1 = np.block([[c, s], [-s, c]])
    k2 = np.arange(FFT_N2)[:, None]
    n2 = np.arange(FFT_N2)[None, :]
    ang = 2 * np.pi * k2 * n2 / FFT_N2
    c, s = np.cos(ang), np.sin(ang)
    d3 = np.block([[c, s], [-s, c]])
    d3i = np.block([[c, -s], [s, c]])
    ang = 2 * np.pi * n1.T * k1.T / n1c
    c, s = np.cos(ang), np.sin(ang)
    f3 = np.block([[c, -s], [s, c]]) / big
    ang = 2 * np.pi * (np.arange(n1c)[:, None] * np.arange(FFT_N2)[None, :]) / big
    twr = np.repeat(np.cos(ang).reshape(-1, 1), LANES, axis=1)
    twi = np.repeat(-np.sin(ang).reshape(-1, 1), LANES, axis=1)
    as32 = lambda a: np.asarray(a, np.float32)

    def split(a):
        parts, rest = [], np.asarray(a, np.float64)
        for _ in range(3):
            p = np.asarray(rest, BF16)
            parts.append(p)
            rest = rest - p.astype(np.float64)
        return np.stack(parts)
    return split(f1), split(d3), split(d3i), split(f3), as32(twr), as32(twi)


def _hdot(a, b):
    return jnp.dot(a, b, preferred_element_type=F32, precision=HIGHEST)


def _dft_dot(m_ref, x):
    x_hi = x.astype(BF16)
    x_lo = (x - x_hi.astype(F32)).astype(BF16)
    m_hi = m_ref[0]
    dot = functools.partial(jnp.dot, preferred_element_type=F32)
    return (dot(m_hi, x_hi) + dot(m_hi, x_lo)) + (dot(m_ref[1], x_hi) + dot(m_ref[2], x_hi))


def _fft_stage1(u_re, u_im, f1_ref, a_re, a_im):
    half = f1_ref.shape[2] // 2
    n1c = f1_ref.shape[1] // 2

    def column(n2):
        return jnp.concatenate([u_re[pl.ds(n2, half, stride=FFT_PITCH), :],
                                u_im[pl.ds(n2, half, stride=FFT_PITCH), :]], axis=0)

    def body(i, carry):
        a = _dft_dot(f1_ref, jnp.concatenate([column(2 * i), column(2 * i + 1)], axis=1))
        for s in range(2):
            a_re[pl.ds(2 * i + s, n1c, stride=FFT_PITCH), :] = a[:n1c, s * LANES:(s + 1) * LANES]
            a_im[pl.ds(2 * i + s, n1c, stride=FFT_PITCH), :] = a[n1c:, s * LANES:(s + 1) * LANES]
        return carry
    lax.fori_loop(0, FFT_N2 // 2, body, 0, unroll=FFT_UNROLL)


def _twiddled_pair(a_re, a_im, twr_ref, twi_ref, i):
    t0 = pl.multiple_of(2 * i * FFT_N2, 2 * FFT_N2)
    rows = [pl.multiple_of((2 * i + s) * FFT_PITCH, 8) for s in range(2)]
    side = lambda ref, r: jnp.concatenate([ref[pl.ds(r[0], FFT_N2), :], ref[pl.ds(r[1], FFT_N2), :]], axis=1)
    tws = [pl.multiple_of(t0 + s * FFT_N2, FFT_N2) for s in range(2)]
    return rows, side(a_re, rows), side(a_im, rows), side(twr_ref, tws), side(twi_ref, tws)


def _fft_conv_middle(a_re, a_im, d3_ref, d3i_ref, twr_ref, twi_ref, hre_ref, him_ref):
    n1c = hre_ref.shape[1]

    def body(i, carry):
        rows, ar, ai, twr, twi = _twiddled_pair(a_re, a_im, twr_ref, twi_ref, i)
        x = _dft_dot(d3_ref, jnp.concatenate([ar * twr - ai * twi, ar * twi + ai * twr], axis=0))
        xr, xi = x[:FFT_N2], x[FFT_N2:]
        hr = jnp.concatenate([hre_ref[0, 2 * i], hre_ref[0, 2 * i + 1]], axis=1)
        hi = jnp.concatenate([him_ref[0, 2 * i], him_ref[0, 2 * i + 1]], axis=1)
        y = _dft_dot(d3i_ref, jnp.concatenate([xr * hr - xi * hi, xr * hi + xi * hr], axis=0))
        br, bi = y[:FFT_N2], y[FFT_N2:]
        out_re = br * twr + bi * twi
        out_im = bi * twr - br * twi
        for s in range(2):
            a_re[pl.ds(rows[s], FFT_N2), :] = out_re[:, s * LANES:(s + 1) * LANES]
            a_im[pl.ds(rows[s], FFT_N2), :] = out_im[:, s * LANES:(s + 1) * LANES]
        return carry
    lax.fori_loop(0, n1c // 2, body, 0, unroll=FFT_UNROLL)


def _fft_stage_inv(a_re, a_im, f3_ref, y_re, y_im):
    half = f3_ref.shape[1] // 2
    n1c = f3_ref.shape[2] // 2

    def column(n2):
        return jnp.concatenate([a_re[pl.ds(n2, n1c, stride=FFT_PITCH), :],
                                a_im[pl.ds(n2, n1c, stride=FFT_PITCH), :]], axis=0)

    def body(i, carry):
        y = _dft_dot(f3_ref, jnp.concatenate([column(2 * i), column(2 * i + 1)], axis=1))
        for s in range(2):
            y_re[pl.ds(2 * i + s, half, stride=FFT_PITCH), :] = y[:half, s * LANES:(s + 1) * LANES]
            y_im[pl.ds(2 * i + s, half, stride=FFT_PITCH), :] = y[half:, s * LANES:(s + 1) * LANES]
        return carry
    lax.fori_loop(0, FFT_N2 // 2, body, 0, unroll=FFT_UNROLL)


def _short_conv_chunk(x_ref, bi, j, nblk, w_ref, b_ref):
    r0 = j * FFT_N2
    cur = x_ref[bi, r0:r0 + FFT_N2, :]
    row = lax.broadcasted_iota(jnp.int32, cur.shape, 0)
    if j == 0:
        prev = jnp.where(row == 0, 0.0, pltpu.roll(cur, 1, 0))
    else:
        prev = x_ref[bi, r0 - 1:r0 + FFT_N2 - 1, :]
    if j == nblk - 1:
        nxt = jnp.where(row == FFT_N2 - 1, 0.0, pltpu.roll(cur, FFT_N2 - 1, 0))
    else:
        nxt = x_ref[bi, r0 + 1:r0 + FFT_N2 + 1, :]
    return prev * w_ref[0:1, :] + cur * w_ref[1:2, :] + nxt * w_ref[2:3, :] + b_ref[...]


def _hy_conv_body(conv_a, a_ref, g_ref, wa_ref, ba_ref, wg_ref, bg_ref, skip_ref, hre_ref, him_ref,
                  f1_ref, d3_ref, d3i_ref, f3_ref, twr_ref, twi_ref, o_ref,
                  u_re, u_im, a_re, a_im, y_re, y_im):
    nblk = a_ref.shape[1] // FFT_N2
    for bi, dst in ((0, u_re), (1, u_im)):
        for j in range(nblk):
            if conv_a:
                blk = _short_conv_chunk(a_ref, bi, j, nblk, wa_ref, ba_ref)
            else:
                blk = a_ref[bi, j * FFT_N2:(j + 1) * FFT_N2, :]
            dst[j * FFT_PITCH:j * FFT_PITCH + FFT_N2, :] = blk
    _fft_stage1(u_re, u_im, f1_ref, a_re, a_im)
    _fft_conv_middle(a_re, a_im, d3_ref, d3i_ref, twr_ref, twi_ref, hre_ref, him_ref)
    _fft_stage_inv(a_re, a_im, f3_ref, y_re, y_im)
    skip = skip_ref[...]
    for bi, (ysrc, usrc) in enumerate(((y_re, u_re), (y_im, u_im))):
        for j in range(nblk):
            rows = slice(j * FFT_PITCH, j * FFT_PITCH + FFT_N2)
            gate = _short_conv_chunk(g_ref, bi, j, nblk, wg_ref, bg_ref)
            o_ref[bi, j * FFT_N2:(j + 1) * FFT_N2, :] = gate * (ysrc[rows, :] + usrc[rows, :] * skip)


def _hy_conv(a, a_col, g, g_col, conv_w, conv_b, skip, hre, him, order, consts, conv_a):
    b, n, _ = a.shape
    f1, d3, d3i, f3, twr, twi = consts
    n1c = f1.shape[1] // 2
    half = n1c // 2
    w = LANES
    tiles = HY_WIDTH // w
    one = pl.Buffered(1)
    data = lambda col: pl.BlockSpec((2, n, w), lambda ct, p: (p, 0, col + ct))
    wspec = lambda col: pl.BlockSpec((3, w), lambda ct, p: (0, col + ct))
    bspec = lambda col: pl.BlockSpec((1, w), lambda ct, p: (0, col + ct))
    hspec = pl.BlockSpec((1, n1c, FFT_N2, w), lambda ct, p: (order, 0, 0, ct), pipeline_mode=one)
    cs = lambda arr: pl.BlockSpec(arr.shape, lambda ct, p: (0,) * arr.ndim, pipeline_mode=one)
    a_wcol = a_col if conv_a else g_col
    return pl.pallas_call(
        functools.partial(_hy_conv_body, conv_a),
        grid=(tiles, b // 2),
        in_specs=[data(a_col), data(g_col), wspec(a_wcol), bspec(a_wcol), wspec(g_col), bspec(g_col),
                  pl.BlockSpec((1, w), lambda ct, p: (0, ct)), hspec, hspec,
                  cs(f1), cs(d3), cs(d3i), cs(f3), cs(twr), cs(twi)],
        out_specs=pl.BlockSpec((2, n, w), lambda ct, p: (p, 0, ct), pipeline_mode=one),
        out_shape=jax.ShapeDtypeStruct((b, n, HY_WIDTH), F32),
        scratch_shapes=[pltpu.VMEM((half * FFT_PITCH, w), F32), pltpu.VMEM((half * FFT_PITCH, w), F32),
                        pltpu.VMEM((n1c * FFT_PITCH, w), F32), pltpu.VMEM((n1c * FFT_PITCH, w), F32),
                        pltpu.VMEM((half * FFT_PITCH, w), F32), pltpu.VMEM((half * FFT_PITCH, w), F32)],
        compiler_params=_cparams(("parallel", "arbitrary")),
        name=f"hyena_conv{order + 1}",
    )(a, g, conv_w, conv_b, conv_w, conv_b, skip, hre, him, f1, d3, d3i, f3, twr, twi)


def _hy_filter_body(z_ref, w1_ref, b1_ref, f1_ref, w2_ref, b2_ref, f2_ref, w3_ref, b3_ref, dl_ref, o_ref):
    z = z_ref[...]
    h = jnp.sin(f1_ref[...] * (_hdot(z, w1_ref[...]) + b1_ref[...]))
    h = jnp.sin(f2_ref[...] * (_hdot(h, w2_ref[...]) + b2_ref[...]))
    h = _hdot(h, w3_ref[...]) + b3_ref[...]
    t = z[:, 0:1]
    o_ref[...] = h * (jnp.exp(-t * dl_ref[...]) + HY_WINDOW_SHIFT)


def _hy_filters(n, w1, b1, fr1, w2, b2, fr2, w3, b3):
    t = jnp.linspace(0.0, 1.0, n, dtype=F32)[:, None]
    wv = 2.0 * math.pi * jnp.arange(n, dtype=F32)[:, None] / n
    bands = jnp.linspace(1e-4, HY_BANDS - 1, HY_BANDS, dtype=F32)[None, :]
    z = jnp.concatenate([t, jnp.cos(bands * wv), -jnp.sin(bands * wv)], axis=-1)
    pos = z.shape[1]
    z = jnp.pad(z, ((0, 0), (0, HY_POS_PAD - pos)))
    w1p = jnp.pad(w1, ((0, HY_POS_PAD - pos), (0, 0)))
    min_decay = math.log(HY_DECAY_TARGET) / HY_SLOW_DECAY_PCT
    max_decay = math.log(HY_DECAY_TARGET) / HY_FAST_DECAY_PCT
    deltas = jnp.abs(jnp.linspace(min_decay, max_decay, HY_WIDTH, dtype=F32))
    n_out = w3.shape[1]
    dl = jnp.tile(deltas, n_out // HY_WIDTH)[None, :]
    tn = 512
    hid = w2.shape[0]
    full = lambda shape: pl.BlockSpec(shape, lambda i: (0, 0))
    return pl.pallas_call(
        _hy_filter_body,
        grid=(n // tn,),
        in_specs=[pl.BlockSpec((tn, HY_POS_PAD), lambda i: (i, 0)), full((HY_POS_PAD, hid)), full((1, hid)),
                  full((1, hid)), full((hid, hid)), full((1, hid)), full((1, hid)), full((hid, n_out)),
                  full((1, n_out)), full((1, n_out))],
        out_specs=pl.BlockSpec((tn, n_out), lambda i: (i, 0)),
        out_shape=jax.ShapeDtypeStruct((n, n_out), F32),
        compiler_params=_cparams(("parallel",)),
        name="hyena_filters",
    )(z, w1p, b1[None], fr1[None], w2, b2[None], fr2[None], w3, b3[None], dl)


def _hy_spectrum_body(f_ref, b_ref, f1_ref, d3_ref, twr_ref, twi_ref, hre_ref, him_ref, u_re, u_im, a_re, a_im):
    nblk = f_ref.shape[0] // FFT_N2
    n1c = hre_ref.shape[1]
    for part, out_ref in enumerate((hre_ref, him_ref)):
        for j in range(nblk):
            f = f_ref[j * FFT_N2:(j + 1) * FFT_N2, :]
            bw = b_ref[j * FFT_N2:(j + 1) * FFT_N2, :]
            if j == 0:
                row = lax.broadcasted_iota(jnp.int32, bw.shape, 0)
                bw = jnp.where(row == 0, 0.0, bw)
            rows = slice(j * FFT_PITCH, j * FFT_PITCH + FFT_N2)
            u_re[rows, :] = f + bw if part == 0 else f - bw
            u_im[rows, :] = jnp.zeros((FFT_N2, LANES), F32)
        _fft_stage1(u_re, u_im, f1_ref, a_re, a_im)

        def body(i, carry, part=part, out_ref=out_ref):
            _, ar, ai, twr, twi = _twiddled_pair(a_re, a_im, twr_ref, twi_ref, i)
            x = _dft_dot(d3_ref, jnp.concatenate([ar * twr - ai * twi, ar * twi + ai * twr], axis=0))
            x = x[:FFT_N2] if part == 0 else x[FFT_N2:]
            for s in range(2):
                out_ref[0, 2 * i + s] = x[:, s * LANES:(s + 1) * LANES]
            return carry
        lax.fori_loop(0, n1c // 2, body, 0, unroll=FFT_UNROLL)


def _hy_spectrum(h, consts):
    n = h.shape[0]
    f1, d3, _, _, twr, twi = consts
    n1c = f1.shape[1] // 2
    half = n1c // 2
    tiles = HY_WIDTH // LANES
    cs = lambda arr: pl.BlockSpec(arr.shape, lambda o, ct: (0,) * arr.ndim)
    out = jax.ShapeDtypeStruct((HY_ORDER, n1c, FFT_N2, HY_WIDTH), F32)
    ospec = pl.BlockSpec((1, n1c, FFT_N2, LANES), lambda o, ct: (o, 0, 0, ct))
    return pl.pallas_call(
        _hy_spectrum_body,
        grid=(HY_ORDER, tiles),
        in_specs=[pl.BlockSpec((n, LANES), lambda o, ct: (0, o * tiles + ct)),
                  pl.BlockSpec((n, LANES), lambda o, ct: (0, (HY_ORDER + o) * tiles + ct)),
                  cs(f1), cs(d3), cs(twr), cs(twi)],
        out_specs=[ospec, ospec],
        out_shape=[out, out],
        scratch_shapes=[pltpu.VMEM((half * FFT_PITCH, LANES), F32), pltpu.VMEM((half * FFT_PITCH, LANES), F32),
                        pltpu.VMEM((n1c * FFT_PITCH, LANES), F32), pltpu.VMEM((n1c * FFT_PITCH, LANES), F32)],
        compiler_params=_cparams(("parallel", "parallel")),
        name="hyena_spectrum",
    )(h, h, f1, d3, twr, twi)


def _hyena(p_lat, conv_w, conv_b, w1, b1, fr1, w2, b2, fr2, w3, b3, skip):
    n = p_lat.shape[1]
    consts = _dft_constants(n)
    h = _hy_filters(n, w1, b1, fr1, w2, b2, fr2, w3, b3)
    hre, him = _hy_spectrum(h, consts)
    tiles = HY_WIDTH // LANES
    col = COL_HY // LANES
    cw = jnp.pad(conv_w, ((0, 0), (COL_HY, 0)))
    cb = jnp.pad(conv_b[None], ((0, 0), (COL_HY, 0)))
    y = _hy_conv(p_lat, col, p_lat, col + tiles, cw, cb, skip[0][None], hre, him, 0, consts, True)
    return _hy_conv(y, 0, p_lat, col + 2 * tiles, cw, cb, skip[1][None], hre, him, 1, consts, False)


def _split_bf16(w):
    hi = w.astype(BF16)
    return jnp.stack([hi, (w - hi.astype(F32)).astype(BF16)])


def _pad_heads(w, heads):
    d = w.shape[0]
    w = w.reshape(d, heads, HEAD_DIM)
    w = jnp.pad(w, ((0, 0), (0, 0), (0, LANES - HEAD_DIM)))
    return w.reshape(d, heads * LANES)


def kernel(x, c, ctx, c_ctx, w_mod, b_mod, w_in, hy_conv_w, hy_conv_b, hy_f_w1, hy_f_b1, hy_f_freq1, hy_f_w2,
           hy_f_b2, hy_f_freq2, hy_f_w3, hy_f_b3, hy_skip, attn_sink, w_out, ln1_g, ln1_b, peer_wq, peer_keys1,
           peer_keys2, peer_u, peer_v, ln2_g, ln2_b):
    b, n, d = x.shape
    l = 0
    cc = jnp.concatenate([c, c_ctx[None], jnp.zeros((8 - b - 1, d), F32)], axis=0)
    mod = _modulation(cc, w_mod[l], b_mod[l][None])
    mod_lat = mod[:b].reshape(b, 6, 1, d)
    sh1, sc1, g1, sh2, sc2, g2 = (mod_lat[:, i] for i in range(6))
    mod_c = mod[b].reshape(6, 1, 1, d)
    csh1, csc1 = mod_c[0], mod_c[1]

    w = w_in[l]
    w_q = _pad_heads(w[:, PROJ_HY:KV_START], ATT_HEADS)
    w_k = _pad_heads(w[:, KV_START:KV_START + PROJ_KV], ATT_KV_HEADS)
    w_v = _pad_heads(w[:, KV_START + PROJ_KV:], ATT_KV_HEADS)
    w_pad = jnp.concatenate([w_q, w[:, :PROJ_HY], w_k, w_v], axis=1).astype(BF16)
    w_kv = jnp.concatenate([w_k, w_v], axis=1).astype(BF16)

    p_lat = _mod_matmul(x, sc1, sh1, w_pad, 512, "in_proj")
    kv_ctx = _mod_matmul(ctx, csc1, csh1, w_kv, ctx.shape[1], "ctx_kv_proj")

    cos_tab, sin_tab = _rope_tables(n)
    att = _attention(p_lat, kv_ctx, attn_sink[l], cos_tab, sin_tab)

    hy = _hyena(p_lat, hy_conv_w[l], hy_conv_b[l], hy_f_w1[l], hy_f_b1[l], hy_f_freq1[l], hy_f_w2[l], hy_f_b2[l],
                hy_f_freq2[l], hy_f_w3[l], hy_f_b3[l], hy_skip[l])

    wo = w_out[l]
    w_o_hy = wo[:HY_WIDTH].astype(BF16)
    w_o_att = jnp.pad(wo[HY_WIDTH:].reshape(ATT_HEADS, HEAD_DIM, d),
                      ((0, 0), (0, LANES - HEAD_DIM), (0, 0))).reshape(QPAD, d).astype(BF16)
    x1, hq = _outproj_ln(hy, att, x, w_o_hy, w_o_att, g1, sc2, sh2, ln1_g[l][None], ln1_b[l][None])

    hq2 = hq.reshape(b * n, d)
    a_idx, b_idx, gate = _peer_topk(hq2, peer_wq[l].astype(BF16), _split_bf16(peer_keys1[l]),
                                    _split_bf16(peer_keys2[l]))
    s_sel = _peer_scores(hq2, a_idx, b_idx, peer_u[l].astype(BF16))
    out = _peer_values(s_sel, a_idx, b_idx, gate, peer_v[l].astype(BF16), x1.reshape(b * n, d), g2,
                       ln2_g[l][None], ln2_b[l][None], n)
    return out.reshape(b, n, d)
```

```python
import functools
import math

import jax
import jax.numpy as jnp
import numpy as np
from jax import lax
from jax.experimental import pallas as pl
from jax.experimental.pallas import tpu as pltpu

F32 = jnp.float32
BF16 = jnp.bfloat16
HIGHEST = lax.Precision.HIGHEST

LANES = 128
VMEM_LIMIT = 60000 * 1024

D_MODEL = 1024
GRID_W = 64
HY_WIDTH = 512
HY_ORDER = 2
HY_BANDS = 8
HY_DECAY_TARGET = 1e-2
HY_FAST_DECAY_PCT = 0.3
HY_SLOW_DECAY_PCT = 1.5
HY_WINDOW_SHIFT = 0.05
ATT_HEADS = 8
ATT_KV_HEADS = 2
ATT_REP = ATT_HEADS // ATT_KV_HEADS
HEAD_DIM = 64
WINDOW = 128
BLOCK = 128
ROPE_BASE = 10000.0
ROPE_FREQS = HEAD_DIM // 4
PROJ_HY = (HY_ORDER + 1) * HY_WIDTH
PROJ_Q = ATT_HEADS * HEAD_DIM
PROJ_KV = ATT_KV_HEADS * HEAD_DIM
KV_START = PROJ_HY + PROJ_Q
PEER_KEYS = 128
PEER_HEADS = 8
PEER_QDIM = 256
PEER_TOPK = 16
LN_EPS = 1e-5
NEG_INF = -1e30
DEPTH = 1
DEEPNORM_ALPHA = (2.0 * DEPTH) ** 0.25

QPAD = ATT_HEADS * LANES
KVPAD = ATT_KV_HEADS * LANES
COL_Q = 0
COL_HY = QPAD
COL_K = QPAD + PROJ_HY
COL_V = COL_K + KVPAD
PROJ_PAD = COL_V + KVPAD


def _cparams(sem):
    return pltpu.CompilerParams(dimension_semantics=sem, vmem_limit_bytes=VMEM_LIMIT)


def _mod_body(c_ref, w_ref, b_ref, o_ref):
    c = c_ref[...]
    a = c * jax.nn.sigmoid(c)
    o_ref[...] = jnp.dot(a, w_ref[...], preferred_element_type=F32, precision=HIGHEST) + b_ref[...]


def _modulation(cc, w_mod, b_mod):
    rows, d = cc.shape
    n_out = w_mod.shape[1]
    tn = 1536
    return pl.pallas_call(
        _mod_body,
        grid=(n_out // tn,),
        in_specs=[
            pl.BlockSpec((rows, d), lambda j: (0, 0)),
            pl.BlockSpec((d, tn), lambda j: (0, j)),
            pl.BlockSpec((1, tn), lambda j: (0, j)),
        ],
        out_specs=pl.BlockSpec((rows, tn), lambda j: (0, j)),
        out_shape=jax.ShapeDtypeStruct((rows, n_out), F32),
        compiler_params=_cparams(("arbitrary",)),
        name="modulation",
    )(cc, w_mod, b_mod)


def _mod_matmul_body(x_ref, sc_ref, sh_ref, w_ref, o_ref):
    h = x_ref[0] * (1.0 + sc_ref[0]) + sh_ref[0]
    o_ref[0] = jnp.dot(h.astype(BF16), w_ref[...], preferred_element_type=F32)


def _mod_matmul(x, sc, sh, w, tm, name):
    b, n, d = x.shape
    n_out = w.shape[1]
    per_batch = sc.shape[0] == b
    mod_map = (lambda bi, i: (bi, 0, 0)) if per_batch else (lambda bi, i: (0, 0, 0))
    return pl.pallas_call(
        _mod_matmul_body,
        grid=(b, n // tm),
        in_specs=[
            pl.BlockSpec((1, tm, d), lambda bi, i: (bi, i, 0)),
            pl.BlockSpec((1, 1, d), mod_map),
            pl.BlockSpec((1, 1, d), mod_map),
            pl.BlockSpec((d, n_out), lambda bi, i: (0, 0)),
        ],
        out_specs=pl.BlockSpec((1, tm, n_out), lambda bi, i: (bi, i, 0)),
        out_shape=jax.ShapeDtypeStruct((b, n, n_out), F32),
        compiler_params=_cparams(("parallel", "parallel")),
        name=name,
    )(x, sc, sh, w)


def _rope_head(x, cos, sin_signed):
    lane = lax.broadcasted_iota(jnp.int32, x.shape, 1)
    first_half = (lane % 32) < 16
    partner = jnp.where(first_half, pltpu.roll(x, LANES - 16, 1), pltpu.roll(x, 16, 1))
    return x * cos + partner * sin_signed


ATT_TQ = 512
ROPE_CHUNK = 512


def _attn_body(sink_ref, q_ref, k_ref, v_ref, kvc_ref, cosq_ref, sinq_ref, cosk_ref, sink_tab_ref,
               o_ref, ks_ref, vs_ref, kcs_ref, vcs_ref):
    n = k_ref.shape[1]
    iq = pl.program_id(1)
    scale = HEAD_DIM ** -0.5

    @pl.when(iq == 0)
    def _prepare_keys():
        def chunk(ci, carry):
            r0 = pl.multiple_of(ci * ROPE_CHUNK, ROPE_CHUNK)
            cos = cosk_ref[pl.ds(r0, ROPE_CHUNK), :]
            sin = sink_tab_ref[pl.ds(r0, ROPE_CHUNK), :]
            for g in range(ATT_KV_HEADS):
                kg = k_ref[0, pl.ds(r0, ROPE_CHUNK), g * LANES:(g + 1) * LANES]
                ks_ref[pl.ds(r0, ROPE_CHUNK), g * LANES:(g + 1) * LANES] = _rope_head(kg, cos, sin).astype(BF16)
            vs_ref[pl.ds(r0, ROPE_CHUNK), :] = v_ref[0, pl.ds(r0, ROPE_CHUNK), :].astype(BF16)
            return carry
        lax.fori_loop(0, n // ROPE_CHUNK, chunk, 0)
        kcs_ref[...] = kvc_ref[0, :, 0:KVPAD].astype(BF16)
        vcs_ref[...] = kvc_ref[0, :, KVPAD:2 * KVPAD].astype(BF16)

    n_loc = 3 * BLOCK
    rows = ATT_REP * BLOCK
    row_i = lax.broadcasted_iota(jnp.int32, (rows, n_loc), 0)
    col_i = lax.broadcasted_iota(jnp.int32, (rows, n_loc), 1)
    rel = col_i - (row_i % BLOCK)
    head_of_row = lax.broadcasted_iota(jnp.int32, (rows, 1), 0) // BLOCK

    for j in range(ATT_TQ // BLOCK):
        blk = iq * (ATT_TQ // BLOCK) + j
        start = pl.multiple_of(jnp.clip((blk - 1) * BLOCK, 0, n - n_loc), BLOCK)
        qrows = slice(j * BLOCK, (j + 1) * BLOCK)
        cosq = cosq_ref[qrows, :]
        sinq = sinq_ref[qrows, :]
        delta = rel + (start - blk * BLOCK)
        in_window = jnp.abs(delta) <= WINDOW
        for g in range(ATT_KV_HEADS):
            heads = [ATT_REP * g + r for r in range(ATT_REP)]
            qg = jnp.concatenate(
                [(_rope_head(q_ref[0, qrows, h * LANES:(h + 1) * LANES], cosq, sinq) * scale).astype(BF16)
                 for h in heads], axis=0)
            kg = ks_ref[pl.ds(start, n_loc), g * LANES:(g + 1) * LANES]
            vg = vs_ref[pl.ds(start, n_loc), g * LANES:(g + 1) * LANES]
            kcg = kcs_ref[:, g * LANES:(g + 1) * LANES]
            vcg = vcs_ref[:, g * LANES:(g + 1) * LANES]
            nt = (((1,), (1,)), ((), ()))
            s_loc = lax.dot_general(qg, kg, nt, preferred_element_type=F32)
            s_ctx = lax.dot_general(qg, kcg, nt, preferred_element_type=F32)
            s_loc = jnp.where(in_window, s_loc, NEG_INF)
            sink_col = jnp.zeros((rows, 1), F32)
            for r, h in enumerate(heads):
                sink_col = jnp.where(head_of_row == r, sink_ref[h], sink_col)
            m = jnp.maximum(jnp.maximum(jnp.max(s_loc, axis=1, keepdims=True),
                                        jnp.max(s_ctx, axis=1, keepdims=True)), sink_col)
            p_loc = jnp.exp(s_loc - m)
            p_ctx = jnp.exp(s_ctx - m)
            den = (jnp.sum(p_loc, axis=1, keepdims=True) + jnp.sum(p_ctx, axis=1, keepdims=True)
                   + jnp.exp(sink_col - m))
            o = (jnp.dot(p_loc.astype(BF16), vg, preferred_element_type=F32)
                 + jnp.dot(p_ctx.astype(BF16), vcg, preferred_element_type=F32)) / den
            for r, h in enumerate(heads):
                o_ref[0, qrows, h * LANES:(h + 1) * LANES] = o[r * BLOCK:(r + 1) * BLOCK].astype(BF16)


def _attention(p_lat, kv_ctx, sink, cos_tab, sin_tab):
    b, n, _ = p_lat.shape
    n_ctx = kv_ctx.shape[1]
    grid_spec = pltpu.PrefetchScalarGridSpec(
        num_scalar_prefetch=1,
        grid=(b, n // ATT_TQ),
        in_specs=[
            pl.BlockSpec((1, ATT_TQ, QPAD), lambda bi, i, s: (bi, i, COL_Q // QPAD)),
            pl.BlockSpec((1, n, KVPAD), lambda bi, i, s: (bi, 0, COL_K // KVPAD)),
            pl.BlockSpec((1, n, KVPAD), lambda bi, i, s: (bi, 0, COL_V // KVPAD)),
            pl.BlockSpec((1, n_ctx, 2 * KVPAD), lambda bi, i, s: (bi, 0, 0)),
            pl.BlockSpec((ATT_TQ, LANES), lambda bi, i, s: (i, 0)),
            pl.BlockSpec((ATT_TQ, LANES), lambda bi, i, s: (i, 0)),
            pl.BlockSpec((n, LANES), lambda bi, i, s: (0, 0)),
            pl.BlockSpec((n, LANES), lambda bi, i, s: (0, 0)),
        ],
        out_specs=pl.BlockSpec((1, ATT_TQ, QPAD), lambda bi, i, s: (bi, i, 0)),
        scratch_shapes=[
            pltpu.VMEM((n, KVPAD), BF16),
            pltpu.VMEM((n, KVPAD), BF16),
            pltpu.VMEM((n_ctx, KVPAD), BF16),
            pltpu.VMEM((n_ctx, KVPAD), BF16),
        ],
    )
    return pl.pallas_call(
        _attn_body,
        grid_spec=grid_spec,
        out_shape=jax.ShapeDtypeStruct((b, n, QPAD), BF16),
        compiler_params=_cparams(("parallel", "arbitrary")),
        name="window_attention",
    )(sink, p_lat, p_lat, p_lat, kv_ctx, cos_tab, sin_tab, cos_tab, sin_tab)


def _rope_tables(n):
    rows = n // GRID_W
    row = jnp.repeat(jnp.arange(rows, dtype=F32), GRID_W)
    col = jnp.tile(jnp.arange(GRID_W, dtype=F32), rows)
    inv = ROPE_BASE ** (-jnp.arange(ROPE_FREQS, dtype=F32) / ROPE_FREQS)
    ang_r = row[:, None] * inv
    ang_c = col[:, None] * inv
    pad1 = jnp.ones((n, LANES - HEAD_DIM), F32)
    pad0 = jnp.zeros((n, LANES - HEAD_DIM), F32)
    cos = jnp.concatenate([jnp.cos(ang_r), jnp.cos(ang_r), jnp.cos(ang_c), jnp.cos(ang_c), pad1], axis=1)
    sin = jnp.concatenate([-jnp.sin(ang_r), jnp.sin(ang_r), -jnp.sin(ang_c), jnp.sin(ang_c), pad0], axis=1)
    return cos, sin


def _layer_norm(r, g, b):
    mu = jnp.mean(r, axis=-1, keepdims=True)
    var = jnp.mean(jnp.square(r - mu), axis=-1, keepdims=True)
    return (r - mu) * lax.rsqrt(var + LN_EPS) * g + b


def _outproj_body(hy_ref, att_ref, x_ref, wh_ref, wa_ref, g1_ref, sc2_ref, sh2_ref, lg_ref, lb_ref,
                  x1_ref, hq_ref):
    y = (jnp.dot(hy_ref[0].astype(BF16), wh_ref[...], preferred_element_type=F32)
         + jnp.dot(att_ref[0], wa_ref[...], preferred_element_type=F32))
    x1 = _layer_norm(DEEPNORM_ALPHA * x_ref[0] + g1_ref[0] * y, lg_ref[...], lb_ref[...])
    x1_ref[0] = x1
    hq_ref[0] = (x1 * (1.0 + sc2_ref[0]) + sh2_ref[0]).astype(BF16)


def _outproj_ln(hy, att, x, w_hy, w_att, g1, sc2, sh2, ln_g, ln_b, tm=512):
    b, n, d = x.shape
    modspec = pl.BlockSpec((1, 1, d), lambda bi, i: (bi, 0, 0))
    vecspec = pl.BlockSpec((1, d), lambda bi, i: (0, 0))
    return pl.pallas_call(
        _outproj_body,
        grid=(b, n // tm),
        in_specs=[
            pl.BlockSpec((1, tm, HY_WIDTH), lambda bi, i: (bi, i, 0)),
            pl.BlockSpec((1, tm, QPAD), lambda bi, i: (bi, i, 0)),
            pl.BlockSpec((1, tm, d), lambda bi, i: (bi, i, 0)),
            pl.BlockSpec(w_hy.shape, lambda bi, i: (0, 0)),
            pl.BlockSpec(w_att.shape, lambda bi, i: (0, 0)),
            modspec, modspec, modspec, vecspec, vecspec,
        ],
        out_specs=[
            pl.BlockSpec((1, tm, d), lambda bi, i: (bi, i, 0)),
            pl.BlockSpec((1, tm, d), lambda bi, i: (bi, i, 0)),
        ],
        out_shape=[jax.ShapeDtypeStruct((b, n, d), F32), jax.ShapeDtypeStruct((b, n, d), BF16)],
        compiler_params=_cparams(("parallel", "parallel")),
        name="outproj_ln1",
    )(hy, att, x, w_hy, w_att, g1, sc2, sh2, ln_g, ln_b)


PEER_TM = 1024
TOPK_UNROLL = 4
_STAIR = sorted(((i, j) for i in range(PEER_TOPK) for j in range(PEER_TOPK) if (i + 1) * (j + 1) <= PEER_TOPK),
                key=lambda p: p[0] * PEER_TOPK + p[1])
_STAIR_ROWS = -(-len(_STAIR) // 8) * 8
_STAIR_COUNT = [PEER_TOPK // (i + 1) for i in range(PEER_TOPK)]
_STAIR_START = [sum(_STAIR_COUNT[:i]) for i in range(PEER_TOPK)]


def _stair_ids(tm):
    ids = np.full((_STAIR_ROWS,), float(PEER_TOPK * PEER_TOPK), np.float32)
    ids[:len(_STAIR)] = [i * PEER_TOPK + j for i, j in _STAIR]
    return np.repeat(ids[:, None], tm, axis=1)


def _select_topk(problems, write_row):
    tm = problems[0][0].shape[1]

    def step(k, prev):
        new = []
        for p, ((s_ref, ids, pad_id), prev_id) in enumerate(zip(problems, prev)):
            s = jnp.where(ids == prev_id, -jnp.inf, s_ref[...])
            s_ref[...] = s
            m = jnp.max(s, axis=0, keepdims=True)
            win = jnp.min(jnp.where(s == m, ids, pad_id), axis=0, keepdims=True)
            write_row(p, k, m, win)
            new.append(win)
        return tuple(new)

    lax.fori_loop(0, PEER_TOPK, step, tuple(jnp.full((1, tm), -1.0, F32) for _ in problems), unroll=TOPK_UNROLL)


def _rows_by_rank(rank, table_ref):
    out = jnp.zeros(rank.shape, F32)
    for p in range(PEER_TOPK):
        out = jnp.where(rank == float(p), table_ref[p:p + 1, :], out)
    return out


def _peer_topk_body(hq_ref, wq_ref, k1_ref, k2_ref, sid_ref, a_ref, b_ref, g_ref,
                    q_s, s1_s, s2_s, v1_s, i1_s, v2_s, i2_s, c_s, t_s, f_s, ao_s, bo_s, go_s):
    tm = hq_ref.shape[0]
    q = jnp.dot(hq_ref[...], wq_ref[...], preferred_element_type=F32)
    for c in range(2 * PEER_HEADS):
        q_s[c] = q[:, c * LANES:(c + 1) * LANES]
    nt = (((1,), (1,)), ((), ()))
    key_id = lax.broadcasted_iota(jnp.int32, (PEER_KEYS, tm), 0).astype(F32)

    def head(h, carry):
        for half, (kref, s_s) in enumerate(((k1_ref, s1_s), (k2_ref, s2_s))):
            qh = q_s[2 * h + half]
            q_hi = qh.astype(BF16)
            q_lo = (qh - q_hi.astype(F32)).astype(BF16)
            ntdot = functools.partial(lax.dot_general, dimension_numbers=nt, preferred_element_type=F32)
            s_s[...] = ntdot(kref[0], q_hi) + (ntdot(kref[0], q_lo) + ntdot(kref[1], q_hi))

        def write1(p, k, val, idx):
            vs, is_ = ((v1_s, i1_s), (v2_s, i2_s))[p]
            vs[pl.ds(k, 1), :] = val
            is_[pl.ds(k, 1), :] = idx
        _select_topk([(s1_s, key_id, float(PEER_KEYS)), (s2_s, key_id, float(PEER_KEYS))], write1)

        c_s[...] = jnp.full(c_s.shape, -jnp.inf, F32)
        for i in range(PEER_TOPK):
            r0, cnt = _STAIR_START[i], _STAIR_COUNT[i]
            c_s[r0:r0 + cnt, :] = v1_s[i:i + 1, :] + v2_s[0:cnt, :]

        def write2(p, k, val, idx):
            t_s[pl.ds(k, 1), :] = val
            f_s[pl.ds(k, 1), :] = idx
        _select_topk([(c_s, sid_ref[...], float(PEER_TOPK * PEER_TOPK))], write2)

        flat = f_s[...]
        rank1 = jnp.floor(flat * (1.0 / PEER_TOPK))
        rank2 = flat - rank1 * PEER_TOPK
        rows = pl.ds(pl.multiple_of(h * PEER_TOPK, PEER_TOPK), PEER_TOPK)
        ao_s[rows, :] = _rows_by_rank(rank1, i1_s)
        bo_s[rows, :] = _rows_by_rank(rank2, i2_s)
        t = t_s[...]
        e = jnp.exp(t - jnp.max(t, axis=0, keepdims=True))
        go_s[rows, :] = e / jnp.sum(e, axis=0, keepdims=True)
        return carry

    lax.fori_loop(0, PEER_HEADS, head, 0)
    a_ref[...] = ao_s[...].T
    b_ref[...] = bo_s[...].T
    g_ref[...] = go_s[...].T


def _peer_topk(hq, wq, keys1, keys2):
    t, d = hq.shape
    tm = PEER_TM
    hk = PEER_HEADS * PEER_TOPK
    out = jax.ShapeDtypeStruct((t, hk), F32)
    ospec = pl.BlockSpec((tm, hk), lambda i: (i, 0))
    sid = _stair_ids(tm)
    return pl.pallas_call(
        _peer_topk_body,
        grid=(t // tm,),
        in_specs=[
            pl.BlockSpec((tm, d), lambda i: (i, 0)),
            pl.BlockSpec(wq.shape, lambda i: (0, 0)),
            pl.BlockSpec(keys1.shape, lambda i: (0, 0, 0)),
            pl.BlockSpec(keys2.shape, lambda i: (0, 0, 0)),
            pl.BlockSpec(sid.shape, lambda i: (0, 0)),
        ],
        out_specs=[ospec, ospec, ospec],
        out_shape=[out, out, out],
        scratch_shapes=[
            pltpu.VMEM((2 * PEER_HEADS, tm, LANES), F32),
            pltpu.VMEM((PEER_KEYS, tm), F32), pltpu.VMEM((PEER_KEYS, tm), F32),
            pltpu.VMEM((PEER_TOPK, tm), F32), pltpu.VMEM((PEER_TOPK, tm), F32),
            pltpu.VMEM((PEER_TOPK, tm), F32), pltpu.VMEM((PEER_TOPK, tm), F32),
            pltpu.VMEM((_STAIR_ROWS, tm), F32),
            pltpu.VMEM((PEER_TOPK, tm), F32), pltpu.VMEM((PEER_TOPK, tm), F32),
            pltpu.VMEM((hk, tm), F32), pltpu.VMEM((hk, tm), F32), pltpu.VMEM((hk, tm), F32),
        ],
        compiler_params=_cparams(("parallel",)),
        name="peer_topk",
    )(hq, wq, keys1, keys2, sid)


PEER_TE = 8192
PEER_GROUPS = PEER_TE // PEER_KEYS
PEER_SCORE_TM = 512
PEER_VALUE_TM = 512
PEER_VALUE_TE = 2048
GATE_PITCH = PEER_KEYS + 8
GATE_UNROLL = 64


def _peer_scores_body(hq_ref, a_ref, b_ref, u_ref, o_ref):
    e = pl.program_id(0)
    s = lax.dot_general(hq_ref[...], u_ref[...], (((1,), (1,)), ((), ())),
                        preferred_element_type=F32)
    b_idx = b_ref[...].astype(jnp.int32)
    a_val = a_ref[...]
    cur = jnp.zeros(a_val.shape, F32)
    for jj in range(PEER_GROUPS):
        cand = jnp.take_along_axis(s[:, jj * LANES:(jj + 1) * LANES], b_idx, axis=1)
        cur = jnp.where(a_val == (e * PEER_GROUPS + jj).astype(F32), cand, cur)
    o_ref[0] = cur


def _peer_scores(hq, a, b, u_tab):
    t, d = hq.shape
    tm = PEER_SCORE_TM
    hk = a.shape[1]
    n_e = u_tab.shape[0] // PEER_TE
    tok = lambda e, i: (i, 0)
    return pl.pallas_call(
        _peer_scores_body,
        grid=(n_e, t // tm),
        in_specs=[
            pl.BlockSpec((tm, d), tok),
            pl.BlockSpec((tm, hk), tok), pl.BlockSpec((tm, hk), tok),
            pl.BlockSpec((PEER_TE, d), lambda e, i: (e, 0)),
        ],
        out_specs=pl.BlockSpec((1, tm, hk), lambda e, i: (e, i, 0)),
        out_shape=jax.ShapeDtypeStruct((n_e, t, hk), F32),
        compiler_params=_cparams(("arbitrary", "arbitrary")),
        name="peer_scores",
    )(hq, a, b, u_tab)


def _peer_values_body(ss_ref, a_ref, b_ref, g_ref, v_ref, x1_ref, g2_ref, lg_ref, lb_ref, o_ref,
                      w_s, hd_s, acc_s):
    tm = a_ref.shape[0]
    n_e = pl.num_programs(1)
    e = pl.program_id(1)
    nt = (((1,), (1,)), ((), ()))

    @pl.when(e == 0)
    def _gates():
        s_sel = jnp.sum(ss_ref[...], axis=0)
        act = 0.5 * s_sel * (1.0 + lax.erf(s_sel * (2.0 ** -0.5)))
        w_s[...] = g_ref[...] * act
        sub = lax.broadcasted_iota(jnp.int32, (PEER_KEYS, LANES), 0).astype(F32)

        def token(t, carry):
            a_row = a_ref[pl.ds(t, 1), :]
            b_row = b_ref[pl.ds(t, 1), :]
            w_row = w_s[pl.ds(t, 1), :]
            lhs = jnp.where(sub == a_row, w_row, 0.0).astype(BF16)
            rhs = jnp.where(sub == b_row, 1.0, 0.0).astype(BF16)
            tile = lax.dot_general(lhs, rhs, nt, preferred_element_type=F32)
            hd_s[pl.ds(pl.multiple_of(t * GATE_PITCH, 8), PEER_KEYS), :] = tile
            return carry
        lax.fori_loop(0, tm, token, 0, unroll=GATE_UNROLL)
        acc_s[...] = jnp.zeros(acc_s.shape, F32)

    groups = v_ref.shape[0] // PEER_KEYS
    j0 = e * groups
    lhs = jnp.concatenate(
        [hd_s[pl.ds(j0 + jj, tm, stride=GATE_PITCH), :].astype(BF16) for jj in range(groups)], axis=1)
    acc_s[...] += jnp.dot(lhs, v_ref[...], preferred_element_type=F32)

    @pl.when(e == n_e - 1)
    def _finish():
        o_ref[...] = _layer_norm(DEEPNORM_ALPHA * x1_ref[...] + g2_ref[0] * acc_s[...], lg_ref[...], lb_ref[...])


def _peer_values(ssel, a, b, g, v_tab, x1, g2, ln_g, ln_b, tokens_per_batch):
    t, d = x1.shape
    tm = PEER_VALUE_TM
    hk = a.shape[1]
    n_e = v_tab.shape[0] // PEER_VALUE_TE
    tiles_per_batch = tokens_per_batch // tm
    tok = lambda i, e: (i, 0)
    one = pl.Buffered(1)
    return pl.pallas_call(
        _peer_values_body,
        grid=(t // tm, n_e),
        in_specs=[
            pl.BlockSpec((ssel.shape[0], tm, hk), lambda i, e: (0, i, 0)),
            pl.BlockSpec((tm, hk), tok), pl.BlockSpec((tm, hk), tok), pl.BlockSpec((tm, hk), tok),
            pl.BlockSpec((PEER_VALUE_TE, d), lambda i, e: (e, 0)),
            pl.BlockSpec((tm, d), tok),
            pl.BlockSpec((1, 1, d), lambda i, e: (i // tiles_per_batch, 0, 0)),
            pl.BlockSpec((1, d), lambda i, e: (0, 0)),
            pl.BlockSpec((1, d), lambda i, e: (0, 0)),
        ],
        out_specs=pl.BlockSpec((tm, d), tok, pipeline_mode=one),
        out_shape=jax.ShapeDtypeStruct((t, d), F32),
        scratch_shapes=[
            pltpu.VMEM((tm, hk), F32),
            pltpu.VMEM((tm * GATE_PITCH, LANES), F32),
            pltpu.VMEM((tm, d), F32),
        ],
        compiler_params=_cparams(("parallel", "arbitrary")),
        name="peer_values",
    )(ssel, a, b, g, v_tab, x1, g2, ln_g, ln_b)


FFT_N2 = 128
FFT_PITCH = FFT_N2 + 8
HY_POS_PAD = 32
FFT_UNROLL = 4


def _dft_constants(n):
    big = 2 * n
    n1c = big // FFT_N2
    half = n1c // 2
    k1 = np.arange(n1c)[:, None]
    n1 = np.arange(half)[None, :]
    ang = 2 * np.pi * k1 * n1 / n1c
    c, s = np.cos(ang), np.sin(ang)
    f1 = np.block([[c, s], [-s, c]])
    k2 = np.arange(FFT_N2)[:, None]
    n2 = np.arange(FFT_N2)[None, :]
    ang = 2 * np.pi * k2 * n2 / FFT_N2
    c, s = np.cos(ang), np.sin(ang)
    d3 = np.block([[c, s], [-s, c]])
    d3i = np.block([[c, -s], [s, c]])
    ang = 2 * np.pi * n1.T * k1.T / n1c
    c, s = np.cos(ang), np.sin(ang)
    f3 = np.block([[c, -s], [s, c]]) / big
    ang = 2 * np.pi * (np.arange(n1c)[:, None] * np.arange(FFT_N2)[None, :]) / big
    twr = np.repeat(np.cos(ang).reshape(-1, 1), LANES, axis=1)
    twi = np.repeat(-np.sin(ang).reshape(-1, 1), LANES, axis=1)
    as32 = lambda a: np.asarray(a, np.float32)

    def split(a):
        parts, rest = [], np.asarray(a, np.float64)
        for _ in range(3):
            p = np.asarray(rest, BF16)
            parts.append(p)
            rest = rest - p.astype(np.float64)
        return np.stack(parts)
    return split(f1), split(d3), split(d3i), split(f3), as32(twr), as32(twi)


def _hdot(a, b):
    return jnp.dot(a, b, preferred_element_type=F32, precision=HIGHEST)


def _dft_dot(m_ref, x):
    x_hi = x.astype(BF16)
    x_lo = (x - x_hi.astype(F32)).astype(BF16)
    m_hi = m_ref[0]
    dot = functools.partial(jnp.dot, preferred_element_type=F32)
    return (dot(m_hi, x_hi) + dot(m_hi, x_lo)) + (dot(m_ref[1], x_hi) + dot(m_ref[2], x_hi))


def _fft_stage1(u_re, u_im, f1_ref, a_re, a_im):
    half = f1_ref.shape[2] // 2
    n1c = f1_ref.shape[1] // 2

    def column(n2):
        return jnp.concatenate([u_re[pl.ds(n2, half, stride=FFT_PITCH), :],
                                u_im[pl.ds(n2, half, stride=FFT_PITCH), :]], axis=0)

    def body(i, carry):
        a = _dft_dot(f1_ref, jnp.concatenate([column(2 * i), column(2 * i + 1)], axis=1))
        for s in range(2):
            a_re[pl.ds(2 * i + s, n1c, stride=FFT_PITCH), :] = a[:n1c, s * LANES:(s + 1) * LANES]
            a_im[pl.ds(2 * i + s, n1c, stride=FFT_PITCH), :] = a[n1c:, s * LANES:(s + 1) * LANES]
        return carry
    lax.fori_loop(0, FFT_N2 // 2, body, 0, unroll=FFT_UNROLL)


def _twiddled_pair(a_re, a_im, twr_ref, twi_ref, i):
    t0 = pl.multiple_of(2 * i * FFT_N2, 2 * FFT_N2)
    rows = [pl.multiple_of((2 * i + s) * FFT_PITCH, 8) for s in range(2)]
    side = lambda ref, r: jnp.concatenate([ref[pl.ds(r[0], FFT_N2), :], ref[pl.ds(r[1], FFT_N2), :]], axis=1)
    tws = [pl.multiple_of(t0 + s * FFT_N2, FFT_N2) for s in range(2)]
    return rows, side(a_re, rows), side(a_im, rows), side(twr_ref, tws), side(twi_ref, tws)


def _fft_conv_middle(a_re, a_im, d3_ref, d3i_ref, twr_ref, twi_ref, hre_ref, him_ref):
    n1c = hre_ref.shape[1]

    def body(i, carry):
        rows, ar, ai, twr, twi = _twiddled_pair(a_re, a_im, twr_ref, twi_ref, i)
        x = _dft_dot(d3_ref, jnp.concatenate([ar * twr - ai * twi, ar * twi + ai * twr], axis=0))
        xr, xi = x[:FFT_N2], x[FFT_N2:]
        hr = jnp.concatenate([hre_ref[0, 2 * i], hre_ref[0, 2 * i + 1]], axis=1)
        hi = jnp.concatenate([him_ref[0, 2 * i], him_ref[0, 2 * i + 1]], axis=1)
        y = _dft_dot(d3i_ref, jnp.concatenate([xr * hr - xi * hi, xr * hi + xi * hr], axis=0))
        br, bi = y[:FFT_N2], y[FFT_N2:]
        out_re = br * twr + bi * twi
        out_im = bi * twr - br * twi
        for s in range(2):
            a_re[pl.ds(rows[s], FFT_N2), :] = out_re[:, s * LANES:(s + 1) * LANES]
            a_im[pl.ds(rows[s], FFT_N2), :] = out_im[:, s * LANES:(s + 1) * LANES]
        return carry
    lax.fori_loop(0, n1c // 2, body, 0, unroll=FFT_UNROLL)


def _fft_stage_inv(a_re, a_im, f3_ref, y_re, y_im):
    half = f3_ref.shape[1] // 2
    n1c = f3_ref.shape[2] // 2

    def column(n2):
        return jnp.concatenate([a_re[pl.ds(n2, n1c, stride=FFT_PITCH), :],
                                a_im[pl.ds(n2, n1c, stride=FFT_PITCH), :]], axis=0)

    def body(i, carry):
        y = _dft_dot(f3_ref, jnp.concatenate([column(2 * i), column(2 * i + 1)], axis=1))
        for s in range(2):
            y_re[pl.ds(2 * i + s, half, stride=FFT_PITCH), :] = y[:half, s * LANES:(s + 1) * LANES]
            y_im[pl.ds(2 * i + s, half, stride=FFT_PITCH), :] = y[half:, s * LANES:(s + 1) * LANES]
        return carry
    lax.fori_loop(0, FFT_N2 // 2, body, 0, unroll=FFT_UNROLL)


def _short_conv_chunk(x_ref, bi, j, nblk, w_ref, b_ref):
    r0 = j * FFT_N2
    cur = x_ref[bi, r0:r0 + FFT_N2, :]
    row = lax.broadcasted_iota(jnp.int32, cur.shape, 0)
    if j == 0:
        prev = jnp.where(row == 0, 0.0, pltpu.roll(cur, 1, 0))
    else:
        prev = x_ref[bi, r0 - 1:r0 + FFT_N2 - 1, :]
    if j == nblk - 1:
        nxt = jnp.where(row == FFT_N2 - 1, 0.0, pltpu.roll(cur, FFT_N2 - 1, 0))
    else:
        nxt = x_ref[bi, r0 + 1:r0 + FFT_N2 + 1, :]
    return prev * w_ref[0:1, :] + cur * w_ref[1:2, :] + nxt * w_ref[2:3, :] + b_ref[...]


def _hy_conv_body(conv_a, a_ref, g_ref, wa_ref, ba_ref, wg_ref, bg_ref, skip_ref, hre_ref, him_ref,
                  f1_ref, d3_ref, d3i_ref, f3_ref, twr_ref, twi_ref, o_ref,
                  u_re, u_im, a_re, a_im, y_re, y_im):
    nblk = a_ref.shape[1] // FFT_N2
    for bi, dst in ((0, u_re), (1, u_im)):
        for j in range(nblk):
            if conv_a:
                blk = _short_conv_chunk(a_ref, bi, j, nblk, wa_ref, ba_ref)
            else:
                blk = a_ref[bi, j * FFT_N2:(j + 1) * FFT_N2, :]
            dst[j * FFT_PITCH:j * FFT_PITCH + FFT_N2, :] = blk
    _fft_stage1(u_re, u_im, f1_ref, a_re, a_im)
    _fft_conv_middle(a_re, a_im, d3_ref, d3i_ref, twr_ref, twi_ref, hre_ref, him_ref)
    _fft_stage_inv(a_re, a_im, f3_ref, y_re, y_im)
    skip = skip_ref[...]
    for bi, (ysrc, usrc) in enumerate(((y_re, u_re), (y_im, u_im))):
        for j in range(nblk):
            rows = slice(j * FFT_PITCH, j * FFT_PITCH + FFT_N2)
            gate = _short_conv_chunk(g_ref, bi, j, nblk, wg_ref, bg_ref)
            o_ref[bi, j * FFT_N2:(j + 1) * FFT_N2, :] = gate * (ysrc[rows, :] + usrc[rows, :] * skip)


def _hy_conv(a, a_col, g, g_col, conv_w, conv_b, skip, hre, him, order, consts, conv_a):
    b, n, _ = a.shape
    f1, d3, d3i, f3, twr, twi = consts
    n1c = f1.shape[1] // 2
    half = n1c // 2
    w = LANES
    tiles = HY_WIDTH // w
    one = pl.Buffered(1)
    data = lambda col: pl.BlockSpec((2, n, w), lambda ct, p: (p, 0, col + ct))
    wspec = lambda col: pl.BlockSpec((3, w), lambda ct, p: (0, col + ct))
    bspec = lambda col: pl.BlockSpec((1, w), lambda ct, p: (0, col + ct))
    hspec = pl.BlockSpec((1, n1c, FFT_N2, w), lambda ct, p: (order, 0, 0, ct), pipeline_mode=one)
    cs = lambda arr: pl.BlockSpec(arr.shape, lambda ct, p: (0,) * arr.ndim, pipeline_mode=one)
    a_wcol = a_col if conv_a else g_col
    return pl.pallas_call(
        functools.partial(_hy_conv_body, conv_a),
        grid=(tiles, b // 2),
        in_specs=[data(a_col), data(g_col), wspec(a_wcol), bspec(a_wcol), wspec(g_col), bspec(g_col),
                  pl.BlockSpec((1, w), lambda ct, p: (0, ct)), hspec, hspec,
                  cs(f1), cs(d3), cs(d3i), cs(f3), cs(twr), cs(twi)],
        out_specs=pl.BlockSpec((2, n, w), lambda ct, p: (p, 0, ct), pipeline_mode=one),
        out_shape=jax.ShapeDtypeStruct((b, n, HY_WIDTH), F32),
        scratch_shapes=[pltpu.VMEM((half * FFT_PITCH, w), F32), pltpu.VMEM((half * FFT_PITCH, w), F32),
                        pltpu.VMEM((n1c * FFT_PITCH, w), F32), pltpu.VMEM((n1c * FFT_PITCH, w), F32),
                        pltpu.VMEM((half * FFT_PITCH, w), F32), pltpu.VMEM((half * FFT_PITCH, w), F32)],
        compiler_params=_cparams(("parallel", "arbitrary")),
        name=f"hyena_conv{order + 1}",
    )(a, g, conv_w, conv_b, conv_w, conv_b, skip, hre, him, f1, d3, d3i, f3, twr, twi)


def _hy_filter_body(z_ref, w1_ref, b1_ref, f1_ref, w2_ref, b2_ref, f2_ref, w3_ref, b3_ref, dl_ref, o_ref):
    z = z_ref[...]
    h = jnp.sin(f1_ref[...] * (_hdot(z, w1_ref[...]) + b1_ref[...]))
    h = jnp.sin(f2_ref[...] * (_hdot(h, w2_ref[...]) + b2_ref[...]))
    h = _hdot(h, w3_ref[...]) + b3_ref[...]
    t = z[:, 0:1]
    o_ref[...] = h * (jnp.exp(-t * dl_ref[...]) + HY_WINDOW_SHIFT)


def _hy_filters(n, w1, b1, fr1, w2, b2, fr2, w3, b3):
    t = jnp.linspace(0.0, 1.0, n, dtype=F32)[:, None]
    wv = 2.0 * math.pi * jnp.arange(n, dtype=F32)[:, None] / n
    bands = jnp.linspace(1e-4, HY_BANDS - 1, HY_BANDS, dtype=F32)[None, :]
    z = jnp.concatenate([t, jnp.cos(bands * wv), -jnp.sin(bands * wv)], axis=-1)
    pos = z.shape[1]
    z = jnp.pad(z, ((0, 0), (0, HY_POS_PAD - pos)))
    w1p = jnp.pad(w1, ((0, HY_POS_PAD - pos), (0, 0)))
    min_decay = math.log(HY_DECAY_TARGET) / HY_SLOW_DECAY_PCT
    max_decay = math.log(HY_DECAY_TARGET) / HY_FAST_DECAY_PCT
    deltas = jnp.abs(jnp.linspace(min_decay, max_decay, HY_WIDTH, dtype=F32))
    n_out = w3.shape[1]
    dl = jnp.tile(deltas, n_out // HY_WIDTH)[None, :]
    tn = 512
    hid = w2.shape[0]
    full = lambda shape: pl.BlockSpec(shape, lambda i: (0, 0))
    return pl.pallas_call(
        _hy_filter_body,
        grid=(n // tn,),
        in_specs=[pl.BlockSpec((tn, HY_POS_PAD), lambda i: (i, 0)), full((HY_POS_PAD, hid)), full((1, hid)),
                  full((1, hid)), full((hid, hid)), full((1, hid)), full((1, hid)), full((hid, n_out)),
                  full((1, n_out)), full((1, n_out))],
        out_specs=pl.BlockSpec((tn, n_out), lambda i: (i, 0)),
        out_shape=jax.ShapeDtypeStruct((n, n_out), F32),
        compiler_params=_cparams(("parallel",)),
        name="hyena_filters",
    )(z, w1p, b1[None], fr1[None], w2, b2[None], fr2[None], w3, b3[None], dl)


def _hy_spectrum_body(f_ref, b_ref, f1_ref, d3_ref, twr_ref, twi_ref, hre_ref, him_ref, u_re, u_im, a_re, a_im):
    nblk = f_ref.shape[0] // FFT_N2
    n1c = hre_ref.shape[1]
    for part, out_ref in enumerate((hre_ref, him_ref)):
        for j in range(nblk):
            f = f_ref[j * FFT_N2:(j + 1) * FFT_N2, :]
            bw = b_ref[j * FFT_N2:(j + 1) * FFT_N2, :]
            if j == 0:
                row = lax.broadcasted_iota(jnp.int32, bw.shape, 0)
                bw = jnp.where(row == 0, 0.0, bw)
            rows = slice(j * FFT_PITCH, j * FFT_PITCH + FFT_N2)
            u_re[rows, :] = f + bw if part == 0 else f - bw
            u_im[rows, :] = jnp.zeros((FFT_N2, LANES), F32)
        _fft_stage1(u_re, u_im, f1_ref, a_re, a_im)

        def body(i, carry, part=part, out_ref=out_ref):
            _, ar, ai, twr, twi = _twiddled_pair(a_re, a_im, twr_ref, twi_ref, i)
            x = _dft_dot(d3_ref, jnp.concatenate([ar * twr - ai * twi, ar * twi + ai * twr], axis=0))
            x = x[:FFT_N2] if part == 0 else x[FFT_N2:]
            for s in range(2):
                out_ref[0, 2 * i + s] = x[:, s * LANES:(s + 1) * LANES]
            return carry
        lax.fori_loop(0, n1c // 2, body, 0, unroll=FFT_UNROLL)


def _hy_spectrum(h, consts):
    n = h.shape[0]
    f1, d3, _, _, twr, twi = consts
    n1c = f1.shape[1] // 2
    half = n1c // 2
    tiles = HY_WIDTH // LANES
    cs = lambda arr: pl.BlockSpec(arr.shape, lambda o, ct: (0,) * arr.ndim)
    out = jax.ShapeDtypeStruct((HY_ORDER, n1c, FFT_N2, HY_WIDTH), F32)
    ospec = pl.BlockSpec((1, n1c, FFT_N2, LANES), lambda o, ct: (o, 0, 0, ct))
    return pl.pallas_call(
        _hy_spectrum_body,
        grid=(HY_ORDER, tiles),
        in_specs=[pl.BlockSpec((n, LANES), lambda o, ct: (0, o * tiles + ct)),
                  pl.BlockSpec((n, LANES), lambda o, ct: (0, (HY_ORDER + o) * tiles + ct)),
                  cs(f1), cs(d3), cs(twr), cs(twi)],
        out_specs=[ospec, ospec],
        out_shape=[out, out],
        scratch_shapes=[pltpu.VMEM((half * FFT_PITCH, LANES), F32), pltpu.VMEM((half * FFT_PITCH, LANES), F32),
                        pltpu.VMEM((n1c * FFT_PITCH, LANES), F32), pltpu.VMEM((n1c * FFT_PITCH, LANES), F32)],
        compiler_params=_cparams(("parallel", "parallel")),
        name="hyena_spectrum",
    )(h, h, f1, d3, twr, twi)


def _hyena(p_lat, conv_w, conv_b, w1, b1, fr1, w2, b2, fr2, w3, b3, skip):
    n = p_lat.shape[1]
    consts = _dft_constants(n)
    h = _hy_filters(n, w1, b1, fr1, w2, b2, fr2, w3, b3)
    hre, him = _hy_spectrum(h, consts)
    tiles = HY_WIDTH // LANES
    col = COL_HY // LANES
    cw = jnp.pad(conv_w, ((0, 0), (COL_HY, 0)))
    cb = jnp.pad(conv_b[None], ((0, 0), (COL_HY, 0)))
    y = _hy_conv(p_lat, col, p_lat, col + tiles, cw, cb, skip[0][None], hre, him, 0, consts, True)
    return _hy_conv(y, 0, p_lat, col + 2 * tiles, cw, cb, skip[1][None], hre, him, 1, consts, False)


def _split_bf16(w):
    hi = w.astype(BF16)
    return jnp.stack([hi, (w - hi.astype(F32)).astype(BF16)])


def _pad_heads(w, heads):
    d = w.shape[0]
    w = w.reshape(d, heads, HEAD_DIM)
    w = jnp.pad(w, ((0, 0), (0, 0), (0, LANES - HEAD_DIM)))
    return w.reshape(d, heads * LANES)


def kernel(x, c, ctx, c_ctx, w_mod, b_mod, w_in, hy_conv_w, hy_conv_b, hy_f_w1, hy_f_b1, hy_f_freq1, hy_f_w2,
           hy_f_b2, hy_f_freq2, hy_f_w3, hy_f_b3, hy_skip, attn_sink, w_out, ln1_g, ln1_b, peer_wq, peer_keys1,
           peer_keys2, peer_u, peer_v, ln2_g, ln2_b):
    b, n, d = x.shape
    l = 0
    cc = jnp.concatenate([c, c_ctx[None], jnp.zeros((8 - b - 1, d), F32)], axis=0)
    mod = _modulation(cc, w_mod[l], b_mod[l][None])
    mod_lat = mod[:b].reshape(b, 6, 1, d)
    sh1, sc1, g1, sh2, sc2, g2 = (mod_lat[:, i] for i in range(6))
    mod_c = mod[b].reshape(6, 1, 1, d)
    csh1, csc1 = mod_c[0], mod_c[1]

    w = w_in[l]
    w_q = _pad_heads(w[:, PROJ_HY:KV_START], ATT_HEADS)
    w_k = _pad_heads(w[:, KV_START:KV_START + PROJ_KV], ATT_KV_HEADS)
    w_v = _pad_heads(w[:, KV_START + PROJ_KV:], ATT_KV_HEADS)
    w_pad = jnp.concatenate([w_q, w[:, :PROJ_HY], w_k, w_v], axis=1).astype(BF16)
    w_kv = jnp.concatenate([w_k, w_v], axis=1).astype(BF16)

    p_lat = _mod_matmul(x, sc1, sh1, w_pad, 512, "in_proj")
    kv_ctx = _mod_matmul(ctx, csc1, csh1, w_kv, ctx.shape[1], "ctx_kv_proj")

    cos_tab, sin_tab = _rope_tables(n)
    att = _attention(p_lat, kv_ctx, attn_sink[l], cos_tab, sin_tab)

    hy = _hyena(p_lat, hy_conv_w[l], hy_conv_b[l], hy_f_w1[l], hy_f_b1[l], hy_f_freq1[l], hy_f_w2[l], hy_f_b2[l],
                hy_f_freq2[l], hy_f_w3[l], hy_f_b3[l], hy_skip[l])

    wo = w_out[l]
    w_o_hy = wo[:HY_WIDTH].astype(BF16)
    w_o_att = jnp.pad(wo[HY_WIDTH:].reshape(ATT_HEADS, HEAD_DIM, d),
                      ((0, 0), (0, LANES - HEAD_DIM), (0, 0))).reshape(QPAD, d).astype(BF16)
    x1, hq = _outproj_ln(hy, att, x, w_o_hy, w_o_att, g1, sc2, sh2, ln1_g[l][None], ln1_b[l][None])

    hq2 = hq.reshape(b * n, d)
    a_idx, b_idx, gate = _peer_topk(hq2, peer_wq[l].astype(BF16), _split_bf16(peer_keys1[l]),
                                    _split_bf16(peer_keys2[l]))
    s_sel = _peer_scores(hq2, a_idx, b_idx, peer_u[l].astype(BF16))
    out = _peer_values(s_sel, a_idx, b_idx, gate, peer_v[l].astype(BF16), x1.reshape(b * n, d), g2,
                       ln2_g[l][None], ln2_b[l][None], n)
    return out.reshape(b, n, d)
```

```python
import functools
import math

import jax
import jax.numpy as jnp
import numpy as np
from jax import lax
from jax.experimental import pallas as pl
from jax.experimental.pallas import tpu as pltpu

F32 = jnp.float32
BF16 = jnp.bfloat16
HIGHEST = lax.Precision.HIGHEST

LANES = 128
VMEM_LIMIT = 60000 * 1024

D_MODEL = 1024
GRID_W = 64
HY_WIDTH = 512
HY_ORDER = 2
HY_BANDS = 8
HY_DECAY_TARGET = 1e-2
HY_FAST_DECAY_PCT = 0.3
HY_SLOW_DECAY_PCT = 1.5
HY_WINDOW_SHIFT = 0.05
ATT_HEADS = 8
ATT_KV_HEADS = 2
ATT_REP = ATT_HEADS // ATT_KV_HEADS
HEAD_DIM = 64
WINDOW = 128
BLOCK = 128
ROPE_BASE = 10000.0
ROPE_FREQS = HEAD_DIM // 4
PROJ_HY = (HY_ORDER + 1) * HY_WIDTH
PROJ_Q = ATT_HEADS * HEAD_DIM
PROJ_KV = ATT_KV_HEADS * HEAD_DIM
KV_START = PROJ_HY + PROJ_Q
PEER_KEYS = 128
PEER_HEADS = 8
PEER_QDIM = 256
PEER_TOPK = 16
LN_EPS = 1e-5
NEG_INF = -1e30
DEPTH = 1
DEEPNORM_ALPHA = (2.0 * DEPTH) ** 0.25

QPAD = ATT_HEADS * LANES
KVPAD = ATT_KV_HEADS * LANES
COL_Q = 0
COL_HY = QPAD
COL_K = QPAD + PROJ_HY
COL_V = COL_K + KVPAD
PROJ_PAD = COL_V + KVPAD


def _cparams(sem):
    return pltpu.CompilerParams(dimension_semantics=sem, vmem_limit_bytes=VMEM_LIMIT)


def _mod_body(c_ref, w_ref, b_ref, o_ref):
    c = c_ref[...]
    a = c * jax.nn.sigmoid(c)
    o_ref[...] = jnp.dot(a, w_ref[...], preferred_element_type=F32, precision=HIGHEST) + b_ref[...]


def _modulation(cc, w_mod, b_mod):
    rows, d = cc.shape
    n_out = w_mod.shape[1]
    tn = 1536
    return pl.pallas_call(
        _mod_body,
        grid=(n_out // tn,),
        in_specs=[
            pl.BlockSpec((rows, d), lambda j: (0, 0)),
            pl.BlockSpec((d, tn), lambda j: (0, j)),
            pl.BlockSpec((1, tn), lambda j: (0, j)),
        ],
        out_specs=pl.BlockSpec((rows, tn), lambda j: (0, j)),
        out_shape=jax.ShapeDtypeStruct((rows, n_out), F32),
        compiler_params=_cparams(("arbitrary",)),
        name="modulation",
    )(cc, w_mod, b_mod)


def _mod_matmul_body(x_ref, sc_ref, sh_ref, w_ref, o_ref):
    h = x_ref[0] * (1.0 + sc_ref[0]) + sh_ref[0]
    o_ref[0] = jnp.dot(h.astype(BF16), w_ref[...], preferred_element_type=F32)


def _mod_matmul(x, sc, sh, w, tm, name):
    b, n, d = x.shape
    n_out = w.shape[1]
    per_batch = sc.shape[0] == b
    mod_map = (lambda bi, i: (bi, 0, 0)) if per_batch else (lambda bi, i: (0, 0, 0))
    return pl.pallas_call(
        _mod_matmul_body,
        grid=(b, n // tm),
        in_specs=[
            pl.BlockSpec((1, tm, d), lambda bi, i: (bi, i, 0)),
            pl.BlockSpec((1, 1, d), mod_map),
            pl.BlockSpec((1, 1, d), mod_map),
            pl.BlockSpec((d, n_out), lambda bi, i: (0, 0)),
        ],
        out_specs=pl.BlockSpec((1, tm, n_out), lambda bi, i: (bi, i, 0)),
        out_shape=jax.ShapeDtypeStruct((b, n, n_out), F32),
        compiler_params=_cparams(("parallel", "parallel")),
        name=name,
    )(x, sc, sh, w)


def _rope_head(x, cos, sin_signed):
    lane = lax.broadcasted_iota(jnp.int32, x.shape, 1)
    first_half = (lane % 32) < 16
    partner = jnp.where(first_half, pltpu.roll(x, LANES - 16, 1), pltpu.roll(x, 16, 1))
    return x * cos + partner * sin_signed


ATT_TQ = 512
ROPE_CHUNK = 512


def _attn_body(sink_ref, q_ref, k_ref, v_ref, kvc_ref, cosq_ref, sinq_ref, cosk_ref, sink_tab_ref,
               o_ref, ks_ref, vs_ref, kcs_ref, vcs_ref):
    n = k_ref.shape[1]
    iq = pl.program_id(1)
    scale = HEAD_DIM ** -0.5

    @pl.when(iq == 0)
    def _prepare_keys():
        def chunk(ci, carry):
            r0 = pl.multiple_of(ci * ROPE_CHUNK, ROPE_CHUNK)
            cos = cosk_ref[pl.ds(r0, ROPE_CHUNK), :]
            sin = sink_tab_ref[pl.ds(r0, ROPE_CHUNK), :]
            for g in range(ATT_KV_HEADS):
                kg = k_ref[0, pl.ds(r0, ROPE_CHUNK), g * LANES:(g + 1) * LANES]
                ks_ref[pl.ds(r0, ROPE_CHUNK), g * LANES:(g + 1) * LANES] = _rope_head(kg, cos, sin).astype(BF16)
            vs_ref[pl.ds(r0, ROPE_CHUNK), :] = v_ref[0, pl.ds(r0, ROPE_CHUNK), :].astype(BF16)
            return carry
        lax.fori_loop(0, n // ROPE_CHUNK, chunk, 0)
        kcs_ref[...] = kvc_ref[0, :, 0:KVPAD].astype(BF16)
        vcs_ref[...] = kvc_ref[0, :, KVPAD:2 * KVPAD].astype(BF16)

    n_loc = 3 * BLOCK
    rows = ATT_REP * BLOCK
    row_i = lax.broadcasted_iota(jnp.int32, (rows, n_loc), 0)
    col_i = lax.broadcasted_iota(jnp.int32, (rows, n_loc), 1)
    rel = col_i - (row_i % BLOCK)
    head_of_row = lax.broadcasted_iota(jnp.int32, (rows, 1), 0) // BLOCK

    for j in range(ATT_TQ // BLOCK):
        blk = iq * (ATT_TQ // BLOCK) + j
        start = pl.multiple_of(jnp.clip((blk - 1) * BLOCK, 0, n - n_loc), BLOCK)
        qrows = slice(j * BLOCK, (j + 1) * BLOCK)
        cosq = cosq_ref[qrows, :]
        sinq = sinq_ref[qrows, :]
        delta = rel + (start - blk * BLOCK)
        in_window = jnp.abs(delta) <= WINDOW
        for g in range(ATT_KV_HEADS):
            heads = [ATT_REP * g + r for r in range(ATT_REP)]
            qg = jnp.concatenate(
                [(_rope_head(q_ref[0, qrows, h * LANES:(h + 1) * LANES], cosq, sinq) * scale).astype(BF16)
                 for h in heads], axis=0)
            kg = ks_ref[pl.ds(start, n_loc), g * LANES:(g + 1) * LANES]
            vg = vs_ref[pl.ds(start, n_loc), g * LANES:(g + 1) * LANES]
            kcg = kcs_ref[:, g * LANES:(g + 1) * LANES]
            vcg = vcs_ref[:, g * LANES:(g + 1) * LANES]
            nt = (((1,), (1,)), ((), ()))
            s_loc = lax.dot_general(qg, kg, nt, preferred_element_type=F32)
            s_ctx = lax.dot_general(qg, kcg, nt, preferred_element_type=F32)
            s_loc = jnp.where(in_window, s_loc, NEG_INF)
            sink_col = jnp.zeros((rows, 1), F32)
            for r, h in enumerate(heads):
                sink_col = jnp.where(head_of_row == r, sink_ref[h], sink_col)
            m = jnp.maximum(jnp.maximum(jnp.max(s_loc, axis=1, keepdims=True),
                                        jnp.max(s_ctx, axis=1, keepdims=True)), sink_col)
            p_loc = jnp.exp(s_loc - m)
            p_ctx = jnp.exp(s_ctx - m)
            den = (jnp.sum(p_loc, axis=1, keepdims=True) + jnp.sum(p_ctx, axis=1, keepdims=True)
                   + jnp.exp(sink_col - m))
            o = (jnp.dot(p_loc.astype(BF16), vg, preferred_element_type=F32)
                 + jnp.dot(p_ctx.astype(BF16), vcg, preferred_element_type=F32)) / den
            for r, h in enumerate(heads):
                o_ref[0, qrows, h * LANES:(h + 1) * LANES] = o[r * BLOCK:(r + 1) * BLOCK].astype(BF16)


def _attention(p_lat, kv_ctx, sink, cos_tab, sin_tab):
    b, n, _ = p_lat.shape
    n_ctx = kv_ctx.shape[1]
    grid_spec = pltpu.PrefetchScalarGridSpec(
        num_scalar_prefetch=1,
        grid=(b, n // ATT_TQ),
        in_specs=[
            pl.BlockSpec((1, ATT_TQ, QPAD), lambda bi, i, s: (bi, i, COL_Q // QPAD)),
            pl.BlockSpec((1, n, KVPAD), lambda bi, i, s: (bi, 0, COL_K // KVPAD)),
            pl.BlockSpec((1, n, KVPAD), lambda bi, i, s: (bi, 0, COL_V // KVPAD)),
            pl.BlockSpec((1, n_ctx, 2 * KVPAD), lambda bi, i, s: (bi, 0, 0)),
            pl.BlockSpec((ATT_TQ, LANES), lambda bi, i, s: (i, 0)),
            pl.BlockSpec((ATT_TQ, LANES), lambda bi, i, s: (i, 0)),
            pl.BlockSpec((n, LANES), lambda bi, i, s: (0, 0)),
            pl.BlockSpec((n, LANES), lambda bi, i, s: (0, 0)),
        ],
        out_specs=pl.BlockSpec((1, ATT_TQ, QPAD), lambda bi, i, s: (bi, i, 0)),
        scratch_shapes=[
            pltpu.VMEM((n, KVPAD), BF16),
            pltpu.VMEM((n, KVPAD), BF16),
            pltpu.VMEM((n_ctx, KVPAD), BF16),
            pltpu.VMEM((n_ctx, KVPAD), BF16),
        ],
    )
    return pl.pallas_call(
        _attn_body,
        grid_spec=grid_spec,
        out_shape=jax.ShapeDtypeStruct((b, n, QPAD), BF16),
        compiler_params=_cparams(("parallel", "arbitrary")),
        name="window_attention",
    )(sink, p_lat, p_lat, p_lat, kv_ctx, cos_tab, sin_tab, cos_tab, sin_tab)


def _rope_tables(n):
    rows = n // GRID_W
    row = jnp.repeat(jnp.arange(rows, dtype=F32), GRID_W)
    col = jnp.tile(jnp.arange(GRID_W, dtype=F32), rows)
    inv = ROPE_BASE ** (-jnp.arange(ROPE_FREQS, dtype=F32) / ROPE_FREQS)
    ang_r = row[:, None] * inv
    ang_c = col[:, None] * inv
    pad1 = jnp.ones((n, LANES - HEAD_DIM), F32)
    pad0 = jnp.zeros((n, LANES - HEAD_DIM), F32)
    cos = jnp.concatenate([jnp.cos(ang_r), jnp.cos(ang_r), jnp.cos(ang_c), jnp.cos(ang_c), pad1], axis=1)
    sin = jnp.concatenate([-jnp.sin(ang_r), jnp.sin(ang_r), -jnp.sin(ang_c), jnp.sin(ang_c), pad0], axis=1)
    return cos, sin


def _layer_norm(r, g, b):
    mu = jnp.mean(r, axis=-1, keepdims=True)
    var = jnp.mean(jnp.square(r - mu), axis=-1, keepdims=True)
    return (r - mu) * lax.rsqrt(var + LN_EPS) * g + b


def _outproj_body(hy_ref, att_ref, x_ref, wh_ref, wa_ref, g1_ref, sc2_ref, sh2_ref, lg_ref, lb_ref,
                  x1_ref, hq_ref):
    y = (jnp.dot(hy_ref[0].astype(BF16), wh_ref[...], preferred_element_type=F32)
         + jnp.dot(att_ref[0], wa_ref[...], preferred_element_type=F32))
    x1 = _layer_norm(DEEPNORM_ALPHA * x_ref[0] + g1_ref[0] * y, lg_ref[...], lb_ref[...])
    x1_ref[0] = x1
    hq_ref[0] = (x1 * (1.0 + sc2_ref[0]) + sh2_ref[0]).astype(BF16)


def _outproj_ln(hy, att, x, w_hy, w_att, g1, sc2, sh2, ln_g, ln_b, tm=512):
    b, n, d = x.shape
    modspec = pl.BlockSpec((1, 1, d), lambda bi, i: (bi, 0, 0))
    vecspec = pl.BlockSpec((1, d), lambda bi, i: (0, 0))
    return pl.pallas_call(
        _outproj_body,
        grid=(b, n // tm),
        in_specs=[
            pl.BlockSpec((1, tm, HY_WIDTH), lambda bi, i: (bi, i, 0)),
            pl.BlockSpec((1, tm, QPAD), lambda bi, i: (bi, i, 0)),
            pl.BlockSpec((1, tm, d), lambda bi, i: (bi, i, 0)),
            pl.BlockSpec(w_hy.shape, lambda bi, i: (0, 0)),
            pl.BlockSpec(w_att.shape, lambda bi, i: (0, 0)),
            modspec, modspec, modspec, vecspec, vecspec,
        ],
        out_specs=[
            pl.BlockSpec((1, tm, d), lambda bi, i: (bi, i, 0)),
            pl.BlockSpec((1, tm, d), lambda bi, i: (bi, i, 0)),
        ],
        out_shape=[jax.ShapeDtypeStruct((b, n, d), F32), jax.ShapeDtypeStruct((b, n, d), BF16)],
        compiler_params=_cparams(("parallel", "parallel")),
        name="outproj_ln1",
    )(hy, att, x, w_hy, w_att, g1, sc2, sh2, ln_g, ln_b)


PEER_TM = 1024
TOPK_UNROLL = 4
_STAIR = sorted(((i, j) for i in range(PEER_TOPK) for j in range(PEER_TOPK) if (i + 1) * (j + 1) <= PEER_TOPK),
                key=lambda p: p[0] * PEER_TOPK + p[1])
_STAIR_ROWS = -(-len(_STAIR) // 8) * 8
_STAIR_COUNT = [PEER_TOPK // (i + 1) for i in range(PEER_TOPK)]
_STAIR_START = [sum(_STAIR_COUNT[:i]) for i in range(PEER_TOPK)]


def _stair_ids(tm):
    ids = np.full((_STAIR_ROWS,), float(PEER_TOPK * PEER_TOPK), np.float32)
    ids[:len(_STAIR)] = [i * PEER_TOPK + j for i, j in _STAIR]
    return np.repeat(ids[:, None], tm, axis=1)


def _select_topk(problems, write_row):
    tm = problems[0][0].shape[1]

    def step(k, prev):
        new = []
        for p, ((s_ref, ids, pad_id), prev_id) in enumerate(zip(problems, prev)):
            s = jnp.where(ids == prev_id, -jnp.inf, s_ref[...])
            s_ref[...] = s
            m = jnp.max(s, axis=0, keepdims=True)
            win = jnp.min(jnp.where(s == m, ids, pad_id), axis=0, keepdims=True)
            write_row(p, k, m, win)
            new.append(win)
        return tuple(new)

    lax.fori_loop(0, PEER_TOPK, step, tuple(jnp.full((1, tm), -1.0, F32) for _ in problems), unroll=TOPK_UNROLL)


def _rows_by_rank(rank, table_ref):
    out = jnp.zeros(rank.shape, F32)
    for p in range(PEER_TOPK):
        out = jnp.where(rank == float(p), table_ref[p:p + 1, :], out)
    return out


def _peer_topk_body(hq_ref, wq_ref, k1_ref, k2_ref, sid_ref, a_ref, b_ref, g_ref,
                    q_s, s1_s, s2_s, v1_s, i1_s, v2_s, i2_s, c_s, t_s, f_s, ao_s, bo_s, go_s):
    tm = hq_ref.shape[0]
    q = jnp.dot(hq_ref[...], wq_ref[...], preferred_element_type=F32)
    for c in range(2 * PEER_HEADS):
        q_s[c] = q[:, c * LANES:(c + 1) * LANES]
    nt = (((1,), (1,)), ((), ()))
    key_id = lax.broadcasted_iota(jnp.int32, (PEER_KEYS, tm), 0).astype(F32)

    def head(h, carry):
        for half, (kref, s_s) in enumerate(((k1_ref, s1_s), (k2_ref, s2_s))):
            qh = q_s[2 * h + half]
            q_hi = qh.astype(BF16)
            q_lo = (qh - q_hi.astype(F32)).astype(BF16)
            ntdot = functools.partial(lax.dot_general, dimension_numbers=nt, preferred_element_type=F32)
            s_s[...] = ntdot(kref[0], q_hi) + (ntdot(kref[0], q_lo) + ntdot(kref[1], q_hi))

        def write1(p, k, val, idx):
            vs, is_ = ((v1_s, i1_s), (v2_s, i2_s))[p]
            vs[pl.ds(k, 1), :] = val
            is_[pl.ds(k, 1), :] = idx
        _select_topk([(s1_s, key_id, float(PEER_KEYS)), (s2_s, key_id, float(PEER_KEYS))], write1)

        c_s[...] = jnp.full(c_s.shape, -jnp.inf, F32)
        for i in range(PEER_TOPK):
            r0, cnt = _STAIR_START[i], _STAIR_COUNT[i]
            c_s[r0:r0 + cnt, :] = v1_s[i:i + 1, :] + v2_s[0:cnt, :]

        def write2(p, k, val, idx):
            t_s[pl.ds(k, 1), :] = val
            f_s[pl.ds(k, 1), :] = idx
        _select_topk([(c_s, sid_ref[...], float(PEER_TOPK * PEER_TOPK))], write2)

        flat = f_s[...]
        rank1 = jnp.floor(flat * (1.0 / PEER_TOPK))
        rank2 = flat - rank1 * PEER_TOPK
        rows = pl.ds(pl.multiple_of(h * PEER_TOPK, PEER_TOPK), PEER_TOPK)
        ao_s[rows, :] = _rows_by_rank(rank1, i1_s)
        bo_s[rows, :] = _rows_by_rank(rank2, i2_s)
        t = t_s[...]
        e = jnp.exp(t - jnp.max(t, axis=0, keepdims=True))
        go_s[rows, :] = e / jnp.sum(e, axis=0, keepdims=True)
        return carry

    lax.fori_loop(0, PEER_HEADS, head, 0)
    a_ref[...] = ao_s[...].T
    b_ref[...] = bo_s[...].T
    g_ref[...] = go_s[...].T


def _peer_topk(hq, wq, keys1, keys2):
    t, d = hq.shape
    tm = PEER_TM
    hk = PEER_HEADS * PEER_TOPK
    out = jax.ShapeDtypeStruct((t, hk), F32)
    ospec = pl.BlockSpec((tm, hk), lambda i: (i, 0))
    sid = _stair_ids(tm)
    return pl.pallas_call(
        _peer_topk_body,
        grid=(t // tm,),
        in_specs=[
            pl.BlockSpec((tm, d), lambda i: (i, 0)),
            pl.BlockSpec(wq.shape, lambda i: (0, 0)),
            pl.BlockSpec(keys1.shape, lambda i: (0, 0, 0)),
            pl.BlockSpec(keys2.shape, lambda i: (0, 0, 0)),
            pl.BlockSpec(sid.shape, lambda i: (0, 0)),
        ],
        out_specs=[ospec, ospec, ospec],
        out_shape=[out, out, out],
        scratch_shapes=[
            pltpu.VMEM((2 * PEER_HEADS, tm, LANES), F32),
            pltpu.VMEM((PEER_KEYS, tm), F32), pltpu.VMEM((PEER_KEYS, tm), F32),
            pltpu.VMEM((PEER_TOPK, tm), F32), pltpu.VMEM((PEER_TOPK, tm), F32),
            pltpu.VMEM((PEER_TOPK, tm), F32), pltpu.VMEM((PEER_TOPK, tm), F32),
            pltpu.VMEM((_STAIR_ROWS, tm), F32),
            pltpu.VMEM((PEER_TOPK, tm), F32), pltpu.VMEM((PEER_TOPK, tm), F32),
            pltpu.VMEM((hk, tm), F32), pltpu.VMEM((hk, tm), F32), pltpu.VMEM((hk, tm), F32),
        ],
        compiler_params=_cparams(("parallel",)),
        name="peer_topk",
    )(hq, wq, keys1, keys2, sid)


PEER_TE = 8192
PEER_GROUPS = PEER_TE // PEER_KEYS
PEER_SCORE_TM = 512
PEER_VALUE_TM = 512
PEER_VALUE_TE = 2048
GATE_PITCH = PEER_KEYS + 8
GATE_UNROLL = 64


def _peer_scores_body(hq_ref, a_ref, b_ref, u_ref, o_ref):
    e = pl.program_id(0)
    s = lax.dot_general(hq_ref[...], u_ref[...], (((1,), (1,)), ((), ())),
                        preferred_element_type=F32)
    b_idx = b_ref[...].astype(jnp.int32)
    a_val = a_ref[...]
    cur = jnp.zeros(a_val.shape, F32)
    for jj in range(PEER_GROUPS):
        cand = jnp.take_along_axis(s[:, jj * LANES:(jj + 1) * LANES], b_idx, axis=1)
        cur = jnp.where(a_val == (e * PEER_GROUPS + jj).astype(F32), cand, cur)
    o_ref[0] = cur


def _peer_scores(hq, a, b, u_tab):
    t, d = hq.shape
    tm = PEER_SCORE_TM
    hk = a.shape[1]
    n_e = u_tab.shape[0] // PEER_TE
    tok = lambda e, i: (i, 0)
    return pl.pallas_call(
        _peer_scores_body,
        grid=(n_e, t // tm),
        in_specs=[
            pl.BlockSpec((tm, d), tok),
            pl.BlockSpec((tm, hk), tok), pl.BlockSpec((tm, hk), tok),
            pl.BlockSpec((PEER_TE, d), lambda e, i: (e, 0)),
        ],
        out_specs=pl.BlockSpec((1, tm, hk), lambda e, i: (e, i, 0)),
        out_shape=jax.ShapeDtypeStruct((n_e, t, hk), F32),
        compiler_params=_cparams(("arbitrary", "arbitrary")),
        name="peer_scores",
    )(hq, a, b, u_tab)


def _peer_values_body(ss_ref, a_ref, b_ref, g_ref, v_ref, x1_ref, g2_ref, lg_ref, lb_ref, o_ref,
                      w_s, hd_s, acc_s):
    tm = a_ref.shape[0]
    n_e = pl.num_programs(1)
    e = pl.program_id(1)
    nt = (((1,), (1,)), ((), ()))

    @pl.when(e == 0)
    def _gates():
        s_sel = jnp.sum(ss_ref[...], axis=0)
        act = 0.5 * s_sel * (1.0 + lax.erf(s_sel * (2.0 ** -0.5)))
        w_s[...] = g_ref[...] * act
        sub = lax.broadcasted_iota(jnp.int32, (PEER_KEYS, LANES), 0).astype(F32)

        def token(t, carry):
            a_row = a_ref[pl.ds(t, 1), :]
            b_row = b_ref[pl.ds(t, 1), :]
            w_row = w_s[pl.ds(t, 1), :]
            lhs = jnp.where(sub == a_row, w_row, 0.0).astype(BF16)
            rhs = jnp.where(sub == b_row, 1.0, 0.0).astype(BF16)
            tile = lax.dot_general(lhs, rhs, nt, preferred_element_type=F32)
            hd_s[pl.ds(pl.multiple_of(t * GATE_PITCH, 8), PEER_KEYS), :] = tile
            return carry
        lax.fori_loop(0, tm, token, 0, unroll=GATE_UNROLL)
        acc_s[...] = jnp.zeros(acc_s.shape, F32)

    groups = v_ref.shape[0] // PEER_KEYS
    j0 = e * groups
    lhs = jnp.concatenate(
        [hd_s[pl.ds(j0 + jj, tm, stride=GATE_PITCH), :].astype(BF16) for jj in range(groups)], axis=1)
    acc_s[...] += jnp.dot(lhs, v_ref[...], preferred_element_type=F32)

    @pl.when(e == n_e - 1)
    def _finish():
        o_ref[...] = _layer_norm(DEEPNORM_ALPHA * x1_ref[...] + g2_ref[0] * acc_s[...], lg_ref[...], lb_ref[...])


def _peer_values(ssel, a, b, g, v_tab, x1, g2, ln_g, ln_b, tokens_per_batch):
    t, d = x1.shape
    tm = PEER_VALUE_TM
    hk = a.shape[1]
    n_e = v_tab.shape[0] // PEER_VALUE_TE
    tiles_per_batch = tokens_per_batch // tm
    tok = lambda i, e: (i, 0)
    one = pl.Buffered(1)
    return pl.pallas_call(
        _peer_values_body,
        grid=(t // tm, n_e),
        in_specs=[
            pl.BlockSpec((ssel.shape[0], tm, hk), lambda i, e: (0, i, 0)),
            pl.BlockSpec((tm, hk), tok), pl.BlockSpec((tm, hk), tok), pl.BlockSpec((tm, hk), tok),
            pl.BlockSpec((PEER_VALUE_TE, d), lambda i, e: (e, 0)),
            pl.BlockSpec((tm, d), tok),
            pl.BlockSpec((1, 1, d), lambda i, e: (i // tiles_per_batch, 0, 0)),
            pl.BlockSpec((1, d), lambda i, e: (0, 0)),
            pl.BlockSpec((1, d), lambda i, e: (0, 0)),
        ],
        out_specs=pl.BlockSpec((tm, d), tok, pipeline_mode=one),
        out_shape=jax.ShapeDtypeStruct((t, d), F32),
        scratch_shapes=[
            pltpu.VMEM((tm, hk), F32),
            pltpu.VMEM((tm * GATE_PITCH, LANES), F32),
            pltpu.VMEM((tm, d), F32),
        ],
        compiler_params=_cparams(("parallel", "arbitrary")),
        name="peer_values",
    )(ssel, a, b, g, v_tab, x1, g2, ln_g, ln_b)


FFT_N2 = 128
FFT_PITCH = FFT_N2 + 8
HY_POS_PAD = 32
FFT_UNROLL = 4
FFT_STAGE_UNROLL = 8


def _dft_constants(n):
    big = 2 * n
    n1c = big // FFT_N2
    half = n1c // 2
    k1 = np.arange(n1c)[:, None]
    n1 = np.arange(half)[None, :]
    ang = 2 * np.pi * k1 * n1 / n1c
    c, s = np.cos(ang), np.sin(ang)
    f1 = np.block([[c, s], [-s, c]])
    k2 = np.arange(FFT_N2)[:, None]
    n2 = np.arange(FFT_N2)[None, :]
    ang = 2 * np.pi * k2 * n2 / FFT_N2
    c, s = np.cos(ang), np.sin(ang)
    d3 = np.block([[c, s], [-s, c]])
    d3i = np.block([[c, -s], [s, c]])
    ang = 2 * np.pi * n1.T * k1.T / n1c
    c, s = np.cos(ang), np.sin(ang)
    f3 = np.block([[c, -s], [s, c]]) / big
    ang = 2 * np.pi * (np.arange(n1c)[:, None] * np.arange(FFT_N2)[None, :]) / big
    twr = np.repeat(np.cos(ang).reshape(-1, 1), LANES, axis=1)
    twi = np.repeat(-np.sin(ang).reshape(-1, 1), LANES, axis=1)
    as32 = lambda a: np.asarray(a, np.float32)

    def split(a):
        parts, rest = [], np.asarray(a, np.float64)
        for _ in range(3):
            p = np.asarray(rest, BF16)
            parts.append(p)
            rest = rest - p.astype(np.float64)
        return np.stack(parts)
    return split(f1), split(d3), split(d3i), split(f3), as32(twr), as32(twi)


def _hdot(a, b):
    return jnp.dot(a, b, preferred_element_type=F32, precision=HIGHEST)


def _dft_dot(m_ref, x):
    x_hi = x.astype(BF16)
    x_lo = (x - x_hi.astype(F32)).astype(BF16)
    m_hi = m_ref[0]
    dot = functools.partial(jnp.dot, preferred_element_type=F32)
    return (dot(m_hi, x_hi) + dot(m_hi, x_lo)) + (dot(m_ref[1], x_hi) + dot(m_ref[2], x_hi))


def _fft_stage1(u_re, u_im, f1_ref, a_re, a_im):
    half = f1_ref.shape[2] // 2
    n1c = f1_ref.shape[1] // 2

    def column(n2):
        return jnp.concatenate([u_re[pl.ds(n2, half, stride=FFT_PITCH), :],
                                u_im[pl.ds(n2, half, stride=FFT_PITCH), :]], axis=0)

    def body(i, carry):
        a = _dft_dot(f1_ref, jnp.concatenate([column(2 * i), column(2 * i + 1)], axis=1))
        for s in range(2):
            a_re[pl.ds(2 * i + s, n1c, stride=FFT_PITCH), :] = a[:n1c, s * LANES:(s + 1) * LANES]
            a_im[pl.ds(2 * i + s, n1c, stride=FFT_PITCH), :] = a[n1c:, s * LANES:(s + 1) * LANES]
        return carry
    lax.fori_loop(0, FFT_N2 // 2, body, 0, unroll=FFT_STAGE_UNROLL)


def _twiddled_pair(a_re, a_im, twr_ref, twi_ref, i):
    t0 = pl.multiple_of(2 * i * FFT_N2, 2 * FFT_N2)
    rows = [pl.multiple_of((2 * i + s) * FFT_PITCH, 8) for s in range(2)]
    side = lambda ref, r: jnp.concatenate([ref[pl.ds(r[0], FFT_N2), :], ref[pl.ds(r[1], FFT_N2), :]], axis=1)
    tws = [pl.multiple_of(t0 + s * FFT_N2, FFT_N2) for s in range(2)]
    return rows, side(a_re, rows), side(a_im, rows), side(twr_ref, tws), side(twi_ref, tws)


def _fft_conv_middle(a_re, a_im, d3_ref, d3i_ref, twr_ref, twi_ref, hre_ref, him_ref):
    n1c = hre_ref.shape[1]

    def body(i, carry):
        rows, ar, ai, twr, twi = _twiddled_pair(a_re, a_im, twr_ref, twi_ref, i)
        x = _dft_dot(d3_ref, jnp.concatenate([ar * twr - ai * twi, ar * twi + ai * twr], axis=0))
        xr, xi = x[:FFT_N2], x[FFT_N2:]
        hr = jnp.concatenate([hre_ref[0, 2 * i], hre_ref[0, 2 * i + 1]], axis=1)
        hi = jnp.concatenate([him_ref[0, 2 * i], him_ref[0, 2 * i + 1]], axis=1)
        y = _dft_dot(d3i_ref, jnp.concatenate([xr * hr - xi * hi, xr * hi + xi * hr], axis=0))
        br, bi = y[:FFT_N2], y[FFT_N2:]
        out_re = br * twr + bi * twi
        out_im = bi * twr - br * twi
        for s in range(2):
            a_re[pl.ds(rows[s], FFT_N2), :] = out_re[:, s * LANES:(s + 1) * LANES]
            a_im[pl.ds(rows[s], FFT_N2), :] = out_im[:, s * LANES:(s + 1) * LANES]
        return carry
    lax.fori_loop(0, n1c // 2, body, 0, unroll=FFT_UNROLL)


def _fft_stage_inv(a_re, a_im, f3_ref, y_re, y_im):
    half = f3_ref.shape[1] // 2
    n1c = f3_ref.shape[2] // 2

    def column(n2):
        return jnp.concatenate([a_re[pl.ds(n2, n1c, stride=FFT_PITCH), :],
                                a_im[pl.ds(n2, n1c, stride=FFT_PITCH), :]], axis=0)

    def body(i, carry):
        y = _dft_dot(f3_ref, jnp.concatenate([column(2 * i), column(2 * i + 1)], axis=1))
        for s in range(2):
            y_re[pl.ds(2 * i + s, half, stride=FFT_PITCH), :] = y[:half, s * LANES:(s + 1) * LANES]
            y_im[pl.ds(2 * i + s, half, stride=FFT_PITCH), :] = y[half:, s * LANES:(s + 1) * LANES]
        return carry
    lax.fori_loop(0, FFT_N2 // 2, body, 0, unroll=FFT_STAGE_UNROLL)


def _short_conv_chunk(x_ref, bi, j, nblk, w_ref, b_ref):
    r0 = j * FFT_N2
    cur = x_ref[bi, r0:r0 + FFT_N2, :]
    row = lax.broadcasted_iota(jnp.int32, cur.shape, 0)
    if j == 0:
        prev = jnp.where(row == 0, 0.0, pltpu.roll(cur, 1, 0))
    else:
        prev = x_ref[bi, r0 - 1:r0 + FFT_N2 - 1, :]
    if j == nblk - 1:
        nxt = jnp.where(row == FFT_N2 - 1, 0.0, pltpu.roll(cur, FFT_N2 - 1, 0))
    else:
        nxt = x_ref[bi, r0 + 1:r0 + FFT_N2 + 1, :]
    return prev * w_ref[0:1, :] + cur * w_ref[1:2, :] + nxt * w_ref[2:3, :] + b_ref[...]


def _hy_conv_body(conv_a, a_ref, g_ref, wa_ref, ba_ref, wg_ref, bg_ref, skip_ref, hre_ref, him_ref,
                  f1_ref, d3_ref, d3i_ref, f3_ref, twr_ref, twi_ref, o_ref,
                  u_re, u_im, a_re, a_im, y_re, y_im):
    nblk = a_ref.shape[1] // FFT_N2
    for bi, dst in ((0, u_re), (1, u_im)):
        for j in range(nblk):
            if conv_a:
                blk = _short_conv_chunk(a_ref, bi, j, nblk, wa_ref, ba_ref)
            else:
                blk = a_ref[bi, j * FFT_N2:(j + 1) * FFT_N2, :]
            dst[j * FFT_PITCH:j * FFT_PITCH + FFT_N2, :] = blk
    _fft_stage1(u_re, u_im, f1_ref, a_re, a_im)
    _fft_conv_middle(a_re, a_im, d3_ref, d3i_ref, twr_ref, twi_ref, hre_ref, him_ref)
    _fft_stage_inv(a_re, a_im, f3_ref, y_re, y_im)
    skip = skip_ref[...]
    for bi, (ysrc, usrc) in enumerate(((y_re, u_re), (y_im, u_im))):
        for j in range(nblk):
            rows = slice(j * FFT_PITCH, j * FFT_PITCH + FFT_N2)
            gate = _short_conv_chunk(g_ref, bi, j, nblk, wg_ref, bg_ref)
            o_ref[bi, j * FFT_N2:(j + 1) * FFT_N2, :] = gate * (ysrc[rows, :] + usrc[rows, :] * skip)


def _hy_conv(a, a_col, g, g_col, conv_w, conv_b, skip, hre, him, order, consts, conv_a):
    b, n, _ = a.shape
    f1, d3, d3i, f3, twr, twi = consts
    n1c = f1.shape[1] // 2
    half = n1c // 2
    w = LANES
    tiles = HY_WIDTH // w
    one = pl.Buffered(1)
    data = lambda col: pl.BlockSpec((2, n, w), lambda ct, p: (p, 0, col + ct))
    wspec = lambda col: pl.BlockSpec((3, w), lambda ct, p: (0, col + ct))
    bspec = lambda col: pl.BlockSpec((1, w), lambda ct, p: (0, col + ct))
    hspec = pl.BlockSpec((1, n1c, FFT_N2, w), lambda ct, p: (order, 0, 0, ct), pipeline_mode=one)
    cs = lambda arr: pl.BlockSpec(arr.shape, lambda ct, p: (0,) * arr.ndim, pipeline_mode=one)
    a_wcol = a_col if conv_a else g_col
    return pl.pallas_call(
        functools.partial(_hy_conv_body, conv_a),
        grid=(tiles, b // 2),
        in_specs=[data(a_col), data(g_col), wspec(a_wcol), bspec(a_wcol), wspec(g_col), bspec(g_col),
                  pl.BlockSpec((1, w), lambda ct, p: (0, ct)), hspec, hspec,
                  cs(f1), cs(d3), cs(d3i), cs(f3), cs(twr), cs(twi)],
        out_specs=pl.BlockSpec((2, n, w), lambda ct, p: (p, 0, ct), pipeline_mode=one),
        out_shape=jax.ShapeDtypeStruct((b, n, HY_WIDTH), F32),
        scratch_shapes=[pltpu.VMEM((half * FFT_PITCH, w), F32), pltpu.VMEM((half * FFT_PITCH, w), F32),
                        pltpu.VMEM((n1c * FFT_PITCH, w), F32), pltpu.VMEM((n1c * FFT_PITCH, w), F32),
                        pltpu.VMEM((half * FFT_PITCH, w), F32), pltpu.VMEM((half * FFT_PITCH, w), F32)],
        compiler_params=_cparams(("parallel", "arbitrary")),
        name=f"hyena_conv{order + 1}",
    )(a, g, conv_w, conv_b, conv_w, conv_b, skip, hre, him, f1, d3, d3i, f3, twr, twi)


def _hy_filter_body(z_ref, w1_ref, b1_ref, f1_ref, w2_ref, b2_ref, f2_ref, w3_ref, b3_ref, dl_ref, o_ref):
    z = z_ref[...]
    h = jnp.sin(f1_ref[...] * (_hdot(z, w1_ref[...]) + b1_ref[...]))
    h = jnp.sin(f2_ref[...] * (_hdot(h, w2_ref[...]) + b2_ref[...]))
    h = _hdot(h, w3_ref[...]) + b3_ref[...]
    t = z[:, 0:1]
    o_ref[...] = h * (jnp.exp(-t * dl_ref[...]) + HY_WINDOW_SHIFT)


def _hy_filters(n, w1, b1, fr1, w2, b2, fr2, w3, b3):
    t = jnp.linspace(0.0, 1.0, n, dtype=F32)[:, None]
    wv = 2.0 * math.pi * jnp.arange(n, dtype=F32)[:, None] / n
    bands = jnp.linspace(1e-4, HY_BANDS - 1, HY_BANDS, dtype=F32)[None, :]
    z = jnp.concatenate([t, jnp.cos(bands * wv), -jnp.sin(bands * wv)], axis=-1)
    pos = z.shape[1]
    z = jnp.pad(z, ((0, 0), (0, HY_POS_PAD - pos)))
    w1p = jnp.pad(w1, ((0, HY_POS_PAD - pos), (0, 0)))
    min_decay = math.log(HY_DECAY_TARGET) / HY_SLOW_DECAY_PCT
    max_decay = math.log(HY_DECAY_TARGET) / HY_FAST_DECAY_PCT
    deltas = jnp.abs(jnp.linspace(min_decay, max_decay, HY_WIDTH, dtype=F32))
    n_out = w3.shape[1]
    dl = jnp.tile(deltas, n_out // HY_WIDTH)[None, :]
    tn = 512
    hid = w2.shape[0]
    full = lambda shape: pl.BlockSpec(shape, lambda i: (0, 0))
    return pl.pallas_call(
        _hy_filter_body,
        grid=(n // tn,),
        in_specs=[pl.BlockSpec((tn, HY_POS_PAD), lambda i: (i, 0)), full((HY_POS_PAD, hid)), full((1, hid)),
                  full((1, hid)), full((hid, hid)), full((1, hid)), full((1, hid)), full((hid, n_out)),
                  full((1, n_out)), full((1, n_out))],
        out_specs=pl.BlockSpec((tn, n_out), lambda i: (i, 0)),
        out_shape=jax.ShapeDtypeStruct((n, n_out), F32),
        compiler_params=_cparams(("parallel",)),
        name="hyena_filters",
    )(z, w1p, b1[None], fr1[None], w2, b2[None], fr2[None], w3, b3[None], dl)


def _hy_spectrum_body(f_ref, b_ref, f1_ref, d3_ref, twr_ref, twi_ref, hre_ref, him_ref, u_re, u_im, a_re, a_im):
    nblk = f_ref.shape[0] // FFT_N2
    n1c = hre_ref.shape[1]
    for part, out_ref in enumerate((hre_ref, him_ref)):
        for j in range(nblk):
            f = f_ref[j * FFT_N2:(j + 1) * FFT_N2, :]
            bw = b_ref[j * FFT_N2:(j + 1) * FFT_N2, :]
            if j == 0:
                row = lax.broadcasted_iota(jnp.int32, bw.shape, 0)
                bw = jnp.where(row == 0, 0.0, bw)
            rows = slice(j * FFT_PITCH, j * FFT_PITCH + FFT_N2)
            u_re[rows, :] = f + bw if part == 0 else f - bw
            u_im[rows, :] = jnp.zeros((FFT_N2, LANES), F32)
        _fft_stage1(u_re, u_im, f1_ref, a_re, a_im)

        def body(i, carry, part=part, out_ref=out_ref):
            _, ar, ai, twr, twi = _twiddled_pair(a_re, a_im, twr_ref, twi_ref, i)
            x = _dft_dot(d3_ref, jnp.concatenate([ar * twr - ai * twi, ar * twi + ai * twr], axis=0))
            x = x[:FFT_N2] if part == 0 else x[FFT_N2:]
            for s in range(2):
                out_ref[0, 2 * i + s] = x[:, s * LANES:(s + 1) * LANES]
            return carry
        lax.fori_loop(0, n1c // 2, body, 0, unroll=FFT_UNROLL)


def _hy_spectrum(h, consts):
    n = h.shape[0]
    f1, d3, _, _, twr, twi = consts
    n1c = f1.shape[1] // 2
    half = n1c // 2
    tiles = HY_WIDTH // LANES
    cs = lambda arr: pl.BlockSpec(arr.shape, lambda o, ct: (0,) * arr.ndim)
    out = jax.ShapeDtypeStruct((HY_ORDER, n1c, FFT_N2, HY_WIDTH), F32)
    ospec = pl.BlockSpec((1, n1c, FFT_N2, LANES), lambda o, ct: (o, 0, 0, ct))
    return pl.pallas_call(
        _hy_spectrum_body,
        grid=(HY_ORDER, tiles),
        in_specs=[pl.BlockSpec((n, LANES), lambda o, ct: (0, o * tiles + ct)),
                  pl.BlockSpec((n, LANES), lambda o, ct: (0, (HY_ORDER + o) * tiles + ct)),
                  cs(f1), cs(d3), cs(twr), cs(twi)],
        out_specs=[ospec, ospec],
        out_shape=[out, out],
        scratch_shapes=[pltpu.VMEM((half * FFT_PITCH, LANES), F32), pltpu.VMEM((half * FFT_PITCH, LANES), F32),
                        pltpu.VMEM((n1c * FFT_PITCH, LANES), F32), pltpu.VMEM((n1c * FFT_PITCH, LANES), F32)],
        compiler_params=_cparams(("parallel", "parallel")),
        name="hyena_spectrum",
    )(h, h, f1, d3, twr, twi)


def _hyena(p_lat, conv_w, conv_b, w1, b1, fr1, w2, b2, fr2, w3, b3, skip):
    n = p_lat.shape[1]
    consts = _dft_constants(n)
    h = _hy_filters(n, w1, b1, fr1, w2, b2, fr2, w3, b3)
    hre, him = _hy_spectrum(h, consts)
    tiles = HY_WIDTH // LANES
    col = COL_HY // LANES
    cw = jnp.pad(conv_w, ((0, 0), (COL_HY, 0)))
    cb = jnp.pad(conv_b[None], ((0, 0), (COL_HY, 0)))
    y = _hy_conv(p_lat, col, p_lat, col + tiles, cw, cb, skip[0][None], hre, him, 0, consts, True)
    return _hy_conv(y, 0, p_lat, col + 2 * tiles, cw, cb, skip[1][None], hre, him, 1, consts, False)


def _split_bf16(w):
    hi = w.astype(BF16)
    return jnp.stack([hi, (w - hi.astype(F32)).astype(BF16)])


def _pad_heads(w, heads):
    d = w.shape[0]
    w = w.reshape(d, heads, HEAD_DIM)
    w = jnp.pad(w, ((0, 0), (0, 0), (0, LANES - HEAD_DIM)))
    return w.reshape(d, heads * LANES)


def kernel(x, c, ctx, c_ctx, w_mod, b_mod, w_in, hy_conv_w, hy_conv_b, hy_f_w1, hy_f_b1, hy_f_freq1, hy_f_w2,
           hy_f_b2, hy_f_freq2, hy_f_w3, hy_f_b3, hy_skip, attn_sink, w_out, ln1_g, ln1_b, peer_wq, peer_keys1,
           peer_keys2, peer_u, peer_v, ln2_g, ln2_b):
    b, n, d = x.shape
    l = 0
    cc = jnp.concatenate([c, c_ctx[None], jnp.zeros((8 - b - 1, d), F32)], axis=0)
    mod = _modulation(cc, w_mod[l], b_mod[l][None])
    mod_lat = mod[:b].reshape(b, 6, 1, d)
    sh1, sc1, g1, sh2, sc2, g2 = (mod_lat[:, i] for i in range(6))
    mod_c = mod[b].reshape(6, 1, 1, d)
    csh1, csc1 = mod_c[0], mod_c[1]

    w = w_in[l]
    w_q = _pad_heads(w[:, PROJ_HY:KV_START], ATT_HEADS)
    w_k = _pad_heads(w[:, KV_START:KV_START + PROJ_KV], ATT_KV_HEADS)
    w_v = _pad_heads(w[:, KV_START + PROJ_KV:], ATT_KV_HEADS)
    w_pad = jnp.concatenate([w_q, w[:, :PROJ_HY], w_k, w_v], axis=1).astype(BF16)
    w_kv = jnp.concatenate([w_k, w_v], axis=1).astype(BF16)

    p_lat = _mod_matmul(x, sc1, sh1, w_pad, 512, "in_proj")
    kv_ctx = _mod_matmul(ctx, csc1, csh1, w_kv, ctx.shape[1], "ctx_kv_proj")

    cos_tab, sin_tab = _rope_tables(n)
    att = _attention(p_lat, kv_ctx, attn_sink[l], cos_tab, sin_tab)

    hy = _hyena(p_lat, hy_conv_w[l], hy_conv_b[l], hy_f_w1[l], hy_f_b1[l], hy_f_freq1[l], hy_f_w2[l], hy_f_b2[l],
                hy_f_freq2[l], hy_f_w3[l], hy_f_b3[l], hy_skip[l])

    wo = w_out[l]
    w_o_hy = wo[:HY_WIDTH].astype(BF16)
    w_o_att = jnp.pad(wo[HY_WIDTH:].reshape(ATT_HEADS, HEAD_DIM, d),
                      ((0, 0), (0, LANES - HEAD_DIM), (0, 0))).reshape(QPAD, d).astype(BF16)
    x1, hq = _outproj_ln(hy, att, x, w_o_hy, w_o_att, g1, sc2, sh2, ln1_g[l][None], ln1_b[l][None])

    hq2 = hq.reshape(b * n, d)
    a_idx, b_idx, gate = _peer_topk(hq2, peer_wq[l].astype(BF16), _split_bf16(peer_keys1[l]),
                                    _split_bf16(peer_keys2[l]))
    s_sel = _peer_scores(hq2, a_idx, b_idx, peer_u[l].astype(BF16))
    out = _peer_values(s_sel, a_idx, b_idx, gate, peer_v[l].astype(BF16), x1.reshape(b * n, d), g2,
                       ln2_g[l][None], ln2_b[l][None], n)
    return out.reshape(b, n, d)
```

```python
import functools
import math

import jax
import jax.numpy as jnp
import numpy as np
from jax import lax
from jax.experimental import pallas as pl
from jax.experimental.pallas import tpu as pltpu

F32 = jnp.float32
BF16 = jnp.bfloat16
HIGHEST = lax.Precision.HIGHEST

LANES = 128
VMEM_LIMIT = 60000 * 1024

D_MODEL = 1024
GRID_W = 64
HY_WIDTH = 512
HY_ORDER = 2
HY_BANDS = 8
HY_DECAY_TARGET = 1e-2
HY_FAST_DECAY_PCT = 0.3
HY_SLOW_DECAY_PCT = 1.5
HY_WINDOW_SHIFT = 0.05
ATT_HEADS = 8
ATT_KV_HEADS = 2
ATT_REP = ATT_HEADS // ATT_KV_HEADS
HEAD_DIM = 64
WINDOW = 128
BLOCK = 128
ROPE_BASE = 10000.0
ROPE_FREQS = HEAD_DIM // 4
PROJ_HY = (HY_ORDER + 1) * HY_WIDTH
PROJ_Q = ATT_HEADS * HEAD_DIM
PROJ_KV = ATT_KV_HEADS * HEAD_DIM
KV_START = PROJ_HY + PROJ_Q
PEER_KEYS = 128
PEER_HEADS = 8
PEER_QDIM = 256
PEER_TOPK = 16
LN_EPS = 1e-5
NEG_INF = -1e30
DEPTH = 1
DEEPNORM_ALPHA = (2.0 * DEPTH) ** 0.25

QPAD = ATT_HEADS * LANES
KVPAD = ATT_KV_HEADS * LANES
COL_Q = 0
COL_HY = QPAD
COL_K = QPAD + PROJ_HY
COL_V = COL_K + KVPAD
PROJ_PAD = COL_V + KVPAD


def _cparams(sem):
    return pltpu.CompilerParams(dimension_semantics=sem, vmem_limit_bytes=VMEM_LIMIT)


def _mod_body(c_ref, w_ref, b_ref, o_ref):
    c = c_ref[...]
    a = c * jax.nn.sigmoid(c)
    o_ref[...] = jnp.dot(a, w_ref[...], preferred_element_type=F32, precision=HIGHEST) + b_ref[...]


def _modulation(cc, w_mod, b_mod):
    rows, d = cc.shape
    n_out = w_mod.shape[1]
    tn = 1536
    return pl.pallas_call(
        _mod_body,
        grid=(n_out // tn,),
        in_specs=[
            pl.BlockSpec((rows, d), lambda j: (0, 0)),
            pl.BlockSpec((d, tn), lambda j: (0, j)),
            pl.BlockSpec((1, tn), lambda j: (0, j)),
        ],
        out_specs=pl.BlockSpec((rows, tn), lambda j: (0, j)),
        out_shape=jax.ShapeDtypeStruct((rows, n_out), F32),
        compiler_params=_cparams(("arbitrary",)),
        name="modulation",
    )(cc, w_mod, b_mod)


def _mod_matmul_body(x_ref, sc_ref, sh_ref, w_ref, o_ref):
    h = x_ref[0] * (1.0 + sc_ref[0]) + sh_ref[0]
    o_ref[0] = jnp.dot(h.astype(BF16), w_ref[...], preferred_element_type=F32)


def _mod_matmul(x, sc, sh, w, tm, name):
    b, n, d = x.shape
    n_out = w.shape[1]
    per_batch = sc.shape[0] == b
    mod_map = (lambda bi, i: (bi, 0, 0)) if per_batch else (lambda bi, i: (0, 0, 0))
    return pl.pallas_call(
        _mod_matmul_body,
        grid=(b, n // tm),
        in_specs=[
            pl.BlockSpec((1, tm, d), lambda bi, i: (bi, i, 0)),
            pl.BlockSpec((1, 1, d), mod_map),
            pl.BlockSpec((1, 1, d), mod_map),
            pl.BlockSpec((d, n_out), lambda bi, i: (0, 0)),
        ],
        out_specs=pl.BlockSpec((1, tm, n_out), lambda bi, i: (bi, i, 0)),
        out_shape=jax.ShapeDtypeStruct((b, n, n_out), F32),
        compiler_params=_cparams(("parallel", "parallel")),
        name=name,
    )(x, sc, sh, w)


def _rope_head(x, cos, sin_signed):
    lane = lax.broadcasted_iota(jnp.int32, x.shape, 1)
    first_half = (lane % 32) < 16
    partner = jnp.where(first_half, pltpu.roll(x, LANES - 16, 1), pltpu.roll(x, 16, 1))
    return x * cos + partner * sin_signed


ATT_TQ = 512
ROPE_CHUNK = 512


def _attn_body(sink_ref, q_ref, k_ref, v_ref, kvc_ref, cosq_ref, sinq_ref, cosk_ref, sink_tab_ref,
               o_ref, ks_ref, vs_ref, kcs_ref, vcs_ref):
    n = k_ref.shape[1]
    iq = pl.program_id(1)
    scale = HEAD_DIM ** -0.5

    @pl.when(iq == 0)
    def _prepare_keys():
        def chunk(ci, carry):
            r0 = pl.multiple_of(ci * ROPE_CHUNK, ROPE_CHUNK)
            cos = cosk_ref[pl.ds(r0, ROPE_CHUNK), :]
            sin = sink_tab_ref[pl.ds(r0, ROPE_CHUNK), :]
            for g in range(ATT_KV_HEADS):
                kg = k_ref[0, pl.ds(r0, ROPE_CHUNK), g * LANES:(g + 1) * LANES]
                ks_ref[pl.ds(r0, ROPE_CHUNK), g * LANES:(g + 1) * LANES] = _rope_head(kg, cos, sin).astype(BF16)
            vs_ref[pl.ds(r0, ROPE_CHUNK), :] = v_ref[0, pl.ds(r0, ROPE_CHUNK), :].astype(BF16)
            return carry
        lax.fori_loop(0, n // ROPE_CHUNK, chunk, 0)
        kcs_ref[...] = kvc_ref[0, :, 0:KVPAD].astype(BF16)
        vcs_ref[...] = kvc_ref[0, :, KVPAD:2 * KVPAD].astype(BF16)

    n_loc = 3 * BLOCK
    rows = ATT_REP * BLOCK
    row_i = lax.broadcasted_iota(jnp.int32, (rows, n_loc), 0)
    col_i = lax.broadcasted_iota(jnp.int32, (rows, n_loc), 1)
    rel = col_i - (row_i % BLOCK)
    head_of_row = lax.broadcasted_iota(jnp.int32, (rows, 1), 0) // BLOCK

    for j in range(ATT_TQ // BLOCK):
        blk = iq * (ATT_TQ // BLOCK) + j
        start = pl.multiple_of(jnp.clip((blk - 1) * BLOCK, 0, n - n_loc), BLOCK)
        qrows = slice(j * BLOCK, (j + 1) * BLOCK)
        cosq = cosq_ref[qrows, :]
        sinq = sinq_ref[qrows, :]
        delta = rel + (start - blk * BLOCK)
        in_window = jnp.abs(delta) <= WINDOW
        for g in range(ATT_KV_HEADS):
            heads = [ATT_REP * g + r for r in range(ATT_REP)]
            qg = jnp.concatenate(
                [(_rope_head(q_ref[0, qrows, h * LANES:(h + 1) * LANES], cosq, sinq) * scale).astype(BF16)
                 for h in heads], axis=0)
            kg = ks_ref[pl.ds(start, n_loc), g * LANES:(g + 1) * LANES]
            vg = vs_ref[pl.ds(start, n_loc), g * LANES:(g + 1) * LANES]
            kcg = kcs_ref[:, g * LANES:(g + 1) * LANES]
            vcg = vcs_ref[:, g * LANES:(g + 1) * LANES]
            nt = (((1,), (1,)), ((), ()))
            s_loc = lax.dot_general(qg, kg, nt, preferred_element_type=F32)
            s_ctx = lax.dot_general(qg, kcg, nt, preferred_element_type=F32)
            s_loc = jnp.where(in_window, s_loc, NEG_INF)
            sink_col = jnp.zeros((rows, 1), F32)
            for r, h in enumerate(heads):
                sink_col = jnp.where(head_of_row == r, sink_ref[h], sink_col)
            m = jnp.maximum(jnp.maximum(jnp.max(s_loc, axis=1, keepdims=True),
                                        jnp.max(s_ctx, axis=1, keepdims=True)), sink_col)
            p_loc = jnp.exp(s_loc - m)
            p_ctx = jnp.exp(s_ctx - m)
            den = (jnp.sum(p_loc, axis=1, keepdims=True) + jnp.sum(p_ctx, axis=1, keepdims=True)
                   + jnp.exp(sink_col - m))
            o = (jnp.dot(p_loc.astype(BF16), vg, preferred_element_type=F32)
                 + jnp.dot(p_ctx.astype(BF16), vcg, preferred_element_type=F32)) / den
            for r, h in enumerate(heads):
                o_ref[0, qrows, h * LANES:(h + 1) * LANES] = o[r * BLOCK:(r + 1) * BLOCK].astype(BF16)


def _attention(p_lat, kv_ctx, sink, cos_tab, sin_tab):
    b, n, _ = p_lat.shape
    n_ctx = kv_ctx.shape[1]
    grid_spec = pltpu.PrefetchScalarGridSpec(
        num_scalar_prefetch=1,
        grid=(b, n // ATT_TQ),
        in_specs=[
            pl.BlockSpec((1, ATT_TQ, QPAD), lambda bi, i, s: (bi, i, COL_Q // QPAD)),
            pl.BlockSpec((1, n, KVPAD), lambda bi, i, s: (bi, 0, COL_K // KVPAD)),
            pl.BlockSpec((1, n, KVPAD), lambda bi, i, s: (bi, 0, COL_V // KVPAD)),
            pl.BlockSpec((1, n_ctx, 2 * KVPAD), lambda bi, i, s: (bi, 0, 0)),
            pl.BlockSpec((ATT_TQ, LANES), lambda bi, i, s: (i, 0)),
            pl.BlockSpec((ATT_TQ, LANES), lambda bi, i, s: (i, 0)),
            pl.BlockSpec((n, LANES), lambda bi, i, s: (0, 0)),
            pl.BlockSpec((n, LANES), lambda bi, i, s: (0, 0)),
        ],
        out_specs=pl.BlockSpec((1, ATT_TQ, QPAD), lambda bi, i, s: (bi, i, 0)),
        scratch_shapes=[
            pltpu.VMEM((n, KVPAD), BF16),
            pltpu.VMEM((n, KVPAD), BF16),
            pltpu.VMEM((n_ctx, KVPAD), BF16),
            pltpu.VMEM((n_ctx, KVPAD), BF16),
        ],
    )
    return pl.pallas_call(
        _attn_body,
        grid_spec=grid_spec,
        out_shape=jax.ShapeDtypeStruct((b, n, QPAD), BF16),
        compiler_params=_cparams(("parallel", "arbitrary")),
        name="window_attention",
    )(sink, p_lat, p_lat, p_lat, kv_ctx, cos_tab, sin_tab, cos_tab, sin_tab)


def _rope_tables(n):
    rows = n // GRID_W
    row = jnp.repeat(jnp.arange(rows, dtype=F32), GRID_W)
    col = jnp.tile(jnp.arange(GRID_W, dtype=F32), rows)
    inv = ROPE_BASE ** (-jnp.arange(ROPE_FREQS, dtype=F32) / ROPE_FREQS)
    ang_r = row[:, None] * inv
    ang_c = col[:, None] * inv
    pad1 = jnp.ones((n, LANES - HEAD_DIM), F32)
    pad0 = jnp.zeros((n, LANES - HEAD_DIM), F32)
    cos = jnp.concatenate([jnp.cos(ang_r), jnp.cos(ang_r), jnp.cos(ang_c), jnp.cos(ang_c), pad1], axis=1)
    sin = jnp.concatenate([-jnp.sin(ang_r), jnp.sin(ang_r), -jnp.sin(ang_c), jnp.sin(ang_c), pad0], axis=1)
    return cos, sin


def _layer_norm(r, g, b):
    mu = jnp.mean(r, axis=-1, keepdims=True)
    var = jnp.mean(jnp.square(r - mu), axis=-1, keepdims=True)
    return (r - mu) * lax.rsqrt(var + LN_EPS) * g + b


def _outproj_body(hy_ref, att_ref, x_ref, wh_ref, wa_ref, g1_ref, sc2_ref, sh2_ref, lg_ref, lb_ref,
                  x1_ref, hq_ref):
    y = (jnp.dot(hy_ref[0].astype(BF16), wh_ref[...], preferred_element_type=F32)
         + jnp.dot(att_ref[0], wa_ref[...], preferred_element_type=F32))
    x1 = _layer_norm(DEEPNORM_ALPHA * x_ref[0] + g1_ref[0] * y, lg_ref[...], lb_ref[...])
    x1_ref[0] = x1
    hq_ref[0] = (x1 * (1.0 + sc2_ref[0]) + sh2_ref[0]).astype(BF16)


def _outproj_ln(hy, att, x, w_hy, w_att, g1, sc2, sh2, ln_g, ln_b, tm=512):
    b, n, d = x.shape
    modspec = pl.BlockSpec((1, 1, d), lambda bi, i: (bi, 0, 0))
    vecspec = pl.BlockSpec((1, d), lambda bi, i: (0, 0))
    return pl.pallas_call(
        _outproj_body,
        grid=(b, n // tm),
        in_specs=[
            pl.BlockSpec((1, tm, HY_WIDTH), lambda bi, i: (bi, i, 0)),
            pl.BlockSpec((1, tm, QPAD), lambda bi, i: (bi, i, 0)),
            pl.BlockSpec((1, tm, d), lambda bi, i: (bi, i, 0)),
            pl.BlockSpec(w_hy.shape, lambda bi, i: (0, 0)),
            pl.BlockSpec(w_att.shape, lambda bi, i: (0, 0)),
            modspec, modspec, modspec, vecspec, vecspec,
        ],
        out_specs=[
            pl.BlockSpec((1, tm, d), lambda bi, i: (bi, i, 0)),
            pl.BlockSpec((1, tm, d), lambda bi, i: (bi, i, 0)),
        ],
        out_shape=[jax.ShapeDtypeStruct((b, n, d), F32), jax.ShapeDtypeStruct((b, n, d), BF16)],
        compiler_params=_cparams(("parallel", "parallel")),
        name="outproj_ln1",
    )(hy, att, x, w_hy, w_att, g1, sc2, sh2, ln_g, ln_b)


PEER_TM = 1024
TOPK_UNROLL = 4
_STAIR = sorted(((i, j) for i in range(PEER_TOPK) for j in range(PEER_TOPK) if (i + 1) * (j + 1) <= PEER_TOPK),
                key=lambda p: p[0] * PEER_TOPK + p[1])
_STAIR_ROWS = -(-len(_STAIR) // 8) * 8
_STAIR_COUNT = [PEER_TOPK // (i + 1) for i in range(PEER_TOPK)]
_STAIR_START = [sum(_STAIR_COUNT[:i]) for i in range(PEER_TOPK)]


def _stair_ids(tm):
    ids = np.full((_STAIR_ROWS,), float(PEER_TOPK * PEER_TOPK), np.float32)
    ids[:len(_STAIR)] = [i * PEER_TOPK + j for i, j in _STAIR]
    return np.repeat(ids[:, None], tm, axis=1)


def _select_topk(problems, write_row):
    tm = problems[0][0].shape[1]

    def step(k, prev):
        new = []
        for p, ((s_ref, ids, pad_id), prev_id) in enumerate(zip(problems, prev)):
            s = jnp.where(ids == prev_id, -jnp.inf, s_ref[...])
            s_ref[...] = s
            m = jnp.max(s, axis=0, keepdims=True)
            win = jnp.min(jnp.where(s == m, ids, pad_id), axis=0, keepdims=True)
            write_row(p, k, m, win)
            new.append(win)
        return tuple(new)

    lax.fori_loop(0, PEER_TOPK, step, tuple(jnp.full((1, tm), -1.0, F32) for _ in problems), unroll=TOPK_UNROLL)


def _rows_by_rank(rank, table_ref):
    out = jnp.zeros(rank.shape, F32)
    for p in range(PEER_TOPK):
        out = jnp.where(rank == float(p), table_ref[p:p + 1, :], out)
    return out


def _peer_topk_body(hq_ref, wq_ref, k1_ref, k2_ref, sid_ref, a_ref, b_ref, g_ref,
                    q_s, s1_s, s2_s, v1_s, i1_s, v2_s, i2_s, c_s, t_s, f_s, ao_s, bo_s, go_s):
    tm = hq_ref.shape[0]
    q = jnp.dot(hq_ref[...], wq_ref[...], preferred_element_type=F32)
    for c in range(2 * PEER_HEADS):
        q_s[c] = q[:, c * LANES:(c + 1) * LANES]
    nt = (((1,), (1,)), ((), ()))
    key_id = lax.broadcasted_iota(jnp.int32, (PEER_KEYS, tm), 0).astype(F32)

    def head(h, carry):
        for half, (kref, s_s) in enumerate(((k1_ref, s1_s), (k2_ref, s2_s))):
            qh = q_s[2 * h + half]
            q_hi = qh.astype(BF16)
            q_lo = (qh - q_hi.astype(F32)).astype(BF16)
            ntdot = functools.partial(lax.dot_general, dimension_numbers=nt, preferred_element_type=F32)
            s_s[...] = ntdot(kref[0], q_hi) + (ntdot(kref[0], q_lo) + ntdot(kref[1], q_hi))

        def write1(p, k, val, idx):
            vs, is_ = ((v1_s, i1_s), (v2_s, i2_s))[p]
            vs[pl.ds(k, 1), :] = val
            is_[pl.ds(k, 1), :] = idx
        _select_topk([(s1_s, key_id, float(PEER_KEYS)), (s2_s, key_id, float(PEER_KEYS))], write1)

        c_s[...] = jnp.full(c_s.shape, -jnp.inf, F32)
        for i in range(PEER_TOPK):
            r0, cnt = _STAIR_START[i], _STAIR_COUNT[i]
            c_s[r0:r0 + cnt, :] = v1_s[i:i + 1, :] + v2_s[0:cnt, :]

        def write2(p, k, val, idx):
            t_s[pl.ds(k, 1), :] = val
            f_s[pl.ds(k, 1), :] = idx
        _select_topk([(c_s, sid_ref[...], float(PEER_TOPK * PEER_TOPK))], write2)

        flat = f_s[...]
        rank1 = jnp.floor(flat * (1.0 / PEER_TOPK))
        rank2 = flat - rank1 * PEER_TOPK
        rows = pl.ds(pl.multiple_of(h * PEER_TOPK, PEER_TOPK), PEER_TOPK)
        ao_s[rows, :] = _rows_by_rank(rank1, i1_s)
        bo_s[rows, :] = _rows_by_rank(rank2, i2_s)
        t = t_s[...]
        e = jnp.exp(t - jnp.max(t, axis=0, keepdims=True))
        go_s[rows, :] = e / jnp.sum(e, axis=0, keepdims=True)
        return carry

    lax.fori_loop(0, PEER_HEADS, head, 0)
    a_ref[...] = ao_s[...].T
    b_ref[...] = bo_s[...].T
    g_ref[...] = go_s[...].T


def _peer_topk(hq, wq, keys1, keys2):
    t, d = hq.shape
    tm = PEER_TM
    hk = PEER_HEADS * PEER_TOPK
    out = jax.ShapeDtypeStruct((t, hk), F32)
    ospec = pl.BlockSpec((tm, hk), lambda i: (i, 0))
    sid = _stair_ids(tm)
    return pl.pallas_call(
        _peer_topk_body,
        grid=(t // tm,),
        in_specs=[
            pl.BlockSpec((tm, d), lambda i: (i, 0)),
            pl.BlockSpec(wq.shape, lambda i: (0, 0)),
            pl.BlockSpec(keys1.shape, lambda i: (0, 0, 0)),
            pl.BlockSpec(keys2.shape, lambda i: (0, 0, 0)),
            pl.BlockSpec(sid.shape, lambda i: (0, 0)),
        ],
        out_specs=[ospec, ospec, ospec],
        out_shape=[out, out, out],
        scratch_shapes=[
            pltpu.VMEM((2 * PEER_HEADS, tm, LANES), F32),
            pltpu.VMEM((PEER_KEYS, tm), F32), pltpu.VMEM((PEER_KEYS, tm), F32),
            pltpu.VMEM((PEER_TOPK, tm), F32), pltpu.VMEM((PEER_TOPK, tm), F32),
            pltpu.VMEM((PEER_TOPK, tm), F32), pltpu.VMEM((PEER_TOPK, tm), F32),
            pltpu.VMEM((_STAIR_ROWS, tm), F32),
            pltpu.VMEM((PEER_TOPK, tm), F32), pltpu.VMEM((PEER_TOPK, tm), F32),
            pltpu.VMEM((hk, tm), F32), pltpu.VMEM((hk, tm), F32), pltpu.VMEM((hk, tm), F32),
        ],
        compiler_params=_cparams(("parallel",)),
        name="peer_topk",
    )(hq, wq, keys1, keys2, sid)


PEER_TE = 8192
PEER_GROUPS = PEER_TE // PEER_KEYS
PEER_SCORE_TM = 512
PEER_VALUE_TM = 512
PEER_VALUE_TE = 2048
GATE_PITCH = PEER_KEYS + 8
GATE_UNROLL = 64


def _peer_scores_body(hq_ref, a_ref, b_ref, u_ref, o_ref):
    e = pl.program_id(0)
    s = lax.dot_general(hq_ref[...], u_ref[...], (((1,), (1,)), ((), ())),
                        preferred_element_type=F32)
    b_idx = b_ref[...].astype(jnp.int32)
    a_val = a_ref[...]
    cur = jnp.zeros(a_val.shape, F32)
    for jj in range(PEER_GROUPS):
        cand = jnp.take_along_axis(s[:, jj * LANES:(jj + 1) * LANES], b_idx, axis=1)
        cur = jnp.where(a_val == (e * PEER_GROUPS + jj).astype(F32), cand, cur)
    o_ref[0] = cur


def _peer_scores(hq, a, b, u_tab):
    t, d = hq.shape
    tm = PEER_SCORE_TM
    hk = a.shape[1]
    n_e = u_tab.shape[0] // PEER_TE
    tok = lambda e, i: (i, 0)
    return pl.pallas_call(
        _peer_scores_body,
        grid=(n_e, t // tm),
        in_specs=[
            pl.BlockSpec((tm, d), tok),
            pl.BlockSpec((tm, hk), tok), pl.BlockSpec((tm, hk), tok),
            pl.BlockSpec((PEER_TE, d), lambda e, i: (e, 0)),
        ],
        out_specs=pl.BlockSpec((1, tm, hk), lambda e, i: (e, i, 0)),
        out_shape=jax.ShapeDtypeStruct((n_e, t, hk), F32),
        compiler_params=_cparams(("arbitrary", "arbitrary")),
        name="peer_scores",
    )(hq, a, b, u_tab)


def _peer_values_body(ss_ref, a_ref, b_ref, g_ref, v_ref, x1_ref, g2_ref, lg_ref, lb_ref, o_ref,
                      w_s, hd_s, acc_s):
    tm = a_ref.shape[0]
    n_e = pl.num_programs(1)
    e = pl.program_id(1)
    nt = (((1,), (1,)), ((), ()))

    @pl.when(e == 0)
    def _gates():
        s_sel = jnp.sum(ss_ref[...], axis=0)
        act = 0.5 * s_sel * (1.0 + lax.erf(s_sel * (2.0 ** -0.5)))
        w_s[...] = g_ref[...] * act
        sub = lax.broadcasted_iota(jnp.int32, (PEER_KEYS, LANES), 0).astype(F32)

        def token(t, carry):
            a_row = a_ref[pl.ds(t, 1), :]
            b_row = b_ref[pl.ds(t, 1), :]
            w_row = w_s[pl.ds(t, 1), :]
            lhs = jnp.where(sub == a_row, w_row, 0.0).astype(BF16)
            rhs = jnp.where(sub == b_row, 1.0, 0.0).astype(BF16)
            tile = lax.dot_general(lhs, rhs, nt, preferred_element_type=F32)
            hd_s[pl.ds(pl.multiple_of(t * GATE_PITCH, 8), PEER_KEYS), :] = tile
            return carry
        lax.fori_loop(0, tm, token, 0, unroll=GATE_UNROLL)
        acc_s[...] = jnp.zeros(acc_s.shape, F32)

    groups = v_ref.shape[0] // PEER_KEYS
    j0 = e * groups
    lhs = jnp.concatenate(
        [hd_s[pl.ds(j0 + jj, tm, stride=GATE_PITCH), :].astype(BF16) for jj in range(groups)], axis=1)
    acc_s[...] += jnp.dot(lhs, v_ref[...], preferred_element_type=F32)

    @pl.when(e == n_e - 1)
    def _finish():
        o_ref[...] = _layer_norm(DEEPNORM_ALPHA * x1_ref[...] + g2_ref[0] * acc_s[...], lg_ref[...], lb_ref[...])


def _peer_values(ssel, a, b, g, v_tab, x1, g2, ln_g, ln_b, tokens_per_batch):
    t, d = x1.shape
    tm = PEER_VALUE_TM
    hk = a.shape[1]
    n_e = v_tab.shape[0] // PEER_VALUE_TE
    tiles_per_batch = tokens_per_batch // tm
    tok = lambda i, e: (i, 0)
    one = pl.Buffered(1)
    return pl.pallas_call(
        _peer_values_body,
        grid=(t // tm, n_e),
        in_specs=[
            pl.BlockSpec((ssel.shape[0], tm, hk), lambda i, e: (0, i, 0)),
            pl.BlockSpec((tm, hk), tok), pl.BlockSpec((tm, hk), tok), pl.BlockSpec((tm, hk), tok),
            pl.BlockSpec((PEER_VALUE_TE, d), lambda i, e: (e, 0)),
            pl.BlockSpec((tm, d), tok),
            pl.BlockSpec((1, 1, d), lambda i, e: (i // tiles_per_batch, 0, 0)),
            pl.BlockSpec((1, d), lambda i, e: (0, 0)),
            pl.BlockSpec((1, d), lambda i, e: (0, 0)),
        ],
        out_specs=pl.BlockSpec((tm, d), tok, pipeline_mode=one),
        out_shape=jax.ShapeDtypeStruct((t, d), F32),
        scratch_shapes=[
            pltpu.VMEM((tm, hk), F32),
            pltpu.VMEM((tm * GATE_PITCH, LANES), F32),
            pltpu.VMEM((tm, d), F32),
        ],
        compiler_params=_cparams(("parallel", "arbitrary")),
        name="peer_values",
    )(ssel, a, b, g, v_tab, x1, g2, ln_g, ln_b)


FFT_N2 = 128
FFT_PITCH = FFT_N2 + 8
HY_POS_PAD = 32
FFT_UNROLL = 8
FFT_STAGE_UNROLL = 16


def _dft_constants(n):
    big = 2 * n
    n1c = big // FFT_N2
    half = n1c // 2
    k1 = np.arange(n1c)[:, None]
    n1 = np.arange(half)[None, :]
    ang = 2 * np.pi * k1 * n1 / n1c
    c, s = np.cos(ang), np.sin(ang)
    f1 = np.block([[c, s], [-s, c]])
    k2 = np.arange(FFT_N2)[:, None]
    n2 = np.arange(FFT_N2)[None, :]
    ang = 2 * np.pi * k2 * n2 / FFT_N2
    c, s = np.cos(ang), np.sin(ang)
    d3 = np.block([[c, s], [-s, c]])
    d3i = np.block([[c, -s], [s, c]])
    ang = 2 * np.pi * n1.T * k1.T / n1c
    c, s = np.cos(ang), np.sin(ang)
    f3 = np.block([[c, -s], [s, c]]) / big
    ang = 2 * np.pi * (np.arange(n1c)[:, None] * np.arange(FFT_N2)[None, :]) / big
    twr = np.repeat(np.cos(ang).reshape(-1, 1), LANES, axis=1)
    twi = np.repeat(-np.sin(ang).reshape(-1, 1), LANES, axis=1)
    as32 = lambda a: np.asarray(a, np.float32)

    def split(a):
        parts, rest = [], np.asarray(a, np.float64)
        for _ in range(3):
            p = np.asarray(rest, BF16)
            parts.append(p)
            rest = rest - p.astype(np.float64)
        return np.stack(parts)
    return split(f1), split(d3), split(d3i), split(f3), as32(twr), as32(twi)


def _hdot(a, b):
    return jnp.dot(a, b, preferred_element_type=F32, precision=HIGHEST)


def _dft_dot(m_ref, x):
    x_hi = x.astype(BF16)
    x_lo = (x - x_hi.astype(F32)).astype(BF16)
    m_hi = m_ref[0]
    dot = functools.partial(jnp.dot, preferred_element_type=F32)
    return (dot(m_hi, x_hi) + dot(m_hi, x_lo)) + (dot(m_ref[1], x_hi) + dot(m_ref[2], x_hi))


def _fft_stage1(u_re, u_im, f1_ref, a_re, a_im):
    half = f1_ref.shape[2] // 2
    n1c = f1_ref.shape[1] // 2

    def column(n2):
        return jnp.concatenate([u_re[pl.ds(n2, half, stride=FFT_PITCH), :],
                                u_im[pl.ds(n2, half, stride=FFT_PITCH), :]], axis=0)

    def body(i, carry):
        a = _dft_dot(f1_ref, jnp.concatenate([column(2 * i), column(2 * i + 1)], axis=1))
        for s in range(2):
            a_re[pl.ds(2 * i + s, n1c, stride=FFT_PITCH), :] = a[:n1c, s * LANES:(s + 1) * LANES]
            a_im[pl.ds(2 * i + s, n1c, stride=FFT_PITCH), :] = a[n1c:, s * LANES:(s + 1) * LANES]
        return carry
    lax.fori_loop(0, FFT_N2 // 2, body, 0, unroll=FFT_STAGE_UNROLL)


def _twiddled_pair(a_re, a_im, twr_ref, twi_ref, i):
    t0 = pl.multiple_of(2 * i * FFT_N2, 2 * FFT_N2)
    rows = [pl.multiple_of((2 * i + s) * FFT_PITCH, 8) for s in range(2)]
    side = lambda ref, r: jnp.concatenate([ref[pl.ds(r[0], FFT_N2), :], ref[pl.ds(r[1], FFT_N2), :]], axis=1)
    tws = [pl.multiple_of(t0 + s * FFT_N2, FFT_N2) for s in range(2)]
    return rows, side(a_re, rows), side(a_im, rows), side(twr_ref, tws), side(twi_ref, tws)


def _fft_conv_middle(a_re, a_im, d3_ref, d3i_ref, twr_ref, twi_ref, hre_ref, him_ref):
    n1c = hre_ref.shape[1]

    def body(i, carry):
        rows, ar, ai, twr, twi = _twiddled_pair(a_re, a_im, twr_ref, twi_ref, i)
        x = _dft_dot(d3_ref, jnp.concatenate([ar * twr - ai * twi, ar * twi + ai * twr], axis=0))
        xr, xi = x[:FFT_N2], x[FFT_N2:]
        hr = jnp.concatenate([hre_ref[0, 2 * i], hre_ref[0, 2 * i + 1]], axis=1)
        hi = jnp.concatenate([him_ref[0, 2 * i], him_ref[0, 2 * i + 1]], axis=1)
        y = _dft_dot(d3i_ref, jnp.concatenate([xr * hr - xi * hi, xr * hi + xi * hr], axis=0))
        br, bi = y[:FFT_N2], y[FFT_N2:]
        out_re = br * twr + bi * twi
        out_im = bi * twr - br * twi
        for s in range(2):
            a_re[pl.ds(rows[s], FFT_N2), :] = out_re[:, s * LANES:(s + 1) * LANES]
            a_im[pl.ds(rows[s], FFT_N2), :] = out_im[:, s * LANES:(s + 1) * LANES]
        return carry
    lax.fori_loop(0, n1c // 2, body, 0, unroll=FFT_UNROLL)


def _fft_stage_inv(a_re, a_im, f3_ref, y_re, y_im):
    half = f3_ref.shape[1] // 2
    n1c = f3_ref.shape[2] // 2

    def column(n2):
        return jnp.concatenate([a_re[pl.ds(n2, n1c, stride=FFT_PITCH), :],
                                a_im[pl.ds(n2, n1c, stride=FFT_PITCH), :]], axis=0)

    def body(i, carry):
        y = _dft_dot(f3_ref, jnp.concatenate([column(2 * i), column(2 * i + 1)], axis=1))
        for s in range(2):
            y_re[pl.ds(2 * i + s, half, stride=FFT_PITCH), :] = y[:half, s * LANES:(s + 1) * LANES]
            y_im[pl.ds(2 * i + s, half, stride=FFT_PITCH), :] = y[half:, s * LANES:(s + 1) * LANES]
        return carry
    lax.fori_loop(0, FFT_N2 // 2, body, 0, unroll=FFT_STAGE_UNROLL)


def _short_conv_chunk(x_ref, bi, j, nblk, w_ref, b_ref):
    r0 = j * FFT_N2
    cur = x_ref[bi, r0:r0 + FFT_N2, :]
    row = lax.broadcasted_iota(jnp.int32, cur.shape, 0)
    if j == 0:
        prev = jnp.where(row == 0, 0.0, pltpu.roll(cur, 1, 0))
    else:
        prev = x_ref[bi, r0 - 1:r0 + FFT_N2 - 1, :]
    if j == nblk - 1:
        nxt = jnp.where(row == FFT_N2 - 1, 0.0, pltpu.roll(cur, FFT_N2 - 1, 0))
    else:
        nxt = x_ref[bi, r0 + 1:r0 + FFT_N2 + 1, :]
    return prev * w_ref[0:1, :] + cur * w_ref[1:2, :] + nxt * w_ref[2:3, :] + b_ref[...]


def _hy_conv_body(conv_a, a_ref, g_ref, wa_ref, ba_ref, wg_ref, bg_ref, skip_ref, hre_ref, him_ref,
                  f1_ref, d3_ref, d3i_ref, f3_ref, twr_ref, twi_ref, o_ref,
                  u_re, u_im, a_re, a_im, y_re, y_im):
    nblk = a_ref.shape[1] // FFT_N2
    for bi, dst in ((0, u_re), (1, u_im)):
        for j in range(nblk):
            if conv_a:
                blk = _short_conv_chunk(a_ref, bi, j, nblk, wa_ref, ba_ref)
            else:
                blk = a_ref[bi, j * FFT_N2:(j + 1) * FFT_N2, :]
            dst[j * FFT_PITCH:j * FFT_PITCH + FFT_N2, :] = blk
    _fft_stage1(u_re, u_im, f1_ref, a_re, a_im)
    _fft_conv_middle(a_re, a_im, d3_ref, d3i_ref, twr_ref, twi_ref, hre_ref, him_ref)
    _fft_stage_inv(a_re, a_im, f3_ref, y_re, y_im)
    skip = skip_ref[...]
    for bi, (ysrc, usrc) in enumerate(((y_re, u_re), (y_im, u_im))):
        for j in range(nblk):
            rows = slice(j * FFT_PITCH, j * FFT_PITCH + FFT_N2)
            gate = _short_conv_chunk(g_ref, bi, j, nblk, wg_ref, bg_ref)
            o_ref[bi, j * FFT_N2:(j + 1) * FFT_N2, :] = gate * (ysrc[rows, :] + usrc[rows, :] * skip)


def _hy_conv(a, a_col, g, g_col, conv_w, conv_b, skip, hre, him, order, consts, conv_a):
    b, n, _ = a.shape
    f1, d3, d3i, f3, twr, twi = consts
    n1c = f1.shape[1] // 2
    half = n1c // 2
    w = LANES
    tiles = HY_WIDTH // w
    one = pl.Buffered(1)
    data = lambda col: pl.BlockSpec((2, n, w), lambda ct, p: (p, 0, col + ct))
    wspec = lambda col: pl.BlockSpec((3, w), lambda ct, p: (0, col + ct))
    bspec = lambda col: pl.BlockSpec((1, w), lambda ct, p: (0, col + ct))
    hspec = pl.BlockSpec((1, n1c, FFT_N2, w), lambda ct, p: (order, 0, 0, ct), pipeline_mode=one)
    cs = lambda arr: pl.BlockSpec(arr.shape, lambda ct, p: (0,) * arr.ndim, pipeline_mode=one)
    a_wcol = a_col if conv_a else g_col
    return pl.pallas_call(
        functools.partial(_hy_conv_body, conv_a),
        grid=(tiles, b // 2),
        in_specs=[data(a_col), data(g_col), wspec(a_wcol), bspec(a_wcol), wspec(g_col), bspec(g_col),
                  pl.BlockSpec((1, w), lambda ct, p: (0, ct)), hspec, hspec,
                  cs(f1), cs(d3), cs(d3i), cs(f3), cs(twr), cs(twi)],
        out_specs=pl.BlockSpec((2, n, w), lambda ct, p: (p, 0, ct), pipeline_mode=one),
        out_shape=jax.ShapeDtypeStruct((b, n, HY_WIDTH), F32),
        scratch_shapes=[pltpu.VMEM((half * FFT_PITCH, w), F32), pltpu.VMEM((half * FFT_PITCH, w), F32),
                        pltpu.VMEM((n1c * FFT_PITCH, w), F32), pltpu.VMEM((n1c * FFT_PITCH, w), F32),
                        pltpu.VMEM((half * FFT_PITCH, w), F32), pltpu.VMEM((half * FFT_PITCH, w), F32)],
        compiler_params=_cparams(("parallel", "arbitrary")),
        name=f"hyena_conv{order + 1}",
    )(a, g, conv_w, conv_b, conv_w, conv_b, skip, hre, him, f1, d3, d3i, f3, twr, twi)


def _hy_filter_body(z_ref, w1_ref, b1_ref, f1_ref, w2_ref, b2_ref, f2_ref, w3_ref, b3_ref, dl_ref, o_ref):
    z = z_ref[...]
    h = jnp.sin(f1_ref[...] * (_hdot(z, w1_ref[...]) + b1_ref[...]))
    h = jnp.sin(f2_ref[...] * (_hdot(h, w2_ref[...]) + b2_ref[...]))
    h = _hdot(h, w3_ref[...]) + b3_ref[...]
    t = z[:, 0:1]
    o_ref[...] = h * (jnp.exp(-t * dl_ref[...]) + HY_WINDOW_SHIFT)


def _hy_filters(n, w1, b1, fr1, w2, b2, fr2, w3, b3):
    t = jnp.linspace(0.0, 1.0, n, dtype=F32)[:, None]
    wv = 2.0 * math.pi * jnp.arange(n, dtype=F32)[:, None] / n
    bands = jnp.linspace(1e-4, HY_BANDS - 1, HY_BANDS, dtype=F32)[None, :]
    z = jnp.concatenate([t, jnp.cos(bands * wv), -jnp.sin(bands * wv)], axis=-1)
    pos = z.shape[1]
    z = jnp.pad(z, ((0, 0), (0, HY_POS_PAD - pos)))
    w1p = jnp.pad(w1, ((0, HY_POS_PAD - pos), (0, 0)))
    min_decay = math.log(HY_DECAY_TARGET) / HY_SLOW_DECAY_PCT
    max_decay = math.log(HY_DECAY_TARGET) / HY_FAST_DECAY_PCT
    deltas = jnp.abs(jnp.linspace(min_decay, max_decay, HY_WIDTH, dtype=F32))
    n_out = w3.shape[1]
    dl = jnp.tile(deltas, n_out // HY_WIDTH)[None, :]
    tn = 512
    hid = w2.shape[0]
    full = lambda shape: pl.BlockSpec(shape, lambda i: (0, 0))
    return pl.pallas_call(
        _hy_filter_body,
        grid=(n // tn,),
        in_specs=[pl.BlockSpec((tn, HY_POS_PAD), lambda i: (i, 0)), full((HY_POS_PAD, hid)), full((1, hid)),
                  full((1, hid)), full((hid, hid)), full((1, hid)), full((1, hid)), full((hid, n_out)),
                  full((1, n_out)), full((1, n_out))],
        out_specs=pl.BlockSpec((tn, n_out), lambda i: (i, 0)),
        out_shape=jax.ShapeDtypeStruct((n, n_out), F32),
        compiler_params=_cparams(("parallel",)),
        name="hyena_filters",
    )(z, w1p, b1[None], fr1[None], w2, b2[None], fr2[None], w3, b3[None], dl)


def _hy_spectrum_body(f_ref, b_ref, f1_ref, d3_ref, twr_ref, twi_ref, hre_ref, him_ref, u_re, u_im, a_re, a_im):
    nblk = f_ref.shape[0] // FFT_N2
    n1c = hre_ref.shape[1]
    for part, out_ref in enumerate((hre_ref, him_ref)):
        for j in range(nblk):
            f = f_ref[j * FFT_N2:(j + 1) * FFT_N2, :]
            bw = b_ref[j * FFT_N2:(j + 1) * FFT_N2, :]
            if j == 0:
                row = lax.broadcasted_iota(jnp.int32, bw.shape, 0)
                bw = jnp.where(row == 0, 0.0, bw)
            rows = slice(j * FFT_PITCH, j * FFT_PITCH + FFT_N2)
            u_re[rows, :] = f + bw if part == 0 else f - bw
            u_im[rows, :] = jnp.zeros((FFT_N2, LANES), F32)
        _fft_stage1(u_re, u_im, f1_ref, a_re, a_im)

        def body(i, carry, part=part, out_ref=out_ref):
            _, ar, ai, twr, twi = _twiddled_pair(a_re, a_im, twr_ref, twi_ref, i)
            x = _dft_dot(d3_ref, jnp.concatenate([ar * twr - ai * twi, ar * twi + ai * twr], axis=0))
            x = x[:FFT_N2] if part == 0 else x[FFT_N2:]
            for s in range(2):
                out_ref[0, 2 * i + s] = x[:, s * LANES:(s + 1) * LANES]
            return carry
        lax.fori_loop(0, n1c // 2, body, 0, unroll=FFT_UNROLL)


def _hy_spectrum(h, consts):
    n = h.shape[0]
    f1, d3, _, _, twr, twi = consts
    n1c = f1.shape[1] // 2
    half = n1c // 2
    tiles = HY_WIDTH // LANES
    cs = lambda arr: pl.BlockSpec(arr.shape, lambda o, ct: (0,) * arr.ndim)
    out = jax.ShapeDtypeStruct((HY_ORDER, n1c, FFT_N2, HY_WIDTH), F32)
    ospec = pl.BlockSpec((1, n1c, FFT_N2, LANES), lambda o, ct: (o, 0, 0, ct))
    return pl.pallas_call(
        _hy_spectrum_body,
        grid=(HY_ORDER, tiles),
        in_specs=[pl.BlockSpec((n, LANES), lambda o, ct: (0, o * tiles + ct)),
                  pl.BlockSpec((n, LANES), lambda o, ct: (0, (HY_ORDER + o) * tiles + ct)),
                  cs(f1), cs(d3), cs(twr), cs(twi)],
        out_specs=[ospec, ospec],
        out_shape=[out, out],
        scratch_shapes=[pltpu.VMEM((half * FFT_PITCH, LANES), F32), pltpu.VMEM((half * FFT_PITCH, LANES), F32),
                        pltpu.VMEM((n1c * FFT_PITCH, LANES), F32), pltpu.VMEM((n1c * FFT_PITCH, LANES), F32)],
        compiler_params=_cparams(("parallel", "parallel")),
        name="hyena_spectrum",
    )(h, h, f1, d3, twr, twi)


def _hyena(p_lat, conv_w, conv_b, w1, b1, fr1, w2, b2, fr2, w3, b3, skip):
    n = p_lat.shape[1]
    consts = _dft_constants(n)
    h = _hy_filters(n, w1, b1, fr1, w2, b2, fr2, w3, b3)
    hre, him = _hy_spectrum(h, consts)
    tiles = HY_WIDTH // LANES
    col = COL_HY // LANES
    cw = jnp.pad(conv_w, ((0, 0), (COL_HY, 0)))
    cb = jnp.pad(conv_b[None], ((0, 0), (COL_HY, 0)))
    y = _hy_conv(p_lat, col, p_lat, col + tiles, cw, cb, skip[0][None], hre, him, 0, consts, True)
    return _hy_conv(y, 0, p_lat, col + 2 * tiles, cw, cb, skip[1][None], hre, him, 1, consts, False)


def _split_bf16(w):
    hi = w.astype(BF16)
    return jnp.stack([hi, (w - hi.astype(F32)).astype(BF16)])


def _pad_heads(w, heads):
    d = w.shape[0]
    w = w.reshape(d, heads, HEAD_DIM)
    w = jnp.pad(w, ((0, 0), (0, 0), (0, LANES - HEAD_DIM)))
    return w.reshape(d, heads * LANES)


def kernel(x, c, ctx, c_ctx, w_mod, b_mod, w_in, hy_conv_w, hy_conv_b, hy_f_w1, hy_f_b1, hy_f_freq1, hy_f_w2,
           hy_f_b2, hy_f_freq2, hy_f_w3, hy_f_b3, hy_skip, attn_sink, w_out, ln1_g, ln1_b, peer_wq, peer_keys1,
           peer_keys2, peer_u, peer_v, ln2_g, ln2_b):
    b, n, d = x.shape
    l = 0
    cc = jnp.concatenate([c, c_ctx[None], jnp.zeros((8 - b - 1, d), F32)], axis=0)
    mod = _modulation(cc, w_mod[l], b_mod[l][None])
    mod_lat = mod[:b].reshape(b, 6, 1, d)
    sh1, sc1, g1, sh2, sc2, g2 = (mod_lat[:, i] for i in range(6))
    mod_c = mod[b].reshape(6, 1, 1, d)
    csh1, csc1 = mod_c[0], mod_c[1]

    w = w_in[l]
    w_q = _pad_heads(w[:, PROJ_HY:KV_START], ATT_HEADS)
    w_k = _pad_heads(w[:, KV_START:KV_START + PROJ_KV], ATT_KV_HEADS)
    w_v = _pad_heads(w[:, KV_START + PROJ_KV:], ATT_KV_HEADS)
    w_pad = jnp.concatenate([w_q, w[:, :PROJ_HY], w_k, w_v], axis=1).astype(BF16)
    w_kv = jnp.concatenate([w_k, w_v], axis=1).astype(BF16)

    p_lat = _mod_matmul(x, sc1, sh1, w_pad, 512, "in_proj")
    kv_ctx = _mod_matmul(ctx, csc1, csh1, w_kv, ctx.shape[1], "ctx_kv_proj")

    cos_tab, sin_tab = _rope_tables(n)
    att = _attention(p_lat, kv_ctx, attn_sink[l], cos_tab, sin_tab)

    hy = _hyena(p_lat, hy_conv_w[l], hy_conv_b[l], hy_f_w1[l], hy_f_b1[l], hy_f_freq1[l], hy_f_w2[l], hy_f_b2[l],
                hy_f_freq2[l], hy_f_w3[l], hy_f_b3[l], hy_skip[l])

    wo = w_out[l]
    w_o_hy = wo[:HY_WIDTH].astype(BF16)
    w_o_att = jnp.pad(wo[HY_WIDTH:].reshape(ATT_HEADS, HEAD_DIM, d),
                      ((0, 0), (0, LANES - HEAD_DIM), (0, 0))).reshape(QPAD, d).astype(BF16)
    x1, hq = _outproj_ln(hy, att, x, w_o_hy, w_o_att, g1, sc2, sh2, ln1_g[l][None], ln1_b[l][None])

    hq2 = hq.reshape(b * n, d)
    a_idx, b_idx, gate = _peer_topk(hq2, peer_wq[l].astype(BF16), _split_bf16(peer_keys1[l]),
                                    _split_bf16(peer_keys2[l]))
    s_sel = _peer_scores(hq2, a_idx, b_idx, peer_u[l].astype(BF16))
    out = _peer_values(s_sel, a_idx, b_idx, gate, peer_v[l].astype(BF16), x1.reshape(b * n, d), g2,
                       ln2_g[l][None], ln2_b[l][None], n)
    return out.reshape(b, n, d)
```

```python
import functools
import math

import jax
import jax.numpy as jnp
import numpy as np
from jax import lax
from jax.experimental import pallas as pl
from jax.experimental.pallas import tpu as pltpu

F32 = jnp.float32
BF16 = jnp.bfloat16
HIGHEST = lax.Precision.HIGHEST

LANES = 128
VMEM_LIMIT = 60000 * 1024

D_MODEL = 1024
GRID_W = 64
HY_WIDTH = 512
HY_ORDER = 2
HY_BANDS = 8
HY_DECAY_TARGET = 1e-2
HY_FAST_DECAY_PCT = 0.3
HY_SLOW_DECAY_PCT = 1.5
HY_WINDOW_SHIFT = 0.05
ATT_HEADS = 8
ATT_KV_HEADS = 2
ATT_REP = ATT_HEADS // ATT_KV_HEADS
HEAD_DIM = 64
WINDOW = 128
BLOCK = 128
ROPE_BASE = 10000.0
ROPE_FREQS = HEAD_DIM // 4
PROJ_HY = (HY_ORDER + 1) * HY_WIDTH
PROJ_Q = ATT_HEADS * HEAD_DIM
PROJ_KV = ATT_KV_HEADS * HEAD_DIM
KV_START = PROJ_HY + PROJ_Q
PEER_KEYS = 128
PEER_HEADS = 8
PEER_QDIM = 256
PEER_TOPK = 16
LN_EPS = 1e-5
NEG_INF = -1e30
DEPTH = 1
DEEPNORM_ALPHA = (2.0 * DEPTH) ** 0.25

QPAD = ATT_HEADS * LANES
KVPAD = ATT_KV_HEADS * LANES
COL_Q = 0
COL_HY = QPAD
COL_K = QPAD + PROJ_HY
COL_V = COL_K + KVPAD
PROJ_PAD = COL_V + KVPAD


def _cparams(sem):
    return pltpu.CompilerParams(dimension_semantics=sem, vmem_limit_bytes=VMEM_LIMIT)


def _mod_body(c_ref, w_ref, b_ref, o_ref):
    c = c_ref[...]
    a = c * jax.nn.sigmoid(c)
    o_ref[...] = jnp.dot(a, w_ref[...], preferred_element_type=F32, precision=HIGHEST) + b_ref[...]


def _modulation(cc, w_mod, b_mod):
    rows, d = cc.shape
    n_out = w_mod.shape[1]
    tn = 1536
    return pl.pallas_call(
        _mod_body,
        grid=(n_out // tn,),
        in_specs=[
            pl.BlockSpec((rows, d), lambda j: (0, 0)),
            pl.BlockSpec((d, tn), lambda j: (0, j)),
            pl.BlockSpec((1, tn), lambda j: (0, j)),
        ],
        out_specs=pl.BlockSpec((rows, tn), lambda j: (0, j)),
        out_shape=jax.ShapeDtypeStruct((rows, n_out), F32),
        compiler_params=_cparams(("arbitrary",)),
        name="modulation",
    )(cc, w_mod, b_mod)


def _mod_matmul_body(x_ref, sc_ref, sh_ref, w_ref, o_ref):
    h = x_ref[0] * (1.0 + sc_ref[0]) + sh_ref[0]
    o_ref[0] = jnp.dot(h.astype(BF16), w_ref[...], preferred_element_type=F32)


def _mod_matmul(x, sc, sh, w, tm, name):
    b, n, d = x.shape
    n_out = w.shape[1]
    per_batch = sc.shape[0] == b
    mod_map = (lambda bi, i: (bi, 0, 0)) if per_batch else (lambda bi, i: (0, 0, 0))
    return pl.pallas_call(
        _mod_matmul_body,
        grid=(b, n // tm),
        in_specs=[
            pl.BlockSpec((1, tm, d), lambda bi, i: (bi, i, 0)),
            pl.BlockSpec((1, 1, d), mod_map),
            pl.BlockSpec((1, 1, d), mod_map),
            pl.BlockSpec((d, n_out), lambda bi, i: (0, 0)),
        ],
        out_specs=pl.BlockSpec((1, tm, n_out), lambda bi, i: (bi, i, 0)),
        out_shape=jax.ShapeDtypeStruct((b, n, n_out), F32),
        compiler_params=_cparams(("parallel", "parallel")),
        name=name,
    )(x, sc, sh, w)


def _rope_head(x, cos, sin_signed):
    lane = lax.broadcasted_iota(jnp.int32, x.shape, 1)
    first_half = (lane % 32) < 16
    partner = jnp.where(first_half, pltpu.roll(x, LANES - 16, 1), pltpu.roll(x, 16, 1))
    return x * cos + partner * sin_signed


ATT_TQ = 512
ROPE_CHUNK = 512


def _attn_body(sink_ref, q_ref, k_ref, v_ref, kvc_ref, cosq_ref, sinq_ref, cosk_ref, sink_tab_ref,
               o_ref, ks_ref, vs_ref, kcs_ref, vcs_ref):
    n = k_ref.shape[1]
    iq = pl.program_id(1)
    scale = HEAD_DIM ** -0.5

    @pl.when(iq == 0)
    def _prepare_keys():
        def chunk(ci, carry):
            r0 = pl.multiple_of(ci * ROPE_CHUNK, ROPE_CHUNK)
            cos = cosk_ref[pl.ds(r0, ROPE_CHUNK), :]
            sin = sink_tab_ref[pl.ds(r0, ROPE_CHUNK), :]
            for g in range(ATT_KV_HEADS):
                kg = k_ref[0, pl.ds(r0, ROPE_CHUNK), g * LANES:(g + 1) * LANES]
                ks_ref[pl.ds(r0, ROPE_CHUNK), g * LANES:(g + 1) * LANES] = _rope_head(kg, cos, sin).astype(BF16)
            vs_ref[pl.ds(r0, ROPE_CHUNK), :] = v_ref[0, pl.ds(r0, ROPE_CHUNK), :].astype(BF16)
            return carry
        lax.fori_loop(0, n // ROPE_CHUNK, chunk, 0)
        kcs_ref[...] = kvc_ref[0, :, 0:KVPAD].astype(BF16)
        vcs_ref[...] = kvc_ref[0, :, KVPAD:2 * KVPAD].astype(BF16)

    n_loc = 3 * BLOCK
    rows = ATT_REP * BLOCK
    row_i = lax.broadcasted_iota(jnp.int32, (rows, n_loc), 0)
    col_i = lax.broadcasted_iota(jnp.int32, (rows, n_loc), 1)
    rel = col_i - (row_i % BLOCK)
    head_of_row = lax.broadcasted_iota(jnp.int32, (rows, 1), 0) // BLOCK

    for j in range(ATT_TQ // BLOCK):
        blk = iq * (ATT_TQ // BLOCK) + j
        start = pl.multiple_of(jnp.clip((blk - 1) * BLOCK, 0, n - n_loc), BLOCK)
        qrows = slice(j * BLOCK, (j + 1) * BLOCK)
        cosq = cosq_ref[qrows, :]
        sinq = sinq_ref[qrows, :]
        delta = rel + (start - blk * BLOCK)
        in_window = jnp.abs(delta) <= WINDOW
        for g in range(ATT_KV_HEADS):
            heads = [ATT_REP * g + r for r in range(ATT_REP)]
            qg = jnp.concatenate(
                [(_rope_head(q_ref[0, qrows, h * LANES:(h + 1) * LANES], cosq, sinq) * scale).astype(BF16)
                 for h in heads], axis=0)
            kg = ks_ref[pl.ds(start, n_loc), g * LANES:(g + 1) * LANES]
            vg = vs_ref[pl.ds(start, n_loc), g * LANES:(g + 1) * LANES]
            kcg = kcs_ref[:, g * LANES:(g + 1) * LANES]
            vcg = vcs_ref[:, g * LANES:(g + 1) * LANES]
            nt = (((1,), (1,)), ((), ()))
            s_loc = lax.dot_general(qg, kg, nt, preferred_element_type=F32)
            s_ctx = lax.dot_general(qg, kcg, nt, preferred_element_type=F32)
            s_loc = jnp.where(in_window, s_loc, NEG_INF)
            sink_col = jnp.zeros((rows, 1), F32)
            for r, h in enumerate(heads):
                sink_col = jnp.where(head_of_row == r, sink_ref[h], sink_col)
            m = jnp.maximum(jnp.maximum(jnp.max(s_loc, axis=1, keepdims=True),
                                        jnp.max(s_ctx, axis=1, keepdims=True)), sink_col)
            p_loc = jnp.exp(s_loc - m)
            p_ctx = jnp.exp(s_ctx - m)
            den = (jnp.sum(p_loc, axis=1, keepdims=True) + jnp.sum(p_ctx, axis=1, keepdims=True)
                   + jnp.exp(sink_col - m))
            o = (jnp.dot(p_loc.astype(BF16), vg, preferred_element_type=F32)
                 + jnp.dot(p_ctx.astype(BF16), vcg, preferred_element_type=F32)) / den
            for r, h in enumerate(heads):
                o_ref[0, qrows, h * LANES:(h + 1) * LANES] = o[r * BLOCK:(r + 1) * BLOCK].astype(BF16)


def _attention(p_lat, kv_ctx, sink, cos_tab, sin_tab):
    b, n, _ = p_lat.shape
    n_ctx = kv_ctx.shape[1]
    grid_spec = pltpu.PrefetchScalarGridSpec(
        num_scalar_prefetch=1,
        grid=(b, n // ATT_TQ),
        in_specs=[
            pl.BlockSpec((1, ATT_TQ, QPAD), lambda bi, i, s: (bi, i, COL_Q // QPAD)),
            pl.BlockSpec((1, n, KVPAD), lambda bi, i, s: (bi, 0, COL_K // KVPAD)),
            pl.BlockSpec((1, n, KVPAD), lambda bi, i, s: (bi, 0, COL_V // KVPAD)),
            pl.BlockSpec((1, n_ctx, 2 * KVPAD), lambda bi, i, s: (bi, 0, 0)),
            pl.BlockSpec((ATT_TQ, LANES), lambda bi, i, s: (i, 0)),
            pl.BlockSpec((ATT_TQ, LANES), lambda bi, i, s: (i, 0)),
            pl.BlockSpec((n, LANES), lambda bi, i, s: (0, 0)),
            pl.BlockSpec((n, LANES), lambda bi, i, s: (0, 0)),
        ],
        out_specs=pl.BlockSpec((1, ATT_TQ, QPAD), lambda bi, i, s: (bi, i, 0)),
        scratch_shapes=[
            pltpu.VMEM((n, KVPAD), BF16),
            pltpu.VMEM((n, KVPAD), BF16),
            pltpu.VMEM((n_ctx, KVPAD), BF16),
            pltpu.VMEM((n_ctx, KVPAD), BF16),
        ],
    )
    return pl.pallas_call(
        _attn_body,
        grid_spec=grid_spec,
        out_shape=jax.ShapeDtypeStruct((b, n, QPAD), BF16),
        compiler_params=_cparams(("parallel", "arbitrary")),
        name="window_attention",
    )(sink, p_lat, p_lat, p_lat, kv_ctx, cos_tab, sin_tab, cos_tab, sin_tab)


def _rope_tables(n):
    rows = n // GRID_W
    row = jnp.repeat(jnp.arange(rows, dtype=F32), GRID_W)
    col = jnp.tile(jnp.arange(GRID_W, dtype=F32), rows)
    inv = ROPE_BASE ** (-jnp.arange(ROPE_FREQS, dtype=F32) / ROPE_FREQS)
    ang_r = row[:, None] * inv
    ang_c = col[:, None] * inv
    pad1 = jnp.ones((n, LANES - HEAD_DIM), F32)
    pad0 = jnp.zeros((n, LANES - HEAD_DIM), F32)
    cos = jnp.concatenate([jnp.cos(ang_r), jnp.cos(ang_r), jnp.cos(ang_c), jnp.cos(ang_c), pad1], axis=1)
    sin = jnp.concatenate([-jnp.sin(ang_r), jnp.sin(ang_r), -jnp.sin(ang_c), jnp.sin(ang_c), pad0], axis=1)
    return cos, sin


def _layer_norm(r, g, b):
    mu = jnp.mean(r, axis=-1, keepdims=True)
    var = jnp.mean(jnp.square(r - mu), axis=-1, keepdims=True)
    return (r - mu) * lax.rsqrt(var + LN_EPS) * g + b


def _outproj_body(hy_ref, att_ref, x_ref, wh_ref, wa_ref, g1_ref, sc2_ref, sh2_ref, lg_ref, lb_ref,
                  x1_ref, hq_ref):
    y = (jnp.dot(hy_ref[0].astype(BF16), wh_ref[...], preferred_element_type=F32)
         + jnp.dot(att_ref[0], wa_ref[...], preferred_element_type=F32))
    x1 = _layer_norm(DEEPNORM_ALPHA * x_ref[0] + g1_ref[0] * y, lg_ref[...], lb_ref[...])
    x1_ref[0] = x1
    hq_ref[0] = (x1 * (1.0 + sc2_ref[0]) + sh2_ref[0]).astype(BF16)


def _outproj_ln(hy, att, x, w_hy, w_att, g1, sc2, sh2, ln_g, ln_b, tm=512):
    b, n, d = x.shape
    modspec = pl.BlockSpec((1, 1, d), lambda bi, i: (bi, 0, 0))
    vecspec = pl.BlockSpec((1, d), lambda bi, i: (0, 0))
    return pl.pallas_call(
        _outproj_body,
        grid=(b, n // tm),
        in_specs=[
            pl.BlockSpec((1, tm, HY_WIDTH), lambda bi, i: (bi, i, 0)),
            pl.BlockSpec((1, tm, QPAD), lambda bi, i: (bi, i, 0)),
            pl.BlockSpec((1, tm, d), lambda bi, i: (bi, i, 0)),
            pl.BlockSpec(w_hy.shape, lambda bi, i: (0, 0)),
            pl.BlockSpec(w_att.shape, lambda bi, i: (0, 0)),
            modspec, modspec, modspec, vecspec, vecspec,
        ],
        out_specs=[
            pl.BlockSpec((1, tm, d), lambda bi, i: (bi, i, 0)),
            pl.BlockSpec((1, tm, d), lambda bi, i: (bi, i, 0)),
        ],
        out_shape=[jax.ShapeDtypeStruct((b, n, d), F32), jax.ShapeDtypeStruct((b, n, d), BF16)],
        compiler_params=_cparams(("parallel", "parallel")),
        name="outproj_ln1",
    )(hy, att, x, w_hy, w_att, g1, sc2, sh2, ln_g, ln_b)


PEER_TM = 1024
TOPK_UNROLL = 4
_STAIR = sorted(((i, j) for i in range(PEER_TOPK) for j in range(PEER_TOPK) if (i + 1) * (j + 1) <= PEER_TOPK),
                key=lambda p: p[0] * PEER_TOPK + p[1])
_STAIR_ROWS = -(-len(_STAIR) // 8) * 8
_STAIR_COUNT = [PEER_TOPK // (i + 1) for i in range(PEER_TOPK)]
_STAIR_START = [sum(_STAIR_COUNT[:i]) for i in range(PEER_TOPK)]


def _stair_ids(tm):
    ids = np.full((_STAIR_ROWS,), float(PEER_TOPK * PEER_TOPK), np.float32)
    ids[:len(_STAIR)] = [i * PEER_TOPK + j for i, j in _STAIR]
    return np.repeat(ids[:, None], tm, axis=1)


def _select_topk(problems, write_row):
    tm = problems[0][0].shape[1]

    def step(k, prev):
        new = []
        for p, ((s_ref, ids, pad_id), prev_id) in enumerate(zip(problems, prev)):
            s = jnp.where(ids == prev_id, -jnp.inf, s_ref[...])
            s_ref[...] = s
            m = jnp.max(s, axis=0, keepdims=True)
            win = jnp.min(jnp.where(s == m, ids, pad_id), axis=0, keepdims=True)
            write_row(p, k, m, win)
            new.append(win)
        return tuple(new)

    lax.fori_loop(0, PEER_TOPK, step, tuple(jnp.full((1, tm), -1.0, F32) for _ in problems), unroll=TOPK_UNROLL)


def _rows_by_rank(rank, table_ref):
    out = jnp.zeros(rank.shape, F32)
    for p in range(PEER_TOPK):
        out = jnp.where(rank == float(p), table_ref[p:p + 1, :], out)
    return out


def _peer_topk_body(hq_ref, wq_ref, k1_ref, k2_ref, sid_ref, a_ref, b_ref, g_ref,
                    q_s, s1_s, s2_s, v1_s, i1_s, v2_s, i2_s, c_s, t_s, f_s, ao_s, bo_s, go_s):
    tm = hq_ref.shape[0]
    q = jnp.dot(hq_ref[...], wq_ref[...], preferred_element_type=F32)
    for c in range(2 * PEER_HEADS):
        q_s[c] = q[:, c * LANES:(c + 1) * LANES]
    nt = (((1,), (1,)), ((), ()))
    key_id = lax.broadcasted_iota(jnp.int32, (PEER_KEYS, tm), 0).astype(F32)

    def head(h, carry):
        for half, (kref, s_s) in enumerate(((k1_ref, s1_s), (k2_ref, s2_s))):
            qh = q_s[2 * h + half]
            q_hi = qh.astype(BF16)
            q_lo = (qh - q_hi.astype(F32)).astype(BF16)
            ntdot = functools.partial(lax.dot_general, dimension_numbers=nt, preferred_element_type=F32)
            s_s[...] = ntdot(kref[0], q_hi) + (ntdot(kref[0], q_lo) + ntdot(kref[1], q_hi))

        def write1(p, k, val, idx):
            vs, is_ = ((v1_s, i1_s), (v2_s, i2_s))[p]
            vs[pl.ds(k, 1), :] = val
            is_[pl.ds(k, 1), :] = idx
        _select_topk([(s1_s, key_id, float(PEER_KEYS)), (s2_s, key_id, float(PEER_KEYS))], write1)

        c_s[...] = jnp.full(c_s.shape, -jnp.inf, F32)
        for i in range(PEER_TOPK):
            r0, cnt = _STAIR_START[i], _STAIR_COUNT[i]
            c_s[r0:r0 + cnt, :] = v1_s[i:i + 1, :] + v2_s[0:cnt, :]

        def write2(p, k, val, idx):
            t_s[pl.ds(k, 1), :] = val
            f_s[pl.ds(k, 1), :] = idx
        _select_topk([(c_s, sid_ref[...], float(PEER_TOPK * PEER_TOPK))], write2)

        flat = f_s[...]
        rank1 = jnp.floor(flat * (1.0 / PEER_TOPK))
        rank2 = flat - rank1 * PEER_TOPK
        rows = pl.ds(pl.multiple_of(h * PEER_TOPK, PEER_TOPK), PEER_TOPK)
        ao_s[rows, :] = _rows_by_rank(rank1, i1_s)
        bo_s[rows, :] = _rows_by_rank(rank2, i2_s)
        t = t_s[...]
        e = jnp.exp(t - jnp.max(t, axis=0, keepdims=True))
        go_s[rows, :] = e / jnp.sum(e, axis=0, keepdims=True)
        return carry

    lax.fori_loop(0, PEER_HEADS, head, 0)
    a_ref[...] = ao_s[...].T
    b_ref[...] = bo_s[...].T
    g_ref[...] = go_s[...].T


def _peer_topk(hq, wq, keys1, keys2):
    t, d = hq.shape
    tm = PEER_TM
    hk = PEER_HEADS * PEER_TOPK
    out = jax.ShapeDtypeStruct((t, hk), F32)
    ospec = pl.BlockSpec((tm, hk), lambda i: (i, 0))
    sid = _stair_ids(tm)
    return pl.pallas_call(
        _peer_topk_body,
        grid=(t // tm,),
        in_specs=[
            pl.BlockSpec((tm, d), lambda i: (i, 0)),
            pl.BlockSpec(wq.shape, lambda i: (0, 0)),
            pl.BlockSpec(keys1.shape, lambda i: (0, 0, 0)),
            pl.BlockSpec(keys2.shape, lambda i: (0, 0, 0)),
            pl.BlockSpec(sid.shape, lambda i: (0, 0)),
        ],
        out_specs=[ospec, ospec, ospec],
        out_shape=[out, out, out],
        scratch_shapes=[
            pltpu.VMEM((2 * PEER_HEADS, tm, LANES), F32),
            pltpu.VMEM((PEER_KEYS, tm), F32), pltpu.VMEM((PEER_KEYS, tm), F32),
            pltpu.VMEM((PEER_TOPK, tm), F32), pltpu.VMEM((PEER_TOPK, tm), F32),
            pltpu.VMEM((PEER_TOPK, tm), F32), pltpu.VMEM((PEER_TOPK, tm), F32),
            pltpu.VMEM((_STAIR_ROWS, tm), F32),
            pltpu.VMEM((PEER_TOPK, tm), F32), pltpu.VMEM((PEER_TOPK, tm), F32),
            pltpu.VMEM((hk, tm), F32), pltpu.VMEM((hk, tm), F32), pltpu.VMEM((hk, tm), F32),
        ],
        compiler_params=_cparams(("parallel",)),
        name="peer_topk",
    )(hq, wq, keys1, keys2, sid)


PEER_TE = 8192
PEER_GROUPS = PEER_TE // PEER_KEYS
PEER_SCORE_TM = 512
PEER_VALUE_TM = 512
PEER_VALUE_TE = 2048
GATE_PITCH = PEER_KEYS + 8
GATE_UNROLL = 64


def _peer_scores_body(hq_ref, a_ref, b_ref, u_ref, o_ref):
    e = pl.program_id(0)
    s = lax.dot_general(hq_ref[...], u_ref[...], (((1,), (1,)), ((), ())),
                        preferred_element_type=F32)
    b_idx = b_ref[...].astype(jnp.int32)
    a_val = a_ref[...]
    cur = jnp.zeros(a_val.shape, F32)
    for jj in range(PEER_GROUPS):
        cand = jnp.take_along_axis(s[:, jj * LANES:(jj + 1) * LANES], b_idx, axis=1)
        cur = jnp.where(a_val == (e * PEER_GROUPS + jj).astype(F32), cand, cur)
    o_ref[0] = cur


def _peer_scores(hq, a, b, u_tab):
    t, d = hq.shape
    tm = PEER_SCORE_TM
    hk = a.shape[1]
    n_e = u_tab.shape[0] // PEER_TE
    tok = lambda e, i: (i, 0)
    return pl.pallas_call(
        _peer_scores_body,
        grid=(n_e, t // tm),
        in_specs=[
            pl.BlockSpec((tm, d), tok),
            pl.BlockSpec((tm, hk), tok), pl.BlockSpec((tm, hk), tok),
            pl.BlockSpec((PEER_TE, d), lambda e, i: (e, 0)),
        ],
        out_specs=pl.BlockSpec((1, tm, hk), lambda e, i: (e, i, 0)),
        out_shape=jax.ShapeDtypeStruct((n_e, t, hk), F32),
        compiler_params=_cparams(("arbitrary", "arbitrary")),
        name="peer_scores",
    )(hq, a, b, u_tab)


def _peer_values_body(ss_ref, a_ref, b_ref, g_ref, v_ref, x1_ref, g2_ref, lg_ref, lb_ref, o_ref,
                      w_s, hd_s, acc_s):
    tm = a_ref.shape[0]
    n_e = pl.num_programs(1)
    e = pl.program_id(1)
    nt = (((1,), (1,)), ((), ()))

    @pl.when(e == 0)
    def _gates():
        s_sel = jnp.sum(ss_ref[...], axis=0)
        act = 0.5 * s_sel * (1.0 + lax.erf(s_sel * (2.0 ** -0.5)))
        w_s[...] = g_ref[...] * act
        sub = lax.broadcasted_iota(jnp.int32, (PEER_KEYS, LANES), 0).astype(F32)

        def token(t, carry):
            a_row = a_ref[pl.ds(t, 1), :]
            b_row = b_ref[pl.ds(t, 1), :]
            w_row = w_s[pl.ds(t, 1), :]
            lhs = jnp.where(sub == a_row, w_row, 0.0).astype(BF16)
            rhs = jnp.where(sub == b_row, 1.0, 0.0).astype(BF16)
            tile = lax.dot_general(lhs, rhs, nt, preferred_element_type=F32)
            hd_s[pl.ds(pl.multiple_of(t * GATE_PITCH, 8), PEER_KEYS), :] = tile
            return carry
        lax.fori_loop(0, tm, token, 0, unroll=GATE_UNROLL)
        acc_s[...] = jnp.zeros(acc_s.shape, F32)

    groups = v_ref.shape[0] // PEER_KEYS
    j0 = e * groups
    lhs = jnp.concatenate(
        [hd_s[pl.ds(j0 + jj, tm, stride=GATE_PITCH), :].astype(BF16) for jj in range(groups)], axis=1)
    acc_s[...] += jnp.dot(lhs, v_ref[...], preferred_element_type=F32)

    @pl.when(e == n_e - 1)
    def _finish():
        o_ref[...] = _layer_norm(DEEPNORM_ALPHA * x1_ref[...] + g2_ref[0] * acc_s[...], lg_ref[...], lb_ref[...])


def _peer_values(ssel, a, b, g, v_tab, x1, g2, ln_g, ln_b, tokens_per_batch):
    t, d = x1.shape
    tm = PEER_VALUE_TM
    hk = a.shape[1]
    n_e = v_tab.shape[0] // PEER_VALUE_TE
    tiles_per_batch = tokens_per_batch // tm
    tok = lambda i, e: (i, 0)
    one = pl.Buffered(1)
    return pl.pallas_call(
        _peer_values_body,
        grid=(t // tm, n_e),
        in_specs=[
            pl.BlockSpec((ssel.shape[0], tm, hk), lambda i, e: (0, i, 0)),
            pl.BlockSpec((tm, hk), tok), pl.BlockSpec((tm, hk), tok), pl.BlockSpec((tm, hk), tok),
            pl.BlockSpec((PEER_VALUE_TE, d), lambda i, e: (e, 0)),
            pl.BlockSpec((tm, d), tok),
            pl.BlockSpec((1, 1, d), lambda i, e: (i // tiles_per_batch, 0, 0)),
            pl.BlockSpec((1, d), lambda i, e: (0, 0)),
            pl.BlockSpec((1, d), lambda i, e: (0, 0)),
        ],
        out_specs=pl.BlockSpec((tm, d), tok, pipeline_mode=one),
        out_shape=jax.ShapeDtypeStruct((t, d), F32),
        scratch_shapes=[
            pltpu.VMEM((tm, hk), F32),
            pltpu.VMEM((tm * GATE_PITCH, LANES), F32),
            pltpu.VMEM((tm, d), F32),
        ],
        compiler_params=_cparams(("parallel", "arbitrary")),
        name="peer_values",
    )(ssel, a, b, g, v_tab, x1, g2, ln_g, ln_b)


FFT_N2 = 128
FFT_PITCH = FFT_N2 + 8
HY_POS_PAD = 32
FFT_UNROLL = 8
FFT_STAGE_UNROLL = 16


def _dft_constants(n):
    big = 2 * n
    n1c = big // FFT_N2
    half = n1c // 2
    k1 = np.arange(n1c)[:, None]
    n1 = np.arange(half)[None, :]
    ang = 2 * np.pi * k1 * n1 / n1c
    c, s = np.cos(ang), np.sin(ang)
    f1 = np.block([[c, s], [-s, c]])
    k2 = np.arange(FFT_N2)[:, None]
    n2 = np.arange(FFT_N2)[None, :]
    ang = 2 * np.pi * k2 * n2 / FFT_N2
    c, s = np.cos(ang), np.sin(ang)
    d3 = np.block([[c, s], [-s, c]])
    d3i = np.block([[c, -s], [s, c]])
    ang = 2 * np.pi * n1.T * k1.T / n1c
    c, s = np.cos(ang), np.sin(ang)
    f3 = np.block([[c, -s], [s, c]]) / big
    ang = 2 * np.pi * (np.arange(n1c)[:, None] * np.arange(FFT_N2)[None, :]) / big
    twr = np.repeat(np.cos(ang).reshape(-1, 1), LANES, axis=1)
    twi = np.repeat(-np.sin(ang).reshape(-1, 1), LANES, axis=1)
    as32 = lambda a: np.asarray(a, np.float32)

    def split(a):
        parts, rest = [], np.asarray(a, np.float64)
        for _ in range(3):
            p = np.asarray(rest, BF16)
            parts.append(p)
            rest = rest - p.astype(np.float64)
        return np.stack(parts)
    return split(f1), split(d3), split(d3i), split(f3), as32(twr), as32(twi)


def _hdot(a, b):
    a_hi = a.astype(BF16)
    a_lo = (a - a_hi.astype(F32)).astype(BF16)
    b_hi = b.astype(BF16)
    b_lo = (b - b_hi.astype(F32)).astype(BF16)
    dot = functools.partial(jnp.dot, preferred_element_type=F32)
    return dot(a_hi, b_hi) + (dot(a_hi, b_lo) + dot(a_lo, b_hi))


def _dft_dot(m_ref, x):
    x_hi = x.astype(BF16)
    x_lo = (x - x_hi.astype(F32)).astype(BF16)
    m_hi = m_ref[0]
    dot = functools.partial(jnp.dot, preferred_element_type=F32)
    return (dot(m_hi, x_hi) + dot(m_hi, x_lo)) + (dot(m_ref[1], x_hi) + dot(m_ref[2], x_hi))


def _fft_stage1(u_re, u_im, f1_ref, a_re, a_im):
    half = f1_ref.shape[2] // 2
    n1c = f1_ref.shape[1] // 2

    def column(n2):
        return jnp.concatenate([u_re[pl.ds(n2, half, stride=FFT_PITCH), :],
                                u_im[pl.ds(n2, half, stride=FFT_PITCH), :]], axis=0)

    def body(i, carry):
        a = _dft_dot(f1_ref, jnp.concatenate([column(2 * i), column(2 * i + 1)], axis=1))
        for s in range(2):
            a_re[pl.ds(2 * i + s, n1c, stride=FFT_PITCH), :] = a[:n1c, s * LANES:(s + 1) * LANES]
            a_im[pl.ds(2 * i + s, n1c, stride=FFT_PITCH), :] = a[n1c:, s * LANES:(s + 1) * LANES]
        return carry
    lax.fori_loop(0, FFT_N2 // 2, body, 0, unroll=FFT_STAGE_UNROLL)


def _twiddled_pair(a_re, a_im, twr_ref, twi_ref, i):
    t0 = pl.multiple_of(2 * i * FFT_N2, 2 * FFT_N2)
    rows = [pl.multiple_of((2 * i + s) * FFT_PITCH, 8) for s in range(2)]
    side = lambda ref, r: jnp.concatenate([ref[pl.ds(r[0], FFT_N2), :], ref[pl.ds(r[1], FFT_N2), :]], axis=1)
    tws = [pl.multiple_of(t0 + s * FFT_N2, FFT_N2) for s in range(2)]
    return rows, side(a_re, rows), side(a_im, rows), side(twr_ref, tws), side(twi_ref, tws)


def _fft_conv_middle(a_re, a_im, d3_ref, d3i_ref, twr_ref, twi_ref, hre_ref, him_ref):
    n1c = hre_ref.shape[1]

    def body(i, carry):
        rows, ar, ai, twr, twi = _twiddled_pair(a_re, a_im, twr_ref, twi_ref, i)
        x = _dft_dot(d3_ref, jnp.concatenate([ar * twr - ai * twi, ar * twi + ai * twr], axis=0))
        xr, xi = x[:FFT_N2], x[FFT_N2:]
        hr = jnp.concatenate([hre_ref[0, 2 * i], hre_ref[0, 2 * i + 1]], axis=1)
        hi = jnp.concatenate([him_ref[0, 2 * i], him_ref[0, 2 * i + 1]], axis=1)
        y = _dft_dot(d3i_ref, jnp.concatenate([xr * hr - xi * hi, xr * hi + xi * hr], axis=0))
        br, bi = y[:FFT_N2], y[FFT_N2:]
        out_re = br * twr + bi * twi
        out_im = bi * twr - br * twi
        for s in range(2):
            a_re[pl.ds(rows[s], FFT_N2), :] = out_re[:, s * LANES:(s + 1) * LANES]
            a_im[pl.ds(rows[s], FFT_N2), :] = out_im[:, s * LANES:(s + 1) * LANES]
        return carry
    lax.fori_loop(0, n1c // 2, body, 0, unroll=FFT_UNROLL)


def _fft_stage_inv(a_re, a_im, f3_ref, y_re, y_im):
    half = f3_ref.shape[1] // 2
    n1c = f3_ref.shape[2] // 2

    def column(n2):
        return jnp.concatenate([a_re[pl.ds(n2, n1c, stride=FFT_PITCH), :],
                                a_im[pl.ds(n2, n1c, stride=FFT_PITCH), :]], axis=0)

    def body(i, carry):
        y = _dft_dot(f3_ref, jnp.concatenate([column(2 * i), column(2 * i + 1)], axis=1))
        for s in range(2):
            y_re[pl.ds(2 * i + s, half, stride=FFT_PITCH), :] = y[:half, s * LANES:(s + 1) * LANES]
            y_im[pl.ds(2 * i + s, half, stride=FFT_PITCH), :] = y[half:, s * LANES:(s + 1) * LANES]
        return carry
    lax.fori_loop(0, FFT_N2 // 2, body, 0, unroll=FFT_STAGE_UNROLL)


def _short_conv_chunk(x_ref, bi, j, nblk, w_ref, b_ref):
    r0 = j * FFT_N2
    cur = x_ref[bi, r0:r0 + FFT_N2, :]
    row = lax.broadcasted_iota(jnp.int32, cur.shape, 0)
    if j == 0:
        prev = jnp.where(row == 0, 0.0, pltpu.roll(cur, 1, 0))
    else:
        prev = x_ref[bi, r0 - 1:r0 + FFT_N2 - 1, :]
    if j == nblk - 1:
        nxt = jnp.where(row == FFT_N2 - 1, 0.0, pltpu.roll(cur, FFT_N2 - 1, 0))
    else:
        nxt = x_ref[bi, r0 + 1:r0 + FFT_N2 + 1, :]
    return prev * w_ref[0:1, :] + cur * w_ref[1:2, :] + nxt * w_ref[2:3, :] + b_ref[...]


def _hy_conv_body(conv_a, a_ref, g_ref, wa_ref, ba_ref, wg_ref, bg_ref, skip_ref, hre_ref, him_ref,
                  f1_ref, d3_ref, d3i_ref, f3_ref, twr_ref, twi_ref, o_ref,
                  u_re, u_im, a_re, a_im, y_re, y_im):
    nblk = a_ref.shape[1] // FFT_N2
    for bi, dst in ((0, u_re), (1, u_im)):
        for j in range(nblk):
            if conv_a:
                blk = _short_conv_chunk(a_ref, bi, j, nblk, wa_ref, ba_ref)
            else:
                blk = a_ref[bi, j * FFT_N2:(j + 1) * FFT_N2, :]
            dst[j * FFT_PITCH:j * FFT_PITCH + FFT_N2, :] = blk
    _fft_stage1(u_re, u_im, f1_ref, a_re, a_im)
    _fft_conv_middle(a_re, a_im, d3_ref, d3i_ref, twr_ref, twi_ref, hre_ref, him_ref)
    _fft_stage_inv(a_re, a_im, f3_ref, y_re, y_im)
    skip = skip_ref[...]
    for bi, (ysrc, usrc) in enumerate(((y_re, u_re), (y_im, u_im))):
        for j in range(nblk):
            rows = slice(j * FFT_PITCH, j * FFT_PITCH + FFT_N2)
            gate = _short_conv_chunk(g_ref, bi, j, nblk, wg_ref, bg_ref)
            o_ref[bi, j * FFT_N2:(j + 1) * FFT_N2, :] = gate * (ysrc[rows, :] + usrc[rows, :] * skip)


def _hy_conv(a, a_col, g, g_col, conv_w, conv_b, skip, hre, him, order, consts, conv_a):
    b, n, _ = a.shape
    f1, d3, d3i, f3, twr, twi = consts
    n1c = f1.shape[1] // 2
    half = n1c // 2
    w = LANES
    tiles = HY_WIDTH // w
    one = pl.Buffered(1)
    data = lambda col: pl.BlockSpec((2, n, w), lambda ct, p: (p, 0, col + ct))
    wspec = lambda col: pl.BlockSpec((3, w), lambda ct, p: (0, col + ct))
    bspec = lambda col: pl.BlockSpec((1, w), lambda ct, p: (0, col + ct))
    hspec = pl.BlockSpec((1, n1c, FFT_N2, w), lambda ct, p: (order, 0, 0, ct), pipeline_mode=one)
    cs = lambda arr: pl.BlockSpec(arr.shape, lambda ct, p: (0,) * arr.ndim, pipeline_mode=one)
    a_wcol = a_col if conv_a else g_col
    return pl.pallas_call(
        functools.partial(_hy_conv_body, conv_a),
        grid=(tiles, b // 2),
        in_specs=[data(a_col), data(g_col), wspec(a_wcol), bspec(a_wcol), wspec(g_col), bspec(g_col),
                  pl.BlockSpec((1, w), lambda ct, p: (0, ct)), hspec, hspec,
                  cs(f1), cs(d3), cs(d3i), cs(f3), cs(twr), cs(twi)],
        out_specs=pl.BlockSpec((2, n, w), lambda ct, p: (p, 0, ct), pipeline_mode=one),
        out_shape=jax.ShapeDtypeStruct((b, n, HY_WIDTH), F32),
        scratch_shapes=[pltpu.VMEM((half * FFT_PITCH, w), F32), pltpu.VMEM((half * FFT_PITCH, w), F32),
                        pltpu.VMEM((n1c * FFT_PITCH, w), F32), pltpu.VMEM((n1c * FFT_PITCH, w), F32),
                        pltpu.VMEM((half * FFT_PITCH, w), F32), pltpu.VMEM((half * FFT_PITCH, w), F32)],
        compiler_params=_cparams(("parallel", "arbitrary")),
        name=f"hyena_conv{order + 1}",
    )(a, g, conv_w, conv_b, conv_w, conv_b, skip, hre, him, f1, d3, d3i, f3, twr, twi)


def _hy_filter_body(z_ref, w1_ref, b1_ref, f1_ref, w2_ref, b2_ref, f2_ref, w3_ref, b3_ref, dl_ref, o_ref):
    z = z_ref[...]
    h = jnp.sin(f1_ref[...] * (_hdot(z, w1_ref[...]) + b1_ref[...]))
    h = jnp.sin(f2_ref[...] * (_hdot(h, w2_ref[...]) + b2_ref[...]))
    h = _hdot(h, w3_ref[...]) + b3_ref[...]
    t = z[:, 0:1]
    o_ref[...] = h * (jnp.exp(-t * dl_ref[...]) + HY_WINDOW_SHIFT)


def _hy_filters(n, w1, b1, fr1, w2, b2, fr2, w3, b3):
    t = jnp.linspace(0.0, 1.0, n, dtype=F32)[:, None]
    wv = 2.0 * math.pi * jnp.arange(n, dtype=F32)[:, None] / n
    bands = jnp.linspace(1e-4, HY_BANDS - 1, HY_BANDS, dtype=F32)[None, :]
    z = jnp.concatenate([t, jnp.cos(bands * wv), -jnp.sin(bands * wv)], axis=-1)
    pos = z.shape[1]
    z = jnp.pad(z, ((0, 0), (0, HY_POS_PAD - pos)))
    w1p = jnp.pad(w1, ((0, HY_POS_PAD - pos), (0, 0)))
    min_decay = math.log(HY_DECAY_TARGET) / HY_SLOW_DECAY_PCT
    max_decay = math.log(HY_DECAY_TARGET) / HY_FAST_DECAY_PCT
    deltas = jnp.abs(jnp.linspace(min_decay, max_decay, HY_WIDTH, dtype=F32))
    n_out = w3.shape[1]
    dl = jnp.tile(deltas, n_out // HY_WIDTH)[None, :]
    tn = 512
    hid = w2.shape[0]
    full = lambda shape: pl.BlockSpec(shape, lambda i: (0, 0))
    return pl.pallas_call(
        _hy_filter_body,
        grid=(n // tn,),
        in_specs=[pl.BlockSpec((tn, HY_POS_PAD), lambda i: (i, 0)), full((HY_POS_PAD, hid)), full((1, hid)),
                  full((1, hid)), full((hid, hid)), full((1, hid)), full((1, hid)), full((hid, n_out)),
                  full((1, n_out)), full((1, n_out))],
        out_specs=pl.BlockSpec((tn, n_out), lambda i: (i, 0)),
        out_shape=jax.ShapeDtypeStruct((n, n_out), F32),
        compiler_params=_cparams(("parallel",)),
        name="hyena_filters",
    )(z, w1p, b1[None], fr1[None], w2, b2[None], fr2[None], w3, b3[None], dl)


def _hy_spectrum_body(f_ref, b_ref, f1_ref, d3_ref, twr_ref, twi_ref, hre_ref, him_ref, u_re, u_im, a_re, a_im):
    nblk = f_ref.shape[0] // FFT_N2
    n1c = hre_ref.shape[1]
    for part, out_ref in enumerate((hre_ref, him_ref)):
        for j in range(nblk):
            f = f_ref[j * FFT_N2:(j + 1) * FFT_N2, :]
            bw = b_ref[j * FFT_N2:(j + 1) * FFT_N2, :]
            if j == 0:
                row = lax.broadcasted_iota(jnp.int32, bw.shape, 0)
                bw = jnp.where(row == 0, 0.0, bw)
            rows = slice(j * FFT_PITCH, j * FFT_PITCH + FFT_N2)
            u_re[rows, :] = f + bw if part == 0 else f - bw
            u_im[rows, :] = jnp.zeros((FFT_N2, LANES), F32)
        _fft_stage1(u_re, u_im, f1_ref, a_re, a_im)

        def body(i, carry, part=part, out_ref=out_ref):
            _, ar, ai, twr, twi = _twiddled_pair(a_re, a_im, twr_ref, twi_ref, i)
            x = _dft_dot(d3_ref, jnp.concatenate([ar * twr - ai * twi, ar * twi + ai * twr], axis=0))
            x = x[:FFT_N2] if part == 0 else x[FFT_N2:]
            for s in range(2):
                out_ref[0, 2 * i + s] = x[:, s * LANES:(s + 1) * LANES]
            return carry
        lax.fori_loop(0, n1c // 2, body, 0, unroll=FFT_UNROLL)


def _hy_spectrum(h, consts):
    n = h.shape[0]
    f1, d3, _, _, twr, twi = consts
    n1c = f1.shape[1] // 2
    half = n1c // 2
    tiles = HY_WIDTH // LANES
    cs = lambda arr: pl.BlockSpec(arr.shape, lambda o, ct: (0,) * arr.ndim)
    out = jax.ShapeDtypeStruct((HY_ORDER, n1c, FFT_N2, HY_WIDTH), F32)
    ospec = pl.BlockSpec((1, n1c, FFT_N2, LANES), lambda o, ct: (o, 0, 0, ct))
    return pl.pallas_call(
        _hy_spectrum_body,
        grid=(HY_ORDER, tiles),
        in_specs=[pl.BlockSpec((n, LANES), lambda o, ct: (0, o * tiles + ct)),
                  pl.BlockSpec((n, LANES), lambda o, ct: (0, (HY_ORDER + o) * tiles + ct)),
                  cs(f1), cs(d3), cs(twr), cs(twi)],
        out_specs=[ospec, ospec],
        out_shape=[out, out],
        scratch_shapes=[pltpu.VMEM((half * FFT_PITCH, LANES), F32), pltpu.VMEM((half * FFT_PITCH, LANES), F32),
                        pltpu.VMEM((n1c * FFT_PITCH, LANES), F32), pltpu.VMEM((n1c * FFT_PITCH, LANES), F32)],
        compiler_params=_cparams(("parallel", "parallel")),
        name="hyena_spectrum",
    )(h, h, f1, d3, twr, twi)


def _hyena(p_lat, conv_w, conv_b, w1, b1, fr1, w2, b2, fr2, w3, b3, skip):
    n = p_lat.shape[1]
    consts = _dft_constants(n)
    h = _hy_filters(n, w1, b1, fr1, w2, b2, fr2, w3, b3)
    hre, him = _hy_spectrum(h, consts)
    tiles = HY_WIDTH // LANES
    col = COL_HY // LANES
    cw = jnp.pad(conv_w, ((0, 0), (COL_HY, 0)))
    cb = jnp.pad(conv_b[None], ((0, 0), (COL_HY, 0)))
    y = _hy_conv(p_lat, col, p_lat, col + tiles, cw, cb, skip[0][None], hre, him, 0, consts, True)
    return _hy_conv(y, 0, p_lat, col + 2 * tiles, cw, cb, skip[1][None], hre, him, 1, consts, False)


def _split_bf16(w):
    hi = w.astype(BF16)
    return jnp.stack([hi, (w - hi.astype(F32)).astype(BF16)])


def _pad_heads(w, heads):
    d = w.shape[0]
    w = w.reshape(d, heads, HEAD_DIM)
    w = jnp.pad(w, ((0, 0), (0, 0), (0, LANES - HEAD_DIM)))
    return w.reshape(d, heads * LANES)


def kernel(x, c, ctx, c_ctx, w_mod, b_mod, w_in, hy_conv_w, hy_conv_b, hy_f_w1, hy_f_b1, hy_f_freq1, hy_f_w2,
           hy_f_b2, hy_f_freq2, hy_f_w3, hy_f_b3, hy_skip, attn_sink, w_out, ln1_g, ln1_b, peer_wq, peer_keys1,
           peer_keys2, peer_u, peer_v, ln2_g, ln2_b):
    b, n, d = x.shape
    l = 0
    cc = jnp.concatenate([c, c_ctx[None], jnp.zeros((8 - b - 1, d), F32)], axis=0)
    mod = _modulation(cc, w_mod[l], b_mod[l][None])
    mod_lat = mod[:b].reshape(b, 6, 1, d)
    sh1, sc1, g1, sh2, sc2, g2 = (mod_lat[:, i] for i in range(6))
    mod_c = mod[b].reshape(6, 1, 1, d)
    csh1, csc1 = mod_c[0], mod_c[1]

    w = w_in[l]
    w_q = _pad_heads(w[:, PROJ_HY:KV_START], ATT_HEADS)
    w_k = _pad_heads(w[:, KV_START:KV_START + PROJ_KV], ATT_KV_HEADS)
    w_v = _pad_heads(w[:, KV_START + PROJ_KV:], ATT_KV_HEADS)
    w_pad = jnp.concatenate([w_q, w[:, :PROJ_HY], w_k, w_v], axis=1).astype(BF16)
    w_kv = jnp.concatenate([w_k, w_v], axis=1).astype(BF16)

    p_lat = _mod_matmul(x, sc1, sh1, w_pad, 512, "in_proj")
    kv_ctx = _mod_matmul(ctx, csc1, csh1, w_kv, ctx.shape[1], "ctx_kv_proj")

    cos_tab, sin_tab = _rope_tables(n)
    att = _attention(p_lat, kv_ctx, attn_sink[l], cos_tab, sin_tab)

    hy = _hyena(p_lat, hy_conv_w[l], hy_conv_b[l], hy_f_w1[l], hy_f_b1[l], hy_f_freq1[l], hy_f_w2[l], hy_f_b2[l],
                hy_f_freq2[l], hy_f_w3[l], hy_f_b3[l], hy_skip[l])

    wo = w_out[l]
    w_o_hy = wo[:HY_WIDTH].astype(BF16)
    w_o_att = jnp.pad(wo[HY_WIDTH:].reshape(ATT_HEADS, HEAD_DIM, d),
                      ((0, 0), (0, LANES - HEAD_DIM), (0, 0))).reshape(QPAD, d).astype(BF16)
    x1, hq = _outproj_ln(hy, att, x, w_o_hy, w_o_att, g1, sc2, sh2, ln1_g[l][None], ln1_b[l][None])

    hq2 = hq.reshape(b * n, d)
    a_idx, b_idx, gate = _peer_topk(hq2, peer_wq[l].astype(BF16), _split_bf16(peer_keys1[l]),
                                    _split_bf16(peer_keys2[l]))
    s_sel = _peer_scores(hq2, a_idx, b_idx, peer_u[l].astype(BF16))
    out = _peer_values(s_sel, a_idx, b_idx, gate, peer_v[l].astype(BF16), x1.reshape(b * n, d), g2,
                       ln2_g[l][None], ln2_b[l][None], n)
    return out.reshape(b, n, d)
```
